```python
import math
import jax, jax.numpy as jnp
from jax import lax
import numpy as np

D_MODEL = 1024
BATCH = 8
SEQ = 2048
DEPTH = 2

CTX_LEN = 256
GRID_W = 64
HEAD_DIM = 64
D_MIX = D_MODEL
NA_HEADS = D_MIX // (4 * HEAD_DIM)
NA_WIDTH = NA_HEADS * HEAD_DIM
CONV_CH = D_MIX // 4
WA_HEADS = D_MIX // (2 * HEAD_DIM)
WA_KV_HEADS = WA_HEADS // 4
WA_QW = WA_HEADS * HEAD_DIM
WA_KVW = WA_KV_HEADS * HEAD_DIM
NA_WIN_ROWS = 8
NA_WIN_COLS = 16
CONV_WIDTH = 3
WA_WINDOW = 128
WA_BLOCK = 128
FFN_DIM = 4 * D_MODEL
ROPE_BASE = 10000.0
EPS = 1e-6
NEG_INF = -1e30
IN_SPLITS = (NA_WIDTH, NA_WIDTH, NA_WIDTH, CONV_CH, CONV_CH, CONV_CH, WA_QW, WA_KVW, WA_KVW)
IN_OFF = tuple(int(v) for v in np.cumsum((0,) + IN_SPLITS))
IN_WIDTH = IN_OFF[-1]

kernel_name = "hymba_natten_shortconv_swa_prefix_dit"


def rms_norm(x, gain):
    x32 = x.astype(jnp.float32)
    y = x32 * lax.rsqrt(jnp.mean(x32 * x32, axis=-1, keepdims=True) + EPS)
    return (y * gain.astype(jnp.float32)).astype(x.dtype)


def split_heads(t, n_heads):
    return t.reshape(*t.shape[:-1], n_heads, HEAD_DIM)


def split_proj(p):
    return jnp.split(p, list(IN_OFF[1:-1]), axis=-1)


def modulation(cond, w, b):
    return jnp.split(jax.nn.silu(cond) @ w + b, 6, axis=-1)


def axial_rope(t, row_pos, col_pos):
    quarter = HEAD_DIM // 4
    half = HEAD_DIM // 2
    inv = ROPE_BASE ** (-jnp.arange(quarter, dtype=jnp.float32) / quarter)

    def rot(u, pos):
        ang = pos[:, None] * inv[None, :]
        cos = jnp.cos(ang)[None, :, None, :]
        sin = jnp.sin(ang)[None, :, None, :]
        u1, u2 = u[..., :quarter], u[..., quarter:]
        return jnp.concatenate([u1 * cos - u2 * sin, u2 * cos + u1 * sin], axis=-1)

    t32 = t.astype(jnp.float32)
    out = jnp.concatenate([rot(t32[..., :half], row_pos), rot(t32[..., half:], col_pos)], axis=-1)
    return out.astype(t.dtype)


def neighborhood_attention(q, k, v, k_ctx, v_ctx, rpb, rows):
    B, S, H, hd = q.shape
    kr = min(NA_WIN_ROWS, rows)
    kc = NA_WIN_COLS
    scale = hd ** -0.5
    qg = q.reshape(B, rows, GRID_W, H, hd)
    kg = k.reshape(B, rows, GRID_W, H, hd)
    vg = v.reshape(B, rows, GRID_W, H, hd)
    r = jnp.arange(rows)
    row_idx = jnp.clip(r - kr // 2, 0, rows - kr)[:, None] + jnp.arange(kr)[None, :]
    k_loc = jnp.take(kg, row_idx, axis=1)
    v_loc = jnp.take(vg, row_idx, axis=1)
    col = jnp.arange(GRID_W)
    col_start = jnp.clip(col - kc // 2, 0, GRID_W - kc)
    col_ok = (col[None, :] >= col_start[:, None]) & (col[None, :] < col_start[:, None] + kc)
    dr = row_idx - r[:, None] + (NA_WIN_ROWS - 1)
    dc = jnp.clip(col[None, :] - col[:, None], -(kc - 1), kc - 1) + (kc - 1)
    bias = rpb[:, dr[:, None, :, None], dc[None, :, None, :]]
    bias = jnp.transpose(bias, (1, 0, 2, 3, 4)).astype(jnp.float32)
    s_loc = jnp.einsum('brqhd,brjkhd->brhqjk', qg, k_loc).astype(jnp.float32) * scale + bias[None]
    s_loc = jnp.where(col_ok[:, None, :], s_loc, NEG_INF).reshape(B, rows, H, GRID_W, kr * GRID_W)
    s_ctx = jnp.einsum('brqhd,bchd->brhqc', qg, k_ctx).astype(jnp.float32) * scale
    p = jax.nn.softmax(jnp.concatenate([s_ctx, s_loc], axis=-1), axis=-1).astype(v.dtype)
    lc = k_ctx.shape[1]
    p_ctx = p[..., :lc]
    p_loc = p[..., lc:].reshape(B, rows, H, GRID_W, kr, GRID_W)
    out = (jnp.einsum('brhqc,bchd->brqhd', p_ctx, v_ctx)
           + jnp.einsum('brhqjk,brjkhd->brqhd', p_loc, v_loc))
    return out.reshape(B, S, H * hd)


def window_attention(q, k, v, k_ctx, v_ctx, sink):
    B, S, H, hd = q.shape
    kvh = k.shape[2]
    g = H // kvh
    nb = S // WA_BLOCK
    scale = hd ** -0.5
    qb = q.reshape(B, nb, WA_BLOCK, kvh, g, hd)

    def band(t):
        tp = jnp.pad(t, ((0, 0), (WA_BLOCK, WA_BLOCK), (0, 0), (0, 0))).reshape(B, nb + 2, WA_BLOCK, kvh, hd)
        return jnp.concatenate([tp[:, :-2], tp[:, 1:-1], tp[:, 2:]], axis=2)

    kb, vb = band(k), band(v)
    i = jnp.arange(WA_BLOCK)[:, None]
    j = jnp.arange(3 * WA_BLOCK)[None, :]
    kpos = (jnp.arange(nb) * WA_BLOCK - WA_BLOCK)[:, None, None] + j[None]
    mask = (jnp.abs(j - WA_BLOCK - i) <= WA_WINDOW)[None] & (kpos >= 0) & (kpos < S)
    s_loc = jnp.einsum('bnqkgd,bnjkd->bnkgqj', qb, kb).astype(jnp.float32) * scale
    s_loc = jnp.where(mask[None, :, None, None], s_loc, NEG_INF)
    s_ctx = jnp.einsum('bnqkgd,bckd->bnkgqc', qb, k_ctx).astype(jnp.float32) * scale
    s_sink = jnp.broadcast_to(sink.astype(jnp.float32).reshape(1, 1, kvh, g, 1, 1), s_ctx.shape[:-1] + (1,))
    p = jax.nn.softmax(jnp.concatenate([s_sink, s_ctx, s_loc], axis=-1), axis=-1).astype(v.dtype)
    lc = k_ctx.shape[1]
    out = (jnp.einsum('bnkgqc,bckd->bnqkgd', p[..., 1:1 + lc], v_ctx)
           + jnp.einsum('bnkgqj,bnjkd->bnqkgd', p[..., 1 + lc:], vb))
    return out.reshape(B, S, H * hd)


def context_attention(q, k, v, sink=None):
    B, L, H, hd = q.shape
    kvh = k.shape[2]
    g = H // kvh
    qg = q.reshape(B, L, kvh, g, hd)
    s = jnp.einsum('bqkgd,bckd->bkgqc', qg, k).astype(jnp.float32) * (hd ** -0.5)
    if sink is not None:
        s_sink = jnp.broadcast_to(sink.astype(jnp.float32).reshape(1, kvh, g, 1, 1), s.shape[:-1] + (1,))
        p = jax.nn.softmax(jnp.concatenate([s_sink, s], axis=-1), axis=-1)[..., 1:]
    else:
        p = jax.nn.softmax(s, axis=-1)
    out = jnp.einsum('bkgqc,bckd->bqkgd', p.astype(v.dtype), v)
    return out.reshape(B, L, H * hd)


def short_conv(u, w, b):
    pad = CONV_WIDTH // 2
    length = u.shape[1]
    up = jnp.pad(u, ((0, 0), (pad, pad), (0, 0)))
    y = b
    for tap in range(CONV_WIDTH):
        y = y + w[tap] * up[:, tap:tap + length]
    return y


def mix_out(o_na, o_cv, o_wa, g_out, w_o):
    o = jnp.concatenate([
        rms_norm(o_na, g_out[:NA_WIDTH]),
        rms_norm(o_cv, g_out[NA_WIDTH:NA_WIDTH + CONV_CH]),
        rms_norm(o_wa, g_out[NA_WIDTH + CONV_CH:]),
    ], axis=-1)
    return o @ w_o


def sq_relu_mlp(h, w1, w2):
    return jnp.square(jax.nn.relu(h @ w1)) @ w2


def setup_inputs(seed: int = 0) -> dict:
    key = jax.random.key(seed)
    ks = jax.random.split(key, 21)
    f32 = jnp.float32

    def nrm(k, shape, s):
        return jax.random.normal(k, shape, f32) * s

    return {
        "x": nrm(ks[0], (BATCH, SEQ, D_MODEL), 1.0),
        "c": nrm(ks[1], (BATCH, D_MODEL), 1.0),
        "ctx": nrm(ks[2], (BATCH, CTX_LEN, D_MODEL), 1.0),
        "c_ctx": nrm(ks[3], (D_MODEL,), 1.0),
        "w_mod": nrm(ks[4], (DEPTH, D_MODEL, 6 * D_MODEL), 0.5 * D_MODEL ** -0.5),
        "b_mod": nrm(ks[5], (DEPTH, 6 * D_MODEL), 0.02),
        "g_norm1": 1.0 + nrm(ks[6], (DEPTH, D_MODEL), 0.02),
        "g_norm2": 1.0 + nrm(ks[7], (DEPTH, D_MODEL), 0.02),
        "w_in": nrm(ks[8], (DEPTH, D_MODEL, IN_WIDTH), D_MODEL ** -0.5),
        "na_q_gain": 1.0 + nrm(ks[9], (DEPTH, HEAD_DIM), 0.02),
        "na_k_gain": 1.0 + nrm(ks[10], (DEPTH, HEAD_DIM), 0.02),
        "na_rpb": nrm(ks[11], (DEPTH, NA_HEADS, 2 * NA_WIN_ROWS - 1, 2 * NA_WIN_COLS - 1), 0.1),
        "conv_w": nrm(ks[12], (DEPTH, CONV_WIDTH, CONV_CH), CONV_WIDTH ** -0.5),
        "conv_bias": nrm(ks[13], (DEPTH, CONV_CH), 0.02),
        "wa_q_gain": 1.0 + nrm(ks[14], (DEPTH, HEAD_DIM), 0.02),
        "wa_k_gain": 1.0 + nrm(ks[15], (DEPTH, HEAD_DIM), 0.02),
        "wa_sink": nrm(ks[16], (DEPTH, WA_HEADS), 0.5),
        "g_out": 1.0 + nrm(ks[17], (DEPTH, D_MIX), 0.02),
        "w_o": nrm(ks[18], (DEPTH, D_MIX, D_MODEL), D_MIX ** -0.5),
        "w_fc1": nrm(ks[19], (DEPTH, D_MODEL, FFN_DIM), D_MODEL ** -0.5),
        "w_fc2": nrm(ks[20], (DEPTH, FFN_DIM, D_MODEL), FFN_DIM ** -0.5),
    }


def reference(x, c, ctx, c_ctx, w_mod, b_mod, g_norm1, g_norm2, w_in, na_q_gain, na_k_gain, na_rpb,
              conv_w, conv_bias, wa_q_gain, wa_k_gain, wa_sink, g_out, w_o, w_fc1, w_fc2):
    S = x.shape[1]
    rows = S // GRID_W
    t = jnp.arange(S)
    row_pos = (t // GRID_W).astype(jnp.float32)
    col_pos = (t % GRID_W).astype(jnp.float32)
    for l in range(DEPTH):
        last = l == DEPTH - 1
        sh1, sc1, gt1, sh2, sc2, gt2 = [m[:, None, :] for m in modulation(c, w_mod[l], b_mod[l])]
        csh1, csc1, cgt1, csh2, csc2, cgt2 = modulation(c_ctx, w_mod[l], b_mod[l])

        h = rms_norm(x, g_norm1[l]) * (1 + sc1) + sh1
        hc = rms_norm(ctx, g_norm1[l]) * (1 + csc1) + csh1
        na_q, na_k, na_v, cv_x, cv_b, cv_c, wa_q, wa_k, wa_v = split_proj(h @ w_in[l])
        if last:
            na_kc, na_vc, wa_kc, wa_vc = [hc @ w_in[l][:, IN_OFF[i]:IN_OFF[i + 1]] for i in (1, 2, 7, 8)]
        else:
            na_qc, na_kc, na_vc, cv_xc, cv_bc, cv_cc, wa_qc, wa_kc, wa_vc = split_proj(hc @ w_in[l])

        na_kc = rms_norm(split_heads(na_kc, NA_HEADS), na_k_gain[l])
        na_vc = split_heads(na_vc, NA_HEADS)
        wa_kc = rms_norm(split_heads(wa_kc, WA_KV_HEADS), wa_k_gain[l])
        wa_vc = split_heads(wa_vc, WA_KV_HEADS)

        o_na = neighborhood_attention(
            rms_norm(split_heads(na_q, NA_HEADS), na_q_gain[l]),
            rms_norm(split_heads(na_k, NA_HEADS), na_k_gain[l]),
            split_heads(na_v, NA_HEADS), na_kc, na_vc, na_rpb[l], rows)
        o_cv = cv_b * short_conv(cv_c * cv_x, conv_w[l], conv_bias[l])
        o_wa = window_attention(
            axial_rope(rms_norm(split_heads(wa_q, WA_HEADS), wa_q_gain[l]), row_pos, col_pos),
            axial_rope(rms_norm(split_heads(wa_k, WA_KV_HEADS), wa_k_gain[l]), row_pos, col_pos),
            split_heads(wa_v, WA_KV_HEADS), wa_kc, wa_vc, wa_sink[l])
        x = x + gt1 * mix_out(o_na, o_cv, o_wa, g_out[l], w_o[l])

        if not last:
            oc_na = context_attention(rms_norm(split_heads(na_qc, NA_HEADS), na_q_gain[l]), na_kc, na_vc)
            oc_cv = cv_bc * short_conv(cv_cc * cv_xc, conv_w[l], conv_bias[l])
            oc_wa = context_attention(rms_norm(split_heads(wa_qc, WA_HEADS), wa_q_gain[l]), wa_kc, wa_vc,
                                      wa_sink[l])
            ctx = ctx + cgt1 * mix_out(oc_na, oc_cv, oc_wa, g_out[l], w_o[l])

        x = x + gt2 * sq_relu_mlp(rms_norm(x, g_norm2[l]) * (1 + sc2) + sh2, w_fc1[l], w_fc2[l])
        if not last:
            ctx = ctx + cgt2 * sq_relu_mlp(rms_norm(ctx, g_norm2[l]) * (1 + csc2) + csh2, w_fc1[l], w_fc2[l])
    return x
```

```python
import functools

import jax
import jax.numpy as jnp
from jax import lax
from jax.experimental import pallas as pl
from jax.experimental.pallas import tpu as pltpu

D_MODEL = 1024
GRID_W = 64
HEAD_DIM = 64
NA_HEADS = 4
NA_WIDTH = NA_HEADS * HEAD_DIM
CONV_CH = 256
WA_HEADS = 8
WA_KV_HEADS = 2
WA_GROUP = WA_HEADS // WA_KV_HEADS
WA_QW = WA_HEADS * HEAD_DIM
WA_KVW = WA_KV_HEADS * HEAD_DIM
NA_WIN_ROWS = 8
NA_WIN_COLS = 16
NA_DROWS = 2 * NA_WIN_ROWS - 1
NA_DCOLS = 2 * NA_WIN_COLS - 1
WA_WINDOW = 128
WA_BLOCK = 128
WA_BAND = 3 * WA_BLOCK
FFN_DIM = 4 * D_MODEL
FFN_CHUNK = 1024
ROPE_BASE = 10000.0
EPS = 1e-6
NEG_INF = -1e30
IN_WIDTH = 2304
OFF_NA_Q, OFF_NA_K, OFF_NA_V = 0, 256, 512
OFF_CV_X, OFF_CV_B, OFF_CV_C = 768, 1024, 1280
OFF_WA_Q, OFF_WA_K, OFF_WA_V = 1536, 2048, 2176
MOD_ROWS = 16
MOD_TN = 1024
LANES = 128
MXU_DIM = 256
VMEM_LIMIT = 48 * 1024 * 1024

F32 = jnp.float32
BF16 = jnp.bfloat16


def _dot(a, b):
    return jnp.dot(a, b, preferred_element_type=F32)


def _dot_nt(a, b):
    return lax.dot_general(a, b, (((1,), (1,)), ((), ())), preferred_element_type=F32)


def _split_bf16(a):
    hi = a.astype(BF16)
    lo = (a - hi.astype(F32)).astype(BF16)
    return hi, lo


def _iota(shape, dim):
    return lax.broadcasted_iota(jnp.int32, shape, dim)


def _params(**kw):
    return pltpu.CompilerParams(vmem_limit_bytes=VMEM_LIMIT, **kw)


def _mod_kernel(cond_ref, w_ref, b_ref, o_ref):
    a = cond_ref[...]
    a = a * (1.0 / (1.0 + jnp.exp(-a)))
    ah, al = _split_bf16(a)
    wh, wl = _split_bf16(w_ref[...])
    o_ref[...] = _dot(ah, wh) + _dot(al, wh) + _dot(ah, wl) + b_ref[...]


def _modulation(cond, w_mod, b_mod):
    depth = w_mod.shape[0]
    n_out = w_mod.shape[2]
    return pl.pallas_call(
        _mod_kernel,
        out_shape=jax.ShapeDtypeStruct((depth, MOD_ROWS, n_out), F32),
        grid=(depth, n_out // MOD_TN),
        in_specs=[
            pl.BlockSpec((MOD_ROWS, D_MODEL), lambda l, j: (0, 0)),
            pl.BlockSpec((None, D_MODEL, MOD_TN), lambda l, j: (l, 0, j)),
            pl.BlockSpec((None, 1, MOD_TN), lambda l, j: (l, 0, j)),
        ],
        out_specs=pl.BlockSpec((None, MOD_ROWS, MOD_TN), lambda l, j: (l, 0, j)),
        compiler_params=_params(),
        name="modulation",
    )(cond, w_mod, b_mod.reshape(depth, 1, n_out))


def _rpb_kernel(rpb_ref, o_ref):
    l = pl.program_id(0)
    shape = (GRID_W, LANES)
    q = _iota(shape, 0)
    lane = _iota(shape, 1)
    k = lane % GRID_W
    left = lane < GRID_W
    dc = jnp.clip(k - q, -(NA_WIN_COLS - 1), NA_WIN_COLS - 1) + (NA_WIN_COLS - 1)
    col_start = jnp.clip(q - NA_WIN_COLS // 2, 0, GRID_W - NA_WIN_COLS)
    col_ok = (k >= col_start) & (k < col_start + NA_WIN_COLS)
    base_l = l * (NA_HEADS * NA_DROWS * NA_DCOLS)

    def body(d, carry):
        for h in range(NA_HEADS):
            base = base_l + (h * NA_DROWS + d) * NA_DCOLS
            t = jnp.zeros(shape, F32)
            for i in range(NA_DCOLS):
                val = jnp.where(left, rpb_ref[base + i], rpb_ref[base + NA_DCOLS + i])
                t = jnp.where(dc == i, val, t)
            o_ref[d, h * GRID_W:(h + 1) * GRID_W, :] = jnp.where(col_ok, t, NEG_INF)
        return carry

    lax.fori_loop(0, NA_DROWS - 1, body, 0)


def _rpb_tables(na_rpb):
    depth = na_rpb.shape[0]
    return pl.pallas_call(
        _rpb_kernel,
        out_shape=jax.ShapeDtypeStruct((depth, NA_DROWS - 1, NA_HEADS * GRID_W, LANES), F32),
        grid=(depth,),
        in_specs=[pl.BlockSpec(memory_space=pltpu.SMEM)],
        out_specs=pl.BlockSpec((None, NA_DROWS - 1, NA_HEADS * GRID_W, LANES), lambda l: (l, 0, 0, 0)),
        compiler_params=_params(),
        name="rpb_tables",
    )(na_rpb.reshape(-1))


def _head_rms(t, gain):
    width = t.shape[1]
    sq = t * t
    hi, lo = _split_bf16(sq)
    blk = min(width, MXU_DIM)
    ones = jnp.where(_iota((blk, blk), 0) // HEAD_DIM == _iota((blk, blk), 1) // HEAD_DIM, 1.0, 0.0).astype(BF16)
    sums = [_dot(hi[:, c:c + blk], ones) + _dot(lo[:, c:c + blk], ones) for c in range(0, width, blk)]
    ss = sums[0] if len(sums) == 1 else jnp.concatenate(sums, axis=1)
    return t * lax.rsqrt(ss * (1.0 / HEAD_DIM) + EPS) * gain


def _rope(t, cos, sin):
    lane = _iota((t.shape[0], LANES), 1)
    first = (lane % (HEAD_DIM // 2)) < (HEAD_DIM // 4)
    out = []
    for c in range(0, t.shape[1], LANES):
        u = t[:, c:c + LANES]
        partner = jnp.where(first, pltpu.roll(u, LANES - HEAD_DIM // 4, axis=1), pltpu.roll(u, HEAD_DIM // 4, axis=1))
        out.append(u * cos + partner * sin)
    return out[0] if len(out) == 1 else jnp.concatenate(out, axis=1)


def _dup_kv(t):
    lane = _iota(t.shape, 1)
    swapped = pltpu.roll(t, HEAD_DIM, axis=1)
    left = lane < HEAD_DIM
    return jnp.concatenate([jnp.where(left, t, swapped), jnp.where(left, swapped, t)], axis=1)


def _inproj_kernel(x_ref, mod_ref, g1_ref, w_ref, gnq_ref, gnk_ref, gwq_ref, gwk_ref, *rest, rope):
    if rope:
        cos_ref, sin_ref = rest[:2]
        rest = rest[2:]
    naq_ref, nak_ref, nav_ref, cvu_ref, cvb_ref, waq_ref, wak_ref, wav_ref = rest
    x = x_ref[...]
    ms = jnp.mean(x * x, axis=-1, keepdims=True)
    h = x * lax.rsqrt(ms + EPS) * g1_ref[...]
    h = h * (1.0 + mod_ref[1]) + mod_ref[0]
    p = _dot(h.astype(BF16), w_ref[...])
    scale = HEAD_DIM ** -0.5
    naq_ref[...] = (_head_rms(p[:, OFF_NA_Q:OFF_NA_K], gnq_ref[...]) * scale).astype(BF16)
    nak_ref[...] = _head_rms(p[:, OFF_NA_K:OFF_NA_V], gnk_ref[...]).astype(BF16)
    nav_ref[...] = p[:, OFF_NA_V:OFF_CV_X].astype(BF16)
    cvu_ref[...] = p[:, OFF_CV_C:OFF_WA_Q] * p[:, OFF_CV_X:OFF_CV_B]
    cvb_ref[...] = p[:, OFF_CV_B:OFF_CV_C]
    wq = _head_rms(p[:, OFF_WA_Q:OFF_WA_K], gwq_ref[...])
    wk = _head_rms(p[:, OFF_WA_K:OFF_WA_V], gwk_ref[...])
    if rope:
        wq = _rope(wq, cos_ref[...], sin_ref[...])
        wk = _rope(wk, cos_ref[...], sin_ref[...])
    waq_ref[...] = (wq * scale).astype(BF16)
    wak_ref[...] = _dup_kv(wk).astype(BF16)
    wav_ref[...] = _dup_kv(p[:, OFF_WA_V:IN_WIDTH]).astype(BF16)


def _in_proj(x2d, mod_l, mod_row, g1, w_in, gains, rope_tabs, tm):
    rows = x2d.shape[0]
    rope = rope_tabs is not None
    row_spec = lambda w: pl.BlockSpec((tm, w), lambda i: (i, 0))
    const = lambda shape: pl.BlockSpec(shape, lambda i: (0,) * len(shape))
    in_specs = [
        row_spec(D_MODEL),
        pl.BlockSpec((None, 6, 1, D_MODEL), lambda i: (mod_row(i), 0, 0, 0)),
        const((1, D_MODEL)),
        const((D_MODEL, IN_WIDTH)),
        const((1, NA_WIDTH)), const((1, NA_WIDTH)), const((1, WA_QW)), const((1, WA_KVW)),
    ]
    args = [x2d, mod_l, g1, w_in, *gains]
    if rope:
        seq_tiles = rope_tabs[0].shape[0] // tm
        in_specs += [pl.BlockSpec((tm, LANES), lambda i: (i % seq_tiles, 0))] * 2
        args += list(rope_tabs)
    widths = [(NA_WIDTH, BF16), (NA_WIDTH, BF16), (NA_WIDTH, BF16), (CONV_CH, F32), (CONV_CH, F32),
              (WA_QW, BF16), (2 * WA_KVW, BF16), (2 * WA_KVW, BF16)]
    return pl.pallas_call(
        functools.partial(_inproj_kernel, rope=rope),
        out_shape=[jax.ShapeDtypeStruct((rows, w), dt) for w, dt in widths],
        grid=(rows // tm,),
        in_specs=in_specs,
        out_specs=[row_spec(w) for w, _ in widths],
        compiler_params=_params(),
        name="in_proj_rope" if rope else "in_proj_ctx",
    )(*args)


def _softmax_pv(parts, extra=None):
    m = None
    for s, _ in parts:
        sm = jnp.max(s, axis=-1, keepdims=True)
        m = sm if m is None else jnp.maximum(m, sm)
    if extra is not None:
        m = jnp.maximum(m, extra)
    denom = None if extra is None else jnp.exp(extra - m)
    acc = None
    for s, v in parts:
        p = jnp.exp(s - m)
        ps = jnp.sum(p, axis=-1, keepdims=True)
        denom = ps if denom is None else denom + ps
        pv = _dot(p.astype(BF16), v)
        acc = pv if acc is None else acc + pv
    return acc / denom


def _stack_heads(q, n_heads):
    head = _iota(q.shape, 1) // HEAD_DIM
    zero = jnp.zeros_like(q)
    return jnp.concatenate([jnp.where(head == h, q, zero) for h in range(n_heads)], axis=0)


def _unstack_heads(o, n_heads):
    rows = o.shape[0] // n_heads
    head = _iota((rows, o.shape[1]), 1) // HEAD_DIM
    out = jnp.where(head == 0, o[:rows], 0.0)
    for h in range(1, n_heads):
        out = jnp.where(head == h, o[h * rows:(h + 1) * rows], out)
    return out


def _group_rms(o, gain):
    ms = jnp.mean(o * o, axis=-1, keepdims=True)
    return o * lax.rsqrt(ms + EPS) * gain


def _wa_group(qg, keys_values, sinks):
    rows = qg.shape[0]
    left = _iota((rows, LANES), 1) < HEAD_DIM
    zero = jnp.zeros((rows, LANES), qg.dtype)
    blocks = []
    for pair in range(WA_GROUP // 2):
        qp = qg[:, pair * LANES:(pair + 1) * LANES]
        blocks += [jnp.where(left, qp, zero), jnp.where(left, zero, qp)]
    qs = jnp.concatenate(blocks, axis=0)
    parts = []
    for k2, v2, bias in keys_values:
        s = _dot_nt(qs, k2)
        if bias is not None:
            s = jnp.concatenate([s[g * rows:(g + 1) * rows] + bias for g in range(WA_GROUP)], axis=0)
        parts.append((s, v2))
    sink_col = jnp.concatenate([jnp.full((rows, 1), sinks[g], F32) for g in range(WA_GROUP)], axis=0)
    o2 = _softmax_pv(parts, extra=sink_col)
    pairs = [jnp.where(left, o2[(2 * pr) * rows:(2 * pr + 1) * rows], o2[(2 * pr + 1) * rows:(2 * pr + 2) * rows])
             for pr in range(WA_GROUP // 2)]
    return jnp.concatenate(pairs, axis=1)


def _na_kernel(q_ref, k_ref, v_ref, kc_ref, vc_ref, bias_ref, g_ref, o_ref):
    n_rows = q_ref.shape[0] // GRID_W
    win = NA_WIN_ROWS * GRID_W
    kc = kc_ref[...]
    vc = vc_ref[...]
    gain = g_ref[...]

    def body(r, carry):
        start = jnp.clip(r - NA_WIN_ROWS // 2, 0, n_rows - NA_WIN_ROWS)
        d0 = start - r + (NA_WIN_ROWS - 1)
        q = q_ref[pl.ds(pl.multiple_of(r * GRID_W, GRID_W), GRID_W), :]
        qs = _stack_heads(q, NA_HEADS)
        tok0 = pl.multiple_of(start * GRID_W, GRID_W)
        kw = k_ref[pl.ds(tok0, win), :]
        vw = v_ref[pl.ds(tok0, win), :]
        bias = jnp.concatenate([bias_ref[d0 + 2 * j] for j in range(NA_WIN_ROWS // 2)], axis=1)
        s_loc = _dot_nt(qs, kw) + bias
        s_ctx = _dot_nt(qs, kc)
        o4 = _softmax_pv([(s_ctx, vc), (s_loc, vw)])
        o = _unstack_heads(o4, NA_HEADS)
        o_ref[pl.ds(pl.multiple_of(r * GRID_W, GRID_W), GRID_W), :] = _group_rms(o, gain).astype(o_ref.dtype)
        return carry

    lax.fori_loop(0, n_rows, body, 0)


def _na_attention(q, k, v, kc, vc, bias, gain):
    batch, seq, _ = q.shape
    lc = kc.shape[1]
    per_b = lambda n: pl.BlockSpec((None, n, NA_WIDTH), lambda b: (b, 0, 0))
    return pl.pallas_call(
        _na_kernel,
        out_shape=jax.ShapeDtypeStruct((batch, seq, NA_WIDTH), BF16),
        grid=(batch,),
        in_specs=[per_b(seq), per_b(seq), per_b(seq), per_b(lc), per_b(lc),
                  pl.BlockSpec(bias.shape, lambda b: (0, 0, 0)),
                  pl.BlockSpec((1, NA_WIDTH), lambda b: (0, 0))],
        out_specs=per_b(seq),
        compiler_params=_params(),
        name="na_attention",
    )(q, k, v, kc, vc, bias, gain)


def _wa_kernel(sink_ref, q_ref, k_ref, v_ref, kc_ref, vc_ref, g_ref, o_ref):
    seq = q_ref.shape[0]
    gain = g_ref[...]
    row = _iota((WA_BLOCK, WA_BAND), 0)
    col = _iota((WA_BLOCK, WA_BAND), 1)

    def body(n, carry):
        q0 = pl.multiple_of(n * WA_BLOCK, WA_BLOCK)
        k0 = pl.multiple_of(jnp.clip(q0 - WA_BLOCK, 0, seq - WA_BAND), WA_BLOCK)
        delta = (k0 + col) - (q0 + row)
        band = jnp.where(jnp.abs(delta) <= WA_WINDOW, 0.0, NEG_INF)
        outs = []
        for kh in range(WA_KV_HEADS):
            lanes = slice(kh * LANES, (kh + 1) * LANES)
            qg = q_ref[pl.ds(q0, WA_BLOCK), kh * WA_GROUP * HEAD_DIM:(kh + 1) * WA_GROUP * HEAD_DIM]
            kvs = [(kc_ref[:, lanes], vc_ref[:, lanes], None),
                   (k_ref[pl.ds(k0, WA_BAND), lanes], v_ref[pl.ds(k0, WA_BAND), lanes], band)]
            sinks = [sink_ref[kh * WA_GROUP + g] for g in range(WA_GROUP)]
            outs.append(_wa_group(qg, kvs, sinks))
        o = jnp.concatenate(outs, axis=1)
        o_ref[pl.ds(q0, WA_BLOCK), :] = _group_rms(o, gain).astype(o_ref.dtype)
        return carry

    lax.fori_loop(0, seq // WA_BLOCK, body, 0)


def _wa_attention(sink, q, k2, v2, kc2, vc2, gain):
    batch, seq, _ = q.shape
    lc = kc2.shape[1]
    per_b = lambda n, w: pl.BlockSpec((None, n, w), lambda b: (b, 0, 0))
    return pl.pallas_call(
        _wa_kernel,
        out_shape=jax.ShapeDtypeStruct((batch, seq, WA_QW), BF16),
        grid=(batch,),
        in_specs=[pl.BlockSpec(memory_space=pltpu.SMEM),
                  per_b(seq, WA_QW), per_b(seq, 2 * WA_KVW), per_b(seq, 2 * WA_KVW),
                  per_b(lc, 2 * WA_KVW), per_b(lc, 2 * WA_KVW),
                  pl.BlockSpec((1, WA_QW), lambda b: (0, 0))],
        out_specs=per_b(seq, WA_QW),
        compiler_params=_params(),
        name="wa_attention",
    )(sink, q, k2, v2, kc2, vc2, gain)


def _ctx_attn_kernel(sink_ref, qn_ref, kn_ref, vn_ref, qw_ref, kw_ref, vw_ref, gn_ref, gw_ref, on_ref, ow_ref):
    qs = _stack_heads(qn_ref[...], NA_HEADS)
    o4 = _softmax_pv([(_dot_nt(qs, kn_ref[...]), vn_ref[...])])
    on_ref[...] = _group_rms(_unstack_heads(o4, NA_HEADS), gn_ref[...]).astype(on_ref.dtype)
    outs = []
    for kh in range(WA_KV_HEADS):
        lanes = slice(kh * LANES, (kh + 1) * LANES)
        qg = qw_ref[:, kh * WA_GROUP * HEAD_DIM:(kh + 1) * WA_GROUP * HEAD_DIM]
        sinks = [sink_ref[kh * WA_GROUP + g] for g in range(WA_GROUP)]
        outs.append(_wa_group(qg, [(kw_ref[:, lanes], vw_ref[:, lanes], None)], sinks))
    ow_ref[...] = _group_rms(jnp.concatenate(outs, axis=1), gw_ref[...]).astype(ow_ref.dtype)


def _ctx_attention(sink, qn, kn, vn, qw, kw2, vw2, gain_na, gain_wa):
    batch, lc, _ = qn.shape
    per_b = lambda w: pl.BlockSpec((None, lc, w), lambda b: (b, 0, 0))
    return pl.pallas_call(
        _ctx_attn_kernel,
        out_shape=[jax.ShapeDtypeStruct((batch, lc, NA_WIDTH), BF16),
                   jax.ShapeDtypeStruct((batch, lc, WA_QW), BF16)],
        grid=(batch,),
        in_specs=[pl.BlockSpec(memory_space=pltpu.SMEM),
                  per_b(NA_WIDTH), per_b(NA_WIDTH), per_b(NA_WIDTH),
                  per_b(WA_QW), per_b(2 * WA_KVW), per_b(2 * WA_KVW),
                  pl.BlockSpec((1, NA_WIDTH), lambda b: (0, 0)),
                  pl.BlockSpec((1, WA_QW), lambda b: (0, 0))],
        out_specs=[per_b(NA_WIDTH), per_b(WA_QW)],
        compiler_params=_params(),
        name="ctx_attention",
    )(sink, qn, kn, vn, qw, kw2, vw2, gain_na, gain_wa)


def _out_kernel(x_ref, mod_ref, na_ref, wa_ref, u_ref, uprev_ref, unext_ref, cvb_ref, cw_ref, cb_ref, g_ref,
                wo_ref, o_ref, *, seq_len):
    tm = x_ref.shape[0]
    i = pl.program_id(0)
    u = u_ref[...]
    at_start = (i * tm) % seq_len == 0
    at_end = ((i + 1) * tm) % seq_len == 0
    prev_row = jnp.where(at_start, 0.0, uprev_ref[7:8, :])
    next_row = jnp.where(at_end, 0.0, unext_ref[0:1, :])
    row = _iota(u.shape, 0)
    up = jnp.where(row == 0, prev_row, pltpu.roll(u, 1, axis=0))
    dn = jnp.where(row == tm - 1, next_row, pltpu.roll(u, tm - 1, axis=0))
    y = cb_ref[...] + cw_ref[0:1, :] * up + cw_ref[1:2, :] * u + cw_ref[2:3, :] * dn
    cv = _group_rms(cvb_ref[...] * y, g_ref[...]).astype(BF16)
    mixed = jnp.concatenate([na_ref[...], cv, wa_ref[...]], axis=1)
    o_ref[...] = x_ref[...] + mod_ref[2] * _dot(mixed, wo_ref[...])


def _out_proj(x2d, mod_l, mod_row, na_n, wa_n, cv_u, cv_b, conv_w, conv_b, gain_cv, w_o, seq_len, tm):
    rows = x2d.shape[0]
    halo = 8
    per_tile = tm // halo
    last = rows // halo - 1
    row_spec = lambda w: pl.BlockSpec((tm, w), lambda i: (i, 0))
    const = lambda shape: pl.BlockSpec(shape, lambda i: (0,) * len(shape))
    return pl.pallas_call(
        functools.partial(_out_kernel, seq_len=seq_len),
        out_shape=jax.ShapeDtypeStruct((rows, D_MODEL), F32),
        grid=(rows // tm,),
        in_specs=[
            row_spec(D_MODEL),
            pl.BlockSpec((None, 6, 1, D_MODEL), lambda i: (mod_row(i), 0, 0, 0)),
            row_spec(NA_WIDTH), row_spec(WA_QW), row_spec(CONV_CH),
            pl.BlockSpec((halo, CONV_CH), lambda i: (jnp.maximum(i * per_tile - 1, 0), 0)),
            pl.BlockSpec((halo, CONV_CH), lambda i: (jnp.minimum((i + 1) * per_tile, last), 0)),
            row_spec(CONV_CH),
            const((3, CONV_CH)), const((1, CONV_CH)), const((1, CONV_CH)),
            const((D_MODEL, D_MODEL)),
        ],
        out_specs=row_spec(D_MODEL),
        compiler_params=_params(),
        name="out_proj",
    )(x2d, mod_l, na_n, wa_n, cv_u, cv_u, cv_u, cv_b, conv_w, conv_b, gain_cv, w_o)


def _mlp_kernel(x_ref, mod_ref, g2_ref, w1_ref, w2_ref, o_ref):
    x = x_ref[...]
    ms = jnp.mean(x * x, axis=-1, keepdims=True)
    h = x * lax.rsqrt(ms + EPS) * g2_ref[...]
    h = (h * (1.0 + mod_ref[4]) + mod_ref[3]).astype(BF16)
    acc = None
    for c in range(0, FFN_DIM, FFN_CHUNK):
        a = jnp.maximum(_dot(h, w1_ref[:, c:c + FFN_CHUNK]), 0.0)
        part = _dot((a * a).astype(BF16), w2_ref[c:c + FFN_CHUNK, :])
        acc = part if acc is None else acc + part
    o_ref[...] = x + mod_ref[5] * acc


def _mlp(x2d, mod_l, mod_row, g2, w1, w2, tm):
    rows = x2d.shape[0]
    row_spec = pl.BlockSpec((tm, D_MODEL), lambda i: (i, 0))
    return pl.pallas_call(
        _mlp_kernel,
        out_shape=jax.ShapeDtypeStruct((rows, D_MODEL), F32),
        grid=(rows // tm,),
        in_specs=[
            row_spec,
            pl.BlockSpec((None, 6, 1, D_MODEL), lambda i: (mod_row(i), 0, 0, 0)),
            pl.BlockSpec((1, D_MODEL), lambda i: (0, 0)),
            pl.BlockSpec((D_MODEL, FFN_DIM), lambda i: (0, 0)),
            pl.BlockSpec((FFN_DIM, D_MODEL), lambda i: (0, 0)),
        ],
        out_specs=row_spec,
        compiler_params=_params(),
        name="mlp",
    )(x2d, mod_l, g2, w1, w2)


def _rope_tables(seq):
    quarter = HEAD_DIM // 4
    inv = ROPE_BASE ** (-jnp.arange(quarter, dtype=F32) / quarter)
    t = jnp.arange(seq)
    ang_r = (t // GRID_W).astype(F32)[:, None] * inv[None, :]
    ang_c = (t % GRID_W).astype(F32)[:, None] * inv[None, :]
    cos = jnp.concatenate([jnp.cos(ang_r), jnp.cos(ang_r), jnp.cos(ang_c), jnp.cos(ang_c)], axis=1)
    sin = jnp.concatenate([-jnp.sin(ang_r), jnp.sin(ang_r), -jnp.sin(ang_c), jnp.sin(ang_c)], axis=1)
    reps = LANES // HEAD_DIM
    return jnp.tile(cos, (1, reps)), jnp.tile(sin, (1, reps))


def kernel(x, c, ctx, c_ctx, w_mod, b_mod, g_norm1, g_norm2, w_in, na_q_gain, na_k_gain, na_rpb,
           conv_w, conv_bias, wa_q_gain, wa_k_gain, wa_sink, g_out, w_o, w_fc1, w_fc2):
    batch, seq, d = x.shape
    lc = ctx.shape[1]
    depth = w_mod.shape[0]
    assert d == D_MODEL and seq % WA_BLOCK == 0 and seq % GRID_W == 0 and batch < MOD_ROWS
    tm = 512
    tm_ctx = 256
    assert seq % tm == 0 and lc % tm_ctx == 0

    cond = jnp.zeros((MOD_ROWS, d), F32).at[:batch].set(c).at[batch].set(c_ctx)
    mod = _modulation(cond, w_mod, b_mod).reshape(depth, MOD_ROWS, 6, 1, d)
    rpb_bias = _rpb_tables(na_rpb)
    rope_tabs = _rope_tables(seq)

    lat_row = lambda i: i // (seq // tm)
    ctx_row = lambda i: batch

    xs = x.reshape(batch * seq, d)
    cs = ctx.reshape(batch * lc, d)
    for l in range(depth):
        last = l == depth - 1
        w_in_l = w_in[l].astype(BF16)
        w_o_l = w_o[l].astype(BF16)
        w1_l = w_fc1[l].astype(BF16)
        w2_l = w_fc2[l].astype(BF16)
        g1 = g_norm1[l].reshape(1, d)
        g2 = g_norm2[l].reshape(1, d)
        gains = [jnp.tile(na_q_gain[l], NA_HEADS).reshape(1, -1), jnp.tile(na_k_gain[l], NA_HEADS).reshape(1, -1),
                 jnp.tile(wa_q_gain[l], WA_HEADS).reshape(1, -1), jnp.tile(wa_k_gain[l], WA_KV_HEADS).reshape(1, -1)]
        go_na = g_out[l, :NA_WIDTH].reshape(1, -1)
        go_cv = g_out[l, NA_WIDTH:NA_WIDTH + CONV_CH].reshape(1, -1)
        go_wa = g_out[l, NA_WIDTH + CONV_CH:].reshape(1, -1)
        cb = conv_bias[l].reshape(1, -1)

        naq, nak, nav, cvu, cvb, waq, wak, wav = _in_proj(xs, mod[l], lat_row, g1, w_in_l, gains, rope_tabs, tm)
        cnaq, cnak, cnav, ccvu, ccvb, cwaq, cwak, cwav = _in_proj(cs, mod[l], ctx_row, g1, w_in_l, gains, None, tm_ctx)

        b3 = lambda a, n: a.reshape(batch, n, a.shape[-1])
        na_n = _na_attention(b3(naq, seq), b3(nak, seq), b3(nav, seq), b3(cnak, lc), b3(cnav, lc), rpb_bias[l], go_na)
        wa_n = _wa_attention(wa_sink[l], b3(waq, seq), b3(wak, seq), b3(wav, seq), b3(cwak, lc), b3(cwav, lc), go_wa)
        xs = _out_proj(xs, mod[l], lat_row, na_n.reshape(batch * seq, -1), wa_n.reshape(batch * seq, -1),
                       cvu, cvb, conv_w[l], cb, go_cv, w_o_l, seq, tm)
        if not last:
            cna_n, cwa_n = _ctx_attention(wa_sink[l], b3(cnaq, lc), b3(cnak, lc), b3(cnav, lc),
                                          b3(cwaq, lc), b3(cwak, lc), b3(cwav, lc), go_na, go_wa)
            cs = _out_proj(cs, mod[l], ctx_row, cna_n.reshape(batch * lc, -1), cwa_n.reshape(batch * lc, -1),
                           ccvu, ccvb, conv_w[l], cb, go_cv, w_o_l, lc, tm_ctx)
        xs = _mlp(xs, mod[l], lat_row, g2, w1_l, w2_l, tm)
        if not last:
            cs = _mlp(cs, mod[l], ctx_row, g2, w1_l, w2_l, tm_ctx)
    return xs.reshape(batch, seq, d)
```

```python
import functools

import jax
import jax.numpy as jnp
from jax import lax
from jax.experimental import pallas as pl
from jax.experimental.pallas import tpu as pltpu

D_MODEL = 1024
GRID_W = 64
HEAD_DIM = 64
NA_HEADS = 4
NA_WIDTH = NA_HEADS * HEAD_DIM
CONV_CH = 256
WA_HEADS = 8
WA_KV_HEADS = 2
WA_GROUP = WA_HEADS // WA_KV_HEADS
WA_QW = WA_HEADS * HEAD_DIM
WA_KVW = WA_KV_HEADS * HEAD_DIM
NA_WIN_ROWS = 8
NA_WIN_COLS = 16
NA_DROWS = 2 * NA_WIN_ROWS - 1
NA_DCOLS = 2 * NA_WIN_COLS - 1
WA_WINDOW = 128
WA_BLOCK = 128
WA_BAND = 3 * WA_BLOCK
FFN_DIM = 4 * D_MODEL
FFN_CHUNK = 1024
ROPE_BASE = 10000.0
EPS = 1e-6
NEG_INF = -1e30
IN_WIDTH = 2304
OFF_NA_Q, OFF_NA_K, OFF_NA_V = 0, 256, 512
OFF_CV_X, OFF_CV_B, OFF_CV_C = 768, 1024, 1280
OFF_WA_Q, OFF_WA_K, OFF_WA_V = 1536, 2048, 2176
MOD_ROWS = 16
MOD_TN = 1024
LANES = 128
MXU_DIM = 256
VMEM_LIMIT = 48 * 1024 * 1024

F32 = jnp.float32
BF16 = jnp.bfloat16


def _dot(a, b):
    return jnp.dot(a, b, preferred_element_type=F32)


def _dot_nt(a, b):
    return lax.dot_general(a, b, (((1,), (1,)), ((), ())), preferred_element_type=F32)


def _split_bf16(a):
    hi = a.astype(BF16)
    lo = (a - hi.astype(F32)).astype(BF16)
    return hi, lo


def _iota(shape, dim):
    return lax.broadcasted_iota(jnp.int32, shape, dim)


def _params(**kw):
    return pltpu.CompilerParams(vmem_limit_bytes=VMEM_LIMIT, **kw)


def _mod_kernel(cond_ref, w_ref, b_ref, o_ref):
    a = cond_ref[...]
    a = a * (1.0 / (1.0 + jnp.exp(-a)))
    ah, al = _split_bf16(a)
    wh, wl = _split_bf16(w_ref[...])
    o_ref[...] = _dot(ah, wh) + _dot(al, wh) + _dot(ah, wl) + b_ref[...]


def _modulation(cond, w_mod, b_mod):
    depth = w_mod.shape[0]
    n_out = w_mod.shape[2]
    return pl.pallas_call(
        _mod_kernel,
        out_shape=jax.ShapeDtypeStruct((depth, MOD_ROWS, n_out), F32),
        grid=(depth, n_out // MOD_TN),
        in_specs=[
            pl.BlockSpec((MOD_ROWS, D_MODEL), lambda l, j: (0, 0)),
            pl.BlockSpec((None, D_MODEL, MOD_TN), lambda l, j: (l, 0, j)),
            pl.BlockSpec((None, 1, MOD_TN), lambda l, j: (l, 0, j)),
        ],
        out_specs=pl.BlockSpec((None, MOD_ROWS, MOD_TN), lambda l, j: (l, 0, j)),
        compiler_params=_params(),
        name="modulation",
    )(cond, w_mod, b_mod.reshape(depth, 1, n_out))


def _rpb_kernel(rpb_ref, o_ref):
    l = pl.program_id(0)
    shape = (GRID_W, LANES)
    q = _iota(shape, 0)
    lane = _iota(shape, 1)
    k = lane % GRID_W
    left = lane < GRID_W
    dc = jnp.clip(k - q, -(NA_WIN_COLS - 1), NA_WIN_COLS - 1) + (NA_WIN_COLS - 1)
    col_start = jnp.clip(q - NA_WIN_COLS // 2, 0, GRID_W - NA_WIN_COLS)
    col_ok = (k >= col_start) & (k < col_start + NA_WIN_COLS)
    base_l = l * (NA_HEADS * NA_DROWS * NA_DCOLS)

    def body(d, carry):
        for h in range(NA_HEADS):
            base = base_l + (h * NA_DROWS + d) * NA_DCOLS
            t = jnp.zeros(shape, F32)
            for i in range(NA_DCOLS):
                val = jnp.where(left, rpb_ref[base + i], rpb_ref[base + NA_DCOLS + i])
                t = jnp.where(dc == i, val, t)
            o_ref[d, h * GRID_W:(h + 1) * GRID_W, :] = jnp.where(col_ok, t, NEG_INF)
        return carry

    lax.fori_loop(0, NA_DROWS - 1, body, 0)


def _rpb_tables(na_rpb):
    depth = na_rpb.shape[0]
    return pl.pallas_call(
        _rpb_kernel,
        out_shape=jax.ShapeDtypeStruct((depth, NA_DROWS - 1, NA_HEADS * GRID_W, LANES), F32),
        grid=(depth,),
        in_specs=[pl.BlockSpec(memory_space=pltpu.SMEM)],
        out_specs=pl.BlockSpec((None, NA_DROWS - 1, NA_HEADS * GRID_W, LANES), lambda l: (l, 0, 0, 0)),
        compiler_params=_params(),
        name="rpb_tables",
    )(na_rpb.reshape(-1))


def _head_rms(t, gain):
    width = t.shape[1]
    sq = t * t
    hi, lo = _split_bf16(sq)
    blk = min(width, MXU_DIM)
    ones = jnp.where(_iota((blk, blk), 0) // HEAD_DIM == _iota((blk, blk), 1) // HEAD_DIM, 1.0, 0.0).astype(BF16)
    sums = [_dot(hi[:, c:c + blk], ones) + _dot(lo[:, c:c + blk], ones) for c in range(0, width, blk)]
    ss = sums[0] if len(sums) == 1 else jnp.concatenate(sums, axis=1)
    return t * lax.rsqrt(ss * (1.0 / HEAD_DIM) + EPS) * gain


def _rope(t, cos, sin):
    lane = _iota((t.shape[0], LANES), 1)
    first = (lane % (HEAD_DIM // 2)) < (HEAD_DIM // 4)
    out = []
    for c in range(0, t.shape[1], LANES):
        u = t[:, c:c + LANES]
        partner = jnp.where(first, pltpu.roll(u, LANES - HEAD_DIM // 4, axis=1), pltpu.roll(u, HEAD_DIM // 4, axis=1))
        out.append(u * cos + partner * sin)
    return out[0] if len(out) == 1 else jnp.concatenate(out, axis=1)


def _dup_kv(t):
    lane = _iota(t.shape, 1)
    swapped = pltpu.roll(t, HEAD_DIM, axis=1)
    left = lane < HEAD_DIM
    return jnp.concatenate([jnp.where(left, t, swapped), jnp.where(left, swapped, t)], axis=1)


def _inproj_kernel(x_ref, mod_ref, g1_ref, w_ref, gnq_ref, gnk_ref, gwq_ref, gwk_ref, *rest, rope):
    if rope:
        cos_ref, sin_ref = rest[:2]
        rest = rest[2:]
    naq_ref, nak_ref, nav_ref, cvu_ref, cvb_ref, waq_ref, wak_ref, wav_ref = rest
    x = x_ref[...]
    ms = jnp.mean(x * x, axis=-1, keepdims=True)
    h = x * lax.rsqrt(ms + EPS) * g1_ref[...]
    h = h * (1.0 + mod_ref[1]) + mod_ref[0]
    p = _dot(h.astype(BF16), w_ref[...])
    scale = HEAD_DIM ** -0.5
    naq_ref[...] = (_head_rms(p[:, OFF_NA_Q:OFF_NA_K], gnq_ref[...]) * scale).astype(BF16)
    nak_ref[...] = _head_rms(p[:, OFF_NA_K:OFF_NA_V], gnk_ref[...]).astype(BF16)
    nav_ref[...] = p[:, OFF_NA_V:OFF_CV_X].astype(BF16)
    cvu_ref[...] = p[:, OFF_CV_C:OFF_WA_Q] * p[:, OFF_CV_X:OFF_CV_B]
    cvb_ref[...] = p[:, OFF_CV_B:OFF_CV_C]
    wq = _head_rms(p[:, OFF_WA_Q:OFF_WA_K], gwq_ref[...])
    wk = _head_rms(p[:, OFF_WA_K:OFF_WA_V], gwk_ref[...])
    if rope:
        wq = _rope(wq, cos_ref[...], sin_ref[...])
        wk = _rope(wk, cos_ref[...], sin_ref[...])
    waq_ref[...] = (wq * scale).astype(BF16)
    wak_ref[...] = _dup_kv(wk).astype(BF16)
    wav_ref[...] = _dup_kv(p[:, OFF_WA_V:IN_WIDTH]).astype(BF16)


def _in_proj(x2d, mod_l, mod_row, g1, w_in, gains, rope_tabs, tm):
    rows = x2d.shape[0]
    rope = rope_tabs is not None
    row_spec = lambda w: pl.BlockSpec((tm, w), lambda i: (i, 0))
    const = lambda shape: pl.BlockSpec(shape, lambda i: (0,) * len(shape))
    in_specs = [
        row_spec(D_MODEL),
        pl.BlockSpec((None, 6, 1, D_MODEL), lambda i: (mod_row(i), 0, 0, 0)),
        const((1, D_MODEL)),
        const((D_MODEL, IN_WIDTH)),
        const((1, NA_WIDTH)), const((1, NA_WIDTH)), const((1, WA_QW)), const((1, WA_KVW)),
    ]
    args = [x2d, mod_l, g1, w_in, *gains]
    if rope:
        seq_tiles = rope_tabs[0].shape[0] // tm
        in_specs += [pl.BlockSpec((tm, LANES), lambda i: (i % seq_tiles, 0))] * 2
        args += list(rope_tabs)
    widths = [(NA_WIDTH, BF16), (NA_WIDTH, BF16), (NA_WIDTH, BF16), (CONV_CH, F32), (CONV_CH, F32),
              (WA_QW, BF16), (2 * WA_KVW, BF16), (2 * WA_KVW, BF16)]
    return pl.pallas_call(
        functools.partial(_inproj_kernel, rope=rope),
        out_shape=[jax.ShapeDtypeStruct((rows, w), dt) for w, dt in widths],
        grid=(rows // tm,),
        in_specs=in_specs,
        out_specs=[row_spec(w) for w, _ in widths],
        compiler_params=_params(),
        name="in_proj_rope" if rope else "in_proj_ctx",
    )(*args)


def _softmax_pv(parts, extra=None):
    lane_chunks = lambda s: [s[:, c:c + LANES] for c in range(0, s.shape[1], LANES)]
    chunks = [ch for s, _ in parts for ch in lane_chunks(s)]
    if extra is not None:
        chunks.append(extra)
    m = jnp.max(functools.reduce(jnp.maximum, chunks), axis=-1, keepdims=True)
    acc = None
    psum = None
    if extra is not None:
        psum = jnp.where(_iota(extra.shape, 1) == 0, jnp.exp(extra - m), 0.0)
    for s, v in parts:
        p = jnp.exp(s - m)
        for ch in lane_chunks(p):
            psum = ch if psum is None else psum + ch
        pv = _dot(p.astype(BF16), v)
        acc = pv if acc is None else acc + pv
    return acc / jnp.sum(psum, axis=-1, keepdims=True)


def _stack_heads(q, n_heads):
    head = _iota(q.shape, 1) // HEAD_DIM
    zero = jnp.zeros_like(q)
    return jnp.concatenate([jnp.where(head == h, q, zero) for h in range(n_heads)], axis=0)


def _unstack_heads(o, n_heads):
    rows = o.shape[0] // n_heads
    head = _iota((rows, o.shape[1]), 1) // HEAD_DIM
    out = jnp.where(head == 0, o[:rows], 0.0)
    for h in range(1, n_heads):
        out = jnp.where(head == h, o[h * rows:(h + 1) * rows], out)
    return out


def _group_rms(o, gain):
    ms = jnp.mean(o * o, axis=-1, keepdims=True)
    return o * lax.rsqrt(ms + EPS) * gain


def _wa_group(qg, keys_values, sinks):
    rows = qg.shape[0]
    left = _iota((rows, LANES), 1) < HEAD_DIM
    zero = jnp.zeros((rows, LANES), qg.dtype)
    blocks = []
    for pair in range(WA_GROUP // 2):
        qp = qg[:, pair * LANES:(pair + 1) * LANES]
        blocks += [jnp.where(left, qp, zero), jnp.where(left, zero, qp)]
    qs = jnp.concatenate(blocks, axis=0)
    parts = []
    for k2, v2, bias in keys_values:
        s = _dot_nt(qs, k2)
        if bias is not None:
            s = jnp.concatenate([s[g * rows:(g + 1) * rows] + bias for g in range(WA_GROUP)], axis=0)
        parts.append((s, v2))
    sink_col = jnp.concatenate([jnp.full((rows, LANES), sinks[g], F32) for g in range(WA_GROUP)], axis=0)
    o2 = _softmax_pv(parts, extra=sink_col)
    pairs = [jnp.where(left, o2[(2 * pr) * rows:(2 * pr + 1) * rows], o2[(2 * pr + 1) * rows:(2 * pr + 2) * rows])
             for pr in range(WA_GROUP // 2)]
    return jnp.concatenate(pairs, axis=1)


def _na_kernel(q_ref, k_ref, v_ref, kc_ref, vc_ref, bias_ref, g_ref, o_ref):
    n_rows = q_ref.shape[0] // GRID_W
    win = NA_WIN_ROWS * GRID_W
    kc = kc_ref[...]
    vc = vc_ref[...]
    gain = g_ref[...]

    def body(r, carry):
        start = jnp.clip(r - NA_WIN_ROWS // 2, 0, n_rows - NA_WIN_ROWS)
        d0 = start - r + (NA_WIN_ROWS - 1)
        q = q_ref[pl.ds(pl.multiple_of(r * GRID_W, GRID_W), GRID_W), :]
        qs = _stack_heads(q, NA_HEADS)
        tok0 = pl.multiple_of(start * GRID_W, GRID_W)
        kw = k_ref[pl.ds(tok0, win), :]
        vw = v_ref[pl.ds(tok0, win), :]
        bias = jnp.concatenate([bias_ref[d0 + 2 * j] for j in range(NA_WIN_ROWS // 2)], axis=1)
        s_loc = _dot_nt(qs, kw) + bias
        s_ctx = _dot_nt(qs, kc)
        o4 = _softmax_pv([(s_ctx, vc), (s_loc, vw)])
        o = _unstack_heads(o4, NA_HEADS)
        o_ref[pl.ds(pl.multiple_of(r * GRID_W, GRID_W), GRID_W), :] = _group_rms(o, gain).astype(o_ref.dtype)
        return carry

    lax.fori_loop(0, n_rows, body, 0, unroll=4)


def _na_attention(q, k, v, kc, vc, bias, gain):
    batch, seq, _ = q.shape
    lc = kc.shape[1]
    per_b = lambda n: pl.BlockSpec((None, n, NA_WIDTH), lambda b: (b, 0, 0))
    return pl.pallas_call(
        _na_kernel,
        out_shape=jax.ShapeDtypeStruct((batch, seq, NA_WIDTH), BF16),
        grid=(batch,),
        in_specs=[per_b(seq), per_b(seq), per_b(seq), per_b(lc), per_b(lc),
                  pl.BlockSpec(bias.shape, lambda b: (0, 0, 0)),
                  pl.BlockSpec((1, NA_WIDTH), lambda b: (0, 0))],
        out_specs=per_b(seq),
        compiler_params=_params(),
        name="na_attention",
    )(q, k, v, kc, vc, bias, gain)


def _wa_kernel(sink_ref, q_ref, k_ref, v_ref, kc_ref, vc_ref, g_ref, o_ref):
    seq = q_ref.shape[0]
    gain = g_ref[...]
    row = _iota((WA_BLOCK, WA_BAND), 0)
    col = _iota((WA_BLOCK, WA_BAND), 1)

    def body(n, carry):
        q0 = pl.multiple_of(n * WA_BLOCK, WA_BLOCK)
        k0 = pl.multiple_of(jnp.clip(q0 - WA_BLOCK, 0, seq - WA_BAND), WA_BLOCK)
        delta = (k0 + col) - (q0 + row)
        band = jnp.where(jnp.abs(delta) <= WA_WINDOW, 0.0, NEG_INF)
        outs = []
        for kh in range(WA_KV_HEADS):
            lanes = slice(kh * LANES, (kh + 1) * LANES)
            qg = q_ref[pl.ds(q0, WA_BLOCK), kh * WA_GROUP * HEAD_DIM:(kh + 1) * WA_GROUP * HEAD_DIM]
            kvs = [(kc_ref[:, lanes], vc_ref[:, lanes], None),
                   (k_ref[pl.ds(k0, WA_BAND), lanes], v_ref[pl.ds(k0, WA_BAND), lanes], band)]
            sinks = [sink_ref[kh * WA_GROUP + g] for g in range(WA_GROUP)]
            outs.append(_wa_group(qg, kvs, sinks))
        o = jnp.concatenate(outs, axis=1)
        o_ref[pl.ds(q0, WA_BLOCK), :] = _group_rms(o, gain).astype(o_ref.dtype)
        return carry

    lax.fori_loop(0, seq // WA_BLOCK, body, 0, unroll=2)


def _wa_attention(sink, q, k2, v2, kc2, vc2, gain):
    batch, seq, _ = q.shape
    lc = kc2.shape[1]
    per_b = lambda n, w: pl.BlockSpec((None, n, w), lambda b: (b, 0, 0))
    return pl.pallas_call(
        _wa_kernel,
        out_shape=jax.ShapeDtypeStruct((batch, seq, WA_QW), BF16),
        grid=(batch,),
        in_specs=[pl.BlockSpec(memory_space=pltpu.SMEM),
                  per_b(seq, WA_QW), per_b(seq, 2 * WA_KVW), per_b(seq, 2 * WA_KVW),
                  per_b(lc, 2 * WA_KVW), per_b(lc, 2 * WA_KVW),
                  pl.BlockSpec((1, WA_QW), lambda b: (0, 0))],
        out_specs=per_b(seq, WA_QW),
        compiler_params=_params(),
        name="wa_attention",
    )(sink, q, k2, v2, kc2, vc2, gain)


def _ctx_attn_kernel(sink_ref, qn_ref, kn_ref, vn_ref, qw_ref, kw_ref, vw_ref, gn_ref, gw_ref, on_ref, ow_ref):
    qs = _stack_heads(qn_ref[...], NA_HEADS)
    o4 = _softmax_pv([(_dot_nt(qs, kn_ref[...]), vn_ref[...])])
    on_ref[...] = _group_rms(_unstack_heads(o4, NA_HEADS), gn_ref[...]).astype(on_ref.dtype)
    outs = []
    for kh in range(WA_KV_HEADS):
        lanes = slice(kh * LANES, (kh + 1) * LANES)
        qg = qw_ref[:, kh * WA_GROUP * HEAD_DIM:(kh + 1) * WA_GROUP * HEAD_DIM]
        sinks = [sink_ref[kh * WA_GROUP + g] for g in range(WA_GROUP)]
        outs.append(_wa_group(qg, [(kw_ref[:, lanes], vw_ref[:, lanes], None)], sinks))
    ow_ref[...] = _group_rms(jnp.concatenate(outs, axis=1), gw_ref[...]).astype(ow_ref.dtype)


def _ctx_attention(sink, qn, kn, vn, qw, kw2, vw2, gain_na, gain_wa):
    batch, lc, _ = qn.shape
    per_b = lambda w: pl.BlockSpec((None, lc, w), lambda b: (b, 0, 0))
    return pl.pallas_call(
        _ctx_attn_kernel,
        out_shape=[jax.ShapeDtypeStruct((batch, lc, NA_WIDTH), BF16),
                   jax.ShapeDtypeStruct((batch, lc, WA_QW), BF16)],
        grid=(batch,),
        in_specs=[pl.BlockSpec(memory_space=pltpu.SMEM),
                  per_b(NA_WIDTH), per_b(NA_WIDTH), per_b(NA_WIDTH),
                  per_b(WA_QW), per_b(2 * WA_KVW), per_b(2 * WA_KVW),
                  pl.BlockSpec((1, NA_WIDTH), lambda b: (0, 0)),
                  pl.BlockSpec((1, WA_QW), lambda b: (0, 0))],
        out_specs=[per_b(NA_WIDTH), per_b(WA_QW)],
        compiler_params=_params(),
        name="ctx_attention",
    )(sink, qn, kn, vn, qw, kw2, vw2, gain_na, gain_wa)


def _out_kernel(x_ref, mod_ref, na_ref, wa_ref, u_ref, uprev_ref, unext_ref, cvb_ref, cw_ref, cb_ref, g_ref,
                wo_ref, o_ref, *, seq_len):
    tm = x_ref.shape[0]
    i = pl.program_id(0)
    u = u_ref[...]
    at_start = (i * tm) % seq_len == 0
    at_end = ((i + 1) * tm) % seq_len == 0
    prev_row = jnp.where(at_start, 0.0, uprev_ref[7:8, :])
    next_row = jnp.where(at_end, 0.0, unext_ref[0:1, :])
    row = _iota(u.shape, 0)
    up = jnp.where(row == 0, prev_row, pltpu.roll(u, 1, axis=0))
    dn = jnp.where(row == tm - 1, next_row, pltpu.roll(u, tm - 1, axis=0))
    y = cb_ref[...] + cw_ref[0:1, :] * up + cw_ref[1:2, :] * u + cw_ref[2:3, :] * dn
    cv = _group_rms(cvb_ref[...] * y, g_ref[...]).astype(BF16)
    mixed = jnp.concatenate([na_ref[...], cv, wa_ref[...]], axis=1)
    o_ref[...] = x_ref[...] + mod_ref[2] * _dot(mixed, wo_ref[...])


def _out_proj(x2d, mod_l, mod_row, na_n, wa_n, cv_u, cv_b, conv_w, conv_b, gain_cv, w_o, seq_len, tm):
    rows = x2d.shape[0]
    halo = 8
    per_tile = tm // halo
    last = rows // halo - 1
    row_spec = lambda w: pl.BlockSpec((tm, w), lambda i: (i, 0))
    const = lambda shape: pl.BlockSpec(shape, lambda i: (0,) * len(shape))
    return pl.pallas_call(
        functools.partial(_out_kernel, seq_len=seq_len),
        out_shape=jax.ShapeDtypeStruct((rows, D_MODEL), F32),
        grid=(rows // tm,),
        in_specs=[
            row_spec(D_MODEL),
            pl.BlockSpec((None, 6, 1, D_MODEL), lambda i: (mod_row(i), 0, 0, 0)),
            row_spec(NA_WIDTH), row_spec(WA_QW), row_spec(CONV_CH),
            pl.BlockSpec((halo, CONV_CH), lambda i: (jnp.maximum(i * per_tile - 1, 0), 0)),
            pl.BlockSpec((halo, CONV_CH), lambda i: (jnp.minimum((i + 1) * per_tile, last), 0)),
            row_spec(CONV_CH),
            const((3, CONV_CH)), const((1, CONV_CH)), const((1, CONV_CH)),
            const((D_MODEL, D_MODEL)),
        ],
        out_specs=row_spec(D_MODEL),
        compiler_params=_params(),
        name="out_proj",
    )(x2d, mod_l, na_n, wa_n, cv_u, cv_u, cv_u, cv_b, conv_w, conv_b, gain_cv, w_o)


def _mlp_kernel(x_ref, mod_ref, g2_ref, w1_ref, w2_ref, o_ref):
    x = x_ref[...]
    ms = jnp.mean(x * x, axis=-1, keepdims=True)
    h = x * lax.rsqrt(ms + EPS) * g2_ref[...]
    h = (h * (1.0 + mod_ref[4]) + mod_ref[3]).astype(BF16)
    acc = None
    for c in range(0, FFN_DIM, FFN_CHUNK):
        a = jnp.maximum(_dot(h, w1_ref[:, c:c + FFN_CHUNK]), 0.0)
        part = _dot((a * a).astype(BF16), w2_ref[c:c + FFN_CHUNK, :])
        acc = part if acc is None else acc + part
    o_ref[...] = x + mod_ref[5] * acc


def _mlp(x2d, mod_l, mod_row, g2, w1, w2, tm):
    rows = x2d.shape[0]
    row_spec = pl.BlockSpec((tm, D_MODEL), lambda i: (i, 0))
    return pl.pallas_call(
        _mlp_kernel,
        out_shape=jax.ShapeDtypeStruct((rows, D_MODEL), F32),
        grid=(rows // tm,),
        in_specs=[
            row_spec,
            pl.BlockSpec((None, 6, 1, D_MODEL), lambda i: (mod_row(i), 0, 0, 0)),
            pl.BlockSpec((1, D_MODEL), lambda i: (0, 0)),
            pl.BlockSpec((D_MODEL, FFN_DIM), lambda i: (0, 0)),
            pl.BlockSpec((FFN_DIM, D_MODEL), lambda i: (0, 0)),
        ],
        out_specs=row_spec,
        compiler_params=_params(),
        name="mlp",
    )(x2d, mod_l, g2, w1, w2)


def _rope_tables(seq):
    quarter = HEAD_DIM // 4
    inv = ROPE_BASE ** (-jnp.arange(quarter, dtype=F32) / quarter)
    t = jnp.arange(seq)
    ang_r = (t // GRID_W).astype(F32)[:, None] * inv[None, :]
    ang_c = (t % GRID_W).astype(F32)[:, None] * inv[None, :]
    cos = jnp.concatenate([jnp.cos(ang_r), jnp.cos(ang_r), jnp.cos(ang_c), jnp.cos(ang_c)], axis=1)
    sin = jnp.concatenate([-jnp.sin(ang_r), jnp.sin(ang_r), -jnp.sin(ang_c), jnp.sin(ang_c)], axis=1)
    reps = LANES // HEAD_DIM
    return jnp.tile(cos, (1, reps)), jnp.tile(sin, (1, reps))


def kernel(x, c, ctx, c_ctx, w_mod, b_mod, g_norm1, g_norm2, w_in, na_q_gain, na_k_gain, na_rpb,
           conv_w, conv_bias, wa_q_gain, wa_k_gain, wa_sink, g_out, w_o, w_fc1, w_fc2):
    batch, seq, d = x.shape
    lc = ctx.shape[1]
    depth = w_mod.shape[0]
    assert d == D_MODEL and seq % WA_BLOCK == 0 and seq % GRID_W == 0 and batch < MOD_ROWS
    tm = 512
    tm_ctx = 256
    assert seq % tm == 0 and lc % tm_ctx == 0

    cond = jnp.zeros((MOD_ROWS, d), F32).at[:batch].set(c).at[batch].set(c_ctx)
    mod = _modulation(cond, w_mod, b_mod).reshape(depth, MOD_ROWS, 6, 1, d)
    rpb_bias = _rpb_tables(na_rpb)
    rope_tabs = _rope_tables(seq)

    lat_row = lambda i: i // (seq // tm)
    ctx_row = lambda i: batch

    xs = x.reshape(batch * seq, d)
    cs = ctx.reshape(batch * lc, d)
    for l in range(depth):
        last = l == depth - 1
        w_in_l = w_in[l].astype(BF16)
        w_o_l = w_o[l].astype(BF16)
        w1_l = w_fc1[l].astype(BF16)
        w2_l = w_fc2[l].astype(BF16)
        g1 = g_norm1[l].reshape(1, d)
        g2 = g_norm2[l].reshape(1, d)
        gains = [jnp.tile(na_q_gain[l], NA_HEADS).reshape(1, -1), jnp.tile(na_k_gain[l], NA_HEADS).reshape(1, -1),
                 jnp.tile(wa_q_gain[l], WA_HEADS).reshape(1, -1), jnp.tile(wa_k_gain[l], WA_KV_HEADS).reshape(1, -1)]
        go_na = g_out[l, :NA_WIDTH].reshape(1, -1)
        go_cv = g_out[l, NA_WIDTH:NA_WIDTH + CONV_CH].reshape(1, -1)
        go_wa = g_out[l, NA_WIDTH + CONV_CH:].reshape(1, -1)
        cb = conv_bias[l].reshape(1, -1)

        naq, nak, nav, cvu, cvb, waq, wak, wav = _in_proj(xs, mod[l], lat_row, g1, w_in_l, gains, rope_tabs, tm)
        cnaq, cnak, cnav, ccvu, ccvb, cwaq, cwak, cwav = _in_proj(cs, mod[l], ctx_row, g1, w_in_l, gains, None, tm_ctx)

        b3 = lambda a, n: a.reshape(batch, n, a.shape[-1])
        na_n = _na_attention(b3(naq, seq), b3(nak, seq), b3(nav, seq), b3(cnak, lc), b3(cnav, lc), rpb_bias[l], go_na)
        wa_n = _wa_attention(wa_sink[l], b3(waq, seq), b3(wak, seq), b3(wav, seq), b3(cwak, lc), b3(cwav, lc), go_wa)
        xs = _out_proj(xs, mod[l], lat_row, na_n.reshape(batch * seq, -1), wa_n.reshape(batch * seq, -1),
                       cvu, cvb, conv_w[l], cb, go_cv, w_o_l, seq, tm)
        if not last:
            cna_n, cwa_n = _ctx_attention(wa_sink[l], b3(cnaq, lc), b3(cnak, lc), b3(cnav, lc),
                                          b3(cwaq, lc), b3(cwak, lc), b3(cwav, lc), go_na, go_wa)
            cs = _out_proj(cs, mod[l], ctx_row, cna_n.reshape(batch * lc, -1), cwa_n.reshape(batch * lc, -1),
                           ccvu, ccvb, conv_w[l], cb, go_cv, w_o_l, lc, tm_ctx)
        xs = _mlp(xs, mod[l], lat_row, g2, w1_l, w2_l, tm)
        if not last:
            cs = _mlp(cs, mod[l], ctx_row, g2, w1_l, w2_l, tm_ctx)
    return xs.reshape(batch, seq, d)
```

```python
import functools

import jax
import jax.numpy as jnp
from jax import lax
from jax.experimental import pallas as pl
from jax.experimental.pallas import tpu as pltpu

D_MODEL = 1024
GRID_W = 64
HEAD_DIM = 64
NA_HEADS = 4
NA_WIDTH = NA_HEADS * HEAD_DIM
CONV_CH = 256
WA_HEADS = 8
WA_KV_HEADS = 2
WA_GROUP = WA_HEADS // WA_KV_HEADS
WA_QW = WA_HEADS * HEAD_DIM
WA_KVW = WA_KV_HEADS * HEAD_DIM
WA_K2W = 2 * WA_KVW
WA_V3W = 4 * WA_KVW
NA_WIN_ROWS = 8
NA_WIN_COLS = 16
NA_DROWS = 2 * NA_WIN_ROWS - 1
NA_DCOLS = 2 * NA_WIN_COLS - 1
WA_WINDOW = 128
WA_BLOCK = 128
WA_BAND = 3 * WA_BLOCK
FFN_DIM = 4 * D_MODEL
FFN_CHUNK = 1024
ROPE_BASE = 10000.0
EPS = 1e-6
NEG_INF = -1e30
IN_WIDTH = 2304
OFF_NA_Q, OFF_NA_K, OFF_NA_V = 0, 256, 512
OFF_CV_X, OFF_CV_B, OFF_CV_C = 768, 1024, 1280
OFF_WA_Q, OFF_WA_K, OFF_WA_V = 1536, 2048, 2176
MOD_ROWS = 16
MOD_TN = 1024
LANES = 128
MXU_DIM = 256
VMEM_LIMIT = 48 * 1024 * 1024
LOG2E = 1.4426950408889634
Q_SCALE = LOG2E * HEAD_DIM ** -0.5

F32 = jnp.float32
BF16 = jnp.bfloat16


def _dot(a, b):
    return jnp.dot(a, b, preferred_element_type=F32)


def _dot_nt(a, b):
    return lax.dot_general(a, b, (((1,), (1,)), ((), ())), preferred_element_type=F32)


def _split_bf16(a):
    hi = a.astype(BF16)
    lo = (a - hi.astype(F32)).astype(BF16)
    return hi, lo


def _iota(shape, dim):
    return lax.broadcasted_iota(jnp.int32, shape, dim)


def _params(**kw):
    return pltpu.CompilerParams(vmem_limit_bytes=VMEM_LIMIT, **kw)


def _mod_kernel(cond_ref, w_ref, b_ref, o_ref):
    a = cond_ref[...]
    a = a * (1.0 / (1.0 + jnp.exp(-a)))
    ah, al = _split_bf16(a)
    wh, wl = _split_bf16(w_ref[...])
    o_ref[...] = _dot(ah, wh) + _dot(al, wh) + _dot(ah, wl) + b_ref[...]


def _modulation(cond, w_mod, b_mod):
    depth = w_mod.shape[0]
    n_out = w_mod.shape[2]
    return pl.pallas_call(
        _mod_kernel,
        out_shape=jax.ShapeDtypeStruct((depth, MOD_ROWS, n_out), F32),
        grid=(depth, n_out // MOD_TN),
        in_specs=[
            pl.BlockSpec((MOD_ROWS, D_MODEL), lambda l, j: (0, 0)),
            pl.BlockSpec((None, D_MODEL, MOD_TN), lambda l, j: (l, 0, j)),
            pl.BlockSpec((None, 1, MOD_TN), lambda l, j: (l, 0, j)),
        ],
        out_specs=pl.BlockSpec((None, MOD_ROWS, MOD_TN), lambda l, j: (l, 0, j)),
        compiler_params=_params(),
        name="modulation",
    )(cond, w_mod, b_mod.reshape(depth, 1, n_out))


def _rpb_kernel(rpb_ref, o_ref):
    l = pl.program_id(0)
    shape = (GRID_W, LANES)
    q = _iota(shape, 0)
    lane = _iota(shape, 1)
    k = lane % GRID_W
    left = lane < GRID_W
    dc = jnp.clip(k - q, -(NA_WIN_COLS - 1), NA_WIN_COLS - 1) + (NA_WIN_COLS - 1)
    col_start = jnp.clip(q - NA_WIN_COLS // 2, 0, GRID_W - NA_WIN_COLS)
    col_ok = (k >= col_start) & (k < col_start + NA_WIN_COLS)
    base_l = l * (NA_HEADS * NA_DROWS * NA_DCOLS)

    def body(d, carry):
        for h in range(NA_HEADS):
            base = base_l + (h * NA_DROWS + d) * NA_DCOLS
            t = jnp.zeros(shape, F32)
            for i in range(NA_DCOLS):
                val = jnp.where(left, rpb_ref[base + i], rpb_ref[base + NA_DCOLS + i])
                t = jnp.where(dc == i, val, t)
            o_ref[d, h * GRID_W:(h + 1) * GRID_W, :] = jnp.where(col_ok, t * LOG2E, NEG_INF)
        return carry

    lax.fori_loop(0, NA_DROWS - 1, body, 0)


def _rpb_tables(na_rpb):
    depth = na_rpb.shape[0]
    return pl.pallas_call(
        _rpb_kernel,
        out_shape=jax.ShapeDtypeStruct((depth, NA_DROWS - 1, NA_HEADS * GRID_W, LANES), F32),
        grid=(depth,),
        in_specs=[pl.BlockSpec(memory_space=pltpu.SMEM)],
        out_specs=pl.BlockSpec((None, NA_DROWS - 1, NA_HEADS * GRID_W, LANES), lambda l: (l, 0, 0, 0)),
        compiler_params=_params(),
        name="rpb_tables",
    )(na_rpb.reshape(-1))


def _head_rms(t, gain):
    width = t.shape[1]
    sq = t * t
    hi, lo = _split_bf16(sq)
    blk = min(width, MXU_DIM)
    ones = jnp.where(_iota((blk, blk), 0) // HEAD_DIM == _iota((blk, blk), 1) // HEAD_DIM, 1.0, 0.0).astype(BF16)
    sums = [_dot(hi[:, c:c + blk], ones) + _dot(lo[:, c:c + blk], ones) for c in range(0, width, blk)]
    ss = sums[0] if len(sums) == 1 else jnp.concatenate(sums, axis=1)
    return t * lax.rsqrt(ss * (1.0 / HEAD_DIM) + EPS) * gain


def _rope(t, cos, sin):
    lane = _iota((t.shape[0], LANES), 1)
    first = (lane % (HEAD_DIM // 2)) < (HEAD_DIM // 4)
    out = []
    for c in range(0, t.shape[1], LANES):
        u = t[:, c:c + LANES]
        partner = jnp.where(first, pltpu.roll(u, LANES - HEAD_DIM // 4, axis=1), pltpu.roll(u, HEAD_DIM // 4, axis=1))
        out.append(u * cos + partner * sin)
    return out[0] if len(out) == 1 else jnp.concatenate(out, axis=1)


def _dup_kv(t):
    lane = _iota(t.shape, 1)
    swapped = pltpu.roll(t, HEAD_DIM, axis=1)
    left = lane < HEAD_DIM
    return jnp.concatenate([jnp.where(left, t, swapped), jnp.where(left, swapped, t)], axis=1)


def _inproj_kernel(x_ref, mod_ref, g1_ref, w_ref, gnq_ref, gnk_ref, gwq_ref, gwk_ref, *rest, rope):
    if rope:
        cos_ref, sin_ref = rest[:2]
        rest = rest[2:]
    naq_ref, nak_ref, nav_ref, cvu_ref, cvb_ref, waq_ref, wak_ref, wav_ref = rest
    x = x_ref[...]
    ms = jnp.mean(x * x, axis=-1, keepdims=True)
    h = x * lax.rsqrt(ms + EPS) * g1_ref[...]
    h = h * (1.0 + mod_ref[1]) + mod_ref[0]
    p = _dot(h.astype(BF16), w_ref[...])
    naq_ref[...] = (_head_rms(p[:, OFF_NA_Q:OFF_NA_K], gnq_ref[...]) * Q_SCALE).astype(BF16)
    nak_ref[...] = _head_rms(p[:, OFF_NA_K:OFF_NA_V], gnk_ref[...]).astype(BF16)
    nav_ref[...] = p[:, OFF_NA_V:OFF_CV_X].astype(BF16)
    cvu_ref[...] = p[:, OFF_CV_C:OFF_WA_Q] * p[:, OFF_CV_X:OFF_CV_B]
    cvb_ref[...] = p[:, OFF_CV_B:OFF_CV_C]
    wq = _head_rms(p[:, OFF_WA_Q:OFF_WA_K], gwq_ref[...])
    wk = _head_rms(p[:, OFF_WA_K:OFF_WA_V], gwk_ref[...])
    if rope:
        wq = _rope(wq, cos_ref[...], sin_ref[...])
        wk = _rope(wk, cos_ref[...], sin_ref[...])
    waq_ref[...] = (wq * Q_SCALE).astype(BF16)
    wak_ref[...] = _dup_kv(wk).astype(BF16)
    v2 = _dup_kv(p[:, OFF_WA_V:IN_WIDTH]).astype(BF16)
    ones = jnp.ones((v2.shape[0], LANES), BF16)
    wav_ref[...] = jnp.concatenate([v2[:, :LANES], ones, v2[:, LANES:], ones], axis=1)


def _in_proj(x2d, mod_l, mod_row, g1, w_in, gains, rope_tabs, tm):
    rows = x2d.shape[0]
    rope = rope_tabs is not None
    row_spec = lambda w: pl.BlockSpec((tm, w), lambda i: (i, 0))
    const = lambda shape: pl.BlockSpec(shape, lambda i: (0,) * len(shape))
    in_specs = [
        row_spec(D_MODEL),
        pl.BlockSpec((None, 6, 1, D_MODEL), lambda i: (mod_row(i), 0, 0, 0)),
        const((1, D_MODEL)),
        const((D_MODEL, IN_WIDTH)),
        const((1, NA_WIDTH)), const((1, NA_WIDTH)), const((1, WA_QW)), const((1, WA_KVW)),
    ]
    args = [x2d, mod_l, g1, w_in, *gains]
    if rope:
        seq_tiles = rope_tabs[0].shape[0] // tm
        in_specs += [pl.BlockSpec((tm, LANES), lambda i: (i % seq_tiles, 0))] * 2
        args += list(rope_tabs)
    widths = [(NA_WIDTH, BF16), (NA_WIDTH, BF16), (NA_WIDTH, BF16), (CONV_CH, F32), (CONV_CH, F32),
              (WA_QW, BF16), (WA_K2W, BF16), (WA_V3W, BF16)]
    return pl.pallas_call(
        functools.partial(_inproj_kernel, rope=rope),
        out_shape=[jax.ShapeDtypeStruct((rows, w), dt) for w, dt in widths],
        grid=(rows // tm,),
        in_specs=in_specs,
        out_specs=[row_spec(w) for w, _ in widths],
        compiler_params=_params(),
        name="in_proj_rope" if rope else "in_proj_ctx",
    )(*args)


def _lane_chunks(s):
    return [s[:, c:c + LANES] for c in range(0, s.shape[1], LANES)]


def _row_max(scores, extra=None):
    chunks = [ch for s in scores for ch in _lane_chunks(s)]
    if extra is not None:
        chunks.append(extra)
    folded = functools.reduce(jnp.maximum, chunks)
    return jnp.broadcast_to(jnp.max(folded, axis=-1, keepdims=True), folded.shape)


def _softmax_pv(parts, extra=None, sums_from_values=False, m=None):
    lane_chunks = _lane_chunks
    if m is None:
        m = _row_max([s for s, _ in parts], extra)
    ps = [jnp.concatenate([jnp.exp2(ch - m) for ch in lane_chunks(s)], axis=1) for s, _ in parts]
    pv = _dot(jnp.concatenate([p.astype(BF16) for p in ps], axis=1),
              jnp.concatenate([v for _, v in parts], axis=0))
    if sums_from_values:
        denom = pv[:, -LANES:]
        pv = pv[:, :-LANES]
        if extra is not None:
            denom = denom + jnp.exp2(extra - m)
        return pv / denom
    psum = functools.reduce(jnp.add, [ch for p in ps for ch in lane_chunks(p)])
    if extra is not None:
        psum = psum + jnp.where(_iota(extra.shape, 1) == 0, jnp.exp2(extra - m), 0.0)
    return pv / jnp.sum(psum, axis=-1, keepdims=True)


def _stack_heads(q, n_heads):
    head = _iota(q.shape, 1) // HEAD_DIM
    zero = jnp.zeros_like(q)
    return jnp.concatenate([jnp.where(head == h, q, zero) for h in range(n_heads)], axis=0)


def _unstack_heads(o, n_heads):
    rows = o.shape[0] // n_heads
    head = _iota((rows, o.shape[1]), 1) // HEAD_DIM
    out = jnp.where(head == 0, o[:rows], 0.0)
    for h in range(1, n_heads):
        out = jnp.where(head == h, o[h * rows:(h + 1) * rows], out)
    return out


def _group_rms(o, gain):
    ms = jnp.mean(o * o, axis=-1, keepdims=True)
    return o * lax.rsqrt(ms + EPS) * gain


def _wa_scores(qg, keys):
    rows = qg.shape[0]
    left = _iota((rows, LANES), 1) < HEAD_DIM
    zero = jnp.zeros((rows, LANES), qg.dtype)
    blocks = []
    for pair in range(WA_GROUP // 2):
        qp = qg[:, pair * LANES:(pair + 1) * LANES]
        blocks += [jnp.where(left, qp, zero), jnp.where(left, zero, qp)]
    qs = jnp.concatenate(blocks, axis=0)
    scores = []
    for k2, bias in keys:
        s = _dot_nt(qs, k2)
        if bias is not None:
            s = jnp.concatenate([s[g * rows:(g + 1) * rows] + bias for g in range(WA_GROUP)], axis=0)
        scores.append(s)
    return scores


def _wa_sink_col(rows, sinks):
    return jnp.concatenate([jnp.full((rows, LANES), sinks[g] * LOG2E, F32) for g in range(WA_GROUP)], axis=0)


def _wa_finish(scores, values, sinks, m=None):
    rows = scores[0].shape[0] // WA_GROUP
    left = _iota((rows, LANES), 1) < HEAD_DIM
    o2 = _softmax_pv(list(zip(scores, values)), extra=_wa_sink_col(rows, sinks), sums_from_values=True, m=m)
    pairs = [jnp.where(left, o2[(2 * pr) * rows:(2 * pr + 1) * rows], o2[(2 * pr + 1) * rows:(2 * pr + 2) * rows])
             for pr in range(WA_GROUP // 2)]
    return jnp.concatenate(pairs, axis=1)


def _na_kernel(q_ref, k_ref, v_ref, kc_ref, vc_ref, bias_ref, g_ref, o_ref, sa_ref, sb_ref):
    n_rows = q_ref.shape[0] // GRID_W
    win = NA_WIN_ROWS * GRID_W
    lc = kc_ref.shape[0]
    gain = g_ref[...]

    def window(r):
        start = jnp.clip(r - NA_WIN_ROWS // 2, 0, n_rows - NA_WIN_ROWS)
        return pl.multiple_of(start * GRID_W, GRID_W), start - r + (NA_WIN_ROWS - 1)

    def scores(r, s_ref):
        tok0, d0 = window(r)
        q = q_ref[pl.ds(pl.multiple_of(r * GRID_W, GRID_W), GRID_W), :]
        qs = _stack_heads(q, NA_HEADS)
        bias = jnp.concatenate([bias_ref[d0 + 2 * j] for j in range(NA_WIN_ROWS // 2)], axis=1)
        s_ctx = _dot_nt(qs, kc_ref[...])
        s_loc = _dot_nt(qs, k_ref[pl.ds(tok0, win), :]) + bias
        s_ref[:, :lc] = s_ctx
        s_ref[:, lc:lc + win] = s_loc
        s_ref[:, lc + win:] = _row_max([s_ctx, s_loc])

    def finish(r, s_ref):
        tok0, _ = window(r)
        o4 = _softmax_pv([(s_ref[:, :lc], vc_ref[...]), (s_ref[:, lc:lc + win], v_ref[pl.ds(tok0, win), :])],
                         m=s_ref[:, lc + win:])
        o = _unstack_heads(o4, NA_HEADS)
        o_ref[pl.ds(pl.multiple_of(r * GRID_W, GRID_W), GRID_W), :] = _group_rms(o, gain).astype(o_ref.dtype)

    def step(r, cur_ref, nxt_ref):
        scores(jnp.minimum(r + 1, n_rows - 1), nxt_ref)
        finish(r, cur_ref)

    scores(0, sa_ref)

    def body(i, carry):
        step(2 * i, sa_ref, sb_ref)
        step(2 * i + 1, sb_ref, sa_ref)
        return carry

    lax.fori_loop(0, n_rows // 2, body, 0, unroll=4)


def _na_attention(q, k, v, kc, vc, bias, gain):
    batch, seq, _ = q.shape
    lc = kc.shape[1]
    assert (seq // GRID_W) % 4 == 0
    per_b = lambda n: pl.BlockSpec((None, n, NA_WIDTH), lambda b: (b, 0, 0))
    score_scratch = pltpu.VMEM((NA_HEADS * GRID_W, lc + NA_WIN_ROWS * GRID_W + LANES), F32)
    return pl.pallas_call(
        _na_kernel,
        out_shape=jax.ShapeDtypeStruct((batch, seq, NA_WIDTH), BF16),
        grid=(batch,),
        in_specs=[per_b(seq), per_b(seq), per_b(seq), per_b(lc), per_b(lc),
                  pl.BlockSpec(bias.shape, lambda b: (0, 0, 0)),
                  pl.BlockSpec((1, NA_WIDTH), lambda b: (0, 0))],
        out_specs=per_b(seq),
        scratch_shapes=[score_scratch, score_scratch],
        compiler_params=_params(),
        name="na_attention",
    )(q, k, v, kc, vc, bias, gain)


def _wa_kernel(sink_ref, q_ref, k_ref, v_ref, kc_ref, vc_ref, g_ref, o_ref, sa_ref, sb_ref):
    seq = q_ref.shape[0]
    n_blocks = seq // WA_BLOCK
    lc = kc_ref.shape[0]
    gain = g_ref[...]
    row = _iota((WA_BLOCK, WA_BAND), 0)
    col = _iota((WA_BLOCK, WA_BAND), 1)

    def window(n):
        q0 = pl.multiple_of(n * WA_BLOCK, WA_BLOCK)
        k0 = pl.multiple_of(jnp.clip(q0 - WA_BLOCK, 0, seq - WA_BAND), WA_BLOCK)
        return q0, k0

    def scores(n, kh, s_ref):
        q0, k0 = window(n)
        band = jnp.where(jnp.abs((k0 + col) - (q0 + row)) <= WA_WINDOW, 0.0, NEG_INF)
        lanes = slice(kh * LANES, (kh + 1) * LANES)
        qg = q_ref[pl.ds(q0, WA_BLOCK), kh * WA_GROUP * HEAD_DIM:(kh + 1) * WA_GROUP * HEAD_DIM]
        s_ctx, s_loc = _wa_scores(qg, [(kc_ref[:, lanes], None), (k_ref[pl.ds(k0, WA_BAND), lanes], band)])
        s_ref[kh, :, :lc] = s_ctx
        s_ref[kh, :, lc:lc + WA_BAND] = s_loc
        s_ref[kh, :, lc + WA_BAND:] = _row_max([s_ctx, s_loc], _wa_sink_col(WA_BLOCK, sinks(kh)))

    def sinks(kh):
        return [sink_ref[kh * WA_GROUP + g] for g in range(WA_GROUP)]

    def finish(n, kh, s_ref):
        _, k0 = window(n)
        vlanes = slice(kh * 2 * LANES, (kh + 1) * 2 * LANES)
        return _wa_finish([s_ref[kh, :, :lc], s_ref[kh, :, lc:lc + WA_BAND]],
                          [vc_ref[:, vlanes], v_ref[pl.ds(k0, WA_BAND), vlanes]], sinks(kh),
                          m=s_ref[kh, :, lc + WA_BAND:])

    def step(n, cur_ref, nxt_ref):
        nxt = jnp.minimum(n + 1, n_blocks - 1)
        for kh in range(WA_KV_HEADS):
            scores(nxt, kh, nxt_ref)
        o = jnp.concatenate([finish(n, kh, cur_ref) for kh in range(WA_KV_HEADS)], axis=1)
        q0, _ = window(n)
        o_ref[pl.ds(q0, WA_BLOCK), :] = _group_rms(o, gain).astype(o_ref.dtype)

    for kh in range(WA_KV_HEADS):
        scores(0, kh, sa_ref)

    def body(i, carry):
        step(2 * i, sa_ref, sb_ref)
        step(2 * i + 1, sb_ref, sa_ref)
        return carry

    lax.fori_loop(0, n_blocks // 2, body, 0)


def _wa_attention(sink, q, k2, v2, kc2, vc2, gain):
    batch, seq, _ = q.shape
    lc = kc2.shape[1]
    assert (seq // WA_BLOCK) % 2 == 0
    per_b = lambda n, w: pl.BlockSpec((None, n, w), lambda b: (b, 0, 0))
    score_scratch = pltpu.VMEM((WA_KV_HEADS, WA_GROUP * WA_BLOCK, lc + WA_BAND + LANES), F32)
    return pl.pallas_call(
        _wa_kernel,
        out_shape=jax.ShapeDtypeStruct((batch, seq, WA_QW), BF16),
        grid=(batch,),
        in_specs=[pl.BlockSpec(memory_space=pltpu.SMEM),
                  per_b(seq, WA_QW), per_b(seq, WA_K2W), per_b(seq, WA_V3W),
                  per_b(lc, WA_K2W), per_b(lc, WA_V3W),
                  pl.BlockSpec((1, WA_QW), lambda b: (0, 0))],
        out_specs=per_b(seq, WA_QW),
        scratch_shapes=[score_scratch, score_scratch],
        compiler_params=_params(),
        name="wa_attention",
    )(sink, q, k2, v2, kc2, vc2, gain)


def _ctx_attn_kernel(sink_ref, qn_ref, kn_ref, vn_ref, qw_ref, kw_ref, vw_ref, gn_ref, gw_ref, on_ref, ow_ref):
    qs = _stack_heads(qn_ref[...], NA_HEADS)
    o4 = _softmax_pv([(_dot_nt(qs, kn_ref[...]), vn_ref[...])])
    on_ref[...] = _group_rms(_unstack_heads(o4, NA_HEADS), gn_ref[...]).astype(on_ref.dtype)
    outs = []
    for kh in range(WA_KV_HEADS):
        lanes = slice(kh * LANES, (kh + 1) * LANES)
        vlanes = slice(kh * 2 * LANES, (kh + 1) * 2 * LANES)
        qg = qw_ref[:, kh * WA_GROUP * HEAD_DIM:(kh + 1) * WA_GROUP * HEAD_DIM]
        sinks = [sink_ref[kh * WA_GROUP + g] for g in range(WA_GROUP)]
        outs.append(_wa_finish(_wa_scores(qg, [(kw_ref[:, lanes], None)]), [vw_ref[:, vlanes]], sinks))
    ow_ref[...] = _group_rms(jnp.concatenate(outs, axis=1), gw_ref[...]).astype(ow_ref.dtype)


def _ctx_attention(sink, qn, kn, vn, qw, kw2, vw2, gain_na, gain_wa):
    batch, lc, _ = qn.shape
    per_b = lambda w: pl.BlockSpec((None, lc, w), lambda b: (b, 0, 0))
    return pl.pallas_call(
        _ctx_attn_kernel,
        out_shape=[jax.ShapeDtypeStruct((batch, lc, NA_WIDTH), BF16),
                   jax.ShapeDtypeStruct((batch, lc, WA_QW), BF16)],
        grid=(batch,),
        in_specs=[pl.BlockSpec(memory_space=pltpu.SMEM),
                  per_b(NA_WIDTH), per_b(NA_WIDTH), per_b(NA_WIDTH),
                  per_b(WA_QW), per_b(WA_K2W), per_b(WA_V3W),
                  pl.BlockSpec((1, NA_WIDTH), lambda b: (0, 0)),
                  pl.BlockSpec((1, WA_QW), lambda b: (0, 0))],
        out_specs=[per_b(NA_WIDTH), per_b(WA_QW)],
        compiler_params=_params(),
        name="ctx_attention",
    )(sink, qn, kn, vn, qw, kw2, vw2, gain_na, gain_wa)


def _out_kernel(x_ref, mod_ref, na_ref, wa_ref, u_ref, uprev_ref, unext_ref, cvb_ref, cw_ref, cb_ref, g_ref,
                wo_ref, o_ref, *, seq_len):
    tm = x_ref.shape[0]
    i = pl.program_id(0)
    u = u_ref[...]
    at_start = (i * tm) % seq_len == 0
    at_end = ((i + 1) * tm) % seq_len == 0
    prev_row = jnp.where(at_start, 0.0, uprev_ref[7:8, :])
    next_row = jnp.where(at_end, 0.0, unext_ref[0:1, :])
    row = _iota(u.shape, 0)
    up = jnp.where(row == 0, prev_row, pltpu.roll(u, 1, axis=0))
    dn = jnp.where(row == tm - 1, next_row, pltpu.roll(u, tm - 1, axis=0))
    y = cb_ref[...] + cw_ref[0:1, :] * up + cw_ref[1:2, :] * u + cw_ref[2:3, :] * dn
    cv = _group_rms(cvb_ref[...] * y, g_ref[...]).astype(BF16)
    mixed = jnp.concatenate([na_ref[...], cv, wa_ref[...]], axis=1)
    o_ref[...] = x_ref[...] + mod_ref[2] * _dot(mixed, wo_ref[...])


def _out_proj(x2d, mod_l, mod_row, na_n, wa_n, cv_u, cv_b, conv_w, conv_b, gain_cv, w_o, seq_len, tm):
    rows = x2d.shape[0]
    halo = 8
    per_tile = tm // halo
    last = rows // halo - 1
    row_spec = lambda w: pl.BlockSpec((tm, w), lambda i: (i, 0))
    const = lambda shape: pl.BlockSpec(shape, lambda i: (0,) * len(shape))
    return pl.pallas_call(
        functools.partial(_out_kernel, seq_len=seq_len),
        out_shape=jax.ShapeDtypeStruct((rows, D_MODEL), F32),
        grid=(rows // tm,),
        in_specs=[
            row_spec(D_MODEL),
            pl.BlockSpec((None, 6, 1, D_MODEL), lambda i: (mod_row(i), 0, 0, 0)),
            row_spec(NA_WIDTH), row_spec(WA_QW), row_spec(CONV_CH),
            pl.BlockSpec((halo, CONV_CH), lambda i: (jnp.maximum(i * per_tile - 1, 0), 0)),
            pl.BlockSpec((halo, CONV_CH), lambda i: (jnp.minimum((i + 1) * per_tile, last), 0)),
            row_spec(CONV_CH),
            const((3, CONV_CH)), const((1, CONV_CH)), const((1, CONV_CH)),
            const((D_MODEL, D_MODEL)),
        ],
        out_specs=row_spec(D_MODEL),
        compiler_params=_params(),
        name="out_proj",
    )(x2d, mod_l, na_n, wa_n, cv_u, cv_u, cv_u, cv_b, conv_w, conv_b, gain_cv, w_o)


def _mlp_kernel(x_ref, mod_ref, g2_ref, w1_ref, w2_ref, o_ref):
    x = x_ref[...]
    ms = jnp.mean(x * x, axis=-1, keepdims=True)
    h = x * lax.rsqrt(ms + EPS) * g2_ref[...]
    h = (h * (1.0 + mod_ref[4]) + mod_ref[3]).astype(BF16)
    acc = None
    for c in range(0, FFN_DIM, FFN_CHUNK):
        a = jnp.maximum(_dot(h, w1_ref[:, c:c + FFN_CHUNK]), 0.0)
        part = _dot((a * a).astype(BF16), w2_ref[c:c + FFN_CHUNK, :])
        acc = part if acc is None else acc + part
    o_ref[...] = x + mod_ref[5] * acc


def _mlp(x2d, mod_l, mod_row, g2, w1, w2, tm):
    rows = x2d.shape[0]
    row_spec = pl.BlockSpec((tm, D_MODEL), lambda i: (i, 0))
    return pl.pallas_call(
        _mlp_kernel,
        out_shape=jax.ShapeDtypeStruct((rows, D_MODEL), F32),
        grid=(rows // tm,),
        in_specs=[
            row_spec,
            pl.BlockSpec((None, 6, 1, D_MODEL), lambda i: (mod_row(i), 0, 0, 0)),
            pl.BlockSpec((1, D_MODEL), lambda i: (0, 0)),
            pl.BlockSpec((D_MODEL, FFN_DIM), lambda i: (0, 0)),
            pl.BlockSpec((FFN_DIM, D_MODEL), lambda i: (0, 0)),
        ],
        out_specs=row_spec,
        compiler_params=_params(),
        name="mlp",
    )(x2d, mod_l, g2, w1, w2)


def _rope_tables(seq):
    quarter = HEAD_DIM // 4
    inv = ROPE_BASE ** (-jnp.arange(quarter, dtype=F32) / quarter)
    t = jnp.arange(seq)
    ang_r = (t // GRID_W).astype(F32)[:, None] * inv[None, :]
    ang_c = (t % GRID_W).astype(F32)[:, None] * inv[None, :]
    cos = jnp.concatenate([jnp.cos(ang_r), jnp.cos(ang_r), jnp.cos(ang_c), jnp.cos(ang_c)], axis=1)
    sin = jnp.concatenate([-jnp.sin(ang_r), jnp.sin(ang_r), -jnp.sin(ang_c), jnp.sin(ang_c)], axis=1)
    reps = LANES // HEAD_DIM
    return jnp.tile(cos, (1, reps)), jnp.tile(sin, (1, reps))


def kernel(x, c, ctx, c_ctx, w_mod, b_mod, g_norm1, g_norm2, w_in, na_q_gain, na_k_gain, na_rpb,
           conv_w, conv_bias, wa_q_gain, wa_k_gain, wa_sink, g_out, w_o, w_fc1, w_fc2):
    batch, seq, d = x.shape
    lc = ctx.shape[1]
    depth = w_mod.shape[0]
    assert d == D_MODEL and seq % WA_BLOCK == 0 and seq % GRID_W == 0 and batch < MOD_ROWS
    tm = 512
    tm_ctx = 256
    assert seq % tm == 0 and lc % tm_ctx == 0

    cond = jnp.zeros((MOD_ROWS, d), F32).at[:batch].set(c).at[batch].set(c_ctx)
    mod = _modulation(cond, w_mod, b_mod).reshape(depth, MOD_ROWS, 6, 1, d)
    rpb_bias = _rpb_tables(na_rpb)
    rope_tabs = _rope_tables(seq)

    lat_row = lambda i: i // (seq // tm)
    ctx_row = lambda i: batch

    xs = x.reshape(batch * seq, d)
    cs = ctx.reshape(batch * lc, d)
    for l in range(depth):
        last = l == depth - 1
        w_in_l = w_in[l].astype(BF16)
        w_o_l = w_o[l].astype(BF16)
        w1_l = w_fc1[l].astype(BF16)
        w2_l = w_fc2[l].astype(BF16)
        g1 = g_norm1[l].reshape(1, d)
        g2 = g_norm2[l].reshape(1, d)
        gains = [jnp.tile(na_q_gain[l], NA_HEADS).reshape(1, -1), jnp.tile(na_k_gain[l], NA_HEADS).reshape(1, -1),
                 jnp.tile(wa_q_gain[l], WA_HEADS).reshape(1, -1), jnp.tile(wa_k_gain[l], WA_KV_HEADS).reshape(1, -1)]
        go_na = g_out[l, :NA_WIDTH].reshape(1, -1)
        go_cv = g_out[l, NA_WIDTH:NA_WIDTH + CONV_CH].reshape(1, -1)
        go_wa = g_out[l, NA_WIDTH + CONV_CH:].reshape(1, -1)
        cb = conv_bias[l].reshape(1, -1)

        naq, nak, nav, cvu, cvb, waq, wak, wav = _in_proj(xs, mod[l], lat_row, g1, w_in_l, gains, rope_tabs, tm)
        cnaq, cnak, cnav, ccvu, ccvb, cwaq, cwak, cwav = _in_proj(cs, mod[l], ctx_row, g1, w_in_l, gains, None, tm_ctx)

        b3 = lambda a, n: a.reshape(batch, n, a.shape[-1])
        na_n = _na_attention(b3(naq, seq), b3(nak, seq), b3(nav, seq), b3(cnak, lc), b3(cnav, lc), rpb_bias[l], go_na)
        wa_n = _wa_attention(wa_sink[l], b3(waq, seq), b3(wak, seq), b3(wav, seq), b3(cwak, lc), b3(cwav, lc), go_wa)
        xs = _out_proj(xs, mod[l], lat_row, na_n.reshape(batch * seq, -1), wa_n.reshape(batch * seq, -1),
                       cvu, cvb, conv_w[l], cb, go_cv, w_o_l, seq, tm)
        if not last:
            cna_n, cwa_n = _ctx_attention(wa_sink[l], b3(cnaq, lc), b3(cnak, lc), b3(cnav, lc),
                                          b3(cwaq, lc), b3(cwak, lc), b3(cwav, lc), go_na, go_wa)
            cs = _out_proj(cs, mod[l], ctx_row, cna_n.reshape(batch * lc, -1), cwa_n.reshape(batch * lc, -1),
                           ccvu, ccvb, conv_w[l], cb, go_cv, w_o_l, lc, tm_ctx)
        xs = _mlp(xs, mod[l], lat_row, g2, w1_l, w2_l, tm)
        if not last:
            cs = _mlp(cs, mod[l], ctx_row, g2, w1_l, w2_l, tm_ctx)
    return xs.reshape(batch, seq, d)
```

```python
import functools

import jax
import jax.numpy as jnp
from jax import lax
from jax.experimental import pallas as pl
from jax.experimental.pallas import tpu as pltpu

D_MODEL = 1024
GRID_W = 64
HEAD_DIM = 64
NA_HEADS = 4
NA_WIDTH = NA_HEADS * HEAD_DIM
CONV_CH = 256
WA_HEADS = 8
WA_KV_HEADS = 2
WA_GROUP = WA_HEADS // WA_KV_HEADS
WA_QW = WA_HEADS * HEAD_DIM
WA_KVW = WA_KV_HEADS * HEAD_DIM
WA_K2W = 2 * WA_KVW
WA_V3W = 4 * WA_KVW
NA_WIN_ROWS = 8
NA_WIN_COLS = 16
NA_DROWS = 2 * NA_WIN_ROWS - 1
NA_DCOLS = 2 * NA_WIN_COLS - 1
WA_WINDOW = 128
WA_BLOCK = 128
WA_BAND = 3 * WA_BLOCK
FFN_DIM = 4 * D_MODEL
FFN_CHUNK = 1024
ROPE_BASE = 10000.0
EPS = 1e-6
NEG_INF = -1e30
IN_WIDTH = 2304
OFF_NA_Q, OFF_NA_K, OFF_NA_V = 0, 256, 512
OFF_CV_X, OFF_CV_B, OFF_CV_C = 768, 1024, 1280
OFF_WA_Q, OFF_WA_K, OFF_WA_V = 1536, 2048, 2176
MOD_ROWS = 16
MOD_TN = 1024
IN_SUB = 256
MLP_SUB = 256
LANES = 128
MXU_DIM = 256
VMEM_LIMIT = 48 * 1024 * 1024
LOG2E = 1.4426950408889634
Q_SCALE = LOG2E * HEAD_DIM ** -0.5

F32 = jnp.float32
BF16 = jnp.bfloat16


def _dot(a, b):
    return jnp.dot(a, b, preferred_element_type=F32)


def _dot_nt(a, b):
    return lax.dot_general(a, b, (((1,), (1,)), ((), ())), preferred_element_type=F32)


def _split_bf16(a):
    hi = a.astype(BF16)
    lo = (a - hi.astype(F32)).astype(BF16)
    return hi, lo


def _iota(shape, dim):
    return lax.broadcasted_iota(jnp.int32, shape, dim)


def _params(**kw):
    return pltpu.CompilerParams(vmem_limit_bytes=VMEM_LIMIT, **kw)


def _mod_kernel(cond_ref, w_ref, b_ref, o_ref):
    a = cond_ref[...]
    a = a * (1.0 / (1.0 + jnp.exp(-a)))
    ah, al = _split_bf16(a)
    wh, wl = _split_bf16(w_ref[...])
    o_ref[...] = _dot(ah, wh) + _dot(al, wh) + _dot(ah, wl) + b_ref[...]


def _modulation(cond, w_mod, b_mod):
    depth = w_mod.shape[0]
    n_out = w_mod.shape[2]
    return pl.pallas_call(
        _mod_kernel,
        out_shape=jax.ShapeDtypeStruct((depth, MOD_ROWS, n_out), F32),
        grid=(depth, n_out // MOD_TN),
        in_specs=[
            pl.BlockSpec((MOD_ROWS, D_MODEL), lambda l, j: (0, 0)),
            pl.BlockSpec((None, D_MODEL, MOD_TN), lambda l, j: (l, 0, j)),
            pl.BlockSpec((None, 1, MOD_TN), lambda l, j: (l, 0, j)),
        ],
        out_specs=pl.BlockSpec((None, MOD_ROWS, MOD_TN), lambda l, j: (l, 0, j)),
        compiler_params=_params(),
        name="modulation",
    )(cond, w_mod, b_mod.reshape(depth, 1, n_out))


def _rpb_kernel(rpb_ref, o_ref):
    l = pl.program_id(0)
    shape = (GRID_W, LANES)
    q = _iota(shape, 0)
    lane = _iota(shape, 1)
    k = lane % GRID_W
    left = lane < GRID_W
    dc = jnp.clip(k - q, -(NA_WIN_COLS - 1), NA_WIN_COLS - 1) + (NA_WIN_COLS - 1)
    col_start = jnp.clip(q - NA_WIN_COLS // 2, 0, GRID_W - NA_WIN_COLS)
    col_ok = (k >= col_start) & (k < col_start + NA_WIN_COLS)
    base_l = l * (NA_HEADS * NA_DROWS * NA_DCOLS)

    def body(d, carry):
        for h in range(NA_HEADS):
            base = base_l + (h * NA_DROWS + d) * NA_DCOLS
            t = jnp.zeros(shape, F32)
            for i in range(NA_DCOLS):
                val = jnp.where(left, rpb_ref[base + i], rpb_ref[base + NA_DCOLS + i])
                t = jnp.where(dc == i, val, t)
            o_ref[d, h * GRID_W:(h + 1) * GRID_W, :] = jnp.where(col_ok, t * LOG2E, NEG_INF)
        return carry

    lax.fori_loop(0, NA_DROWS - 1, body, 0)


def _rpb_tables(na_rpb):
    depth = na_rpb.shape[0]
    return pl.pallas_call(
        _rpb_kernel,
        out_shape=jax.ShapeDtypeStruct((depth, NA_DROWS - 1, NA_HEADS * GRID_W, LANES), F32),
        grid=(depth,),
        in_specs=[pl.BlockSpec(memory_space=pltpu.SMEM)],
        out_specs=pl.BlockSpec((None, NA_DROWS - 1, NA_HEADS * GRID_W, LANES), lambda l: (l, 0, 0, 0)),
        compiler_params=_params(),
        name="rpb_tables",
    )(na_rpb.reshape(-1))


def _head_rms(t, gain):
    width = t.shape[1]
    sq = (t * t).astype(BF16)
    blk = min(width, MXU_DIM)
    ones = jnp.where(_iota((blk, blk), 0) // HEAD_DIM == _iota((blk, blk), 1) // HEAD_DIM, 1.0, 0.0).astype(BF16)
    sums = [_dot(sq[:, c:c + blk], ones) for c in range(0, width, blk)]
    ss = sums[0] if len(sums) == 1 else jnp.concatenate(sums, axis=1)
    return t * lax.rsqrt(ss * (1.0 / HEAD_DIM) + EPS) * gain


def _rope(t, cos, sin):
    lane = _iota((t.shape[0], LANES), 1)
    first = (lane % (HEAD_DIM // 2)) < (HEAD_DIM // 4)
    out = []
    for c in range(0, t.shape[1], LANES):
        u = t[:, c:c + LANES]
        partner = jnp.where(first, pltpu.roll(u, LANES - HEAD_DIM // 4, axis=1), pltpu.roll(u, HEAD_DIM // 4, axis=1))
        out.append(u * cos + partner * sin)
    return out[0] if len(out) == 1 else jnp.concatenate(out, axis=1)


def _dup_kv(t):
    lane = _iota(t.shape, 1)
    swapped = pltpu.roll(t, HEAD_DIM, axis=1)
    left = lane < HEAD_DIM
    return jnp.concatenate([jnp.where(left, t, swapped), jnp.where(left, swapped, t)], axis=1)


def _inproj_kernel(x_ref, mod_ref, g1_ref, w_ref, gnq_ref, gnk_ref, gwq_ref, gwk_ref, *rest, rope, sub):
    if rope:
        cos_ref, sin_ref = rest[:2]
        rest = rest[2:]
    naq_ref, nak_ref, nav_ref, cvu_ref, cvb_ref, waq_ref, wak_ref, wav_ref = rest
    for r0 in range(0, x_ref.shape[0], sub):
        rows = slice(r0, r0 + sub)
        x = x_ref[rows, :]
        ms = jnp.mean(x * x, axis=-1, keepdims=True)
        h = x * lax.rsqrt(ms + EPS) * g1_ref[...]
        h = h * (1.0 + mod_ref[1]) + mod_ref[0]
        p = _dot(h.astype(BF16), w_ref[...])
        naq_ref[rows, :] = (_head_rms(p[:, OFF_NA_Q:OFF_NA_K], gnq_ref[...]) * Q_SCALE).astype(BF16)
        nak_ref[rows, :] = _head_rms(p[:, OFF_NA_K:OFF_NA_V], gnk_ref[...]).astype(BF16)
        nav_ref[rows, :] = p[:, OFF_NA_V:OFF_CV_X].astype(BF16)
        cvu_ref[rows, :] = p[:, OFF_CV_C:OFF_WA_Q] * p[:, OFF_CV_X:OFF_CV_B]
        cvb_ref[rows, :] = p[:, OFF_CV_B:OFF_CV_C]
        wq = _head_rms(p[:, OFF_WA_Q:OFF_WA_K], gwq_ref[...])
        wk = _head_rms(p[:, OFF_WA_K:OFF_WA_V], gwk_ref[...])
        if rope:
            wq = _rope(wq, cos_ref[rows, :], sin_ref[rows, :])
            wk = _rope(wk, cos_ref[rows, :], sin_ref[rows, :])
        waq_ref[rows, :] = (wq * Q_SCALE).astype(BF16)
        wak_ref[rows, :] = _dup_kv(wk).astype(BF16)
        v2 = _dup_kv(p[:, OFF_WA_V:IN_WIDTH]).astype(BF16)
        ones = jnp.ones((sub, LANES), BF16)
        wav_ref[rows, :] = jnp.concatenate([v2[:, :LANES], ones, v2[:, LANES:], ones], axis=1)


def _in_proj(x2d, mod_l, mod_row, g1, w_in, gains, rope_tabs, tm):
    rows = x2d.shape[0]
    rope = rope_tabs is not None
    row_spec = lambda w: pl.BlockSpec((tm, w), lambda i: (i, 0))
    const = lambda shape: pl.BlockSpec(shape, lambda i: (0,) * len(shape))
    in_specs = [
        row_spec(D_MODEL),
        pl.BlockSpec((None, 6, 1, D_MODEL), lambda i: (mod_row(i), 0, 0, 0)),
        const((1, D_MODEL)),
        const((D_MODEL, IN_WIDTH)),
        const((1, NA_WIDTH)), const((1, NA_WIDTH)), const((1, WA_QW)), const((1, WA_KVW)),
    ]
    args = [x2d, mod_l, g1, w_in, *gains]
    if rope:
        seq_tiles = rope_tabs[0].shape[0] // tm
        in_specs += [pl.BlockSpec((tm, LANES), lambda i: (i % seq_tiles, 0))] * 2
        args += list(rope_tabs)
    widths = [(NA_WIDTH, BF16), (NA_WIDTH, BF16), (NA_WIDTH, BF16), (CONV_CH, F32), (CONV_CH, F32),
              (WA_QW, BF16), (WA_K2W, BF16), (WA_V3W, BF16)]
    return pl.pallas_call(
        functools.partial(_inproj_kernel, rope=rope, sub=min(tm, IN_SUB)),
        out_shape=[jax.ShapeDtypeStruct((rows, w), dt) for w, dt in widths],
        grid=(rows // tm,),
        in_specs=in_specs,
        out_specs=[row_spec(w) for w, _ in widths],
        compiler_params=_params(),
        name="in_proj_rope" if rope else "in_proj_ctx",
    )(*args)


def _lane_chunks(s):
    return [s[:, c:c + LANES] for c in range(0, s.shape[1], LANES)]


def _row_max(scores, extra=None):
    chunks = [ch for s in scores for ch in _lane_chunks(s)]
    if extra is not None:
        chunks.append(extra)
    folded = functools.reduce(jnp.maximum, chunks)
    return jnp.broadcast_to(jnp.max(folded, axis=-1, keepdims=True), folded.shape)


def _softmax_pv(parts, extra=None, sums_from_values=False, m=None):
    lane_chunks = _lane_chunks
    if m is None:
        m = _row_max([s for s, _ in parts], extra)
    ps = [jnp.concatenate([jnp.exp2(ch - m) for ch in lane_chunks(s)], axis=1) for s, _ in parts]
    pv = _dot(jnp.concatenate([p.astype(BF16) for p in ps], axis=1),
              jnp.concatenate([v for _, v in parts], axis=0))
    if sums_from_values:
        denom = pv[:, -LANES:]
        pv = pv[:, :-LANES]
        if extra is not None:
            denom = denom + jnp.exp2(extra - m)
        return pv / denom
    psum = functools.reduce(jnp.add, [ch for p in ps for ch in lane_chunks(p)])
    if extra is not None:
        psum = psum + jnp.where(_iota(extra.shape, 1) == 0, jnp.exp2(extra - m), 0.0)
    return pv / jnp.sum(psum, axis=-1, keepdims=True)


def _stack_heads(q, n_heads):
    head = _iota(q.shape, 1) // HEAD_DIM
    zero = jnp.zeros_like(q)
    return jnp.concatenate([jnp.where(head == h, q, zero) for h in range(n_heads)], axis=0)


def _unstack_heads(o, n_heads):
    rows = o.shape[0] // n_heads
    head = _iota((rows, o.shape[1]), 1) // HEAD_DIM
    out = jnp.where(head == 0, o[:rows], 0.0)
    for h in range(1, n_heads):
        out = jnp.where(head == h, o[h * rows:(h + 1) * rows], out)
    return out


def _group_rms(o, gain):
    ms = jnp.mean(o * o, axis=-1, keepdims=True)
    return o * lax.rsqrt(ms + EPS) * gain


def _wa_scores(qg, keys):
    rows = qg.shape[0]
    left = _iota((rows, LANES), 1) < HEAD_DIM
    zero = jnp.zeros((rows, LANES), qg.dtype)
    blocks = []
    for pair in range(WA_GROUP // 2):
        qp = qg[:, pair * LANES:(pair + 1) * LANES]
        blocks += [jnp.where(left, qp, zero), jnp.where(left, zero, qp)]
    qs = jnp.concatenate(blocks, axis=0)
    scores = []
    for k2, bias in keys:
        s = _dot_nt(qs, k2)
        if bias is not None:
            s = jnp.concatenate([s[g * rows:(g + 1) * rows] + bias for g in range(WA_GROUP)], axis=0)
        scores.append(s)
    return scores


def _wa_sink_col(rows, sinks):
    return jnp.concatenate([jnp.full((rows, LANES), sinks[g] * LOG2E, F32) for g in range(WA_GROUP)], axis=0)


def _wa_finish(scores, values, sinks, m=None):
    rows = scores[0].shape[0] // WA_GROUP
    left = _iota((rows, LANES), 1) < HEAD_DIM
    o2 = _softmax_pv(list(zip(scores, values)), extra=_wa_sink_col(rows, sinks), sums_from_values=True, m=m)
    pairs = [jnp.where(left, o2[(2 * pr) * rows:(2 * pr + 1) * rows], o2[(2 * pr + 1) * rows:(2 * pr + 2) * rows])
             for pr in range(WA_GROUP // 2)]
    return jnp.concatenate(pairs, axis=1)


def _na_kernel(q_ref, k_ref, v_ref, kc_ref, vc_ref, bias_ref, g_ref, o_ref, sa_ref, sb_ref):
    n_rows = q_ref.shape[0] // GRID_W
    win = NA_WIN_ROWS * GRID_W
    lc = kc_ref.shape[0]
    gain = g_ref[...]

    def window(r):
        start = jnp.clip(r - NA_WIN_ROWS // 2, 0, n_rows - NA_WIN_ROWS)
        return pl.multiple_of(start * GRID_W, GRID_W), start - r + (NA_WIN_ROWS - 1)

    def scores(r, s_ref):
        tok0, d0 = window(r)
        q = q_ref[pl.ds(pl.multiple_of(r * GRID_W, GRID_W), GRID_W), :]
        qs = _stack_heads(q, NA_HEADS)
        bias = jnp.concatenate([bias_ref[d0 + 2 * j] for j in range(NA_WIN_ROWS // 2)], axis=1)
        s_ctx = _dot_nt(qs, kc_ref[...])
        s_loc = _dot_nt(qs, k_ref[pl.ds(tok0, win), :]) + bias
        s_ref[:, :lc] = s_ctx
        s_ref[:, lc:lc + win] = s_loc
        s_ref[:, lc + win:] = _row_max([s_ctx, s_loc])

    def finish(r, s_ref):
        tok0, _ = window(r)
        o4 = _softmax_pv([(s_ref[:, :lc], vc_ref[...]), (s_ref[:, lc:lc + win], v_ref[pl.ds(tok0, win), :])],
                         m=s_ref[:, lc + win:])
        o = _unstack_heads(o4, NA_HEADS)
        o_ref[pl.ds(pl.multiple_of(r * GRID_W, GRID_W), GRID_W), :] = _group_rms(o, gain).astype(o_ref.dtype)

    def step(r, cur_ref, nxt_ref):
        scores(jnp.minimum(r + 1, n_rows - 1), nxt_ref)
        finish(r, cur_ref)

    scores(0, sa_ref)

    def body(i, carry):
        step(2 * i, sa_ref, sb_ref)
        step(2 * i + 1, sb_ref, sa_ref)
        return carry

    lax.fori_loop(0, n_rows // 2, body, 0, unroll=4)


def _na_attention(q, k, v, kc, vc, bias, gain):
    batch, seq, _ = q.shape
    lc = kc.shape[1]
    assert (seq // GRID_W) % 4 == 0
    per_b = lambda n: pl.BlockSpec((None, n, NA_WIDTH), lambda b: (b, 0, 0))
    score_scratch = pltpu.VMEM((NA_HEADS * GRID_W, lc + NA_WIN_ROWS * GRID_W + LANES), F32)
    return pl.pallas_call(
        _na_kernel,
        out_shape=jax.ShapeDtypeStruct((batch, seq, NA_WIDTH), BF16),
        grid=(batch,),
        in_specs=[per_b(seq), per_b(seq), per_b(seq), per_b(lc), per_b(lc),
                  pl.BlockSpec(bias.shape, lambda b: (0, 0, 0)),
                  pl.BlockSpec((1, NA_WIDTH), lambda b: (0, 0))],
        out_specs=per_b(seq),
        scratch_shapes=[score_scratch, score_scratch],
        compiler_params=_params(),
        name="na_attention",
    )(q, k, v, kc, vc, bias, gain)


def _wa_kernel(sink_ref, q_ref, k_ref, v_ref, kc_ref, vc_ref, g_ref, o_ref, sa_ref, sb_ref):
    seq = q_ref.shape[0]
    n_blocks = seq // WA_BLOCK
    lc = kc_ref.shape[0]
    gain = g_ref[...]
    row = _iota((WA_BLOCK, WA_BAND), 0)
    col = _iota((WA_BLOCK, WA_BAND), 1)

    def window(n):
        q0 = pl.multiple_of(n * WA_BLOCK, WA_BLOCK)
        k0 = pl.multiple_of(jnp.clip(q0 - WA_BLOCK, 0, seq - WA_BAND), WA_BLOCK)
        return q0, k0

    def scores(n, kh, s_ref):
        q0, k0 = window(n)
        band = jnp.where(jnp.abs((k0 + col) - (q0 + row)) <= WA_WINDOW, 0.0, NEG_INF)
        lanes = slice(kh * LANES, (kh + 1) * LANES)
        qg = q_ref[pl.ds(q0, WA_BLOCK), kh * WA_GROUP * HEAD_DIM:(kh + 1) * WA_GROUP * HEAD_DIM]
        s_ctx, s_loc = _wa_scores(qg, [(kc_ref[:, lanes], None), (k_ref[pl.ds(k0, WA_BAND), lanes], band)])
        s_ref[kh, :, :lc] = s_ctx
        s_ref[kh, :, lc:lc + WA_BAND] = s_loc
        s_ref[kh, :, lc + WA_BAND:] = _row_max([s_ctx, s_loc], _wa_sink_col(WA_BLOCK, sinks(kh)))

    def sinks(kh):
        return [sink_ref[kh * WA_GROUP + g] for g in range(WA_GROUP)]

    def finish(n, kh, s_ref):
        _, k0 = window(n)
        vlanes = slice(kh * 2 * LANES, (kh + 1) * 2 * LANES)
        return _wa_finish([s_ref[kh, :, :lc], s_ref[kh, :, lc:lc + WA_BAND]],
                          [vc_ref[:, vlanes], v_ref[pl.ds(k0, WA_BAND), vlanes]], sinks(kh),
                          m=s_ref[kh, :, lc + WA_BAND:])

    def step(n, cur_ref, nxt_ref):
        nxt = jnp.minimum(n + 1, n_blocks - 1)
        for kh in range(WA_KV_HEADS):
            scores(nxt, kh, nxt_ref)
        o = jnp.concatenate([finish(n, kh, cur_ref) for kh in range(WA_KV_HEADS)], axis=1)
        q0, _ = window(n)
        o_ref[pl.ds(q0, WA_BLOCK), :] = _group_rms(o, gain).astype(o_ref.dtype)

    for kh in range(WA_KV_HEADS):
        scores(0, kh, sa_ref)

    def body(i, carry):
        step(2 * i, sa_ref, sb_ref)
        step(2 * i + 1, sb_ref, sa_ref)
        return carry

    lax.fori_loop(0, n_blocks // 2, body, 0)


def _wa_attention(sink, q, k2, v2, kc2, vc2, gain):
    batch, seq, _ = q.shape
    lc = kc2.shape[1]
    assert (seq // WA_BLOCK) % 2 == 0
    per_b = lambda n, w: pl.BlockSpec((None, n, w), lambda b: (b, 0, 0))
    score_scratch = pltpu.VMEM((WA_KV_HEADS, WA_GROUP * WA_BLOCK, lc + WA_BAND + LANES), F32)
    return pl.pallas_call(
        _wa_kernel,
        out_shape=jax.ShapeDtypeStruct((batch, seq, WA_QW), BF16),
        grid=(batch,),
        in_specs=[pl.BlockSpec(memory_space=pltpu.SMEM),
                  per_b(seq, WA_QW), per_b(seq, WA_K2W), per_b(seq, WA_V3W),
                  per_b(lc, WA_K2W), per_b(lc, WA_V3W),
                  pl.BlockSpec((1, WA_QW), lambda b: (0, 0))],
        out_specs=per_b(seq, WA_QW),
        scratch_shapes=[score_scratch, score_scratch],
        compiler_params=_params(),
        name="wa_attention",
    )(sink, q, k2, v2, kc2, vc2, gain)


def _ctx_attn_kernel(sink_ref, qn_ref, kn_ref, vn_ref, qw_ref, kw_ref, vw_ref, gn_ref, gw_ref, on_ref, ow_ref):
    qs = _stack_heads(qn_ref[...], NA_HEADS)
    o4 = _softmax_pv([(_dot_nt(qs, kn_ref[...]), vn_ref[...])])
    on_ref[...] = _group_rms(_unstack_heads(o4, NA_HEADS), gn_ref[...]).astype(on_ref.dtype)
    outs = []
    for kh in range(WA_KV_HEADS):
        lanes = slice(kh * LANES, (kh + 1) * LANES)
        vlanes = slice(kh * 2 * LANES, (kh + 1) * 2 * LANES)
        qg = qw_ref[:, kh * WA_GROUP * HEAD_DIM:(kh + 1) * WA_GROUP * HEAD_DIM]
        sinks = [sink_ref[kh * WA_GROUP + g] for g in range(WA_GROUP)]
        outs.append(_wa_finish(_wa_scores(qg, [(kw_ref[:, lanes], None)]), [vw_ref[:, vlanes]], sinks))
    ow_ref[...] = _group_rms(jnp.concatenate(outs, axis=1), gw_ref[...]).astype(ow_ref.dtype)


def _ctx_attention(sink, qn, kn, vn, qw, kw2, vw2, gain_na, gain_wa):
    batch, lc, _ = qn.shape
    per_b = lambda w: pl.BlockSpec((None, lc, w), lambda b: (b, 0, 0))
    return pl.pallas_call(
        _ctx_attn_kernel,
        out_shape=[jax.ShapeDtypeStruct((batch, lc, NA_WIDTH), BF16),
                   jax.ShapeDtypeStruct((batch, lc, WA_QW), BF16)],
        grid=(batch,),
        in_specs=[pl.BlockSpec(memory_space=pltpu.SMEM),
                  per_b(NA_WIDTH), per_b(NA_WIDTH), per_b(NA_WIDTH),
                  per_b(WA_QW), per_b(WA_K2W), per_b(WA_V3W),
                  pl.BlockSpec((1, NA_WIDTH), lambda b: (0, 0)),
                  pl.BlockSpec((1, WA_QW), lambda b: (0, 0))],
        out_specs=[per_b(NA_WIDTH), per_b(WA_QW)],
        compiler_params=_params(),
        name="ctx_attention",
    )(sink, qn, kn, vn, qw, kw2, vw2, gain_na, gain_wa)


def _out_mlp_kernel(x_ref, mod_ref, na_ref, wa_ref, u_ref, uprev_ref, unext_ref, cvb_ref, cw_ref, cb_ref, g_ref,
                    wo_ref, g2_ref, w1_ref, w2_ref, o_ref, *, seq_len, sub):
    tm = x_ref.shape[0]
    i = pl.program_id(0)
    u = u_ref[...]
    at_start = (i * tm) % seq_len == 0
    at_end = ((i + 1) * tm) % seq_len == 0
    prev_row = jnp.where(at_start, 0.0, uprev_ref[7:8, :])
    next_row = jnp.where(at_end, 0.0, unext_ref[0:1, :])
    row = _iota(u.shape, 0)
    up = jnp.where(row == 0, prev_row, pltpu.roll(u, 1, axis=0))
    dn = jnp.where(row == tm - 1, next_row, pltpu.roll(u, tm - 1, axis=0))
    y = cb_ref[...] + cw_ref[0:1, :] * up + cw_ref[1:2, :] * u + cw_ref[2:3, :] * dn
    cv = _group_rms(cvb_ref[...] * y, g_ref[...]).astype(BF16)
    for r0 in range(0, tm, sub):
        rows = slice(r0, r0 + sub)
        mixed = jnp.concatenate([na_ref[rows, :], cv[rows, :], wa_ref[rows, :]], axis=1)
        x = x_ref[rows, :] + mod_ref[2] * _dot(mixed, wo_ref[...])
        ms = jnp.mean(x * x, axis=-1, keepdims=True)
        h = x * lax.rsqrt(ms + EPS) * g2_ref[...]
        h = (h * (1.0 + mod_ref[4]) + mod_ref[3]).astype(BF16)
        acc = None
        for c in range(0, FFN_DIM, FFN_CHUNK):
            a = jnp.maximum(_dot(h, w1_ref[:, c:c + FFN_CHUNK]), 0.0)
            part = _dot((a * a).astype(BF16), w2_ref[c:c + FFN_CHUNK, :])
            acc = part if acc is None else acc + part
        o_ref[rows, :] = x + mod_ref[5] * acc


def _out_mlp(x2d, mod_l, mod_row, na_n, wa_n, cv_u, cv_b, conv_w, conv_b, gain_cv, w_o, g2, w1, w2, seq_len, tm):
    rows = x2d.shape[0]
    halo = 8
    per_tile = tm // halo
    last = rows // halo - 1
    row_spec = lambda w: pl.BlockSpec((tm, w), lambda i: (i, 0))
    const = lambda shape: pl.BlockSpec(shape, lambda i: (0,) * len(shape), pipeline_mode=pl.Buffered(1))
    return pl.pallas_call(
        functools.partial(_out_mlp_kernel, seq_len=seq_len, sub=min(tm, MLP_SUB)),
        out_shape=jax.ShapeDtypeStruct((rows, D_MODEL), F32),
        grid=(rows // tm,),
        in_specs=[
            row_spec(D_MODEL),
            pl.BlockSpec((None, 6, 1, D_MODEL), lambda i: (mod_row(i), 0, 0, 0)),
            row_spec(NA_WIDTH), row_spec(WA_QW), row_spec(CONV_CH),
            pl.BlockSpec((halo, CONV_CH), lambda i: (jnp.maximum(i * per_tile - 1, 0), 0)),
            pl.BlockSpec((halo, CONV_CH), lambda i: (jnp.minimum((i + 1) * per_tile, last), 0)),
            row_spec(CONV_CH),
            const((3, CONV_CH)), const((1, CONV_CH)), const((1, CONV_CH)),
            const((D_MODEL, D_MODEL)),
            const((1, D_MODEL)),
            const((D_MODEL, FFN_DIM)),
            const((FFN_DIM, D_MODEL)),
        ],
        out_specs=row_spec(D_MODEL),
        compiler_params=_params(),
        name="out_mlp",
    )(x2d, mod_l, na_n, wa_n, cv_u, cv_u, cv_u, cv_b, conv_w, conv_b, gain_cv, w_o, g2, w1, w2)


def _rope_tables(seq):
    quarter = HEAD_DIM // 4
    inv = ROPE_BASE ** (-jnp.arange(quarter, dtype=F32) / quarter)
    t = jnp.arange(seq)
    ang_r = (t // GRID_W).astype(F32)[:, None] * inv[None, :]
    ang_c = (t % GRID_W).astype(F32)[:, None] * inv[None, :]
    cos = jnp.concatenate([jnp.cos(ang_r), jnp.cos(ang_r), jnp.cos(ang_c), jnp.cos(ang_c)], axis=1)
    sin = jnp.concatenate([-jnp.sin(ang_r), jnp.sin(ang_r), -jnp.sin(ang_c), jnp.sin(ang_c)], axis=1)
    reps = LANES // HEAD_DIM
    return jnp.tile(cos, (1, reps)), jnp.tile(sin, (1, reps))


def kernel(x, c, ctx, c_ctx, w_mod, b_mod, g_norm1, g_norm2, w_in, na_q_gain, na_k_gain, na_rpb,
           conv_w, conv_bias, wa_q_gain, wa_k_gain, wa_sink, g_out, w_o, w_fc1, w_fc2):
    batch, seq, d = x.shape
    lc = ctx.shape[1]
    depth = w_mod.shape[0]
    assert d == D_MODEL and seq % WA_BLOCK == 0 and seq % GRID_W == 0 and batch < MOD_ROWS
    tm = 512
    tm_in = 1024
    tm_ctx = 256
    assert seq % tm == 0 and seq % tm_in == 0 and lc % tm_ctx == 0

    cond = jnp.zeros((MOD_ROWS, d), F32).at[:batch].set(c).at[batch].set(c_ctx)
    mod = _modulation(cond, w_mod, b_mod).reshape(depth, MOD_ROWS, 6, 1, d)
    rpb_bias = _rpb_tables(na_rpb)
    rope_tabs = _rope_tables(seq)

    lat_row = lambda i: i // (seq // tm)
    lat_row_in = lambda i: i // (seq // tm_in)
    ctx_row = lambda i: batch

    xs = x.reshape(batch * seq, d)
    cs = ctx.reshape(batch * lc, d)
    for l in range(depth):
        last = l == depth - 1
        w_in_l = w_in[l].astype(BF16)
        w_o_l = w_o[l].astype(BF16)
        w1_l = w_fc1[l].astype(BF16)
        w2_l = w_fc2[l].astype(BF16)
        g1 = g_norm1[l].reshape(1, d)
        g2 = g_norm2[l].reshape(1, d)
        gains = [jnp.tile(na_q_gain[l], NA_HEADS).reshape(1, -1), jnp.tile(na_k_gain[l], NA_HEADS).reshape(1, -1),
                 jnp.tile(wa_q_gain[l], WA_HEADS).reshape(1, -1), jnp.tile(wa_k_gain[l], WA_KV_HEADS).reshape(1, -1)]
        go_na = g_out[l, :NA_WIDTH].reshape(1, -1)
        go_cv = g_out[l, NA_WIDTH:NA_WIDTH + CONV_CH].reshape(1, -1)
        go_wa = g_out[l, NA_WIDTH + CONV_CH:].reshape(1, -1)
        cb = conv_bias[l].reshape(1, -1)

        naq, nak, nav, cvu, cvb, waq, wak, wav = _in_proj(xs, mod[l], lat_row_in, g1, w_in_l, gains, rope_tabs, tm_in)
        cnaq, cnak, cnav, ccvu, ccvb, cwaq, cwak, cwav = _in_proj(cs, mod[l], ctx_row, g1, w_in_l, gains, None, tm_ctx)

        b3 = lambda a, n: a.reshape(batch, n, a.shape[-1])
        na_n = _na_attention(b3(naq, seq), b3(nak, seq), b3(nav, seq), b3(cnak, lc), b3(cnav, lc), rpb_bias[l], go_na)
        wa_n = _wa_attention(wa_sink[l], b3(waq, seq), b3(wak, seq), b3(wav, seq), b3(cwak, lc), b3(cwav, lc), go_wa)
        xs = _out_mlp(xs, mod[l], lat_row, na_n.reshape(batch * seq, -1), wa_n.reshape(batch * seq, -1),
                      cvu, cvb, conv_w[l], cb, go_cv, w_o_l, g2, w1_l, w2_l, seq, tm)
        if not last:
            cna_n, cwa_n = _ctx_attention(wa_sink[l], b3(cnaq, lc), b3(cnak, lc), b3(cnav, lc),
                                          b3(cwaq, lc), b3(cwak, lc), b3(cwav, lc), go_na, go_wa)
            cs = _out_mlp(cs, mod[l], ctx_row, cna_n.reshape(batch * lc, -1), cwa_n.reshape(batch * lc, -1),
                          ccvu, ccvb, conv_w[l], cb, go_cv, w_o_l, g2, w1_l, w2_l, lc, tm_ctx)
    return xs.reshape(batch, seq, d)
```

```python
import functools

import jax
import jax.numpy as jnp
from jax import lax
from jax.experimental import pallas as pl
from jax.experimental.pallas import tpu as pltpu

D_MODEL = 1024
GRID_W = 64
HEAD_DIM = 64
NA_HEADS = 4
NA_WIDTH = NA_HEADS * HEAD_DIM
CONV_CH = 256
WA_HEADS = 8
WA_KV_HEADS = 2
WA_GROUP = WA_HEADS // WA_KV_HEADS
WA_QW = WA_HEADS * HEAD_DIM
WA_KVW = WA_KV_HEADS * HEAD_DIM
WA_K2W = 2 * WA_KVW
WA_V3W = 4 * WA_KVW
NA_WIN_ROWS = 8
NA_WIN_COLS = 16
NA_DROWS = 2 * NA_WIN_ROWS - 1
NA_DCOLS = 2 * NA_WIN_COLS - 1
WA_WINDOW = 128
WA_BLOCK = 128
WA_BAND = 3 * WA_BLOCK
FFN_DIM = 4 * D_MODEL
FFN_CHUNK = 1024
ROPE_BASE = 10000.0
EPS = 1e-6
NEG_INF = -1e30
IN_WIDTH = 2304
OFF_NA_Q, OFF_NA_K, OFF_NA_V = 0, 256, 512
OFF_CV_X, OFF_CV_B, OFF_CV_C = 768, 1024, 1280
OFF_WA_Q, OFF_WA_K, OFF_WA_V = 1536, 2048, 2176
MOD_ROWS = 16
MOD_TN = 1024
IN_SUB = 256
MLP_SUB = 256
LANES = 128
MXU_DIM = 256
VMEM_LIMIT = 60 * 1024 * 1024
LOG2E = 1.4426950408889634
Q_SCALE = LOG2E * HEAD_DIM ** -0.5

F32 = jnp.float32
BF16 = jnp.bfloat16


def _dot(a, b):
    return jnp.dot(a, b, preferred_element_type=F32)


def _dot_nt(a, b):
    return lax.dot_general(a, b, (((1,), (1,)), ((), ())), preferred_element_type=F32)


def _split_bf16(a):
    hi = a.astype(BF16)
    lo = (a - hi.astype(F32)).astype(BF16)
    return hi, lo


def _iota(shape, dim):
    return lax.broadcasted_iota(jnp.int32, shape, dim)


def _params(**kw):
    return pltpu.CompilerParams(vmem_limit_bytes=VMEM_LIMIT, **kw)


def _mod_kernel(cond_ref, w_ref, b_ref, o_ref):
    a = cond_ref[...]
    a = a * (1.0 / (1.0 + jnp.exp(-a)))
    ah, al = _split_bf16(a)
    wh, wl = _split_bf16(w_ref[...])
    o_ref[...] = _dot(ah, wh) + _dot(al, wh) + _dot(ah, wl) + b_ref[...]


def _modulation(cond, w_mod, b_mod):
    depth = w_mod.shape[0]
    n_out = w_mod.shape[2]
    return pl.pallas_call(
        _mod_kernel,
        out_shape=jax.ShapeDtypeStruct((depth, MOD_ROWS, n_out), F32),
        grid=(depth, n_out // MOD_TN),
        in_specs=[
            pl.BlockSpec((MOD_ROWS, D_MODEL), lambda l, j: (0, 0)),
            pl.BlockSpec((None, D_MODEL, MOD_TN), lambda l, j: (l, 0, j)),
            pl.BlockSpec((None, 1, MOD_TN), lambda l, j: (l, 0, j)),
        ],
        out_specs=pl.BlockSpec((None, MOD_ROWS, MOD_TN), lambda l, j: (l, 0, j)),
        compiler_params=_params(),
        name="modulation",
    )(cond, w_mod, b_mod.reshape(depth, 1, n_out))


def _rpb_kernel(rpb_ref, o_ref):
    l = pl.program_id(0)
    shape = (GRID_W, LANES)
    q = _iota(shape, 0)
    lane = _iota(shape, 1)
    k = lane % GRID_W
    left = lane < GRID_W
    dc = jnp.clip(k - q, -(NA_WIN_COLS - 1), NA_WIN_COLS - 1) + (NA_WIN_COLS - 1)
    col_start = jnp.clip(q - NA_WIN_COLS // 2, 0, GRID_W - NA_WIN_COLS)
    col_ok = (k >= col_start) & (k < col_start + NA_WIN_COLS)
    base_l = l * (NA_HEADS * NA_DROWS * NA_DCOLS)

    def body(d, carry):
        for h in range(NA_HEADS):
            base = base_l + (h * NA_DROWS + d) * NA_DCOLS
            t = jnp.zeros(shape, F32)
            for i in range(NA_DCOLS):
                val = jnp.where(left, rpb_ref[base + i], rpb_ref[base + NA_DCOLS + i])
                t = jnp.where(dc == i, val, t)
            o_ref[d, h * GRID_W:(h + 1) * GRID_W, :] = jnp.where(col_ok, t * LOG2E, NEG_INF)
        return carry

    lax.fori_loop(0, NA_DROWS - 1, body, 0)


def _rpb_tables(na_rpb):
    depth = na_rpb.shape[0]
    return pl.pallas_call(
        _rpb_kernel,
        out_shape=jax.ShapeDtypeStruct((depth, NA_DROWS - 1, NA_HEADS * GRID_W, LANES), F32),
        grid=(depth,),
        in_specs=[pl.BlockSpec(memory_space=pltpu.SMEM)],
        out_specs=pl.BlockSpec((None, NA_DROWS - 1, NA_HEADS * GRID_W, LANES), lambda l: (l, 0, 0, 0)),
        compiler_params=_params(),
        name="rpb_tables",
    )(na_rpb.reshape(-1))


def _head_rms(t, gain):
    width = t.shape[1]
    sq = (t * t).astype(BF16)
    blk = min(width, MXU_DIM)
    ones = jnp.where(_iota((blk, blk), 0) // HEAD_DIM == _iota((blk, blk), 1) // HEAD_DIM, 1.0, 0.0).astype(BF16)
    sums = [_dot(sq[:, c:c + blk], ones) for c in range(0, width, blk)]
    ss = sums[0] if len(sums) == 1 else jnp.concatenate(sums, axis=1)
    return t * lax.rsqrt(ss * (1.0 / HEAD_DIM) + EPS) * gain


def _rope(t, cos, sin):
    lane = _iota((t.shape[0], LANES), 1)
    first = (lane % (HEAD_DIM // 2)) < (HEAD_DIM // 4)
    out = []
    for c in range(0, t.shape[1], LANES):
        u = t[:, c:c + LANES]
        partner = jnp.where(first, pltpu.roll(u, LANES - HEAD_DIM // 4, axis=1), pltpu.roll(u, HEAD_DIM // 4, axis=1))
        out.append(u * cos + partner * sin)
    return out[0] if len(out) == 1 else jnp.concatenate(out, axis=1)


def _dup_kv(t):
    lane = _iota(t.shape, 1)
    swapped = pltpu.roll(t, HEAD_DIM, axis=1)
    left = lane < HEAD_DIM
    return jnp.concatenate([jnp.where(left, t, swapped), jnp.where(left, swapped, t)], axis=1)


def _inproj_kernel(x_ref, mod_ref, g1_ref, w_ref, gnq_ref, gnk_ref, gwq_ref, gwk_ref, *rest, rope, sub):
    if rope:
        cos_ref, sin_ref = rest[:2]
        rest = rest[2:]
    naq_ref, nak_ref, nav_ref, cvu_ref, cvb_ref, waq_ref, wak_ref, wav_ref = rest
    for r0 in range(0, x_ref.shape[0], sub):
        rows = slice(r0, r0 + sub)
        x = x_ref[rows, :]
        ms = jnp.mean(x * x, axis=-1, keepdims=True)
        h = x * lax.rsqrt(ms + EPS) * g1_ref[...]
        h = h * (1.0 + mod_ref[1]) + mod_ref[0]
        p = _dot(h.astype(BF16), w_ref[...])
        naq_ref[rows, :] = (_head_rms(p[:, OFF_NA_Q:OFF_NA_K], gnq_ref[...]) * Q_SCALE).astype(BF16)
        nak_ref[rows, :] = _head_rms(p[:, OFF_NA_K:OFF_NA_V], gnk_ref[...]).astype(BF16)
        nav_ref[rows, :] = p[:, OFF_NA_V:OFF_CV_X].astype(BF16)
        cvu_ref[rows, :] = p[:, OFF_CV_C:OFF_WA_Q] * p[:, OFF_CV_X:OFF_CV_B]
        cvb_ref[rows, :] = p[:, OFF_CV_B:OFF_CV_C]
        wq = _head_rms(p[:, OFF_WA_Q:OFF_WA_K], gwq_ref[...])
        wk = _head_rms(p[:, OFF_WA_K:OFF_WA_V], gwk_ref[...])
        if rope:
            wq = _rope(wq, cos_ref[rows, :], sin_ref[rows, :])
            wk = _rope(wk, cos_ref[rows, :], sin_ref[rows, :])
        waq_ref[rows, :] = (wq * Q_SCALE).astype(BF16)
        wak_ref[rows, :] = _dup_kv(wk).astype(BF16)
        v2 = _dup_kv(p[:, OFF_WA_V:IN_WIDTH]).astype(BF16)
        ones = jnp.ones((sub, LANES), BF16)
        wav_ref[rows, :] = jnp.concatenate([v2[:, :LANES], ones, v2[:, LANES:], ones], axis=1)


def _in_proj(x2d, mod_l, mod_row, g1, w_in, gains, rope_tabs, tm):
    rows = x2d.shape[0]
    rope = rope_tabs is not None
    row_spec = lambda w: pl.BlockSpec((tm, w), lambda i: (i, 0))
    const = lambda shape: pl.BlockSpec(shape, lambda i: (0,) * len(shape))
    in_specs = [
        row_spec(D_MODEL),
        pl.BlockSpec((None, 6, 1, D_MODEL), lambda i: (mod_row(i), 0, 0, 0)),
        const((1, D_MODEL)),
        const((D_MODEL, IN_WIDTH)),
        const((1, NA_WIDTH)), const((1, NA_WIDTH)), const((1, WA_QW)), const((1, WA_KVW)),
    ]
    args = [x2d, mod_l, g1, w_in, *gains]
    if rope:
        seq_tiles = rope_tabs[0].shape[0] // tm
        in_specs += [pl.BlockSpec((tm, LANES), lambda i: (i % seq_tiles, 0))] * 2
        args += list(rope_tabs)
    widths = [(NA_WIDTH, BF16), (NA_WIDTH, BF16), (NA_WIDTH, BF16), (CONV_CH, F32), (CONV_CH, F32),
              (WA_QW, BF16), (WA_K2W, BF16), (WA_V3W, BF16)]
    return pl.pallas_call(
        functools.partial(_inproj_kernel, rope=rope, sub=min(tm, IN_SUB)),
        out_shape=[jax.ShapeDtypeStruct((rows, w), dt) for w, dt in widths],
        grid=(rows // tm,),
        in_specs=in_specs,
        out_specs=[row_spec(w) for w, _ in widths],
        compiler_params=_params(),
        name="in_proj_rope" if rope else "in_proj_ctx",
    )(*args)


def _lane_chunks(s):
    return [s[:, c:c + LANES] for c in range(0, s.shape[1], LANES)]


def _row_max(scores, extra=None):
    chunks = [ch for s in scores for ch in _lane_chunks(s)]
    if extra is not None:
        chunks.append(extra)
    folded = functools.reduce(jnp.maximum, chunks)
    return jnp.broadcast_to(jnp.max(folded, axis=-1, keepdims=True), folded.shape)


def _softmax_pv(parts, extra=None, sums_from_values=False, m=None):
    lane_chunks = _lane_chunks
    if m is None:
        m = _row_max([s for s, _ in parts], extra)
    ps = [jnp.concatenate([jnp.exp2(ch - m) for ch in lane_chunks(s)], axis=1) for s, _ in parts]
    pv = _dot(jnp.concatenate([p.astype(BF16) for p in ps], axis=1),
              jnp.concatenate([v for _, v in parts], axis=0))
    if sums_from_values:
        denom = pv[:, -LANES:]
        pv = pv[:, :-LANES]
        if extra is not None:
            denom = denom + jnp.exp2(extra - m)
        return pv / denom
    psum = functools.reduce(jnp.add, [ch for p in ps for ch in lane_chunks(p)])
    if extra is not None:
        psum = psum + jnp.where(_iota(extra.shape, 1) == 0, jnp.exp2(extra - m), 0.0)
    return pv / jnp.sum(psum, axis=-1, keepdims=True)


def _stack_heads(q, n_heads):
    head = _iota(q.shape, 1) // HEAD_DIM
    zero = jnp.zeros_like(q)
    return jnp.concatenate([jnp.where(head == h, q, zero) for h in range(n_heads)], axis=0)


def _unstack_heads(o, n_heads):
    rows = o.shape[0] // n_heads
    head = _iota((rows, o.shape[1]), 1) // HEAD_DIM
    out = jnp.where(head == 0, o[:rows], 0.0)
    for h in range(1, n_heads):
        out = jnp.where(head == h, o[h * rows:(h + 1) * rows], out)
    return out


def _group_rms(o, gain):
    ms = jnp.mean(o * o, axis=-1, keepdims=True)
    return o * lax.rsqrt(ms + EPS) * gain


def _wa_scores(qg, keys):
    rows = qg.shape[0]
    left = _iota((rows, LANES), 1) < HEAD_DIM
    zero = jnp.zeros((rows, LANES), qg.dtype)
    blocks = []
    for pair in range(WA_GROUP // 2):
        qp = qg[:, pair * LANES:(pair + 1) * LANES]
        blocks += [jnp.where(left, qp, zero), jnp.where(left, zero, qp)]
    qs = jnp.concatenate(blocks, axis=0)
    scores = []
    for k2, bias in keys:
        s = _dot_nt(qs, k2)
        if bias is not None:
            s = jnp.concatenate([s[g * rows:(g + 1) * rows] + bias for g in range(WA_GROUP)], axis=0)
        scores.append(s)
    return scores


def _wa_sink_col(rows, sinks):
    return jnp.concatenate([jnp.full((rows, LANES), sinks[g] * LOG2E, F32) for g in range(WA_GROUP)], axis=0)


def _wa_finish(scores, values, sinks, m=None):
    rows = scores[0].shape[0] // WA_GROUP
    left = _iota((rows, LANES), 1) < HEAD_DIM
    o2 = _softmax_pv(list(zip(scores, values)), extra=_wa_sink_col(rows, sinks), sums_from_values=True, m=m)
    pairs = [jnp.where(left, o2[(2 * pr) * rows:(2 * pr + 1) * rows], o2[(2 * pr + 1) * rows:(2 * pr + 2) * rows])
             for pr in range(WA_GROUP // 2)]
    return jnp.concatenate(pairs, axis=1)


def _na_kernel(q_ref, k_ref, v_ref, kc_ref, vc_ref, bias_ref, g_ref, o_ref, sa_ref, sb_ref):
    n_rows = q_ref.shape[0] // GRID_W
    win = NA_WIN_ROWS * GRID_W
    lc = kc_ref.shape[0]
    gain = g_ref[...]

    def window(r):
        start = jnp.clip(r - NA_WIN_ROWS // 2, 0, n_rows - NA_WIN_ROWS)
        return pl.multiple_of(start * GRID_W, GRID_W), start - r + (NA_WIN_ROWS - 1)

    def scores(r, s_ref):
        tok0, d0 = window(r)
        q = q_ref[pl.ds(pl.multiple_of(r * GRID_W, GRID_W), GRID_W), :]
        qs = _stack_heads(q, NA_HEADS)
        bias = jnp.concatenate([bias_ref[d0 + 2 * j] for j in range(NA_WIN_ROWS // 2)], axis=1)
        s_ctx = _dot_nt(qs, kc_ref[...])
        s_loc = _dot_nt(qs, k_ref[pl.ds(tok0, win), :]) + bias
        s_ref[:, :lc] = s_ctx
        s_ref[:, lc:lc + win] = s_loc
        s_ref[:, lc + win:] = _row_max([s_ctx, s_loc])

    def finish(r, s_ref):
        tok0, _ = window(r)
        o4 = _softmax_pv([(s_ref[:, :lc], vc_ref[...]), (s_ref[:, lc:lc + win], v_ref[pl.ds(tok0, win), :])],
                         m=s_ref[:, lc + win:])
        o = _unstack_heads(o4, NA_HEADS)
        o_ref[pl.ds(pl.multiple_of(r * GRID_W, GRID_W), GRID_W), :] = _group_rms(o, gain).astype(o_ref.dtype)

    def step(r, cur_ref, nxt_ref):
        scores(jnp.minimum(r + 1, n_rows - 1), nxt_ref)
        finish(r, cur_ref)

    scores(0, sa_ref)

    def body(i, carry):
        step(2 * i, sa_ref, sb_ref)
        step(2 * i + 1, sb_ref, sa_ref)
        return carry

    lax.fori_loop(0, n_rows // 2, body, 0, unroll=4)


def _na_attention(q, k, v, kc, vc, bias, gain):
    batch, seq, _ = q.shape
    lc = kc.shape[1]
    assert (seq // GRID_W) % 4 == 0
    per_b = lambda n: pl.BlockSpec((None, n, NA_WIDTH), lambda b: (b, 0, 0))
    score_scratch = pltpu.VMEM((NA_HEADS * GRID_W, lc + NA_WIN_ROWS * GRID_W + LANES), F32)
    return pl.pallas_call(
        _na_kernel,
        out_shape=jax.ShapeDtypeStruct((batch, seq, NA_WIDTH), BF16),
        grid=(batch,),
        in_specs=[per_b(seq), per_b(seq), per_b(seq), per_b(lc), per_b(lc),
                  pl.BlockSpec(bias.shape, lambda b: (0, 0, 0)),
                  pl.BlockSpec((1, NA_WIDTH), lambda b: (0, 0))],
        out_specs=per_b(seq),
        scratch_shapes=[score_scratch, score_scratch],
        compiler_params=_params(),
        name="na_attention",
    )(q, k, v, kc, vc, bias, gain)


def _wa_kernel(sink_ref, q_ref, k_ref, v_ref, kc_ref, vc_ref, g_ref, o_ref, sa_ref, sb_ref):
    seq = q_ref.shape[0]
    n_blocks = seq // WA_BLOCK
    lc = kc_ref.shape[0]
    gain = g_ref[...]
    row = _iota((WA_BLOCK, WA_BAND), 0)
    col = _iota((WA_BLOCK, WA_BAND), 1)

    def window(n):
        q0 = pl.multiple_of(n * WA_BLOCK, WA_BLOCK)
        k0 = pl.multiple_of(jnp.clip(q0 - WA_BLOCK, 0, seq - WA_BAND), WA_BLOCK)
        return q0, k0

    def scores(n, kh, s_ref):
        q0, k0 = window(n)
        band = jnp.where(jnp.abs((k0 + col) - (q0 + row)) <= WA_WINDOW, 0.0, NEG_INF)
        lanes = slice(kh * LANES, (kh + 1) * LANES)
        qg = q_ref[pl.ds(q0, WA_BLOCK), kh * WA_GROUP * HEAD_DIM:(kh + 1) * WA_GROUP * HEAD_DIM]
        s_ctx, s_loc = _wa_scores(qg, [(kc_ref[:, lanes], None), (k_ref[pl.ds(k0, WA_BAND), lanes], band)])
        s_ref[kh, :, :lc] = s_ctx
        s_ref[kh, :, lc:lc + WA_BAND] = s_loc
        s_ref[kh, :, lc + WA_BAND:] = _row_max([s_ctx, s_loc], _wa_sink_col(WA_BLOCK, sinks(kh)))

    def sinks(kh):
        return [sink_ref[kh * WA_GROUP + g] for g in range(WA_GROUP)]

    def finish(n, kh, s_ref):
        _, k0 = window(n)
        vlanes = slice(kh * 2 * LANES, (kh + 1) * 2 * LANES)
        return _wa_finish([s_ref[kh, :, :lc], s_ref[kh, :, lc:lc + WA_BAND]],
                          [vc_ref[:, vlanes], v_ref[pl.ds(k0, WA_BAND), vlanes]], sinks(kh),
                          m=s_ref[kh, :, lc + WA_BAND:])

    def step(n, cur_ref, nxt_ref):
        nxt = jnp.minimum(n + 1, n_blocks - 1)
        for kh in range(WA_KV_HEADS):
            scores(nxt, kh, nxt_ref)
        o = jnp.concatenate([finish(n, kh, cur_ref) for kh in range(WA_KV_HEADS)], axis=1)
        q0, _ = window(n)
        o_ref[pl.ds(q0, WA_BLOCK), :] = _group_rms(o, gain).astype(o_ref.dtype)

    for kh in range(WA_KV_HEADS):
        scores(0, kh, sa_ref)

    def body(i, carry):
        step(2 * i, sa_ref, sb_ref)
        step(2 * i + 1, sb_ref, sa_ref)
        return carry

    lax.fori_loop(0, n_blocks // 2, body, 0)


def _wa_attention(sink, q, k2, v2, kc2, vc2, gain):
    batch, seq, _ = q.shape
    lc = kc2.shape[1]
    assert (seq // WA_BLOCK) % 2 == 0
    per_b = lambda n, w: pl.BlockSpec((None, n, w), lambda b: (b, 0, 0))
    score_scratch = pltpu.VMEM((WA_KV_HEADS, WA_GROUP * WA_BLOCK, lc + WA_BAND + LANES), F32)
    return pl.pallas_call(
        _wa_kernel,
        out_shape=jax.ShapeDtypeStruct((batch, seq, WA_QW), BF16),
        grid=(batch,),
        in_specs=[pl.BlockSpec(memory_space=pltpu.SMEM),
                  per_b(seq, WA_QW), per_b(seq, WA_K2W), per_b(seq, WA_V3W),
                  per_b(lc, WA_K2W), per_b(lc, WA_V3W),
                  pl.BlockSpec((1, WA_QW), lambda b: (0, 0))],
        out_specs=per_b(seq, WA_QW),
        scratch_shapes=[score_scratch, score_scratch],
        compiler_params=_params(),
        name="wa_attention",
    )(sink, q, k2, v2, kc2, vc2, gain)


def _ctx_attn_kernel(sink_ref, qn_ref, kn_ref, vn_ref, qw_ref, kw_ref, vw_ref, gn_ref, gw_ref, on_ref, ow_ref):
    qs = _stack_heads(qn_ref[...], NA_HEADS)
    o4 = _softmax_pv([(_dot_nt(qs, kn_ref[...]), vn_ref[...])])
    on_ref[...] = _group_rms(_unstack_heads(o4, NA_HEADS), gn_ref[...]).astype(on_ref.dtype)
    outs = []
    for kh in range(WA_KV_HEADS):
        lanes = slice(kh * LANES, (kh + 1) * LANES)
        vlanes = slice(kh * 2 * LANES, (kh + 1) * 2 * LANES)
        qg = qw_ref[:, kh * WA_GROUP * HEAD_DIM:(kh + 1) * WA_GROUP * HEAD_DIM]
        sinks = [sink_ref[kh * WA_GROUP + g] for g in range(WA_GROUP)]
        outs.append(_wa_finish(_wa_scores(qg, [(kw_ref[:, lanes], None)]), [vw_ref[:, vlanes]], sinks))
    ow_ref[...] = _group_rms(jnp.concatenate(outs, axis=1), gw_ref[...]).astype(ow_ref.dtype)


def _ctx_attention(sink, qn, kn, vn, qw, kw2, vw2, gain_na, gain_wa):
    batch, lc, _ = qn.shape
    per_b = lambda w: pl.BlockSpec((None, lc, w), lambda b: (b, 0, 0))
    return pl.pallas_call(
        _ctx_attn_kernel,
        out_shape=[jax.ShapeDtypeStruct((batch, lc, NA_WIDTH), BF16),
                   jax.ShapeDtypeStruct((batch, lc, WA_QW), BF16)],
        grid=(batch,),
        in_specs=[pl.BlockSpec(memory_space=pltpu.SMEM),
                  per_b(NA_WIDTH), per_b(NA_WIDTH), per_b(NA_WIDTH),
                  per_b(WA_QW), per_b(WA_K2W), per_b(WA_V3W),
                  pl.BlockSpec((1, NA_WIDTH), lambda b: (0, 0)),
                  pl.BlockSpec((1, WA_QW), lambda b: (0, 0))],
        out_specs=[per_b(NA_WIDTH), per_b(WA_QW)],
        compiler_params=_params(),
        name="ctx_attention",
    )(sink, qn, kn, vn, qw, kw2, vw2, gain_na, gain_wa)


def _out_mlp_kernel(x_ref, mod_ref, na_ref, wa_ref, u_ref, uprev_ref, unext_ref, cvb_ref, cw_ref, cb_ref, g_ref,
                    wo_ref, g2_ref, w1_ref, w2_ref, o_ref, *, seq_len, sub):
    tm = x_ref.shape[0]
    i = pl.program_id(0)
    u = u_ref[...]
    at_start = (i * tm) % seq_len == 0
    at_end = ((i + 1) * tm) % seq_len == 0
    prev_row = jnp.where(at_start, 0.0, uprev_ref[7:8, :])
    next_row = jnp.where(at_end, 0.0, unext_ref[0:1, :])
    row = _iota(u.shape, 0)
    up = jnp.where(row == 0, prev_row, pltpu.roll(u, 1, axis=0))
    dn = jnp.where(row == tm - 1, next_row, pltpu.roll(u, tm - 1, axis=0))
    y = cb_ref[...] + cw_ref[0:1, :] * up + cw_ref[1:2, :] * u + cw_ref[2:3, :] * dn
    cv = _group_rms(cvb_ref[...] * y, g_ref[...]).astype(BF16)
    for r0 in range(0, tm, sub):
        rows = slice(r0, r0 + sub)
        mixed = jnp.concatenate([na_ref[rows, :], cv[rows, :], wa_ref[rows, :]], axis=1)
        x = x_ref[rows, :] + mod_ref[2] * _dot(mixed, wo_ref[...])
        ms = jnp.mean(x * x, axis=-1, keepdims=True)
        h = x * lax.rsqrt(ms + EPS) * g2_ref[...]
        h = (h * (1.0 + mod_ref[4]) + mod_ref[3]).astype(BF16)
        acc = None
        for c in range(0, FFN_DIM, FFN_CHUNK):
            a = jnp.maximum(_dot(h, w1_ref[:, c:c + FFN_CHUNK]), 0.0)
            part = _dot((a * a).astype(BF16), w2_ref[c:c + FFN_CHUNK, :])
            acc = part if acc is None else acc + part
        o_ref[rows, :] = x + mod_ref[5] * acc


def _out_mlp(x2d, mod_l, mod_row, na_n, wa_n, cv_u, cv_b, conv_w, conv_b, gain_cv, w_o, g2, w1, w2, seq_len, tm):
    rows = x2d.shape[0]
    halo = 8
    per_tile = tm // halo
    last = rows // halo - 1
    row_spec = lambda w: pl.BlockSpec((tm, w), lambda i: (i, 0))
    const = lambda shape: pl.BlockSpec(shape, lambda i: (0,) * len(shape), pipeline_mode=pl.Buffered(1))
    return pl.pallas_call(
        functools.partial(_out_mlp_kernel, seq_len=seq_len, sub=min(tm, MLP_SUB)),
        out_shape=jax.ShapeDtypeStruct((rows, D_MODEL), F32),
        grid=(rows // tm,),
        in_specs=[
            row_spec(D_MODEL),
            pl.BlockSpec((None, 6, 1, D_MODEL), lambda i: (mod_row(i), 0, 0, 0)),
            row_spec(NA_WIDTH), row_spec(WA_QW), row_spec(CONV_CH),
            pl.BlockSpec((halo, CONV_CH), lambda i: (jnp.maximum(i * per_tile - 1, 0), 0)),
            pl.BlockSpec((halo, CONV_CH), lambda i: (jnp.minimum((i + 1) * per_tile, last), 0)),
            row_spec(CONV_CH),
            const((3, CONV_CH)), const((1, CONV_CH)), const((1, CONV_CH)),
            const((D_MODEL, D_MODEL)),
            const((1, D_MODEL)),
            const((D_MODEL, FFN_DIM)),
            const((FFN_DIM, D_MODEL)),
        ],
        out_specs=row_spec(D_MODEL),
        compiler_params=_params(),
        name="out_mlp",
    )(x2d, mod_l, na_n, wa_n, cv_u, cv_u, cv_u, cv_b, conv_w, conv_b, gain_cv, w_o, g2, w1, w2)


def _rope_tables(seq):
    quarter = HEAD_DIM // 4
    inv = ROPE_BASE ** (-jnp.arange(quarter, dtype=F32) / quarter)
    t = jnp.arange(seq)
    ang_r = (t // GRID_W).astype(F32)[:, None] * inv[None, :]
    ang_c = (t % GRID_W).astype(F32)[:, None] * inv[None, :]
    cos = jnp.concatenate([jnp.cos(ang_r), jnp.cos(ang_r), jnp.cos(ang_c), jnp.cos(ang_c)], axis=1)
    sin = jnp.concatenate([-jnp.sin(ang_r), jnp.sin(ang_r), -jnp.sin(ang_c), jnp.sin(ang_c)], axis=1)
    reps = LANES // HEAD_DIM
    return jnp.tile(cos, (1, reps)), jnp.tile(sin, (1, reps))


def kernel(x, c, ctx, c_ctx, w_mod, b_mod, g_norm1, g_norm2, w_in, na_q_gain, na_k_gain, na_rpb,
           conv_w, conv_bias, wa_q_gain, wa_k_gain, wa_sink, g_out, w_o, w_fc1, w_fc2):
    batch, seq, d = x.shape
    lc = ctx.shape[1]
    depth = w_mod.shape[0]
    assert d == D_MODEL and seq % WA_BLOCK == 0 and seq % GRID_W == 0 and batch < MOD_ROWS
    tm = 512
    tm_in = 1024
    tm_ctx = 256
    assert seq % tm == 0 and seq % tm_in == 0 and lc % tm_ctx == 0

    cond = jnp.zeros((MOD_ROWS, d), F32).at[:batch].set(c).at[batch].set(c_ctx)
    mod = _modulation(cond, w_mod, b_mod).reshape(depth, MOD_ROWS, 6, 1, d)
    rpb_bias = _rpb_tables(na_rpb)
    rope_tabs = _rope_tables(seq)

    lat_row = lambda i: i // (seq // tm)
    lat_row_in = lambda i: i // (seq // tm_in)
    ctx_row = lambda i: batch

    xs = x.reshape(batch * seq, d)
    cs = ctx.reshape(batch * lc, d)
    for l in range(depth):
        last = l == depth - 1
        w_in_l = w_in[l].astype(BF16)
        w_o_l = w_o[l].astype(BF16)
        w1_l = w_fc1[l].astype(BF16)
        w2_l = w_fc2[l].astype(BF16)
        g1 = g_norm1[l].reshape(1, d)
        g2 = g_norm2[l].reshape(1, d)
        gains = [jnp.tile(na_q_gain[l], NA_HEADS).reshape(1, -1), jnp.tile(na_k_gain[l], NA_HEADS).reshape(1, -1),
                 jnp.tile(wa_q_gain[l], WA_HEADS).reshape(1, -1), jnp.tile(wa_k_gain[l], WA_KV_HEADS).reshape(1, -1)]
        go_na = g_out[l, :NA_WIDTH].reshape(1, -1)
        go_cv = g_out[l, NA_WIDTH:NA_WIDTH + CONV_CH].reshape(1, -1)
        go_wa = g_out[l, NA_WIDTH + CONV_CH:].reshape(1, -1)
        cb = conv_bias[l].reshape(1, -1)

        naq, nak, nav, cvu, cvb, waq, wak, wav = _in_proj(xs, mod[l], lat_row_in, g1, w_in_l, gains, rope_tabs, tm_in)
        cnaq, cnak, cnav, ccvu, ccvb, cwaq, cwak, cwav = _in_proj(cs, mod[l], ctx_row, g1, w_in_l, gains, None, tm_ctx)

        b3 = lambda a, n: a.reshape(batch, n, a.shape[-1])
        na_n = _na_attention(b3(naq, seq), b3(nak, seq), b3(nav, seq), b3(cnak, lc), b3(cnav, lc), rpb_bias[l], go_na)
        wa_n = _wa_attention(wa_sink[l], b3(waq, seq), b3(wak, seq), b3(wav, seq), b3(cwak, lc), b3(cwav, lc), go_wa)
        xs = _out_mlp(xs, mod[l], lat_row, na_n.reshape(batch * seq, -1), wa_n.reshape(batch * seq, -1),
                      cvu, cvb, conv_w[l], cb, go_cv, w_o_l, g2, w1_l, w2_l, seq, tm)
        if not last:
            cna_n, cwa_n = _ctx_attention(wa_sink[l], b3(cnaq, lc), b3(cnak, lc), b3(cnav, lc),
                                          b3(cwaq, lc), b3(cwak, lc), b3(cwav, lc), go_na, go_wa)
            cs = _out_mlp(cs, mod[l], ctx_row, cna_n.reshape(batch * lc, -1), cwa_n.reshape(batch * lc, -1),
                          ccvu, ccvb, conv_w[l], cb, go_cv, w_o_l, g2, w1_l, w2_l, lc, tm_ctx)
    return xs.reshape(batch, seq, d)
```

```python
import functools

import jax
import jax.numpy as jnp
from jax import lax
from jax.experimental import pallas as pl
from jax.experimental.pallas import tpu as pltpu

D_MODEL = 1024
GRID_W = 64
HEAD_DIM = 64
NA_HEADS = 4
NA_WIDTH = NA_HEADS * HEAD_DIM
CONV_CH = 256
WA_HEADS = 8
WA_KV_HEADS = 2
WA_GROUP = WA_HEADS // WA_KV_HEADS
WA_QW = WA_HEADS * HEAD_DIM
WA_KVW = WA_KV_HEADS * HEAD_DIM
WA_K2W = 2 * WA_KVW
WA_V3W = 4 * WA_KVW
NA_WIN_ROWS = 8
NA_WIN_COLS = 16
NA_DROWS = 2 * NA_WIN_ROWS - 1
NA_DCOLS = 2 * NA_WIN_COLS - 1
WA_WINDOW = 128
WA_BLOCK = 128
WA_BAND = 3 * WA_BLOCK
FFN_DIM = 4 * D_MODEL
FFN_CHUNK = 1024
ROPE_BASE = 10000.0
EPS = 1e-6
NEG_INF = -1e30
IN_WIDTH = 2304
OFF_NA_Q, OFF_NA_K, OFF_NA_V = 0, 256, 512
OFF_CV_X, OFF_CV_B, OFF_CV_C = 768, 1024, 1280
OFF_WA_Q, OFF_WA_K, OFF_WA_V = 1536, 2048, 2176
MOD_ROWS = 16
MOD_TN = 1024
IN_SUB = 256
MLP_SUB = 256
LANES = 128
MXU_DIM = 256
VMEM_LIMIT = 60 * 1024 * 1024
LOG2E = 1.4426950408889634
Q_SCALE = LOG2E * HEAD_DIM ** -0.5

F32 = jnp.float32
BF16 = jnp.bfloat16


def _dot(a, b):
    return jnp.dot(a, b, preferred_element_type=F32)


def _dot_nt(a, b):
    return lax.dot_general(a, b, (((1,), (1,)), ((), ())), preferred_element_type=F32)


def _split_bf16(a):
    hi = a.astype(BF16)
    lo = (a - hi.astype(F32)).astype(BF16)
    return hi, lo


def _iota(shape, dim):
    return lax.broadcasted_iota(jnp.int32, shape, dim)


def _params(**kw):
    return pltpu.CompilerParams(vmem_limit_bytes=VMEM_LIMIT, **kw)


def _mod_kernel(cond_ref, w_ref, b_ref, o_ref):
    a = cond_ref[...]
    a = a * (1.0 / (1.0 + jnp.exp(-a)))
    ah, al = _split_bf16(a)
    wh, wl = _split_bf16(w_ref[...])
    o_ref[...] = _dot(ah, wh) + _dot(al, wh) + _dot(ah, wl) + b_ref[...]


def _modulation(cond, w_mod, b_mod):
    depth = w_mod.shape[0]
    n_out = w_mod.shape[2]
    return pl.pallas_call(
        _mod_kernel,
        out_shape=jax.ShapeDtypeStruct((depth, MOD_ROWS, n_out), F32),
        grid=(depth, n_out // MOD_TN),
        in_specs=[
            pl.BlockSpec((MOD_ROWS, D_MODEL), lambda l, j: (0, 0)),
            pl.BlockSpec((None, D_MODEL, MOD_TN), lambda l, j: (l, 0, j)),
            pl.BlockSpec((None, 1, MOD_TN), lambda l, j: (l, 0, j)),
        ],
        out_specs=pl.BlockSpec((None, MOD_ROWS, MOD_TN), lambda l, j: (l, 0, j)),
        compiler_params=_params(),
        name="modulation",
    )(cond, w_mod, b_mod.reshape(depth, 1, n_out))


def _rpb_kernel(rpb_ref, o_ref):
    l = pl.program_id(0)
    shape = (GRID_W, LANES)
    q = _iota(shape, 0)
    lane = _iota(shape, 1)
    k = lane % GRID_W
    left = lane < GRID_W
    dc = jnp.clip(k - q, -(NA_WIN_COLS - 1), NA_WIN_COLS - 1) + (NA_WIN_COLS - 1)
    col_start = jnp.clip(q - NA_WIN_COLS // 2, 0, GRID_W - NA_WIN_COLS)
    col_ok = (k >= col_start) & (k < col_start + NA_WIN_COLS)
    base_l = l * (NA_HEADS * NA_DROWS * NA_DCOLS)

    def body(d, carry):
        for h in range(NA_HEADS):
            base = base_l + (h * NA_DROWS + d) * NA_DCOLS
            t = jnp.zeros(shape, F32)
            for i in range(NA_DCOLS):
                val = jnp.where(left, rpb_ref[base + i], rpb_ref[base + NA_DCOLS + i])
                t = jnp.where(dc == i, val, t)
            o_ref[d, h * GRID_W:(h + 1) * GRID_W, :] = jnp.where(col_ok, t * LOG2E, NEG_INF)
        return carry

    lax.fori_loop(0, NA_DROWS - 1, body, 0)


def _rpb_tables(na_rpb):
    depth = na_rpb.shape[0]
    return pl.pallas_call(
        _rpb_kernel,
        out_shape=jax.ShapeDtypeStruct((depth, NA_DROWS - 1, NA_HEADS * GRID_W, LANES), F32),
        grid=(depth,),
        in_specs=[pl.BlockSpec(memory_space=pltpu.SMEM)],
        out_specs=pl.BlockSpec((None, NA_DROWS - 1, NA_HEADS * GRID_W, LANES), lambda l: (l, 0, 0, 0)),
        compiler_params=_params(),
        name="rpb_tables",
    )(na_rpb.reshape(-1))


def _head_rms(t, gain):
    width = t.shape[1]
    sq = (t * t).astype(BF16)
    blk = min(width, MXU_DIM)
    ones = jnp.where(_iota((blk, blk), 0) // HEAD_DIM == _iota((blk, blk), 1) // HEAD_DIM, 1.0, 0.0).astype(BF16)
    sums = [_dot(sq[:, c:c + blk], ones) for c in range(0, width, blk)]
    ss = sums[0] if len(sums) == 1 else jnp.concatenate(sums, axis=1)
    return t * lax.rsqrt(ss * (1.0 / HEAD_DIM) + EPS) * gain


def _rope(t, cos, sin):
    lane = _iota((t.shape[0], LANES), 1)
    first = (lane % (HEAD_DIM // 2)) < (HEAD_DIM // 4)
    out = []
    for c in range(0, t.shape[1], LANES):
        u = t[:, c:c + LANES]
        partner = jnp.where(first, pltpu.roll(u, LANES - HEAD_DIM // 4, axis=1), pltpu.roll(u, HEAD_DIM // 4, axis=1))
        out.append(u * cos + partner * sin)
    return out[0] if len(out) == 1 else jnp.concatenate(out, axis=1)


def _dup_kv(t):
    lane = _iota(t.shape, 1)
    swapped = pltpu.roll(t, HEAD_DIM, axis=1)
    left = lane < HEAD_DIM
    return jnp.concatenate([jnp.where(left, t, swapped), jnp.where(left, swapped, t)], axis=1)


def _cast_weights_once(w_refs, wbf_ref):
    @pl.when(pl.program_id(0) == 0)
    def _():
        c = 0
        for w_ref in w_refs:
            wbf_ref[:, c:c + w_ref.shape[1]] = w_ref[...].astype(BF16)
            c += w_ref.shape[1]


def _inproj_kernel(*refs, rope, kv_only, n_w, sub):
    x_ref, mod_ref, g1_ref = refs[:3]
    w_refs = refs[3:3 + n_w]
    gnq_ref, gnk_ref, gwq_ref, gwk_ref = refs[3 + n_w:7 + n_w]
    rest = refs[7 + n_w:]
    if rope:
        cos_ref, sin_ref = rest[:2]
        rest = rest[2:]
    *out_refs, wbf_ref = rest
    if kv_only:
        nak_ref, nav_ref, wak_ref, wav_ref = out_refs
        o_nk, o_nv, o_wk, o_wv, o_end = 0, NA_WIDTH, 2 * NA_WIDTH, 2 * NA_WIDTH + WA_KVW, 2 * NA_WIDTH + 2 * WA_KVW
    else:
        naq_ref, nak_ref, nav_ref, cvu_ref, cvb_ref, waq_ref, wak_ref, wav_ref = out_refs
        o_nk, o_nv, o_wk, o_wv, o_end = OFF_NA_K, OFF_NA_V, OFF_WA_K, OFF_WA_V, IN_WIDTH
    _cast_weights_once(w_refs, wbf_ref)
    for r0 in range(0, x_ref.shape[0], sub):
        rows = slice(r0, r0 + sub)
        x = x_ref[rows, :]
        ms = jnp.mean(x * x, axis=-1, keepdims=True)
        h = x * lax.rsqrt(ms + EPS) * g1_ref[...]
        h = h * (1.0 + mod_ref[1]) + mod_ref[0]
        p = _dot(h.astype(BF16), wbf_ref[...])
        nak_ref[rows, :] = _head_rms(p[:, o_nk:o_nk + NA_WIDTH], gnk_ref[...]).astype(BF16)
        nav_ref[rows, :] = p[:, o_nv:o_nv + NA_WIDTH].astype(BF16)
        wk = _head_rms(p[:, o_wk:o_wv], gwk_ref[...])
        if rope:
            wk = _rope(wk, cos_ref[rows, :], sin_ref[rows, :])
        wak_ref[rows, :] = _dup_kv(wk).astype(BF16)
        v2 = _dup_kv(p[:, o_wv:o_end]).astype(BF16)
        ones = jnp.ones((sub, LANES), BF16)
        wav_ref[rows, :] = jnp.concatenate([v2[:, :LANES], ones, v2[:, LANES:], ones], axis=1)
        if kv_only:
            continue
        naq_ref[rows, :] = (_head_rms(p[:, OFF_NA_Q:OFF_NA_K], gnq_ref[...]) * Q_SCALE).astype(BF16)
        cvu_ref[rows, :] = p[:, OFF_CV_C:OFF_WA_Q] * p[:, OFF_CV_X:OFF_CV_B]
        cvb_ref[rows, :] = p[:, OFF_CV_B:OFF_CV_C]
        wq = _head_rms(p[:, OFF_WA_Q:OFF_WA_K], gwq_ref[...])
        if rope:
            wq = _rope(wq, cos_ref[rows, :], sin_ref[rows, :])
        waq_ref[rows, :] = (wq * Q_SCALE).astype(BF16)


def _in_proj(x2d, mod, layer, mod_row, g1, w_in, gains, rope_tabs, tm, kv_only=False):
    rows = x2d.shape[0]
    rope = rope_tabs is not None
    row_spec = lambda w: pl.BlockSpec((tm, w), lambda i: (i, 0))
    const = lambda shape: pl.BlockSpec(shape, lambda i: (0,) * len(shape))
    once = pl.Buffered(1)
    if kv_only:
        col_blocks = [OFF_NA_K // MXU_DIM, OFF_NA_V // MXU_DIM, OFF_WA_K // MXU_DIM]
        w_specs = [pl.BlockSpec((None, D_MODEL, MXU_DIM), functools.partial(lambda j, i: (layer, 0, j), j),
                                pipeline_mode=once) for j in col_blocks]
        widths = [(NA_WIDTH, BF16), (NA_WIDTH, BF16), (WA_K2W, BF16), (WA_V3W, BF16)]
    else:
        w_specs = [pl.BlockSpec((None, D_MODEL, IN_WIDTH), lambda i: (layer, 0, 0), pipeline_mode=once)]
        widths = [(NA_WIDTH, BF16), (NA_WIDTH, BF16), (NA_WIDTH, BF16), (CONV_CH, F32), (CONV_CH, F32),
                  (WA_QW, BF16), (WA_K2W, BF16), (WA_V3W, BF16)]
    n_w = len(w_specs)
    in_specs = [
        row_spec(D_MODEL),
        pl.BlockSpec((None, None, 6, 1, D_MODEL), lambda i: (layer, mod_row(i), 0, 0, 0)),
        const((1, D_MODEL)),
        *w_specs,
        const((1, NA_WIDTH)), const((1, NA_WIDTH)), const((1, WA_QW)), const((1, WA_KVW)),
    ]
    args = [x2d, mod, g1, *([w_in] * n_w), *gains]
    if rope:
        seq_tiles = rope_tabs[0].shape[0] // tm
        in_specs += [pl.BlockSpec((tm, LANES), lambda i: (i % seq_tiles, 0))] * 2
        args += list(rope_tabs)
    return pl.pallas_call(
        functools.partial(_inproj_kernel, rope=rope, kv_only=kv_only, n_w=n_w, sub=min(tm, IN_SUB)),
        out_shape=[jax.ShapeDtypeStruct((rows, w), dt) for w, dt in widths],
        grid=(rows // tm,),
        in_specs=in_specs,
        out_specs=[row_spec(w) for w, _ in widths],
        scratch_shapes=[pltpu.VMEM((D_MODEL, n_w * MXU_DIM if kv_only else IN_WIDTH), BF16)],
        compiler_params=_params(dimension_semantics=("arbitrary",)),
        name="in_proj_rope" if rope else ("in_proj_ctx_kv" if kv_only else "in_proj_ctx"),
    )(*args)


def _lane_chunks(s):
    return [s[:, c:c + LANES] for c in range(0, s.shape[1], LANES)]


def _row_max(scores, extra=None):
    chunks = [ch for s in scores for ch in _lane_chunks(s)]
    if extra is not None:
        chunks.append(extra)
    folded = functools.reduce(jnp.maximum, chunks)
    return jnp.broadcast_to(jnp.max(folded, axis=-1, keepdims=True), folded.shape)


def _softmax_pv(parts, extra=None, sums_from_values=False, m=None):
    lane_chunks = _lane_chunks
    if m is None:
        m = _row_max([s for s, _ in parts], extra)
    ps = [jnp.concatenate([jnp.exp2(ch - m) for ch in lane_chunks(s)], axis=1) for s, _ in parts]
    pv = _dot(jnp.concatenate([p.astype(BF16) for p in ps], axis=1),
              jnp.concatenate([v for _, v in parts], axis=0))
    if sums_from_values:
        denom = pv[:, -LANES:]
        pv = pv[:, :-LANES]
        if extra is not None:
            denom = denom + jnp.exp2(extra - m)
        return pv / denom
    psum = functools.reduce(jnp.add, [ch for p in ps for ch in lane_chunks(p)])
    if extra is not None:
        psum = psum + jnp.where(_iota(extra.shape, 1) == 0, jnp.exp2(extra - m), 0.0)
    return pv / jnp.sum(psum, axis=-1, keepdims=True)


def _stack_heads(q, n_heads):
    head = _iota(q.shape, 1) // HEAD_DIM
    zero = jnp.zeros_like(q)
    return jnp.concatenate([jnp.where(head == h, q, zero) for h in range(n_heads)], axis=0)


def _unstack_heads(o, n_heads):
    rows = o.shape[0] // n_heads
    head = _iota((rows, o.shape[1]), 1) // HEAD_DIM
    out = jnp.where(head == 0, o[:rows], 0.0)
    for h in range(1, n_heads):
        out = jnp.where(head == h, o[h * rows:(h + 1) * rows], out)
    return out


def _group_rms(o, gain):
    ms = jnp.mean(o * o, axis=-1, keepdims=True)
    return o * lax.rsqrt(ms + EPS) * gain


def _wa_scores(qg, keys):
    rows = qg.shape[0]
    left = _iota((rows, LANES), 1) < HEAD_DIM
    zero = jnp.zeros((rows, LANES), qg.dtype)
    blocks = []
    for pair in range(WA_GROUP // 2):
        qp = qg[:, pair * LANES:(pair + 1) * LANES]
        blocks += [jnp.where(left, qp, zero), jnp.where(left, zero, qp)]
    qs = jnp.concatenate(blocks, axis=0)
    scores = []
    for k2, bias in keys:
        s = _dot_nt(qs, k2)
        if bias is not None:
            s = jnp.concatenate([s[g * rows:(g + 1) * rows] + bias for g in range(WA_GROUP)], axis=0)
        scores.append(s)
    return scores


def _wa_sink_col(rows, sinks):
    return jnp.concatenate([jnp.full((rows, LANES), sinks[g] * LOG2E, F32) for g in range(WA_GROUP)], axis=0)


def _wa_finish(scores, values, sinks, m=None):
    rows = scores[0].shape[0] // WA_GROUP
    left = _iota((rows, LANES), 1) < HEAD_DIM
    o2 = _softmax_pv(list(zip(scores, values)), extra=_wa_sink_col(rows, sinks), sums_from_values=True, m=m)
    pairs = [jnp.where(left, o2[(2 * pr) * rows:(2 * pr + 1) * rows], o2[(2 * pr + 1) * rows:(2 * pr + 2) * rows])
             for pr in range(WA_GROUP // 2)]
    return jnp.concatenate(pairs, axis=1)


def _na_kernel(q_ref, k_ref, v_ref, kc_ref, vc_ref, bias_ref, g_ref, o_ref, sa_ref, sb_ref):
    n_rows = q_ref.shape[0] // GRID_W
    win = NA_WIN_ROWS * GRID_W
    lc = kc_ref.shape[0]
    gain = g_ref[...]

    def window(r):
        start = jnp.clip(r - NA_WIN_ROWS // 2, 0, n_rows - NA_WIN_ROWS)
        return pl.multiple_of(start * GRID_W, GRID_W), start - r + (NA_WIN_ROWS - 1)

    def scores(r, s_ref):
        tok0, d0 = window(r)
        q = q_ref[pl.ds(pl.multiple_of(r * GRID_W, GRID_W), GRID_W), :]
        qs = _stack_heads(q, NA_HEADS)
        bias = jnp.concatenate([bias_ref[d0 + 2 * j] for j in range(NA_WIN_ROWS // 2)], axis=1)
        s_ctx = _dot_nt(qs, kc_ref[...])
        s_loc = _dot_nt(qs, k_ref[pl.ds(tok0, win), :]) + bias
        s_ref[:, :lc] = s_ctx
        s_ref[:, lc:lc + win] = s_loc
        s_ref[:, lc + win:] = _row_max([s_ctx, s_loc])

    def finish(r, s_ref):
        tok0, _ = window(r)
        o4 = _softmax_pv([(s_ref[:, :lc], vc_ref[...]), (s_ref[:, lc:lc + win], v_ref[pl.ds(tok0, win), :])],
                         m=s_ref[:, lc + win:])
        o = _unstack_heads(o4, NA_HEADS)
        o_ref[pl.ds(pl.multiple_of(r * GRID_W, GRID_W), GRID_W), :] = _group_rms(o, gain).astype(o_ref.dtype)

    def step(r, cur_ref, nxt_ref):
        scores(jnp.minimum(r + 1, n_rows - 1), nxt_ref)
        finish(r, cur_ref)

    scores(0, sa_ref)

    def body(i, carry):
        step(2 * i, sa_ref, sb_ref)
        step(2 * i + 1, sb_ref, sa_ref)
        return carry

    lax.fori_loop(0, n_rows // 2, body, 0, unroll=4)


def _na_attention(q, k, v, kc, vc, bias, layer, gain):
    batch, seq, _ = q.shape
    lc = kc.shape[1]
    assert (seq // GRID_W) % 4 == 0
    per_b = lambda n: pl.BlockSpec((None, n, NA_WIDTH), lambda b: (b, 0, 0))
    score_scratch = pltpu.VMEM((NA_HEADS * GRID_W, lc + NA_WIN_ROWS * GRID_W + LANES), F32)
    return pl.pallas_call(
        _na_kernel,
        out_shape=jax.ShapeDtypeStruct((batch, seq, NA_WIDTH), BF16),
        grid=(batch,),
        in_specs=[per_b(seq), per_b(seq), per_b(seq), per_b(lc), per_b(lc),
                  pl.BlockSpec((None,) + bias.shape[1:], lambda b: (layer, 0, 0, 0)),
                  pl.BlockSpec((1, NA_WIDTH), lambda b: (0, 0))],
        out_specs=per_b(seq),
        scratch_shapes=[score_scratch, score_scratch],
        compiler_params=_params(),
        name="na_attention",
    )(q, k, v, kc, vc, bias, gain)


def _wa_kernel(sink_ref, q_ref, k_ref, v_ref, kc_ref, vc_ref, g_ref, o_ref, sa_ref, sb_ref):
    seq = q_ref.shape[0]
    n_blocks = seq // WA_BLOCK
    lc = kc_ref.shape[0]
    gain = g_ref[...]
    row = _iota((WA_BLOCK, WA_BAND), 0)
    col = _iota((WA_BLOCK, WA_BAND), 1)

    def window(n):
        q0 = pl.multiple_of(n * WA_BLOCK, WA_BLOCK)
        k0 = pl.multiple_of(jnp.clip(q0 - WA_BLOCK, 0, seq - WA_BAND), WA_BLOCK)
        return q0, k0

    def scores(n, kh, s_ref):
        q0, k0 = window(n)
        band = jnp.where(jnp.abs((k0 + col) - (q0 + row)) <= WA_WINDOW, 0.0, NEG_INF)
        lanes = slice(kh * LANES, (kh + 1) * LANES)
        qg = q_ref[pl.ds(q0, WA_BLOCK), kh * WA_GROUP * HEAD_DIM:(kh + 1) * WA_GROUP * HEAD_DIM]
        s_ctx, s_loc = _wa_scores(qg, [(kc_ref[:, lanes], None), (k_ref[pl.ds(k0, WA_BAND), lanes], band)])
        s_ref[kh, :, :lc] = s_ctx
        s_ref[kh, :, lc:lc + WA_BAND] = s_loc
        s_ref[kh, :, lc + WA_BAND:] = _row_max([s_ctx, s_loc], _wa_sink_col(WA_BLOCK, sinks(kh)))

    def sinks(kh):
        return [sink_ref[kh * WA_GROUP + g] for g in range(WA_GROUP)]

    def finish(n, kh, s_ref):
        _, k0 = window(n)
        vlanes = slice(kh * 2 * LANES, (kh + 1) * 2 * LANES)
        return _wa_finish([s_ref[kh, :, :lc], s_ref[kh, :, lc:lc + WA_BAND]],
                          [vc_ref[:, vlanes], v_ref[pl.ds(k0, WA_BAND), vlanes]], sinks(kh),
                          m=s_ref[kh, :, lc + WA_BAND:])

    def step(n, cur_ref, nxt_ref):
        nxt = jnp.minimum(n + 1, n_blocks - 1)
        for kh in range(WA_KV_HEADS):
            scores(nxt, kh, nxt_ref)
        o = jnp.concatenate([finish(n, kh, cur_ref) for kh in range(WA_KV_HEADS)], axis=1)
        q0, _ = window(n)
        o_ref[pl.ds(q0, WA_BLOCK), :] = _group_rms(o, gain).astype(o_ref.dtype)

    for kh in range(WA_KV_HEADS):
        scores(0, kh, sa_ref)

    def body(i, carry):
        step(2 * i, sa_ref, sb_ref)
        step(2 * i + 1, sb_ref, sa_ref)
        return carry

    lax.fori_loop(0, n_blocks // 2, body, 0)


def _wa_attention(sink, q, k2, v2, kc2, vc2, gain):
    batch, seq, _ = q.shape
    lc = kc2.shape[1]
    assert (seq // WA_BLOCK) % 2 == 0
    per_b = lambda n, w: pl.BlockSpec((None, n, w), lambda b: (b, 0, 0))
    score_scratch = pltpu.VMEM((WA_KV_HEADS, WA_GROUP * WA_BLOCK, lc + WA_BAND + LANES), F32)
    return pl.pallas_call(
        _wa_kernel,
        out_shape=jax.ShapeDtypeStruct((batch, seq, WA_QW), BF16),
        grid=(batch,),
        in_specs=[pl.BlockSpec(memory_space=pltpu.SMEM),
                  per_b(seq, WA_QW), per_b(seq, WA_K2W), per_b(seq, WA_V3W),
                  per_b(lc, WA_K2W), per_b(lc, WA_V3W),
                  pl.BlockSpec((1, WA_QW), lambda b: (0, 0))],
        out_specs=per_b(seq, WA_QW),
        scratch_shapes=[score_scratch, score_scratch],
        compiler_params=_params(),
        name="wa_attention",
    )(sink, q, k2, v2, kc2, vc2, gain)


def _ctx_attn_kernel(sink_ref, qn_ref, kn_ref, vn_ref, qw_ref, kw_ref, vw_ref, gn_ref, gw_ref, on_ref, ow_ref):
    qs = _stack_heads(qn_ref[...], NA_HEADS)
    o4 = _softmax_pv([(_dot_nt(qs, kn_ref[...]), vn_ref[...])])
    on_ref[...] = _group_rms(_unstack_heads(o4, NA_HEADS), gn_ref[...]).astype(on_ref.dtype)
    outs = []
    for kh in range(WA_KV_HEADS):
        lanes = slice(kh * LANES, (kh + 1) * LANES)
        vlanes = slice(kh * 2 * LANES, (kh + 1) * 2 * LANES)
        qg = qw_ref[:, kh * WA_GROUP * HEAD_DIM:(kh + 1) * WA_GROUP * HEAD_DIM]
        sinks = [sink_ref[kh * WA_GROUP + g] for g in range(WA_GROUP)]
        outs.append(_wa_finish(_wa_scores(qg, [(kw_ref[:, lanes], None)]), [vw_ref[:, vlanes]], sinks))
    ow_ref[...] = _group_rms(jnp.concatenate(outs, axis=1), gw_ref[...]).astype(ow_ref.dtype)


def _ctx_attention(sink, qn, kn, vn, qw, kw2, vw2, gain_na, gain_wa):
    batch, lc, _ = qn.shape
    per_b = lambda w: pl.BlockSpec((None, lc, w), lambda b: (b, 0, 0))
    return pl.pallas_call(
        _ctx_attn_kernel,
        out_shape=[jax.ShapeDtypeStruct((batch, lc, NA_WIDTH), BF16),
                   jax.ShapeDtypeStruct((batch, lc, WA_QW), BF16)],
        grid=(batch,),
        in_specs=[pl.BlockSpec(memory_space=pltpu.SMEM),
                  per_b(NA_WIDTH), per_b(NA_WIDTH), per_b(NA_WIDTH),
                  per_b(WA_QW), per_b(WA_K2W), per_b(WA_V3W),
                  pl.BlockSpec((1, NA_WIDTH), lambda b: (0, 0)),
                  pl.BlockSpec((1, WA_QW), lambda b: (0, 0))],
        out_specs=[per_b(NA_WIDTH), per_b(WA_QW)],
        compiler_params=_params(),
        name="ctx_attention",
    )(sink, qn, kn, vn, qw, kw2, vw2, gain_na, gain_wa)


def _out_mlp_kernel(x_ref, mod_ref, na_ref, wa_ref, u_ref, uprev_ref, unext_ref, cvb_ref, cw_ref, cb_ref, g_ref,
                    wo_ref, g2_ref, w1_ref, w2_ref, o_ref, wobf_ref, *, seq_len, sub):
    tm = x_ref.shape[0]
    i = pl.program_id(0)
    _cast_weights_once([wo_ref], wobf_ref)
    u = u_ref[...]
    at_start = (i * tm) % seq_len == 0
    at_end = ((i + 1) * tm) % seq_len == 0
    prev_row = jnp.where(at_start, 0.0, uprev_ref[7:8, :])
    next_row = jnp.where(at_end, 0.0, unext_ref[0:1, :])
    row = _iota(u.shape, 0)
    up = jnp.where(row == 0, prev_row, pltpu.roll(u, 1, axis=0))
    dn = jnp.where(row == tm - 1, next_row, pltpu.roll(u, tm - 1, axis=0))
    y = cb_ref[...] + cw_ref[0:1, :] * up + cw_ref[1:2, :] * u + cw_ref[2:3, :] * dn
    cv = _group_rms(cvb_ref[...] * y, g_ref[...]).astype(BF16)
    for r0 in range(0, tm, sub):
        rows = slice(r0, r0 + sub)
        mixed = jnp.concatenate([na_ref[rows, :], cv[rows, :], wa_ref[rows, :]], axis=1)
        x = x_ref[rows, :] + mod_ref[2] * _dot(mixed, wobf_ref[...])
        ms = jnp.mean(x * x, axis=-1, keepdims=True)
        h = x * lax.rsqrt(ms + EPS) * g2_ref[...]
        h = (h * (1.0 + mod_ref[4]) + mod_ref[3]).astype(BF16)
        acc = None
        for c in range(0, FFN_DIM, FFN_CHUNK):
            a = jnp.maximum(_dot(h, w1_ref[:, c:c + FFN_CHUNK]), 0.0)
            part = _dot((a * a).astype(BF16), w2_ref[c:c + FFN_CHUNK, :])
            acc = part if acc is None else acc + part
        o_ref[rows, :] = x + mod_ref[5] * acc


def _out_mlp(x2d, mod, layer, mod_row, na_n, wa_n, cv_u, cv_b, conv_w, conv_b, gain_cv, w_o, g2, w1, w2, seq_len, tm):
    rows = x2d.shape[0]
    halo = 8
    per_tile = tm // halo
    last = rows // halo - 1
    row_spec = lambda w: pl.BlockSpec((tm, w), lambda i: (i, 0))
    const = lambda shape: pl.BlockSpec(shape, lambda i: (0,) * len(shape), pipeline_mode=pl.Buffered(1))
    of_layer = lambda shape: pl.BlockSpec((None,) + shape, lambda i: (layer,) + (0,) * len(shape),
                                          pipeline_mode=pl.Buffered(1))
    return pl.pallas_call(
        functools.partial(_out_mlp_kernel, seq_len=seq_len, sub=min(tm, MLP_SUB)),
        out_shape=jax.ShapeDtypeStruct((rows, D_MODEL), F32),
        grid=(rows // tm,),
        in_specs=[
            row_spec(D_MODEL),
            pl.BlockSpec((None, None, 6, 1, D_MODEL), lambda i: (layer, mod_row(i), 0, 0, 0)),
            row_spec(NA_WIDTH), row_spec(WA_QW), row_spec(CONV_CH),
            pl.BlockSpec((halo, CONV_CH), lambda i: (jnp.maximum(i * per_tile - 1, 0), 0)),
            pl.BlockSpec((halo, CONV_CH), lambda i: (jnp.minimum((i + 1) * per_tile, last), 0)),
            row_spec(CONV_CH),
            const((3, CONV_CH)), const((1, CONV_CH)), const((1, CONV_CH)),
            of_layer((D_MODEL, D_MODEL)),
            const((1, D_MODEL)),
            of_layer((D_MODEL, FFN_DIM)),
            of_layer((FFN_DIM, D_MODEL)),
        ],
        out_specs=row_spec(D_MODEL),
        scratch_shapes=[pltpu.VMEM((D_MODEL, D_MODEL), BF16)],
        compiler_params=_params(dimension_semantics=("arbitrary",)),
        name="out_mlp",
    )(x2d, mod, na_n, wa_n, cv_u, cv_u, cv_u, cv_b, conv_w, conv_b, gain_cv, w_o, g2, w1, w2)


def _rope_tables(seq):
    quarter = HEAD_DIM // 4
    inv = ROPE_BASE ** (-jnp.arange(quarter, dtype=F32) / quarter)
    t = jnp.arange(seq)
    ang_r = (t // GRID_W).astype(F32)[:, None] * inv[None, :]
    ang_c = (t % GRID_W).astype(F32)[:, None] * inv[None, :]
    cos = jnp.concatenate([jnp.cos(ang_r), jnp.cos(ang_r), jnp.cos(ang_c), jnp.cos(ang_c)], axis=1)
    sin = jnp.concatenate([-jnp.sin(ang_r), jnp.sin(ang_r), -jnp.sin(ang_c), jnp.sin(ang_c)], axis=1)
    reps = LANES // HEAD_DIM
    return jnp.tile(cos, (1, reps)), jnp.tile(sin, (1, reps))


def kernel(x, c, ctx, c_ctx, w_mod, b_mod, g_norm1, g_norm2, w_in, na_q_gain, na_k_gain, na_rpb,
           conv_w, conv_bias, wa_q_gain, wa_k_gain, wa_sink, g_out, w_o, w_fc1, w_fc2):
    batch, seq, d = x.shape
    lc = ctx.shape[1]
    depth = w_mod.shape[0]
    assert d == D_MODEL and seq % WA_BLOCK == 0 and seq % GRID_W == 0 and batch < MOD_ROWS
    tm = 512
    tm_in = 1024
    tm_ctx = 256
    assert seq % tm == 0 and seq % tm_in == 0 and lc % tm_ctx == 0

    cond = jnp.zeros((MOD_ROWS, d), F32).at[:batch].set(c).at[batch].set(c_ctx)
    mod = _modulation(cond, w_mod, b_mod).reshape(depth, MOD_ROWS, 6, 1, d)
    rpb_bias = _rpb_tables(na_rpb)
    rope_tabs = _rope_tables(seq)

    lat_row = lambda i: i // (seq // tm)
    lat_row_in = lambda i: i // (seq // tm_in)
    ctx_row = lambda i: batch

    w1_bf = w_fc1.astype(BF16)
    w2_bf = w_fc2.astype(BF16)
    xs = x.reshape(batch * seq, d)
    cs = ctx.reshape(batch * lc, d)
    for l in range(depth):
        last = l == depth - 1
        g1 = g_norm1[l].reshape(1, d)
        g2 = g_norm2[l].reshape(1, d)
        gains = [jnp.tile(na_q_gain[l], NA_HEADS).reshape(1, -1), jnp.tile(na_k_gain[l], NA_HEADS).reshape(1, -1),
                 jnp.tile(wa_q_gain[l], WA_HEADS).reshape(1, -1), jnp.tile(wa_k_gain[l], WA_KV_HEADS).reshape(1, -1)]
        go_na = g_out[l, :NA_WIDTH].reshape(1, -1)
        go_cv = g_out[l, NA_WIDTH:NA_WIDTH + CONV_CH].reshape(1, -1)
        go_wa = g_out[l, NA_WIDTH + CONV_CH:].reshape(1, -1)
        cb = conv_bias[l].reshape(1, -1)

        naq, nak, nav, cvu, cvb, waq, wak, wav = _in_proj(xs, mod, l, lat_row_in, g1, w_in, gains, rope_tabs, tm_in)
        if last:
            cnak, cnav, cwak, cwav = _in_proj(cs, mod, l, ctx_row, g1, w_in, gains, None, tm_ctx, kv_only=True)
        else:
            cnaq, cnak, cnav, ccvu, ccvb, cwaq, cwak, cwav = _in_proj(cs, mod, l, ctx_row, g1, w_in, gains, None, tm_ctx)

        b3 = lambda a, n: a.reshape(batch, n, a.shape[-1])
        na_n = _na_attention(b3(naq, seq), b3(nak, seq), b3(nav, seq), b3(cnak, lc), b3(cnav, lc), rpb_bias, l, go_na)
        wa_n = _wa_attention(wa_sink[l], b3(waq, seq), b3(wak, seq), b3(wav, seq), b3(cwak, lc), b3(cwav, lc), go_wa)
        xs = _out_mlp(xs, mod, l, lat_row, na_n.reshape(batch * seq, -1), wa_n.reshape(batch * seq, -1),
                      cvu, cvb, conv_w[l], cb, go_cv, w_o, g2, w1_bf, w2_bf, seq, tm)
        if not last:
            cna_n, cwa_n = _ctx_attention(wa_sink[l], b3(cnaq, lc), b3(cnak, lc), b3(cnav, lc),
                                          b3(cwaq, lc), b3(cwak, lc), b3(cwav, lc), go_na, go_wa)
            cs = _out_mlp(cs, mod, l, ctx_row, cna_n.reshape(batch * lc, -1), cwa_n.reshape(batch * lc, -1),
                          ccvu, ccvb, conv_w[l], cb, go_cv, w_o, g2, w1_bf, w2_bf, lc, tm_ctx)
    return xs.reshape(batch, seq, d)
```

```python
import functools

import jax
import jax.numpy as jnp
from jax import lax
from jax.experimental import pallas as pl
from jax.experimental.pallas import tpu as pltpu

D_MODEL = 1024
GRID_W = 64
HEAD_DIM = 64
NA_HEADS = 4
NA_WIDTH = NA_HEADS * HEAD_DIM
CONV_CH = 256
WA_HEADS = 8
WA_KV_HEADS = 2
WA_GROUP = WA_HEADS // WA_KV_HEADS
WA_QW = WA_HEADS * HEAD_DIM
WA_KVW = WA_KV_HEADS * HEAD_DIM
WA_K2W = 2 * WA_KVW
WA_V3W = 2 * WA_KVW
NA_WIN_ROWS = 8
NA_WIN_COLS = 16
NA_DROWS = 2 * NA_WIN_ROWS - 1
NA_DCOLS = 2 * NA_WIN_COLS - 1
WA_WINDOW = 128
WA_BLOCK = 128
WA_BAND = 3 * WA_BLOCK
FFN_DIM = 4 * D_MODEL
FFN_CHUNK = 1024
ROPE_BASE = 10000.0
EPS = 1e-6
NEG_INF = -1e30
IN_WIDTH = 2304
OFF_NA_Q, OFF_NA_K, OFF_NA_V = 0, 256, 512
OFF_CV_X, OFF_CV_B, OFF_CV_C = 768, 1024, 1280
OFF_WA_Q, OFF_WA_K, OFF_WA_V = 1536, 2048, 2176
MOD_ROWS = 16
MOD_TN = 1024
IN_SUB = 256
MLP_SUB = 256
LANES = 128
MXU_DIM = 256
VMEM_LIMIT = 60 * 1024 * 1024
LOG2E = 1.4426950408889634
Q_SCALE = LOG2E * HEAD_DIM ** -0.5

F32 = jnp.float32
BF16 = jnp.bfloat16


def _dot(a, b):
    return jnp.dot(a, b, preferred_element_type=F32)


def _dot_nt(a, b):
    return lax.dot_general(a, b, (((1,), (1,)), ((), ())), preferred_element_type=F32)


def _split_bf16(a):
    hi = a.astype(BF16)
    lo = (a - hi.astype(F32)).astype(BF16)
    return hi, lo


def _iota(shape, dim):
    return lax.broadcasted_iota(jnp.int32, shape, dim)


def _params(**kw):
    return pltpu.CompilerParams(vmem_limit_bytes=VMEM_LIMIT, **kw)


def _mod_kernel(cond_ref, w_ref, b_ref, o_ref):
    a = cond_ref[...]
    a = a * (1.0 / (1.0 + jnp.exp(-a)))
    ah, al = _split_bf16(a)
    wh, wl = _split_bf16(w_ref[...])
    o_ref[...] = _dot(ah, wh) + _dot(al, wh) + _dot(ah, wl) + b_ref[...]


def _modulation(cond, w_mod, b_mod):
    depth = w_mod.shape[0]
    n_out = w_mod.shape[2]
    return pl.pallas_call(
        _mod_kernel,
        out_shape=jax.ShapeDtypeStruct((depth, MOD_ROWS, n_out), F32),
        grid=(depth, n_out // MOD_TN),
        in_specs=[
            pl.BlockSpec((MOD_ROWS, D_MODEL), lambda l, j: (0, 0)),
            pl.BlockSpec((None, D_MODEL, MOD_TN), lambda l, j: (l, 0, j)),
            pl.BlockSpec((None, 1, MOD_TN), lambda l, j: (l, 0, j)),
        ],
        out_specs=pl.BlockSpec((None, MOD_ROWS, MOD_TN), lambda l, j: (l, 0, j)),
        compiler_params=_params(),
        name="modulation",
    )(cond, w_mod, b_mod.reshape(depth, 1, n_out))


def _rpb_kernel(rpb_ref, o_ref):
    l = pl.program_id(0)
    shape = (GRID_W, LANES)
    q = _iota(shape, 0)
    lane = _iota(shape, 1)
    k = lane % GRID_W
    left = lane < GRID_W
    dc = jnp.clip(k - q, -(NA_WIN_COLS - 1), NA_WIN_COLS - 1) + (NA_WIN_COLS - 1)
    col_start = jnp.clip(q - NA_WIN_COLS // 2, 0, GRID_W - NA_WIN_COLS)
    col_ok = (k >= col_start) & (k < col_start + NA_WIN_COLS)
    base_l = l * (NA_HEADS * NA_DROWS * NA_DCOLS)

    def body(d, carry):
        for h in range(NA_HEADS):
            base = base_l + (h * NA_DROWS + d) * NA_DCOLS
            t = jnp.zeros(shape, F32)
            for i in range(NA_DCOLS):
                val = jnp.where(left, rpb_ref[base + i], rpb_ref[base + NA_DCOLS + i])
                t = jnp.where(dc == i, val, t)
            o_ref[d, h * GRID_W:(h + 1) * GRID_W, :] = jnp.where(col_ok, t * LOG2E, NEG_INF)
        return carry

    lax.fori_loop(0, NA_DROWS - 1, body, 0)


def _rpb_tables(na_rpb):
    depth = na_rpb.shape[0]
    return pl.pallas_call(
        _rpb_kernel,
        out_shape=jax.ShapeDtypeStruct((depth, NA_DROWS - 1, NA_HEADS * GRID_W, LANES), F32),
        grid=(depth,),
        in_specs=[pl.BlockSpec(memory_space=pltpu.SMEM)],
        out_specs=pl.BlockSpec((None, NA_DROWS - 1, NA_HEADS * GRID_W, LANES), lambda l: (l, 0, 0, 0)),
        compiler_params=_params(),
        name="rpb_tables",
    )(na_rpb.reshape(-1))


def _head_rms(t, gain):
    width = t.shape[1]
    sq = (t * t).astype(BF16)
    blk = min(width, MXU_DIM)
    ones = jnp.where(_iota((blk, blk), 0) // HEAD_DIM == _iota((blk, blk), 1) // HEAD_DIM, 1.0, 0.0).astype(BF16)
    sums = [_dot(sq[:, c:c + blk], ones) for c in range(0, width, blk)]
    ss = sums[0] if len(sums) == 1 else jnp.concatenate(sums, axis=1)
    return t * lax.rsqrt(ss * (1.0 / HEAD_DIM) + EPS) * gain


def _rope(t, cos, sin):
    lane = _iota((t.shape[0], LANES), 1)
    first = (lane % (HEAD_DIM // 2)) < (HEAD_DIM // 4)
    out = []
    for c in range(0, t.shape[1], LANES):
        u = t[:, c:c + LANES]
        partner = jnp.where(first, pltpu.roll(u, LANES - HEAD_DIM // 4, axis=1), pltpu.roll(u, HEAD_DIM // 4, axis=1))
        out.append(u * cos + partner * sin)
    return out[0] if len(out) == 1 else jnp.concatenate(out, axis=1)


def _dup_kv(t):
    lane = _iota(t.shape, 1)
    swapped = pltpu.roll(t, HEAD_DIM, axis=1)
    left = lane < HEAD_DIM
    return jnp.concatenate([jnp.where(left, t, swapped), jnp.where(left, swapped, t)], axis=1)


def _cast_weights_once(w_refs, wbf_ref):
    @pl.when(pl.program_id(0) == 0)
    def _():
        c = 0
        for w_ref in w_refs:
            wbf_ref[:, c:c + w_ref.shape[1]] = w_ref[...].astype(BF16)
            c += w_ref.shape[1]


def _inproj_kernel(*refs, rope, kv_only, n_w, sub):
    x_ref, mod_ref, g1_ref = refs[:3]
    w_refs = refs[3:3 + n_w]
    gnq_ref, gnk_ref, gwq_ref, gwk_ref = refs[3 + n_w:7 + n_w]
    rest = refs[7 + n_w:]
    if rope:
        cos_ref, sin_ref = rest[:2]
        rest = rest[2:]
    *out_refs, wbf_ref = rest
    if kv_only:
        nak_ref, nav_ref, wak_ref, wav_ref = out_refs
        o_nk, o_nv, o_wk, o_wv, o_end = 0, NA_WIDTH, 2 * NA_WIDTH, 2 * NA_WIDTH + WA_KVW, 2 * NA_WIDTH + 2 * WA_KVW
    else:
        naq_ref, nak_ref, nav_ref, cvu_ref, cvb_ref, waq_ref, wak_ref, wav_ref = out_refs
        o_nk, o_nv, o_wk, o_wv, o_end = OFF_NA_K, OFF_NA_V, OFF_WA_K, OFF_WA_V, IN_WIDTH
    _cast_weights_once(w_refs, wbf_ref)
    for r0 in range(0, x_ref.shape[0], sub):
        rows = slice(r0, r0 + sub)
        x = x_ref[rows, :]
        ms = jnp.mean(x * x, axis=-1, keepdims=True)
        h = x * lax.rsqrt(ms + EPS) * g1_ref[...]
        h = h * (1.0 + mod_ref[1]) + mod_ref[0]
        p = _dot(h.astype(BF16), wbf_ref[...])
        nak_ref[rows, :] = _head_rms(p[:, o_nk:o_nk + NA_WIDTH], gnk_ref[...]).astype(BF16)
        nav_ref[rows, :] = p[:, o_nv:o_nv + NA_WIDTH].astype(BF16)
        wk = _head_rms(p[:, o_wk:o_wv], gwk_ref[...])
        if rope:
            wk = _rope(wk, cos_ref[rows, :], sin_ref[rows, :])
        wak_ref[rows, :] = _dup_kv(wk).astype(BF16)
        vv = p[:, o_wv:o_end]
        left = _iota(vv.shape, 1) < HEAD_DIM
        wav_ref[rows, :] = jnp.concatenate([jnp.where(left, vv, 1.0), jnp.where(left, pltpu.roll(vv, HEAD_DIM, axis=1), 1.0)],
                                           axis=1).astype(BF16)
        if kv_only:
            continue
        naq_ref[rows, :] = (_head_rms(p[:, OFF_NA_Q:OFF_NA_K], gnq_ref[...]) * Q_SCALE).astype(BF16)
        cvu_ref[rows, :] = p[:, OFF_CV_C:OFF_WA_Q] * p[:, OFF_CV_X:OFF_CV_B]
        cvb_ref[rows, :] = p[:, OFF_CV_B:OFF_CV_C]
        wq = _head_rms(p[:, OFF_WA_Q:OFF_WA_K], gwq_ref[...])
        if rope:
            wq = _rope(wq, cos_ref[rows, :], sin_ref[rows, :])
        waq_ref[rows, :] = (wq * Q_SCALE).astype(BF16)


def _in_proj(x2d, mod, layer, mod_row, g1, w_in, gains, rope_tabs, tm, kv_only=False):
    rows = x2d.shape[0]
    rope = rope_tabs is not None
    row_spec = lambda w: pl.BlockSpec((tm, w), lambda i: (i, 0))
    const = lambda shape: pl.BlockSpec(shape, lambda i: (0,) * len(shape))
    once = pl.Buffered(1)
    if kv_only:
        col_blocks = [OFF_NA_K // MXU_DIM, OFF_NA_V // MXU_DIM, OFF_WA_K // MXU_DIM]
        w_specs = [pl.BlockSpec((None, D_MODEL, MXU_DIM), functools.partial(lambda j, i: (layer, 0, j), j),
                                pipeline_mode=once) for j in col_blocks]
        widths = [(NA_WIDTH, BF16), (NA_WIDTH, BF16), (WA_K2W, BF16), (WA_V3W, BF16)]
    else:
        w_specs = [pl.BlockSpec((None, D_MODEL, IN_WIDTH), lambda i: (layer, 0, 0), pipeline_mode=once)]
        widths = [(NA_WIDTH, BF16), (NA_WIDTH, BF16), (NA_WIDTH, BF16), (CONV_CH, F32), (CONV_CH, F32),
                  (WA_QW, BF16), (WA_K2W, BF16), (WA_V3W, BF16)]
    n_w = len(w_specs)
    in_specs = [
        row_spec(D_MODEL),
        pl.BlockSpec((None, None, 6, 1, D_MODEL), lambda i: (layer, mod_row(i), 0, 0, 0)),
        const((1, D_MODEL)),
        *w_specs,
        const((1, NA_WIDTH)), const((1, NA_WIDTH)), const((1, WA_QW)), const((1, WA_KVW)),
    ]
    args = [x2d, mod, g1, *([w_in] * n_w), *gains]
    if rope:
        seq_tiles = rope_tabs[0].shape[0] // tm
        in_specs += [pl.BlockSpec((tm, LANES), lambda i: (i % seq_tiles, 0))] * 2
        args += list(rope_tabs)
    return pl.pallas_call(
        functools.partial(_inproj_kernel, rope=rope, kv_only=kv_only, n_w=n_w, sub=min(tm, IN_SUB)),
        out_shape=[jax.ShapeDtypeStruct((rows, w), dt) for w, dt in widths],
        grid=(rows // tm,),
        in_specs=in_specs,
        out_specs=[row_spec(w) for w, _ in widths],
        scratch_shapes=[pltpu.VMEM((D_MODEL, n_w * MXU_DIM if kv_only else IN_WIDTH), BF16)],
        compiler_params=_params(dimension_semantics=("arbitrary",)),
        name="in_proj_rope" if rope else ("in_proj_ctx_kv" if kv_only else "in_proj_ctx"),
    )(*args)


def _lane_chunks(s):
    return [s[:, c:c + LANES] for c in range(0, s.shape[1], LANES)]


def _row_max(scores, extra=None):
    chunks = [ch for s in scores for ch in _lane_chunks(s)]
    if extra is not None:
        chunks.append(extra)
    folded = functools.reduce(jnp.maximum, chunks)
    return jnp.broadcast_to(jnp.max(folded, axis=-1, keepdims=True), folded.shape)


def _softmax_pv(parts, extra=None, sums_from_values=False, m=None):
    lane_chunks = _lane_chunks
    if m is None:
        m = _row_max([s for s, _ in parts], extra)
    ps = [jnp.concatenate([jnp.exp2(ch - m) for ch in lane_chunks(s)], axis=1) for s, _ in parts]
    pv = _dot(jnp.concatenate([p.astype(BF16) for p in ps], axis=1),
              jnp.concatenate([v for _, v in parts], axis=0))
    if sums_from_values:
        denom = pv[:, -LANES:]
        pv = pv[:, :-LANES]
        if extra is not None:
            denom = denom + jnp.exp2(extra - m)
        return pv / denom
    psum = functools.reduce(jnp.add, [ch for p in ps for ch in lane_chunks(p)])
    if extra is not None:
        psum = psum + jnp.where(_iota(extra.shape, 1) == 0, jnp.exp2(extra - m), 0.0)
    return pv / jnp.sum(psum, axis=-1, keepdims=True)


def _stack_heads(q, n_heads):
    head = _iota(q.shape, 1) // HEAD_DIM
    zero = jnp.zeros_like(q)
    return jnp.concatenate([jnp.where(head == h, q, zero) for h in range(n_heads)], axis=0)


def _unstack_heads(o, n_heads):
    rows = o.shape[0] // n_heads
    head = _iota((rows, o.shape[1]), 1) // HEAD_DIM
    out = jnp.where(head == 0, o[:rows], 0.0)
    for h in range(1, n_heads):
        out = jnp.where(head == h, o[h * rows:(h + 1) * rows], out)
    return out


def _group_rms(o, gain):
    ms = jnp.mean(o * o, axis=-1, keepdims=True)
    return o * lax.rsqrt(ms + EPS) * gain


SUBLANES = 8


def _wa_scores(qg, keys):
    rows = qg.shape[0]
    left = _iota((rows, LANES), 1) < HEAD_DIM
    zero = jnp.zeros((rows, LANES), qg.dtype)
    blocks = []
    for pair in range(WA_GROUP // 2):
        qp = qg[:, pair * LANES:(pair + 1) * LANES]
        blocks += [jnp.where(left, qp, zero), jnp.where(left, zero, qp)]
    qs = jnp.concatenate(blocks, axis=0)
    scores = []
    for k2, bias in keys:
        s = _dot_nt(k2, qs)
        if bias is not None:
            s = s + jnp.concatenate([bias] * WA_GROUP, axis=1)
        scores.append(s)
    return scores


def _wa_sink_rows(rows, sinks):
    return jnp.concatenate([jnp.full((SUBLANES, rows), sinks[g] * LOG2E, F32) for g in range(WA_GROUP)], axis=1)


def _wa_col_max(scores, sink_rows):
    m = sink_rows[:1]
    for s in scores:
        m = jnp.maximum(m, jnp.max(s, axis=0, keepdims=True))
    return jnp.broadcast_to(m, sink_rows.shape)


def _wa_finish(scores, values, sinks, m=None):
    rows = scores[0].shape[1] // WA_GROUP
    sink_rows = _wa_sink_rows(rows, sinks)
    if m is None:
        m = _wa_col_max(scores, sink_rows)
    p = jnp.concatenate([jnp.exp2(s - m[:1]).astype(BF16) for s in scores], axis=0)
    v = jnp.concatenate(values, axis=0)
    ot = lax.dot_general(v, p, (((0,), (0,)), ((), ())), preferred_element_type=F32)
    denom = ot[HEAD_DIM:HEAD_DIM + SUBLANES] + jnp.exp2(sink_rows - m)
    on = ot[:HEAD_DIM] / denom[:1]
    pairs = [jnp.concatenate([on[:, (2 * pr) * rows:(2 * pr + 1) * rows],
                              on[:, (2 * pr + 1) * rows:(2 * pr + 2) * rows]], axis=0).T
             for pr in range(WA_GROUP // 2)]
    return jnp.concatenate(pairs, axis=1)


def _na_steps(q_ref, k_ref, v_ref, kc_ref, vc_ref, bias_ref, g_ref, o_ref):
    n_rows = q_ref.shape[0] // GRID_W
    win = NA_WIN_ROWS * GRID_W
    lc = kc_ref.shape[0]
    gain = g_ref[...]

    def window(r):
        start = jnp.clip(r - NA_WIN_ROWS // 2, 0, n_rows - NA_WIN_ROWS)
        return pl.multiple_of(start * GRID_W, GRID_W), start - r + (NA_WIN_ROWS - 1)

    def scores(r, s_ref):
        tok0, d0 = window(r)
        q = q_ref[pl.ds(pl.multiple_of(r * GRID_W, GRID_W), GRID_W), :]
        qs = _stack_heads(q, NA_HEADS)
        bias = jnp.concatenate([bias_ref[d0 + 2 * j] for j in range(NA_WIN_ROWS // 2)], axis=1)
        s_ctx = _dot_nt(qs, kc_ref[...])
        s_loc = _dot_nt(qs, k_ref[pl.ds(tok0, win), :]) + bias
        s_ref[:, :lc] = s_ctx
        s_ref[:, lc:lc + win] = s_loc
        s_ref[:, lc + win:] = _row_max([s_ctx, s_loc])

    def finish(r, s_ref):
        tok0, _ = window(r)
        o4 = _softmax_pv([(s_ref[:, :lc], vc_ref[...]), (s_ref[:, lc:lc + win], v_ref[pl.ds(tok0, win), :])],
                         m=s_ref[:, lc + win:])
        o = _unstack_heads(o4, NA_HEADS)
        o_ref[pl.ds(pl.multiple_of(r * GRID_W, GRID_W), GRID_W), :] = _group_rms(o, gain).astype(o_ref.dtype)

    def step(r, cur_ref, nxt_ref):
        scores(jnp.minimum(r + 1, n_rows - 1), nxt_ref)
        finish(r, cur_ref)

    return scores, step


def _wa_steps(sink_ref, q_ref, k_ref, v_ref, kc_ref, vc_ref, g_ref, o_ref):
    seq = q_ref.shape[0]
    n_blocks = seq // WA_BLOCK
    lc = kc_ref.shape[0]
    gain = g_ref[...]
    key = _iota((WA_BAND, WA_BLOCK), 0)
    qry = _iota((WA_BAND, WA_BLOCK), 1)

    def window(n):
        q0 = pl.multiple_of(n * WA_BLOCK, WA_BLOCK)
        k0 = pl.multiple_of(jnp.clip(q0 - WA_BLOCK, 0, seq - WA_BAND), WA_BLOCK)
        return q0, k0

    def scores(n, kh, s_ref):
        q0, k0 = window(n)
        band = jnp.where(jnp.abs((k0 + key) - (q0 + qry)) <= WA_WINDOW, 0.0, NEG_INF)
        lanes = slice(kh * LANES, (kh + 1) * LANES)
        qg = q_ref[pl.ds(q0, WA_BLOCK), kh * WA_GROUP * HEAD_DIM:(kh + 1) * WA_GROUP * HEAD_DIM]
        s_ctx, s_loc = _wa_scores(qg, [(kc_ref[:, lanes], None), (k_ref[pl.ds(k0, WA_BAND), lanes], band)])
        s_ref[kh, :lc, :] = s_ctx
        s_ref[kh, lc:lc + WA_BAND, :] = s_loc
        s_ref[kh, lc + WA_BAND:, :] = _wa_col_max([s_ctx, s_loc], _wa_sink_rows(WA_BLOCK, sinks(kh)))

    def sinks(kh):
        return [sink_ref[kh * WA_GROUP + g] for g in range(WA_GROUP)]

    def finish(n, kh, s_ref):
        _, k0 = window(n)
        lanes = slice(kh * LANES, (kh + 1) * LANES)
        return _wa_finish([s_ref[kh, :lc, :], s_ref[kh, lc:lc + WA_BAND, :]],
                          [vc_ref[:, lanes], v_ref[pl.ds(k0, WA_BAND), lanes]], sinks(kh),
                          m=s_ref[kh, lc + WA_BAND:, :])

    def step(n, cur_ref, nxt_ref):
        nxt = jnp.minimum(n + 1, n_blocks - 1)
        for kh in range(WA_KV_HEADS):
            scores(nxt, kh, nxt_ref)
        o = jnp.concatenate([finish(n, kh, cur_ref) for kh in range(WA_KV_HEADS)], axis=1)
        q0, _ = window(n)
        o_ref[pl.ds(q0, WA_BLOCK), :] = _group_rms(o, gain).astype(o_ref.dtype)

    return scores, step


def _attn_kernel(sink_ref, nq_ref, nk_ref, nv_ref, nkc_ref, nvc_ref, bias_ref, gn_ref,
                 wq_ref, wk_ref, wv_ref, wkc_ref, wvc_ref, gw_ref, on_ref, ow_ref,
                 nsa_ref, nsb_ref, wsa_ref, wsb_ref):
    na_scores, na_step = _na_steps(nq_ref, nk_ref, nv_ref, nkc_ref, nvc_ref, bias_ref, gn_ref, on_ref)
    wa_scores, wa_step = _wa_steps(sink_ref, wq_ref, wk_ref, wv_ref, wkc_ref, wvc_ref, gw_ref, ow_ref)
    n_blocks = wq_ref.shape[0] // WA_BLOCK

    for kh in range(WA_KV_HEADS):
        wa_scores(0, kh, wsa_ref)
    na_scores(0, nsa_ref)

    def body(i, carry):
        for half, (w_cur, w_nxt) in enumerate(((wsa_ref, wsb_ref), (wsb_ref, wsa_ref))):
            n = 2 * i + half
            wa_step(n, w_cur, w_nxt)
            na_step(2 * n, nsa_ref, nsb_ref)
            na_step(2 * n + 1, nsb_ref, nsa_ref)
        return carry

    lax.fori_loop(0, n_blocks // 2, body, 0)


def _attention(sink, nq, nk, nv, nkc, nvc, bias, layer, gain_na, wq, wk2, wv3, wkc2, wvc3, gain_wa):
    batch, seq, _ = nq.shape
    lc = nkc.shape[1]
    assert WA_BLOCK == 2 * GRID_W and (seq // WA_BLOCK) % 2 == 0
    per_b = lambda n, w: pl.BlockSpec((None, n, w), lambda b: (b, 0, 0))
    na_scratch = pltpu.VMEM((NA_HEADS * GRID_W, lc + NA_WIN_ROWS * GRID_W + LANES), F32)
    wa_scratch = pltpu.VMEM((WA_KV_HEADS, lc + WA_BAND + SUBLANES, WA_GROUP * WA_BLOCK), F32)
    return pl.pallas_call(
        _attn_kernel,
        out_shape=[jax.ShapeDtypeStruct((batch, seq, NA_WIDTH), BF16),
                   jax.ShapeDtypeStruct((batch, seq, WA_QW), BF16)],
        grid=(batch,),
        in_specs=[pl.BlockSpec(memory_space=pltpu.SMEM),
                  per_b(seq, NA_WIDTH), per_b(seq, NA_WIDTH), per_b(seq, NA_WIDTH),
                  per_b(lc, NA_WIDTH), per_b(lc, NA_WIDTH),
                  pl.BlockSpec((None,) + bias.shape[1:], lambda b: (layer, 0, 0, 0)),
                  pl.BlockSpec((1, NA_WIDTH), lambda b: (0, 0)),
                  per_b(seq, WA_QW), per_b(seq, WA_K2W), per_b(seq, WA_V3W),
                  per_b(lc, WA_K2W), per_b(lc, WA_V3W),
                  pl.BlockSpec((1, WA_QW), lambda b: (0, 0))],
        out_specs=[per_b(seq, NA_WIDTH), per_b(seq, WA_QW)],
        scratch_shapes=[na_scratch, na_scratch, wa_scratch, wa_scratch],
        compiler_params=_params(),
        name="attention",
    )(sink, nq, nk, nv, nkc, nvc, bias, gain_na, wq, wk2, wv3, wkc2, wvc3, gain_wa)


def _ctx_attn_kernel(sink_ref, qn_ref, kn_ref, vn_ref, qw_ref, kw_ref, vw_ref, gn_ref, gw_ref, on_ref, ow_ref):
    qs = _stack_heads(qn_ref[...], NA_HEADS)
    o4 = _softmax_pv([(_dot_nt(qs, kn_ref[...]), vn_ref[...])])
    on_ref[...] = _group_rms(_unstack_heads(o4, NA_HEADS), gn_ref[...]).astype(on_ref.dtype)
    outs = []
    for kh in range(WA_KV_HEADS):
        lanes = slice(kh * LANES, (kh + 1) * LANES)
        qg = qw_ref[:, kh * WA_GROUP * HEAD_DIM:(kh + 1) * WA_GROUP * HEAD_DIM]
        sinks = [sink_ref[kh * WA_GROUP + g] for g in range(WA_GROUP)]
        outs.append(_wa_finish(_wa_scores(qg, [(kw_ref[:, lanes], None)]), [vw_ref[:, lanes]], sinks))
    ow_ref[...] = _group_rms(jnp.concatenate(outs, axis=1), gw_ref[...]).astype(ow_ref.dtype)


def _ctx_attention(sink, qn, kn, vn, qw, kw2, vw2, gain_na, gain_wa):
    batch, lc, _ = qn.shape
    per_b = lambda w: pl.BlockSpec((None, lc, w), lambda b: (b, 0, 0))
    return pl.pallas_call(
        _ctx_attn_kernel,
        out_shape=[jax.ShapeDtypeStruct((batch, lc, NA_WIDTH), BF16),
                   jax.ShapeDtypeStruct((batch, lc, WA_QW), BF16)],
        grid=(batch,),
        in_specs=[pl.BlockSpec(memory_space=pltpu.SMEM),
                  per_b(NA_WIDTH), per_b(NA_WIDTH), per_b(NA_WIDTH),
                  per_b(WA_QW), per_b(WA_K2W), per_b(WA_V3W),
                  pl.BlockSpec((1, NA_WIDTH), lambda b: (0, 0)),
                  pl.BlockSpec((1, WA_QW), lambda b: (0, 0))],
        out_specs=[per_b(NA_WIDTH), per_b(WA_QW)],
        compiler_params=_params(),
        name="ctx_attention",
    )(sink, qn, kn, vn, qw, kw2, vw2, gain_na, gain_wa)


def _out_mlp_kernel(x_ref, mod_ref, na_ref, wa_ref, u_ref, uprev_ref, unext_ref, cvb_ref, cw_ref, cb_ref, g_ref,
                    wo_ref, g2_ref, w1_ref, w2_ref, o_ref, wobf_ref, *, seq_len, sub):
    tm = x_ref.shape[0]
    i = pl.program_id(0)
    _cast_weights_once([wo_ref], wobf_ref)
    u = u_ref[...]
    at_start = (i * tm) % seq_len == 0
    at_end = ((i + 1) * tm) % seq_len == 0
    prev_row = jnp.where(at_start, 0.0, uprev_ref[7:8, :])
    next_row = jnp.where(at_end, 0.0, unext_ref[0:1, :])
    row = _iota(u.shape, 0)
    up = jnp.where(row == 0, prev_row, pltpu.roll(u, 1, axis=0))
    dn = jnp.where(row == tm - 1, next_row, pltpu.roll(u, tm - 1, axis=0))
    y = cb_ref[...] + cw_ref[0:1, :] * up + cw_ref[1:2, :] * u + cw_ref[2:3, :] * dn
    cv = _group_rms(cvb_ref[...] * y, g_ref[...]).astype(BF16)
    for r0 in range(0, tm, sub):
        rows = slice(r0, r0 + sub)
        mixed = jnp.concatenate([na_ref[rows, :], cv[rows, :], wa_ref[rows, :]], axis=1)
        x = x_ref[rows, :] + mod_ref[2] * _dot(mixed, wobf_ref[...])
        ms = jnp.mean(x * x, axis=-1, keepdims=True)
        h = x * lax.rsqrt(ms + EPS) * g2_ref[...]
        h = (h * (1.0 + mod_ref[4]) + mod_ref[3]).astype(BF16)
        acc = None
        for c in range(0, FFN_DIM, FFN_CHUNK):
            a = jnp.maximum(_dot(h, w1_ref[:, c:c + FFN_CHUNK]), 0.0)
            part = _dot((a * a).astype(BF16), w2_ref[c:c + FFN_CHUNK, :])
            acc = part if acc is None else acc + part
        o_ref[rows, :] = x + mod_ref[5] * acc


def _out_mlp(x2d, mod, layer, mod_row, na_n, wa_n, cv_u, cv_b, conv_w, conv_b, gain_cv, w_o, g2, w1, w2, seq_len, tm):
    rows = x2d.shape[0]
    halo = 8
    per_tile = tm // halo
    last = rows // halo - 1
    row_spec = lambda w: pl.BlockSpec((tm, w), lambda i: (i, 0))
    const = lambda shape: pl.BlockSpec(shape, lambda i: (0,) * len(shape), pipeline_mode=pl.Buffered(1))
    of_layer = lambda shape: pl.BlockSpec((None,) + shape, lambda i: (layer,) + (0,) * len(shape),
                                          pipeline_mode=pl.Buffered(1))
    return pl.pallas_call(
        functools.partial(_out_mlp_kernel, seq_len=seq_len, sub=min(tm, MLP_SUB)),
        out_shape=jax.ShapeDtypeStruct((rows, D_MODEL), F32),
        grid=(rows // tm,),
        in_specs=[
            row_spec(D_MODEL),
            pl.BlockSpec((None, None, 6, 1, D_MODEL), lambda i: (layer, mod_row(i), 0, 0, 0)),
            row_spec(NA_WIDTH), row_spec(WA_QW), row_spec(CONV_CH),
            pl.BlockSpec((halo, CONV_CH), lambda i: (jnp.maximum(i * per_tile - 1, 0), 0)),
            pl.BlockSpec((halo, CONV_CH), lambda i: (jnp.minimum((i + 1) * per_tile, last), 0)),
            row_spec(CONV_CH),
            const((3, CONV_CH)), const((1, CONV_CH)), const((1, CONV_CH)),
            of_layer((D_MODEL, D_MODEL)),
            const((1, D_MODEL)),
            of_layer((D_MODEL, FFN_DIM)),
            of_layer((FFN_DIM, D_MODEL)),
        ],
        out_specs=row_spec(D_MODEL),
        scratch_shapes=[pltpu.VMEM((D_MODEL, D_MODEL), BF16)],
        compiler_params=_params(dimension_semantics=("arbitrary",)),
        name="out_mlp",
    )(x2d, mod, na_n, wa_n, cv_u, cv_u, cv_u, cv_b, conv_w, conv_b, gain_cv, w_o, g2, w1, w2)


def _rope_tables(seq):
    quarter = HEAD_DIM // 4
    inv = ROPE_BASE ** (-jnp.arange(quarter, dtype=F32) / quarter)
    t = jnp.arange(seq)
    ang_r = (t // GRID_W).astype(F32)[:, None] * inv[None, :]
    ang_c = (t % GRID_W).astype(F32)[:, None] * inv[None, :]
    cos = jnp.concatenate([jnp.cos(ang_r), jnp.cos(ang_r), jnp.cos(ang_c), jnp.cos(ang_c)], axis=1)
    sin = jnp.concatenate([-jnp.sin(ang_r), jnp.sin(ang_r), -jnp.sin(ang_c), jnp.sin(ang_c)], axis=1)
    reps = LANES // HEAD_DIM
    return jnp.tile(cos, (1, reps)), jnp.tile(sin, (1, reps))


def kernel(x, c, ctx, c_ctx, w_mod, b_mod, g_norm1, g_norm2, w_in, na_q_gain, na_k_gain, na_rpb,
           conv_w, conv_bias, wa_q_gain, wa_k_gain, wa_sink, g_out, w_o, w_fc1, w_fc2):
    batch, seq, d = x.shape
    lc = ctx.shape[1]
    depth = w_mod.shape[0]
    assert d == D_MODEL and seq % WA_BLOCK == 0 and seq % GRID_W == 0 and batch < MOD_ROWS
    tm = 512
    tm_in = 1024
    tm_ctx = 256
    assert seq % tm == 0 and seq % tm_in == 0 and lc % tm_ctx == 0

    cond = jnp.zeros((MOD_ROWS, d), F32).at[:batch].set(c).at[batch].set(c_ctx)
    mod = _modulation(cond, w_mod, b_mod).reshape(depth, MOD_ROWS, 6, 1, d)
    rpb_bias = _rpb_tables(na_rpb)
    rope_tabs = _rope_tables(seq)

    lat_row = lambda i: i // (seq // tm)
    lat_row_in = lambda i: i // (seq // tm_in)
    ctx_row = lambda i: batch

    w1_bf = w_fc1.astype(BF16)
    w2_bf = w_fc2.astype(BF16)
    xs = x.reshape(batch * seq, d)
    cs = ctx.reshape(batch * lc, d)
    for l in range(depth):
        last = l == depth - 1
        g1 = g_norm1[l].reshape(1, d)
        g2 = g_norm2[l].reshape(1, d)
        gains = [jnp.tile(na_q_gain[l], NA_HEADS).reshape(1, -1), jnp.tile(na_k_gain[l], NA_HEADS).reshape(1, -1),
                 jnp.tile(wa_q_gain[l], WA_HEADS).reshape(1, -1), jnp.tile(wa_k_gain[l], WA_KV_HEADS).reshape(1, -1)]
        go_na = g_out[l, :NA_WIDTH].reshape(1, -1)
        go_cv = g_out[l, NA_WIDTH:NA_WIDTH + CONV_CH].reshape(1, -1)
        go_wa = g_out[l, NA_WIDTH + CONV_CH:].reshape(1, -1)
        cb = conv_bias[l].reshape(1, -1)

        naq, nak, nav, cvu, cvb, waq, wak, wav = _in_proj(xs, mod, l, lat_row_in, g1, w_in, gains, rope_tabs, tm_in)
        if last:
            cnak, cnav, cwak, cwav = _in_proj(cs, mod, l, ctx_row, g1, w_in, gains, None, tm_ctx, kv_only=True)
        else:
            cnaq, cnak, cnav, ccvu, ccvb, cwaq, cwak, cwav = _in_proj(cs, mod, l, ctx_row, g1, w_in, gains, None, tm_ctx)

        b3 = lambda a, n: a.reshape(batch, n, a.shape[-1])
        na_n, wa_n = _attention(wa_sink[l], b3(naq, seq), b3(nak, seq), b3(nav, seq), b3(cnak, lc), b3(cnav, lc),
                                rpb_bias, l, go_na, b3(waq, seq), b3(wak, seq), b3(wav, seq), b3(cwak, lc),
                                b3(cwav, lc), go_wa)
        xs = _out_mlp(xs, mod, l, lat_row, na_n.reshape(batch * seq, -1), wa_n.reshape(batch * seq, -1),
                      cvu, cvb, conv_w[l], cb, go_cv, w_o, g2, w1_bf, w2_bf, seq, tm)
        if not last:
            cna_n, cwa_n = _ctx_attention(wa_sink[l], b3(cnaq, lc), b3(cnak, lc), b3(cnav, lc),
                                          b3(cwaq, lc), b3(cwak, lc), b3(cwav, lc), go_na, go_wa)
            cs = _out_mlp(cs, mod, l, ctx_row, cna_n.reshape(batch * lc, -1), cwa_n.reshape(batch * lc, -1),
                          ccvu, ccvb, conv_w[l], cb, go_cv, w_o, g2, w1_bf, w2_bf, lc, tm_ctx)
    return xs.reshape(batch, seq, d)
```

```python
import functools

import jax
import jax.numpy as jnp
from jax import lax
from jax.experimental import pallas as pl
from jax.experimental.pallas import tpu as pltpu

D_MODEL = 1024
GRID_W = 64
HEAD_DIM = 64
NA_HEADS = 4
NA_WIDTH = NA_HEADS * HEAD_DIM
CONV_CH = 256
WA_HEADS = 8
WA_KV_HEADS = 2
WA_GROUP = WA_HEADS // WA_KV_HEADS
WA_QW = WA_HEADS * HEAD_DIM
WA_KVW = WA_KV_HEADS * HEAD_DIM
WA_K2W = 2 * WA_KVW
WA_V3W = 2 * WA_KVW
NA_WIN_ROWS = 8
NA_WIN_COLS = 16
NA_DROWS = 2 * NA_WIN_ROWS - 1
NA_DCOLS = 2 * NA_WIN_COLS - 1
WA_WINDOW = 128
WA_BLOCK = 128
WA_BAND = 3 * WA_BLOCK
FFN_DIM = 4 * D_MODEL
FFN_CHUNK = 1024
ROPE_BASE = 10000.0
EPS = 1e-6
NEG_INF = -1e30
IN_WIDTH = 2304
OFF_NA_Q, OFF_NA_K, OFF_NA_V = 0, 256, 512
OFF_CV_X, OFF_CV_B, OFF_CV_C = 768, 1024, 1280
OFF_WA_Q, OFF_WA_K, OFF_WA_V = 1536, 2048, 2176
MOD_ROWS = 16
MOD_TN = 1024
IN_SUB = 256
MLP_SUB = 256
LANES = 128
MXU_DIM = 256
VMEM_LIMIT = 60 * 1024 * 1024
LOG2E = 1.4426950408889634
Q_SCALE = LOG2E * HEAD_DIM ** -0.5

F32 = jnp.float32
BF16 = jnp.bfloat16


def _dot(a, b):
    return jnp.dot(a, b, preferred_element_type=F32)


def _dot_nt(a, b):
    return lax.dot_general(a, b, (((1,), (1,)), ((), ())), preferred_element_type=F32)


def _split_bf16(a):
    hi = a.astype(BF16)
    lo = (a - hi.astype(F32)).astype(BF16)
    return hi, lo


def _iota(shape, dim):
    return lax.broadcasted_iota(jnp.int32, shape, dim)


def _params(**kw):
    return pltpu.CompilerParams(vmem_limit_bytes=VMEM_LIMIT, **kw)


def _mod_kernel(cond_ref, w_ref, b_ref, o_ref):
    a = cond_ref[...]
    a = a * (1.0 / (1.0 + jnp.exp(-a)))
    ah, al = _split_bf16(a)
    wh, wl = _split_bf16(w_ref[...])
    o_ref[...] = _dot(ah, wh) + _dot(al, wh) + _dot(ah, wl) + b_ref[...]


def _modulation(cond, w_mod, b_mod):
    depth = w_mod.shape[0]
    n_out = w_mod.shape[2]
    return pl.pallas_call(
        _mod_kernel,
        out_shape=jax.ShapeDtypeStruct((depth, MOD_ROWS, n_out), F32),
        grid=(depth, n_out // MOD_TN),
        in_specs=[
            pl.BlockSpec((MOD_ROWS, D_MODEL), lambda l, j: (0, 0)),
            pl.BlockSpec((None, D_MODEL, MOD_TN), lambda l, j: (l, 0, j)),
            pl.BlockSpec((None, 1, MOD_TN), lambda l, j: (l, 0, j)),
        ],
        out_specs=pl.BlockSpec((None, MOD_ROWS, MOD_TN), lambda l, j: (l, 0, j)),
        compiler_params=_params(),
        name="modulation",
    )(cond, w_mod, b_mod.reshape(depth, 1, n_out))


def _rpb_kernel(rpb_ref, o_ref):
    l = pl.program_id(0)
    shape = (GRID_W, LANES)
    q = _iota(shape, 0)
    lane = _iota(shape, 1)
    k = lane % GRID_W
    left = lane < GRID_W
    dc = jnp.clip(k - q, -(NA_WIN_COLS - 1), NA_WIN_COLS - 1) + (NA_WIN_COLS - 1)
    col_start = jnp.clip(q - NA_WIN_COLS // 2, 0, GRID_W - NA_WIN_COLS)
    col_ok = (k >= col_start) & (k < col_start + NA_WIN_COLS)
    base_l = l * (NA_HEADS * NA_DROWS * NA_DCOLS)

    def body(d, carry):
        for h in range(NA_HEADS):
            base = base_l + (h * NA_DROWS + d) * NA_DCOLS
            t = jnp.zeros(shape, F32)
            for i in range(NA_DCOLS):
                val = jnp.where(left, rpb_ref[base + i], rpb_ref[base + NA_DCOLS + i])
                t = jnp.where(dc == i, val, t)
            o_ref[d, h * GRID_W:(h + 1) * GRID_W, :] = jnp.where(col_ok, t * LOG2E, NEG_INF)
        return carry

    lax.fori_loop(0, NA_DROWS - 1, body, 0)


def _rpb_tables(na_rpb):
    depth = na_rpb.shape[0]
    return pl.pallas_call(
        _rpb_kernel,
        out_shape=jax.ShapeDtypeStruct((depth, NA_DROWS - 1, NA_HEADS * GRID_W, LANES), F32),
        grid=(depth,),
        in_specs=[pl.BlockSpec(memory_space=pltpu.SMEM)],
        out_specs=pl.BlockSpec((None, NA_DROWS - 1, NA_HEADS * GRID_W, LANES), lambda l: (l, 0, 0, 0)),
        compiler_params=_params(),
        name="rpb_tables",
    )(na_rpb.reshape(-1))


def _head_rms(t, gain):
    width = t.shape[1]
    sq = (t * t).astype(BF16)
    blk = min(width, MXU_DIM)
    ones = jnp.where(_iota((blk, blk), 0) // HEAD_DIM == _iota((blk, blk), 1) // HEAD_DIM, 1.0, 0.0).astype(BF16)
    sums = [_dot(sq[:, c:c + blk], ones) for c in range(0, width, blk)]
    ss = sums[0] if len(sums) == 1 else jnp.concatenate(sums, axis=1)
    return t * lax.rsqrt(ss * (1.0 / HEAD_DIM) + EPS) * gain


def _rope(t, cos, sin):
    lane = _iota((t.shape[0], LANES), 1)
    first = (lane % (HEAD_DIM // 2)) < (HEAD_DIM // 4)
    out = []
    for c in range(0, t.shape[1], LANES):
        u = t[:, c:c + LANES]
        partner = jnp.where(first, pltpu.roll(u, LANES - HEAD_DIM // 4, axis=1), pltpu.roll(u, HEAD_DIM // 4, axis=1))
        out.append(u * cos + partner * sin)
    return out[0] if len(out) == 1 else jnp.concatenate(out, axis=1)


def _dup_kv(t):
    lane = _iota(t.shape, 1)
    swapped = pltpu.roll(t, HEAD_DIM, axis=1)
    left = lane < HEAD_DIM
    return jnp.concatenate([jnp.where(left, t, swapped), jnp.where(left, swapped, t)], axis=1)


def _cast_weights_once(w_refs, wbf_ref):
    @pl.when(pl.program_id(0) == 0)
    def _():
        c = 0
        for w_ref in w_refs:
            wbf_ref[:, c:c + w_ref.shape[1]] = w_ref[...].astype(BF16)
            c += w_ref.shape[1]


def _inproj_kernel(*refs, rope, kv_only, n_w, sub):
    x_ref, mod_ref, g1_ref = refs[:3]
    w_refs = refs[3:3 + n_w]
    gnq_ref, gnk_ref, gwq_ref, gwk_ref = refs[3 + n_w:7 + n_w]
    rest = refs[7 + n_w:]
    if rope:
        cos_ref, sin_ref = rest[:2]
        rest = rest[2:]
    *out_refs, wbf_ref = rest
    if kv_only:
        nak_ref, nav_ref, wak_ref, wav_ref = out_refs
        o_nk, o_nv, o_wk, o_wv, o_end = 0, NA_WIDTH, 2 * NA_WIDTH, 2 * NA_WIDTH + WA_KVW, 2 * NA_WIDTH + 2 * WA_KVW
    else:
        naq_ref, nak_ref, nav_ref, cvu_ref, cvb_ref, waq_ref, wak_ref, wav_ref = out_refs
        o_nk, o_nv, o_wk, o_wv, o_end = OFF_NA_K, OFF_NA_V, OFF_WA_K, OFF_WA_V, IN_WIDTH
    _cast_weights_once(w_refs, wbf_ref)
    for r0 in range(0, x_ref.shape[0], sub):
        rows = slice(r0, r0 + sub)
        x = x_ref[rows, :]
        ms = jnp.mean(x * x, axis=-1, keepdims=True)
        h = x * lax.rsqrt(ms + EPS) * g1_ref[...]
        h = h * (1.0 + mod_ref[1]) + mod_ref[0]
        p = _dot(h.astype(BF16), wbf_ref[...])
        nak_ref[rows, :] = _head_rms(p[:, o_nk:o_nk + NA_WIDTH], gnk_ref[...]).astype(BF16)
        nav_ref[rows, :] = p[:, o_nv:o_nv + NA_WIDTH].astype(BF16)
        wk = _head_rms(p[:, o_wk:o_wv], gwk_ref[...])
        if rope:
            wk = _rope(wk, cos_ref[rows, :], sin_ref[rows, :])
        wak_ref[rows, :] = _dup_kv(wk).astype(BF16)
        vv = p[:, o_wv:o_end]
        left = _iota(vv.shape, 1) < HEAD_DIM
        wav_ref[rows, :] = jnp.concatenate([jnp.where(left, vv, 1.0), jnp.where(left, pltpu.roll(vv, HEAD_DIM, axis=1), 1.0)],
                                           axis=1).astype(BF16)
        if kv_only:
            continue
        naq_ref[rows, :] = (_head_rms(p[:, OFF_NA_Q:OFF_NA_K], gnq_ref[...]) * Q_SCALE).astype(BF16)
        cvu_ref[rows, :] = p[:, OFF_CV_C:OFF_WA_Q] * p[:, OFF_CV_X:OFF_CV_B]
        cvb_ref[rows, :] = p[:, OFF_CV_B:OFF_CV_C]
        wq = _head_rms(p[:, OFF_WA_Q:OFF_WA_K], gwq_ref[...])
        if rope:
            wq = _rope(wq, cos_ref[rows, :], sin_ref[rows, :])
        waq_ref[rows, :] = (wq * Q_SCALE).astype(BF16)


def _in_proj(x2d, mod, layer, mod_row, g1, w_in, gains, rope_tabs, tm, kv_only=False):
    rows = x2d.shape[0]
    rope = rope_tabs is not None
    row_spec = lambda w: pl.BlockSpec((tm, w), lambda i: (i, 0))
    const = lambda shape: pl.BlockSpec(shape, lambda i: (0,) * len(shape))
    once = pl.Buffered(1)
    if kv_only:
        col_blocks = [OFF_NA_K // MXU_DIM, OFF_NA_V // MXU_DIM, OFF_WA_K // MXU_DIM]
        widths = [(NA_WIDTH, BF16), (NA_WIDTH, BF16), (WA_K2W, BF16), (WA_V3W, BF16)]
    else:
        col_blocks = list(range(IN_WIDTH // MXU_DIM))
        widths = [(NA_WIDTH, BF16), (NA_WIDTH, BF16), (NA_WIDTH, BF16), (CONV_CH, F32), (CONV_CH, F32),
                  (WA_QW, BF16), (WA_K2W, BF16), (WA_V3W, BF16)]
    w_specs = [pl.BlockSpec((None, D_MODEL, MXU_DIM), functools.partial(lambda j, i: (layer, 0, j), j),
                            pipeline_mode=once) for j in col_blocks]
    n_w = len(w_specs)
    in_specs = [
        row_spec(D_MODEL),
        pl.BlockSpec((None, None, 6, 1, D_MODEL), lambda i: (layer, mod_row(i), 0, 0, 0)),
        const((1, D_MODEL)),
        *w_specs,
        const((1, NA_WIDTH)), const((1, NA_WIDTH)), const((1, WA_QW)), const((1, WA_KVW)),
    ]
    args = [x2d, mod, g1, *([w_in] * n_w), *gains]
    if rope:
        seq_tiles = rope_tabs[0].shape[0] // tm
        in_specs += [pl.BlockSpec((tm, LANES), lambda i: (i % seq_tiles, 0))] * 2
        args += list(rope_tabs)
    return pl.pallas_call(
        functools.partial(_inproj_kernel, rope=rope, kv_only=kv_only, n_w=n_w, sub=min(tm, IN_SUB)),
        out_shape=[jax.ShapeDtypeStruct((rows, w), dt) for w, dt in widths],
        grid=(rows // tm,),
        in_specs=in_specs,
        out_specs=[row_spec(w) for w, _ in widths],
        scratch_shapes=[pltpu.VMEM((D_MODEL, n_w * MXU_DIM), BF16)],
        compiler_params=_params(dimension_semantics=("arbitrary",)),
        name="in_proj_rope" if rope else ("in_proj_ctx_kv" if kv_only else "in_proj_ctx"),
    )(*args)


def _lane_chunks(s):
    return [s[:, c:c + LANES] for c in range(0, s.shape[1], LANES)]


def _row_max(scores, extra=None):
    chunks = [ch for s in scores for ch in _lane_chunks(s)]
    if extra is not None:
        chunks.append(extra)
    folded = functools.reduce(jnp.maximum, chunks)
    return jnp.broadcast_to(jnp.max(folded, axis=-1, keepdims=True), folded.shape)


def _softmax_pv(parts, extra=None, sums_from_values=False, m=None):
    lane_chunks = _lane_chunks
    if m is None:
        m = _row_max([s for s, _ in parts], extra)
    ps = [jnp.concatenate([jnp.exp2(ch - m) for ch in lane_chunks(s)], axis=1) for s, _ in parts]
    pv = _dot(jnp.concatenate([p.astype(BF16) for p in ps], axis=1),
              jnp.concatenate([v for _, v in parts], axis=0))
    if sums_from_values:
        denom = pv[:, -LANES:]
        pv = pv[:, :-LANES]
        if extra is not None:
            denom = denom + jnp.exp2(extra - m)
        return pv / denom
    psum = functools.reduce(jnp.add, [ch for p in ps for ch in lane_chunks(p)])
    if extra is not None:
        psum = psum + jnp.where(_iota(extra.shape, 1) == 0, jnp.exp2(extra - m), 0.0)
    return pv / jnp.sum(psum, axis=-1, keepdims=True)


def _stack_heads(q, n_heads):
    head = _iota(q.shape, 1) // HEAD_DIM
    zero = jnp.zeros_like(q)
    return jnp.concatenate([jnp.where(head == h, q, zero) for h in range(n_heads)], axis=0)


def _unstack_heads(o, n_heads):
    rows = o.shape[0] // n_heads
    head = _iota((rows, o.shape[1]), 1) // HEAD_DIM
    out = jnp.where(head == 0, o[:rows], 0.0)
    for h in range(1, n_heads):
        out = jnp.where(head == h, o[h * rows:(h + 1) * rows], out)
    return out


def _group_rms(o, gain):
    ms = jnp.mean(o * o, axis=-1, keepdims=True)
    return o * lax.rsqrt(ms + EPS) * gain


SUBLANES = 8


def _wa_scores(qg, keys):
    rows = qg.shape[0]
    left = _iota((rows, LANES), 1) < HEAD_DIM
    zero = jnp.zeros((rows, LANES), qg.dtype)
    blocks = []
    for pair in range(WA_GROUP // 2):
        qp = qg[:, pair * LANES:(pair + 1) * LANES]
        blocks += [jnp.where(left, qp, zero), jnp.where(left, zero, qp)]
    qs = jnp.concatenate(blocks, axis=0)
    scores = []
    for k2, bias in keys:
        s = _dot_nt(k2, qs)
        if bias is not None:
            s = s + jnp.concatenate([bias] * WA_GROUP, axis=1)
        scores.append(s)
    return scores


def _wa_sink_rows(rows, sinks):
    return jnp.concatenate([jnp.full((SUBLANES, rows), sinks[g] * LOG2E, F32) for g in range(WA_GROUP)], axis=1)


def _wa_col_max(scores, sink_rows):
    m = sink_rows[:1]
    for s in scores:
        m = jnp.maximum(m, jnp.max(s, axis=0, keepdims=True))
    return jnp.broadcast_to(m, sink_rows.shape)


def _wa_finish(scores, values, sinks, m=None):
    rows = scores[0].shape[1] // WA_GROUP
    sink_rows = _wa_sink_rows(rows, sinks)
    if m is None:
        m = _wa_col_max(scores, sink_rows)
    p = jnp.concatenate([jnp.exp2(s - m[:1]).astype(BF16) for s in scores], axis=0)
    v = jnp.concatenate(values, axis=0)
    ot = lax.dot_general(v, p, (((0,), (0,)), ((), ())), preferred_element_type=F32)
    denom = ot[HEAD_DIM:HEAD_DIM + SUBLANES] + jnp.exp2(sink_rows - m)
    on = ot[:HEAD_DIM] / denom[:1]
    pairs = [jnp.concatenate([on[:, (2 * pr) * rows:(2 * pr + 1) * rows],
                              on[:, (2 * pr + 1) * rows:(2 * pr + 2) * rows]], axis=0).T
             for pr in range(WA_GROUP // 2)]
    return jnp.concatenate(pairs, axis=1)


def _na_steps(q_ref, k_ref, v_ref, kc_ref, vc_ref, bias_ref, g_ref, o_ref):
    n_rows = q_ref.shape[0] // GRID_W
    win = NA_WIN_ROWS * GRID_W
    lc = kc_ref.shape[0]
    gain = g_ref[...]

    def window(r):
        start = jnp.clip(r - NA_WIN_ROWS // 2, 0, n_rows - NA_WIN_ROWS)
        return pl.multiple_of(start * GRID_W, GRID_W), start - r + (NA_WIN_ROWS - 1)

    def scores(r, s_ref):
        tok0, d0 = window(r)
        q = q_ref[pl.ds(pl.multiple_of(r * GRID_W, GRID_W), GRID_W), :]
        qs = _stack_heads(q, NA_HEADS)
        bias = jnp.concatenate([bias_ref[d0 + 2 * j] for j in range(NA_WIN_ROWS // 2)], axis=1)
        s_ctx = _dot_nt(qs, kc_ref[...])
        s_loc = _dot_nt(qs, k_ref[pl.ds(tok0, win), :]) + bias
        s_ref[:, :lc] = s_ctx
        s_ref[:, lc:lc + win] = s_loc
        s_ref[:, lc + win:] = _row_max([s_ctx, s_loc])

    def finish(r, s_ref):
        tok0, _ = window(r)
        o4 = _softmax_pv([(s_ref[:, :lc], vc_ref[...]), (s_ref[:, lc:lc + win], v_ref[pl.ds(tok0, win), :])],
                         m=s_ref[:, lc + win:])
        o = _unstack_heads(o4, NA_HEADS)
        o_ref[pl.ds(pl.multiple_of(r * GRID_W, GRID_W), GRID_W), :] = _group_rms(o, gain).astype(o_ref.dtype)

    def step(r, cur_ref, nxt_ref):
        scores(jnp.minimum(r + 1, n_rows - 1), nxt_ref)
        finish(r, cur_ref)

    return scores, step


def _wa_steps(sink_ref, q_ref, k_ref, v_ref, kc_ref, vc_ref, g_ref, o_ref):
    seq = q_ref.shape[0]
    n_blocks = seq // WA_BLOCK
    lc = kc_ref.shape[0]
    gain = g_ref[...]
    key = _iota((WA_BAND, WA_BLOCK), 0)
    qry = _iota((WA_BAND, WA_BLOCK), 1)

    def window(n):
        q0 = pl.multiple_of(n * WA_BLOCK, WA_BLOCK)
        k0 = pl.multiple_of(jnp.clip(q0 - WA_BLOCK, 0, seq - WA_BAND), WA_BLOCK)
        return q0, k0

    def scores(n, kh, s_ref):
        q0, k0 = window(n)
        band = jnp.where(jnp.abs((k0 + key) - (q0 + qry)) <= WA_WINDOW, 0.0, NEG_INF)
        lanes = slice(kh * LANES, (kh + 1) * LANES)
        qg = q_ref[pl.ds(q0, WA_BLOCK), kh * WA_GROUP * HEAD_DIM:(kh + 1) * WA_GROUP * HEAD_DIM]
        s_ctx, s_loc = _wa_scores(qg, [(kc_ref[:, lanes], None), (k_ref[pl.ds(k0, WA_BAND), lanes], band)])
        s_ref[kh, :lc, :] = s_ctx
        s_ref[kh, lc:lc + WA_BAND, :] = s_loc
        s_ref[kh, lc + WA_BAND:, :] = _wa_col_max([s_ctx, s_loc], _wa_sink_rows(WA_BLOCK, sinks(kh)))

    def sinks(kh):
        return [sink_ref[kh * WA_GROUP + g] for g in range(WA_GROUP)]

    def finish(n, kh, s_ref):
        _, k0 = window(n)
        lanes = slice(kh * LANES, (kh + 1) * LANES)
        return _wa_finish([s_ref[kh, :lc, :], s_ref[kh, lc:lc + WA_BAND, :]],
                          [vc_ref[:, lanes], v_ref[pl.ds(k0, WA_BAND), lanes]], sinks(kh),
                          m=s_ref[kh, lc + WA_BAND:, :])

    def step(n, cur_ref, nxt_ref):
        nxt = jnp.minimum(n + 1, n_blocks - 1)
        for kh in range(WA_KV_HEADS):
            scores(nxt, kh, nxt_ref)
        o = jnp.concatenate([finish(n, kh, cur_ref) for kh in range(WA_KV_HEADS)], axis=1)
        q0, _ = window(n)
        o_ref[pl.ds(q0, WA_BLOCK), :] = _group_rms(o, gain).astype(o_ref.dtype)

    return scores, step


def _attn_kernel(sink_ref, nq_ref, nk_ref, nv_ref, nkc_ref, nvc_ref, bias_ref, gn_ref,
                 wq_ref, wk_ref, wv_ref, wkc_ref, wvc_ref, gw_ref, on_ref, ow_ref,
                 nsa_ref, nsb_ref, wsa_ref, wsb_ref):
    na_scores, na_step = _na_steps(nq_ref, nk_ref, nv_ref, nkc_ref, nvc_ref, bias_ref, gn_ref, on_ref)
    wa_scores, wa_step = _wa_steps(sink_ref, wq_ref, wk_ref, wv_ref, wkc_ref, wvc_ref, gw_ref, ow_ref)
    n_blocks = wq_ref.shape[0] // WA_BLOCK

    for kh in range(WA_KV_HEADS):
        wa_scores(0, kh, wsa_ref)
    na_scores(0, nsa_ref)

    def body(i, carry):
        for half, (w_cur, w_nxt) in enumerate(((wsa_ref, wsb_ref), (wsb_ref, wsa_ref))):
            n = 2 * i + half
            wa_step(n, w_cur, w_nxt)
            na_step(2 * n, nsa_ref, nsb_ref)
            na_step(2 * n + 1, nsb_ref, nsa_ref)
        return carry

    lax.fori_loop(0, n_blocks // 2, body, 0)


def _attention(sink, nq, nk, nv, nkc, nvc, bias, layer, gain_na, wq, wk2, wv3, wkc2, wvc3, gain_wa):
    batch, seq, _ = nq.shape
    lc = nkc.shape[1]
    assert WA_BLOCK == 2 * GRID_W and (seq // WA_BLOCK) % 2 == 0
    per_b = lambda n, w: pl.BlockSpec((None, n, w), lambda b: (b, 0, 0))
    na_scratch = pltpu.VMEM((NA_HEADS * GRID_W, lc + NA_WIN_ROWS * GRID_W + LANES), F32)
    wa_scratch = pltpu.VMEM((WA_KV_HEADS, lc + WA_BAND + SUBLANES, WA_GROUP * WA_BLOCK), F32)
    return pl.pallas_call(
        _attn_kernel,
        out_shape=[jax.ShapeDtypeStruct((batch, seq, NA_WIDTH), BF16),
                   jax.ShapeDtypeStruct((batch, seq, WA_QW), BF16)],
        grid=(batch,),
        in_specs=[pl.BlockSpec(memory_space=pltpu.SMEM),
                  per_b(seq, NA_WIDTH), per_b(seq, NA_WIDTH), per_b(seq, NA_WIDTH),
                  per_b(lc, NA_WIDTH), per_b(lc, NA_WIDTH),
                  pl.BlockSpec((None,) + bias.shape[1:], lambda b: (layer, 0, 0, 0)),
                  pl.BlockSpec((1, NA_WIDTH), lambda b: (0, 0)),
                  per_b(seq, WA_QW), per_b(seq, WA_K2W), per_b(seq, WA_V3W),
                  per_b(lc, WA_K2W), per_b(lc, WA_V3W),
                  pl.BlockSpec((1, WA_QW), lambda b: (0, 0))],
        out_specs=[per_b(seq, NA_WIDTH), per_b(seq, WA_QW)],
        scratch_shapes=[na_scratch, na_scratch, wa_scratch, wa_scratch],
        compiler_params=_params(),
        name="attention",
    )(sink, nq, nk, nv, nkc, nvc, bias, gain_na, wq, wk2, wv3, wkc2, wvc3, gain_wa)


def _ctx_attn_kernel(sink_ref, qn_ref, kn_ref, vn_ref, qw_ref, kw_ref, vw_ref, gn_ref, gw_ref, on_ref, ow_ref):
    qs = _stack_heads(qn_ref[...], NA_HEADS)
    o4 = _softmax_pv([(_dot_nt(qs, kn_ref[...]), vn_ref[...])])
    on_ref[...] = _group_rms(_unstack_heads(o4, NA_HEADS), gn_ref[...]).astype(on_ref.dtype)
    outs = []
    for kh in range(WA_KV_HEADS):
        lanes = slice(kh * LANES, (kh + 1) * LANES)
        qg = qw_ref[:, kh * WA_GROUP * HEAD_DIM:(kh + 1) * WA_GROUP * HEAD_DIM]
        sinks = [sink_ref[kh * WA_GROUP + g] for g in range(WA_GROUP)]
        outs.append(_wa_finish(_wa_scores(qg, [(kw_ref[:, lanes], None)]), [vw_ref[:, lanes]], sinks))
    ow_ref[...] = _group_rms(jnp.concatenate(outs, axis=1), gw_ref[...]).astype(ow_ref.dtype)


def _ctx_attention(sink, qn, kn, vn, qw, kw2, vw2, gain_na, gain_wa):
    batch, lc, _ = qn.shape
    per_b = lambda w: pl.BlockSpec((None, lc, w), lambda b: (b, 0, 0))
    return pl.pallas_call(
        _ctx_attn_kernel,
        out_shape=[jax.ShapeDtypeStruct((batch, lc, NA_WIDTH), BF16),
                   jax.ShapeDtypeStruct((batch, lc, WA_QW), BF16)],
        grid=(batch,),
        in_specs=[pl.BlockSpec(memory_space=pltpu.SMEM),
                  per_b(NA_WIDTH), per_b(NA_WIDTH), per_b(NA_WIDTH),
                  per_b(WA_QW), per_b(WA_K2W), per_b(WA_V3W),
                  pl.BlockSpec((1, NA_WIDTH), lambda b: (0, 0)),
                  pl.BlockSpec((1, WA_QW), lambda b: (0, 0))],
        out_specs=[per_b(NA_WIDTH), per_b(WA_QW)],
        compiler_params=_params(),
        name="ctx_attention",
    )(sink, qn, kn, vn, qw, kw2, vw2, gain_na, gain_wa)


def _out_mlp_kernel(x_ref, mod_ref, na_ref, wa_ref, u_ref, uprev_ref, unext_ref, cvb_ref, cw_ref, cb_ref, g_ref,
                    g2_ref, *rest, seq_len, sub):
    n_ffn = FFN_DIM // FFN_CHUNK
    n_wo = D_MODEL // MXU_DIM
    wo_refs, w1_refs, w2_refs = rest[:n_wo], rest[n_wo:n_wo + n_ffn], rest[n_wo + n_ffn:n_wo + 2 * n_ffn]
    o_ref, wobf_ref = rest[n_wo + 2 * n_ffn:]
    tm = x_ref.shape[0]
    i = pl.program_id(0)
    _cast_weights_once(wo_refs, wobf_ref)
    u = u_ref[...]
    at_start = (i * tm) % seq_len == 0
    at_end = ((i + 1) * tm) % seq_len == 0
    prev_row = jnp.where(at_start, 0.0, uprev_ref[7:8, :])
    next_row = jnp.where(at_end, 0.0, unext_ref[0:1, :])
    row = _iota(u.shape, 0)
    up = jnp.where(row == 0, prev_row, pltpu.roll(u, 1, axis=0))
    dn = jnp.where(row == tm - 1, next_row, pltpu.roll(u, tm - 1, axis=0))
    y = cb_ref[...] + cw_ref[0:1, :] * up + cw_ref[1:2, :] * u + cw_ref[2:3, :] * dn
    cv = _group_rms(cvb_ref[...] * y, g_ref[...]).astype(BF16)
    for r0 in range(0, tm, sub):
        rows = slice(r0, r0 + sub)
        mixed = jnp.concatenate([na_ref[rows, :], cv[rows, :], wa_ref[rows, :]], axis=1)
        x = x_ref[rows, :] + mod_ref[2] * _dot(mixed, wobf_ref[...])
        ms = jnp.mean(x * x, axis=-1, keepdims=True)
        h = x * lax.rsqrt(ms + EPS) * g2_ref[...]
        h = (h * (1.0 + mod_ref[4]) + mod_ref[3]).astype(BF16)
        acc = None
        for w1_ref, w2_ref in zip(w1_refs, w2_refs):
            a = jnp.maximum(_dot(h, w1_ref[...]), 0.0)
            part = _dot((a * a).astype(BF16), w2_ref[...])
            acc = part if acc is None else acc + part
        o_ref[rows, :] = x + mod_ref[5] * acc


def _out_mlp(x2d, mod, layer, mod_row, na_n, wa_n, cv_u, cv_b, conv_w, conv_b, gain_cv, w_o, g2, w1, w2, seq_len, tm):
    rows = x2d.shape[0]
    halo = 8
    per_tile = tm // halo
    last = rows // halo - 1
    row_spec = lambda w: pl.BlockSpec((tm, w), lambda i: (i, 0))
    const = lambda shape: pl.BlockSpec(shape, lambda i: (0,) * len(shape), pipeline_mode=pl.Buffered(1))
    once = pl.Buffered(1)
    n_ffn = FFN_DIM // FFN_CHUNK
    n_wo = D_MODEL // MXU_DIM
    wo_specs = [pl.BlockSpec((None, D_MODEL, MXU_DIM), functools.partial(lambda j, i: (layer, 0, j), j),
                             pipeline_mode=once) for j in range(n_wo)]
    w1_specs = [pl.BlockSpec((None, D_MODEL, FFN_CHUNK), functools.partial(lambda j, i: (layer, 0, j), j),
                             pipeline_mode=once) for j in range(n_ffn)]
    w2_specs = [pl.BlockSpec((None, FFN_CHUNK, D_MODEL), functools.partial(lambda j, i: (layer, j, 0), j),
                             pipeline_mode=once) for j in range(n_ffn)]
    return pl.pallas_call(
        functools.partial(_out_mlp_kernel, seq_len=seq_len, sub=min(tm, MLP_SUB)),
        out_shape=jax.ShapeDtypeStruct((rows, D_MODEL), F32),
        grid=(rows // tm,),
        in_specs=[
            row_spec(D_MODEL),
            pl.BlockSpec((None, None, 6, 1, D_MODEL), lambda i: (layer, mod_row(i), 0, 0, 0)),
            row_spec(NA_WIDTH), row_spec(WA_QW), row_spec(CONV_CH),
            pl.BlockSpec((halo, CONV_CH), lambda i: (jnp.maximum(i * per_tile - 1, 0), 0)),
            pl.BlockSpec((halo, CONV_CH), lambda i: (jnp.minimum((i + 1) * per_tile, last), 0)),
            row_spec(CONV_CH),
            const((3, CONV_CH)), const((1, CONV_CH)), const((1, CONV_CH)),
            const((1, D_MODEL)),
            *wo_specs, *w1_specs, *w2_specs,
        ],
        out_specs=row_spec(D_MODEL),
        scratch_shapes=[pltpu.VMEM((D_MODEL, D_MODEL), BF16)],
        compiler_params=_params(dimension_semantics=("arbitrary",)),
        name="out_mlp",
    )(x2d, mod, na_n, wa_n, cv_u, cv_u, cv_u, cv_b, conv_w, conv_b, gain_cv, g2,
      *([w_o] * n_wo), *([w1] * n_ffn), *([w2] * n_ffn))


def _rope_tables(seq):
    quarter = HEAD_DIM // 4
    inv = ROPE_BASE ** (-jnp.arange(quarter, dtype=F32) / quarter)
    t = jnp.arange(seq)
    ang_r = (t // GRID_W).astype(F32)[:, None] * inv[None, :]
    ang_c = (t % GRID_W).astype(F32)[:, None] * inv[None, :]
    cos = jnp.concatenate([jnp.cos(ang_r), jnp.cos(ang_r), jnp.cos(ang_c), jnp.cos(ang_c)], axis=1)
    sin = jnp.concatenate([-jnp.sin(ang_r), jnp.sin(ang_r), -jnp.sin(ang_c), jnp.sin(ang_c)], axis=1)
    reps = LANES // HEAD_DIM
    return jnp.tile(cos, (1, reps)), jnp.tile(sin, (1, reps))


def kernel(x, c, ctx, c_ctx, w_mod, b_mod, g_norm1, g_norm2, w_in, na_q_gain, na_k_gain, na_rpb,
           conv_w, conv_bias, wa_q_gain, wa_k_gain, wa_sink, g_out, w_o, w_fc1, w_fc2):
    batch, seq, d = x.shape
    lc = ctx.shape[1]
    depth = w_mod.shape[0]
    assert d == D_MODEL and seq % WA_BLOCK == 0 and seq % GRID_W == 0 and batch < MOD_ROWS
    tm = 512
    tm_in = 1024
    tm_ctx = 256
    assert seq % tm == 0 and seq % tm_in == 0 and lc % tm_ctx == 0

    cond = jnp.zeros((MOD_ROWS, d), F32).at[:batch].set(c).at[batch].set(c_ctx)
    mod = _modulation(cond, w_mod, b_mod).reshape(depth, MOD_ROWS, 6, 1, d)
    rpb_bias = _rpb_tables(na_rpb)
    rope_tabs = _rope_tables(seq)

    lat_row = lambda i: i // (seq // tm)
    lat_row_in = lambda i: i // (seq // tm_in)
    ctx_row = lambda i: batch

    w1_bf = w_fc1.astype(BF16)
    w2_bf = w_fc2.astype(BF16)
    xs = x.reshape(batch * seq, d)
    cs = ctx.reshape(batch * lc, d)
    for l in range(depth):
        last = l == depth - 1
        g1 = g_norm1[l].reshape(1, d)
        g2 = g_norm2[l].reshape(1, d)
        gains = [jnp.tile(na_q_gain[l], NA_HEADS).reshape(1, -1), jnp.tile(na_k_gain[l], NA_HEADS).reshape(1, -1),
                 jnp.tile(wa_q_gain[l], WA_HEADS).reshape(1, -1), jnp.tile(wa_k_gain[l], WA_KV_HEADS).reshape(1, -1)]
        go_na = g_out[l, :NA_WIDTH].reshape(1, -1)
        go_cv = g_out[l, NA_WIDTH:NA_WIDTH + CONV_CH].reshape(1, -1)
        go_wa = g_out[l, NA_WIDTH + CONV_CH:].reshape(1, -1)
        cb = conv_bias[l].reshape(1, -1)

        naq, nak, nav, cvu, cvb, waq, wak, wav = _in_proj(xs, mod, l, lat_row_in, g1, w_in, gains, rope_tabs, tm_in)
        if last:
            cnak, cnav, cwak, cwav = _in_proj(cs, mod, l, ctx_row, g1, w_in, gains, None, tm_ctx, kv_only=True)
        else:
            cnaq, cnak, cnav, ccvu, ccvb, cwaq, cwak, cwav = _in_proj(cs, mod, l, ctx_row, g1, w_in, gains, None, tm_ctx)

        b3 = lambda a, n: a.reshape(batch, n, a.shape[-1])
        na_n, wa_n = _attention(wa_sink[l], b3(naq, seq), b3(nak, seq), b3(nav, seq), b3(cnak, lc), b3(cnav, lc),
                                rpb_bias, l, go_na, b3(waq, seq), b3(wak, seq), b3(wav, seq), b3(cwak, lc),
                                b3(cwav, lc), go_wa)
        xs = _out_mlp(xs, mod, l, lat_row, na_n.reshape(batch * seq, -1), wa_n.reshape(batch * seq, -1),
                      cvu, cvb, conv_w[l], cb, go_cv, w_o, g2, w1_bf, w2_bf, seq, tm)
        if not last:
            cna_n, cwa_n = _ctx_attention(wa_sink[l], b3(cnaq, lc), b3(cnak, lc), b3(cnav, lc),
                                          b3(cwaq, lc), b3(cwak, lc), b3(cwav, lc), go_na, go_wa)
            cs = _out_mlp(cs, mod, l, ctx_row, cna_n.reshape(batch * lc, -1), cwa_n.reshape(batch * lc, -1),
                          ccvu, ccvb, conv_w[l], cb, go_cv, w_o, g2, w1_bf, w2_bf, lc, tm_ctx)
    return xs.reshape(batch, seq, d)
```

```python
import functools

import jax
import jax.numpy as jnp
from jax import lax
from jax.experimental import pallas as pl
from jax.experimental.pallas import tpu as pltpu

D_MODEL = 1024
GRID_W = 64
HEAD_DIM = 64
NA_HEADS = 4
NA_WIDTH = NA_HEADS * HEAD_DIM
CONV_CH = 256
WA_HEADS = 8
WA_KV_HEADS = 2
WA_GROUP = WA_HEADS // WA_KV_HEADS
WA_QW = WA_HEADS * HEAD_DIM
WA_KVW = WA_KV_HEADS * HEAD_DIM
WA_K2W = 2 * WA_KVW
WA_V3W = 2 * WA_KVW
NA_WIN_ROWS = 8
NA_WIN_COLS = 16
NA_DROWS = 2 * NA_WIN_ROWS - 1
NA_DCOLS = 2 * NA_WIN_COLS - 1
WA_WINDOW = 128
WA_BLOCK = 128
WA_BAND = 3 * WA_BLOCK
FFN_DIM = 4 * D_MODEL
FFN_CHUNK = 1024
ROPE_BASE = 10000.0
EPS = 1e-6
NEG_INF = -1e30
IN_WIDTH = 2304
OFF_NA_Q, OFF_NA_K, OFF_NA_V = 0, 256, 512
OFF_CV_X, OFF_CV_B, OFF_CV_C = 768, 1024, 1280
OFF_WA_Q, OFF_WA_K, OFF_WA_V = 1536, 2048, 2176
MOD_ROWS = 16
MOD_TN = 1024
IN_SUB = 256
MLP_SUB = 256
LANES = 128
MXU_DIM = 256
VMEM_LIMIT = 60 * 1024 * 1024
LOG2E = 1.4426950408889634
Q_SCALE = LOG2E * HEAD_DIM ** -0.5

F32 = jnp.float32
BF16 = jnp.bfloat16


def _dot(a, b):
    return jnp.dot(a, b, preferred_element_type=F32)


def _dot_nt(a, b):
    return lax.dot_general(a, b, (((1,), (1,)), ((), ())), preferred_element_type=F32)


def _split_bf16(a):
    hi = a.astype(BF16)
    lo = (a - hi.astype(F32)).astype(BF16)
    return hi, lo


def _iota(shape, dim):
    return lax.broadcasted_iota(jnp.int32, shape, dim)


def _params(**kw):
    return pltpu.CompilerParams(vmem_limit_bytes=VMEM_LIMIT, **kw)


def _mod_kernel(cond_ref, w_ref, b_ref, o_ref):
    a = cond_ref[...]
    a = a * (1.0 / (1.0 + jnp.exp(-a)))
    ah, al = _split_bf16(a)
    wh, wl = _split_bf16(w_ref[...])
    o_ref[...] = _dot(ah, wh) + _dot(al, wh) + _dot(ah, wl) + b_ref[...]


def _modulation(cond, w_mod, b_mod):
    depth = w_mod.shape[0]
    n_out = w_mod.shape[2]
    return pl.pallas_call(
        _mod_kernel,
        out_shape=jax.ShapeDtypeStruct((depth, MOD_ROWS, n_out), F32),
        grid=(depth, n_out // MOD_TN),
        in_specs=[
            pl.BlockSpec((MOD_ROWS, D_MODEL), lambda l, j: (0, 0)),
            pl.BlockSpec((None, D_MODEL, MOD_TN), lambda l, j: (l, 0, j)),
            pl.BlockSpec((None, 1, MOD_TN), lambda l, j: (l, 0, j)),
        ],
        out_specs=pl.BlockSpec((None, MOD_ROWS, MOD_TN), lambda l, j: (l, 0, j)),
        compiler_params=_params(),
        name="modulation",
    )(cond, w_mod, b_mod.reshape(depth, 1, n_out))


def _rpb_kernel(rpb_ref, o_ref):
    l = pl.program_id(0)
    shape = (GRID_W, LANES)
    q = _iota(shape, 0)
    lane = _iota(shape, 1)
    k = lane % GRID_W
    left = lane < GRID_W
    dc = jnp.clip(k - q, -(NA_WIN_COLS - 1), NA_WIN_COLS - 1) + (NA_WIN_COLS - 1)
    col_start = jnp.clip(q - NA_WIN_COLS // 2, 0, GRID_W - NA_WIN_COLS)
    col_ok = (k >= col_start) & (k < col_start + NA_WIN_COLS)
    base_l = l * (NA_HEADS * NA_DROWS * NA_DCOLS)

    def body(d, carry):
        for h in range(NA_HEADS):
            base = base_l + (h * NA_DROWS + d) * NA_DCOLS
            t = jnp.zeros(shape, F32)
            for i in range(NA_DCOLS):
                val = jnp.where(left, rpb_ref[base + i], rpb_ref[base + NA_DCOLS + i])
                t = jnp.where(dc == i, val, t)
            o_ref[d, h * GRID_W:(h + 1) * GRID_W, :] = jnp.where(col_ok, t * LOG2E, NEG_INF)
        return carry

    lax.fori_loop(0, NA_DROWS - 1, body, 0)


def _rpb_tables(na_rpb):
    depth = na_rpb.shape[0]
    return pl.pallas_call(
        _rpb_kernel,
        out_shape=jax.ShapeDtypeStruct((depth, NA_DROWS - 1, NA_HEADS * GRID_W, LANES), F32),
        grid=(depth,),
        in_specs=[pl.BlockSpec(memory_space=pltpu.SMEM)],
        out_specs=pl.BlockSpec((None, NA_DROWS - 1, NA_HEADS * GRID_W, LANES), lambda l: (l, 0, 0, 0)),
        compiler_params=_params(),
        name="rpb_tables",
    )(na_rpb.reshape(-1))


def _head_rms(t, gain):
    width = t.shape[1]
    sq = (t * t).astype(BF16)
    blk = min(width, MXU_DIM)
    ones = jnp.where(_iota((blk, blk), 0) // HEAD_DIM == _iota((blk, blk), 1) // HEAD_DIM, 1.0, 0.0).astype(BF16)
    sums = [_dot(sq[:, c:c + blk], ones) for c in range(0, width, blk)]
    ss = sums[0] if len(sums) == 1 else jnp.concatenate(sums, axis=1)
    return t * lax.rsqrt(ss * (1.0 / HEAD_DIM) + EPS) * gain


def _rope(t, cos, sin):
    lane = _iota((t.shape[0], LANES), 1)
    first = (lane % (HEAD_DIM // 2)) < (HEAD_DIM // 4)
    out = []
    for c in range(0, t.shape[1], LANES):
        u = t[:, c:c + LANES]
        partner = jnp.where(first, pltpu.roll(u, LANES - HEAD_DIM // 4, axis=1), pltpu.roll(u, HEAD_DIM // 4, axis=1))
        out.append(u * cos + partner * sin)
    return out[0] if len(out) == 1 else jnp.concatenate(out, axis=1)


def _dup_kv(t):
    lane = _iota(t.shape, 1)
    swapped = pltpu.roll(t, HEAD_DIM, axis=1)
    left = lane < HEAD_DIM
    return jnp.concatenate([jnp.where(left, t, swapped), jnp.where(left, swapped, t)], axis=1)


def _cast_weights_once(w_refs, wbf_ref):
    @pl.when(pl.program_id(0) == 0)
    def _():
        c = 0
        for w_ref in w_refs:
            wbf_ref[:, c:c + w_ref.shape[1]] = w_ref[...].astype(BF16)
            c += w_ref.shape[1]


def _inproj_kernel(*refs, rope, kv_only, n_w, sub):
    x_ref, mod_ref, g1_ref = refs[:3]
    w_refs = refs[3:3 + n_w]
    gnq_ref, gnk_ref, gwq_ref, gwk_ref = refs[3 + n_w:7 + n_w]
    rest = refs[7 + n_w:]
    if rope:
        cos_ref, sin_ref = rest[:2]
        rest = rest[2:]
    *out_refs, wbf_ref = rest
    if kv_only:
        nak_ref, nav_ref, wak_ref, wav_ref = out_refs
        o_nk, o_nv, o_wk, o_wv, o_end = 0, NA_WIDTH, 2 * NA_WIDTH, 2 * NA_WIDTH + WA_KVW, 2 * NA_WIDTH + 2 * WA_KVW
    else:
        naq_ref, nak_ref, nav_ref, cvu_ref, cvb_ref, waq_ref, wak_ref, wav_ref = out_refs
        o_nk, o_nv, o_wk, o_wv, o_end = OFF_NA_K, OFF_NA_V, OFF_WA_K, OFF_WA_V, IN_WIDTH
    _cast_weights_once(w_refs, wbf_ref)
    for r0 in range(0, x_ref.shape[0], sub):
        rows = slice(r0, r0 + sub)
        x = x_ref[rows, :]
        ms = jnp.mean(x * x, axis=-1, keepdims=True)
        h = x * lax.rsqrt(ms + EPS) * g1_ref[...]
        h = h * (1.0 + mod_ref[1]) + mod_ref[0]
        p = _dot(h.astype(BF16), wbf_ref[...])
        nak_ref[rows, :] = _head_rms(p[:, o_nk:o_nk + NA_WIDTH], gnk_ref[...]).astype(BF16)
        nav_ref[rows, :] = p[:, o_nv:o_nv + NA_WIDTH].astype(BF16)
        wk = _head_rms(p[:, o_wk:o_wv], gwk_ref[...])
        if rope:
            wk = _rope(wk, cos_ref[rows, :], sin_ref[rows, :])
        wak_ref[rows, :] = _dup_kv(wk).astype(BF16)
        vv = p[:, o_wv:o_end]
        left = _iota(vv.shape, 1) < HEAD_DIM
        wav_ref[rows, :] = jnp.concatenate([jnp.where(left, vv, 1.0), jnp.where(left, pltpu.roll(vv, HEAD_DIM, axis=1), 1.0)],
                                           axis=1).astype(BF16)
        if kv_only:
            continue
        naq_ref[rows, :] = (_head_rms(p[:, OFF_NA_Q:OFF_NA_K], gnq_ref[...]) * Q_SCALE).astype(BF16)
        cvu_ref[rows, :] = p[:, OFF_CV_C:OFF_WA_Q] * p[:, OFF_CV_X:OFF_CV_B]
        cvb_ref[rows, :] = p[:, OFF_CV_B:OFF_CV_C]
        wq = _head_rms(p[:, OFF_WA_Q:OFF_WA_K], gwq_ref[...])
        if rope:
            wq = _rope(wq, cos_ref[rows, :], sin_ref[rows, :])
        waq_ref[rows, :] = (wq * Q_SCALE).astype(BF16)


def _in_proj(x2d, mod, layer, mod_row, g1, w_in, gains, rope_tabs, tm, kv_only=False):
    rows = x2d.shape[0]
    rope = rope_tabs is not None
    row_spec = lambda w: pl.BlockSpec((tm, w), lambda i: (i, 0))
    const = lambda shape: pl.BlockSpec(shape, lambda i: (0,) * len(shape))
    once = pl.Buffered(1)
    if kv_only:
        col_blocks = [OFF_NA_K // MXU_DIM, OFF_NA_V // MXU_DIM, OFF_WA_K // MXU_DIM]
        widths = [(NA_WIDTH, BF16), (NA_WIDTH, BF16), (WA_K2W, BF16), (WA_V3W, BF16)]
    else:
        col_blocks = list(range(IN_WIDTH // MXU_DIM))
        widths = [(NA_WIDTH, BF16), (NA_WIDTH, BF16), (NA_WIDTH, BF16), (CONV_CH, F32), (CONV_CH, F32),
                  (WA_QW, BF16), (WA_K2W, BF16), (WA_V3W, BF16)]
    w_specs = [pl.BlockSpec((None, D_MODEL, MXU_DIM), functools.partial(lambda j, i: (layer, 0, j), j),
                            pipeline_mode=once) for j in col_blocks]
    n_w = len(w_specs)
    in_specs = [
        row_spec(D_MODEL),
        pl.BlockSpec((None, None, 6, 1, D_MODEL), lambda i: (layer, mod_row(i), 0, 0, 0)),
        const((1, D_MODEL)),
        *w_specs,
        const((1, NA_WIDTH)), const((1, NA_WIDTH)), const((1, WA_QW)), const((1, WA_KVW)),
    ]
    args = [x2d, mod, g1, *([w_in] * n_w), *gains]
    if rope:
        seq_tiles = rope_tabs[0].shape[0] // tm
        in_specs += [pl.BlockSpec((tm, LANES), lambda i: (i % seq_tiles, 0))] * 2
        args += list(rope_tabs)
    return pl.pallas_call(
        functools.partial(_inproj_kernel, rope=rope, kv_only=kv_only, n_w=n_w, sub=min(tm, IN_SUB)),
        out_shape=[jax.ShapeDtypeStruct((rows, w), dt) for w, dt in widths],
        grid=(rows // tm,),
        in_specs=in_specs,
        out_specs=[row_spec(w) for w, _ in widths],
        scratch_shapes=[pltpu.VMEM((D_MODEL, n_w * MXU_DIM), BF16)],
        compiler_params=_params(dimension_semantics=("arbitrary",)),
        name="in_proj_rope" if rope else ("in_proj_ctx_kv" if kv_only else "in_proj_ctx"),
    )(*args)


def _lane_chunks(s):
    return [s[:, c:c + LANES] for c in range(0, s.shape[1], LANES)]


def _row_max(scores, extra=None):
    chunks = [ch for s in scores for ch in _lane_chunks(s)]
    if extra is not None:
        chunks.append(extra)
    folded = functools.reduce(jnp.maximum, chunks)
    return jnp.broadcast_to(jnp.max(folded, axis=-1, keepdims=True), folded.shape)


def _softmax_pv(parts, extra=None, sums_from_values=False, m=None):
    lane_chunks = _lane_chunks
    if m is None:
        m = _row_max([s for s, _ in parts], extra)
    ps = [jnp.concatenate([jnp.exp2(ch - m) for ch in lane_chunks(s)], axis=1) for s, _ in parts]
    pv = _dot(jnp.concatenate([p.astype(BF16) for p in ps], axis=1),
              jnp.concatenate([v for _, v in parts], axis=0))
    if sums_from_values:
        denom = pv[:, -LANES:]
        pv = pv[:, :-LANES]
        if extra is not None:
            denom = denom + jnp.exp2(extra - m)
        return pv / denom
    psum = functools.reduce(jnp.add, [ch for p in ps for ch in lane_chunks(p)])
    if extra is not None:
        psum = psum + jnp.where(_iota(extra.shape, 1) == 0, jnp.exp2(extra - m), 0.0)
    return pv / jnp.sum(psum, axis=-1, keepdims=True)


def _stack_heads(q, n_heads):
    head = _iota(q.shape, 1) // HEAD_DIM
    zero = jnp.zeros_like(q)
    return jnp.concatenate([jnp.where(head == h, q, zero) for h in range(n_heads)], axis=0)


def _unstack_heads(o, n_heads):
    rows = o.shape[0] // n_heads
    head = _iota((rows, o.shape[1]), 1) // HEAD_DIM
    out = jnp.where(head == 0, o[:rows], 0.0)
    for h in range(1, n_heads):
        out = jnp.where(head == h, o[h * rows:(h + 1) * rows], out)
    return out


def _group_rms(o, gain):
    ms = jnp.mean(o * o, axis=-1, keepdims=True)
    return o * lax.rsqrt(ms + EPS) * gain


SUBLANES = 8


def _wa_scores(qg, keys):
    rows = qg.shape[0]
    left = _iota((rows, LANES), 1) < HEAD_DIM
    zero = jnp.zeros((rows, LANES), qg.dtype)
    blocks = []
    for pair in range(WA_GROUP // 2):
        qp = qg[:, pair * LANES:(pair + 1) * LANES]
        blocks += [jnp.where(left, qp, zero), jnp.where(left, zero, qp)]
    qs = jnp.concatenate(blocks, axis=0)
    scores = []
    for k2, bias in keys:
        s = _dot_nt(k2, qs)
        if bias is not None:
            s = s + jnp.concatenate([bias] * WA_GROUP, axis=1)
        scores.append(s)
    return scores


def _wa_sink_rows(rows, sinks):
    return jnp.concatenate([jnp.full((SUBLANES, rows), sinks[g] * LOG2E, F32) for g in range(WA_GROUP)], axis=1)


def _wa_col_max(scores, sink_rows):
    m = sink_rows[:1]
    for s in scores:
        m = jnp.maximum(m, jnp.max(s, axis=0, keepdims=True))
    return jnp.broadcast_to(m, sink_rows.shape)


def _wa_finish(scores, values, sinks, m=None):
    rows = scores[0].shape[1] // WA_GROUP
    sink_rows = _wa_sink_rows(rows, sinks)
    if m is None:
        m = _wa_col_max(scores, sink_rows)
    p = jnp.concatenate([jnp.exp2(s - m[:1]).astype(BF16) for s in scores], axis=0)
    v = jnp.concatenate(values, axis=0)
    ot = lax.dot_general(v, p, (((0,), (0,)), ((), ())), preferred_element_type=F32)
    denom = ot[HEAD_DIM:HEAD_DIM + SUBLANES] + jnp.exp2(sink_rows - m)
    on = ot[:HEAD_DIM] / denom[:1]
    pairs = [jnp.concatenate([on[:, (2 * pr) * rows:(2 * pr + 1) * rows],
                              on[:, (2 * pr + 1) * rows:(2 * pr + 2) * rows]], axis=0).T
             for pr in range(WA_GROUP // 2)]
    return jnp.concatenate(pairs, axis=1)


def _na_steps(q_ref, k_ref, v_ref, kc_ref, vc_ref, bias_ref, g_ref, o_ref):
    n_rows = q_ref.shape[0] // GRID_W
    win = NA_WIN_ROWS * GRID_W
    lc = kc_ref.shape[0]
    gain = g_ref[...]

    def window(r):
        start = jnp.clip(r - NA_WIN_ROWS // 2, 0, n_rows - NA_WIN_ROWS)
        return pl.multiple_of(start * GRID_W, GRID_W), start - r + (NA_WIN_ROWS - 1)

    def scores(r, s_ref):
        tok0, d0 = window(r)
        q = q_ref[pl.ds(pl.multiple_of(r * GRID_W, GRID_W), GRID_W), :]
        qs = _stack_heads(q, NA_HEADS)
        bias = jnp.concatenate([bias_ref[d0 + 2 * j] for j in range(NA_WIN_ROWS // 2)], axis=1)
        s_ctx = _dot_nt(qs, kc_ref[...])
        s_loc = _dot_nt(qs, k_ref[pl.ds(tok0, win), :]) + bias
        s_ref[:, :lc] = s_ctx
        s_ref[:, lc:lc + win] = s_loc
        s_ref[:, lc + win:] = _row_max([s_ctx, s_loc])

    def finish(r, s_ref):
        tok0, _ = window(r)
        o4 = _softmax_pv([(s_ref[:, :lc], vc_ref[...]), (s_ref[:, lc:lc + win], v_ref[pl.ds(tok0, win), :])],
                         m=s_ref[:, lc + win:])
        o = _unstack_heads(o4, NA_HEADS)
        o_ref[pl.ds(pl.multiple_of(r * GRID_W, GRID_W), GRID_W), :] = _group_rms(o, gain).astype(o_ref.dtype)

    def step(r, cur_ref, nxt_ref):
        scores(jnp.minimum(r + 1, n_rows - 1), nxt_ref)
        finish(r, cur_ref)

    return scores, step


def _wa_steps(sink_ref, q_ref, k_ref, v_ref, kc_ref, vc_ref, g_ref, o_ref):
    seq = q_ref.shape[0]
    n_blocks = seq // WA_BLOCK
    lc = kc_ref.shape[0]
    gain = g_ref[...]
    key = _iota((WA_BAND, WA_BLOCK), 0)
    qry = _iota((WA_BAND, WA_BLOCK), 1)

    def window(n):
        q0 = pl.multiple_of(n * WA_BLOCK, WA_BLOCK)
        k0 = pl.multiple_of(jnp.clip(q0 - WA_BLOCK, 0, seq - WA_BAND), WA_BLOCK)
        return q0, k0

    def scores(n, kh, s_ref):
        q0, k0 = window(n)
        band = jnp.where(jnp.abs((k0 + key) - (q0 + qry)) <= WA_WINDOW, 0.0, NEG_INF)
        lanes = slice(kh * LANES, (kh + 1) * LANES)
        qg = q_ref[pl.ds(q0, WA_BLOCK), kh * WA_GROUP * HEAD_DIM:(kh + 1) * WA_GROUP * HEAD_DIM]
        s_ctx, s_loc = _wa_scores(qg, [(kc_ref[:, lanes], None), (k_ref[pl.ds(k0, WA_BAND), lanes], band)])
        s_ref[kh, :lc, :] = s_ctx
        s_ref[kh, lc:lc + WA_BAND, :] = s_loc
        s_ref[kh, lc + WA_BAND:, :] = _wa_col_max([s_ctx, s_loc], _wa_sink_rows(WA_BLOCK, sinks(kh)))

    def sinks(kh):
        return [sink_ref[kh * WA_GROUP + g] for g in range(WA_GROUP)]

    def finish(n, kh, s_ref):
        _, k0 = window(n)
        lanes = slice(kh * LANES, (kh + 1) * LANES)
        return _wa_finish([s_ref[kh, :lc, :], s_ref[kh, lc:lc + WA_BAND, :]],
                          [vc_ref[:, lanes], v_ref[pl.ds(k0, WA_BAND), lanes]], sinks(kh),
                          m=s_ref[kh, lc + WA_BAND:, :])

    def step(n, cur_ref, nxt_ref):
        nxt = jnp.minimum(n + 1, n_blocks - 1)
        for kh in range(WA_KV_HEADS):
            scores(nxt, kh, nxt_ref)
        o = jnp.concatenate([finish(n, kh, cur_ref) for kh in range(WA_KV_HEADS)], axis=1)
        q0, _ = window(n)
        o_ref[pl.ds(q0, WA_BLOCK), :] = _group_rms(o, gain).astype(o_ref.dtype)

    return scores, step


def _attn_kernel(sink_ref, nq_ref, nk_ref, nv_ref, nkc_ref, nvc_ref, bias_ref, gn_ref,
                 wq_ref, wk_ref, wv_ref, wkc_ref, wvc_ref, gw_ref, on_ref, ow_ref,
                 nsa_ref, nsb_ref, wsa_ref, wsb_ref):
    na_scores, na_step = _na_steps(nq_ref, nk_ref, nv_ref, nkc_ref, nvc_ref, bias_ref, gn_ref, on_ref)
    wa_scores, wa_step = _wa_steps(sink_ref, wq_ref, wk_ref, wv_ref, wkc_ref, wvc_ref, gw_ref, ow_ref)
    n_blocks = wq_ref.shape[0] // WA_BLOCK

    for kh in range(WA_KV_HEADS):
        wa_scores(0, kh, wsa_ref)
    na_scores(0, nsa_ref)

    def body(i, carry):
        for half, (w_cur, w_nxt) in enumerate(((wsa_ref, wsb_ref), (wsb_ref, wsa_ref))):
            n = 2 * i + half
            wa_step(n, w_cur, w_nxt)
            na_step(2 * n, nsa_ref, nsb_ref)
            na_step(2 * n + 1, nsb_ref, nsa_ref)
        return carry

    lax.fori_loop(0, n_blocks // 2, body, 0)


def _attention(sink, nq, nk, nv, nkc, nvc, bias, layer, gain_na, wq, wk2, wv3, wkc2, wvc3, gain_wa):
    batch, seq, _ = nq.shape
    lc = nkc.shape[1]
    assert WA_BLOCK == 2 * GRID_W and (seq // WA_BLOCK) % 2 == 0
    per_b = lambda n, w: pl.BlockSpec((None, n, w), lambda b: (b, 0, 0))
    na_scratch = pltpu.VMEM((NA_HEADS * GRID_W, lc + NA_WIN_ROWS * GRID_W + LANES), F32)
    wa_scratch = pltpu.VMEM((WA_KV_HEADS, lc + WA_BAND + SUBLANES, WA_GROUP * WA_BLOCK), F32)
    return pl.pallas_call(
        _attn_kernel,
        out_shape=[jax.ShapeDtypeStruct((batch, seq, NA_WIDTH), BF16),
                   jax.ShapeDtypeStruct((batch, seq, WA_QW), BF16)],
        grid=(batch,),
        in_specs=[pl.BlockSpec(memory_space=pltpu.SMEM),
                  per_b(seq, NA_WIDTH), per_b(seq, NA_WIDTH), per_b(seq, NA_WIDTH),
                  per_b(lc, NA_WIDTH), per_b(lc, NA_WIDTH),
                  pl.BlockSpec((None,) + bias.shape[1:], lambda b: (layer, 0, 0, 0)),
                  pl.BlockSpec((1, NA_WIDTH), lambda b: (0, 0)),
                  per_b(seq, WA_QW), per_b(seq, WA_K2W), per_b(seq, WA_V3W),
                  per_b(lc, WA_K2W), per_b(lc, WA_V3W),
                  pl.BlockSpec((1, WA_QW), lambda b: (0, 0))],
        out_specs=[per_b(seq, NA_WIDTH), per_b(seq, WA_QW)],
        scratch_shapes=[na_scratch, na_scratch, wa_scratch, wa_scratch],
        compiler_params=_params(),
        name="attention",
    )(sink, nq, nk, nv, nkc, nvc, bias, gain_na, wq, wk2, wv3, wkc2, wvc3, gain_wa)


def _ctx_attn_kernel(sink_ref, qn_ref, kn_ref, vn_ref, qw_ref, kw_ref, vw_ref, gn_ref, gw_ref, on_ref, ow_ref):
    qs = _stack_heads(qn_ref[...], NA_HEADS)
    o4 = _softmax_pv([(_dot_nt(qs, kn_ref[...]), vn_ref[...])])
    on_ref[...] = _group_rms(_unstack_heads(o4, NA_HEADS), gn_ref[...]).astype(on_ref.dtype)
    outs = []
    for kh in range(WA_KV_HEADS):
        lanes = slice(kh * LANES, (kh + 1) * LANES)
        qg = qw_ref[:, kh * WA_GROUP * HEAD_DIM:(kh + 1) * WA_GROUP * HEAD_DIM]
        sinks = [sink_ref[kh * WA_GROUP + g] for g in range(WA_GROUP)]
        outs.append(_wa_finish(_wa_scores(qg, [(kw_ref[:, lanes], None)]), [vw_ref[:, lanes]], sinks))
    ow_ref[...] = _group_rms(jnp.concatenate(outs, axis=1), gw_ref[...]).astype(ow_ref.dtype)


def _ctx_attention(sink, qn, kn, vn, qw, kw2, vw2, gain_na, gain_wa):
    batch, lc, _ = qn.shape
    per_b = lambda w: pl.BlockSpec((None, lc, w), lambda b: (b, 0, 0))
    return pl.pallas_call(
        _ctx_attn_kernel,
        out_shape=[jax.ShapeDtypeStruct((batch, lc, NA_WIDTH), BF16),
                   jax.ShapeDtypeStruct((batch, lc, WA_QW), BF16)],
        grid=(batch,),
        in_specs=[pl.BlockSpec(memory_space=pltpu.SMEM),
                  per_b(NA_WIDTH), per_b(NA_WIDTH), per_b(NA_WIDTH),
                  per_b(WA_QW), per_b(WA_K2W), per_b(WA_V3W),
                  pl.BlockSpec((1, NA_WIDTH), lambda b: (0, 0)),
                  pl.BlockSpec((1, WA_QW), lambda b: (0, 0))],
        out_specs=[per_b(NA_WIDTH), per_b(WA_QW)],
        compiler_params=_params(),
        name="ctx_attention",
    )(sink, qn, kn, vn, qw, kw2, vw2, gain_na, gain_wa)


def _out_mlp_kernel(x_ref, mod_ref, na_ref, wa_ref, u_ref, uprev_ref, unext_ref, cvb_ref, cw_ref, cb_ref, g_ref,
                    g2_ref, *rest, seq_len, sub):
    n_ffn = FFN_DIM // FFN_CHUNK
    n_wo = D_MODEL // MXU_DIM
    wo_refs, w1_refs, w2_refs = rest[:n_wo], rest[n_wo:n_wo + n_ffn], rest[n_wo + n_ffn:n_wo + 2 * n_ffn]
    o_ref, wobf_ref = rest[n_wo + 2 * n_ffn:]
    tm = x_ref.shape[0]
    i = pl.program_id(0)
    _cast_weights_once(wo_refs, wobf_ref)
    u = u_ref[...]
    at_start = (i * tm) % seq_len == 0
    at_end = ((i + 1) * tm) % seq_len == 0
    prev_row = jnp.where(at_start, 0.0, uprev_ref[7:8, :])
    next_row = jnp.where(at_end, 0.0, unext_ref[0:1, :])
    row = _iota(u.shape, 0)
    up = jnp.where(row == 0, prev_row, pltpu.roll(u, 1, axis=0))
    dn = jnp.where(row == tm - 1, next_row, pltpu.roll(u, tm - 1, axis=0))
    y = cb_ref[...] + cw_ref[0:1, :] * up + cw_ref[1:2, :] * u + cw_ref[2:3, :] * dn
    cv = _group_rms(cvb_ref[...] * y, g_ref[...]).astype(BF16)
    for r0 in range(0, tm, sub):
        rows = slice(r0, r0 + sub)
        mixed = jnp.concatenate([na_ref[rows, :], cv[rows, :], wa_ref[rows, :]], axis=1)
        x = x_ref[rows, :] + mod_ref[2] * _dot(mixed, wobf_ref[...])
        ms = jnp.mean(x * x, axis=-1, keepdims=True)
        h = x * lax.rsqrt(ms + EPS) * g2_ref[...]
        h = (h * (1.0 + mod_ref[4]) + mod_ref[3]).astype(BF16)
        acc = None
        for w1_ref, w2_ref in zip(w1_refs, w2_refs):
            a = jnp.maximum(_dot(h, w1_ref[...]), 0.0)
            part = _dot((a * a).astype(BF16), w2_ref[...])
            acc = part if acc is None else acc + part
        o_ref[rows, :] = x + mod_ref[5] * acc


def _out_mlp(x2d, mod, layer, mod_row, na_n, wa_n, cv_u, cv_b, conv_w, conv_b, gain_cv, w_o, g2, w1, w2, seq_len, tm):
    rows = x2d.shape[0]
    halo = 8
    per_tile = tm // halo
    last = rows // halo - 1
    row_spec = lambda w: pl.BlockSpec((tm, w), lambda i: (i, 0))
    const = lambda shape: pl.BlockSpec(shape, lambda i: (0,) * len(shape), pipeline_mode=pl.Buffered(1))
    once = pl.Buffered(1)
    n_ffn = FFN_DIM // FFN_CHUNK
    n_wo = D_MODEL // MXU_DIM
    wo_specs = [pl.BlockSpec((None, D_MODEL, MXU_DIM), functools.partial(lambda j, i: (layer, 0, j), j),
                             pipeline_mode=once) for j in range(n_wo)]
    w1_specs = [pl.BlockSpec((None, D_MODEL, FFN_CHUNK), functools.partial(lambda j, i: (layer, 0, j), j),
                             pipeline_mode=once) for j in range(n_ffn)]
    w2_specs = [pl.BlockSpec((None, FFN_CHUNK, D_MODEL), functools.partial(lambda j, i: (layer, j, 0), j),
                             pipeline_mode=once) for j in range(n_ffn)]
    return pl.pallas_call(
        functools.partial(_out_mlp_kernel, seq_len=seq_len, sub=min(tm, MLP_SUB)),
        out_shape=jax.ShapeDtypeStruct((rows, D_MODEL), F32),
        grid=(rows // tm,),
        in_specs=[
            row_spec(D_MODEL),
            pl.BlockSpec((None, None, 6, 1, D_MODEL), lambda i: (layer, mod_row(i), 0, 0, 0)),
            row_spec(NA_WIDTH), row_spec(WA_QW), row_spec(CONV_CH),
            pl.BlockSpec((halo, CONV_CH), lambda i: (jnp.maximum(i * per_tile - 1, 0), 0)),
            pl.BlockSpec((halo, CONV_CH), lambda i: (jnp.minimum((i + 1) * per_tile, last), 0)),
            row_spec(CONV_CH),
            const((3, CONV_CH)), const((1, CONV_CH)), const((1, CONV_CH)),
            const((1, D_MODEL)),
            *wo_specs, *w1_specs, *w2_specs,
        ],
        out_specs=row_spec(D_MODEL),
        scratch_shapes=[pltpu.VMEM((D_MODEL, D_MODEL), BF16)],
        compiler_params=_params(dimension_semantics=("arbitrary",)),
        name="out_mlp",
    )(x2d, mod, na_n, wa_n, cv_u, cv_u, cv_u, cv_b, conv_w, conv_b, gain_cv, g2,
      *([w_o] * n_wo), *([w1] * n_ffn), *([w2] * n_ffn))


def _rope_tables(seq):
    quarter = HEAD_DIM // 4
    inv = ROPE_BASE ** (-jnp.arange(quarter, dtype=F32) / quarter)
    t = jnp.arange(seq)
    ang_r = (t // GRID_W).astype(F32)[:, None] * inv[None, :]
    ang_c = (t % GRID_W).astype(F32)[:, None] * inv[None, :]
    cos = jnp.concatenate([jnp.cos(ang_r), jnp.cos(ang_r), jnp.cos(ang_c), jnp.cos(ang_c)], axis=1)
    sin = jnp.concatenate([-jnp.sin(ang_r), jnp.sin(ang_r), -jnp.sin(ang_c), jnp.sin(ang_c)], axis=1)
    reps = LANES // HEAD_DIM
    return jnp.tile(cos, (1, reps)), jnp.tile(sin, (1, reps))


def kernel(x, c, ctx, c_ctx, w_mod, b_mod, g_norm1, g_norm2, w_in, na_q_gain, na_k_gain, na_rpb,
           conv_w, conv_bias, wa_q_gain, wa_k_gain, wa_sink, g_out, w_o, w_fc1, w_fc2):
    batch, seq, d = x.shape
    lc = ctx.shape[1]
    depth = w_mod.shape[0]
    assert d == D_MODEL and seq % WA_BLOCK == 0 and seq % GRID_W == 0 and batch < MOD_ROWS
    tm = 1024
    tm_in = 1024
    tm_ctx = 256
    assert seq % tm == 0 and seq % tm_in == 0 and lc % tm_ctx == 0

    cond = jnp.zeros((MOD_ROWS, d), F32).at[:batch].set(c).at[batch].set(c_ctx)
    mod = _modulation(cond, w_mod, b_mod).reshape(depth, MOD_ROWS, 6, 1, d)
    rpb_bias = _rpb_tables(na_rpb)
    rope_tabs = _rope_tables(seq)

    lat_row = lambda i: i // (seq // tm)
    lat_row_in = lambda i: i // (seq // tm_in)
    ctx_row = lambda i: batch

    w1_bf = w_fc1.astype(BF16)
    w2_bf = w_fc2.astype(BF16)
    xs = x.reshape(batch * seq, d)
    cs = ctx.reshape(batch * lc, d)
    for l in range(depth):
        last = l == depth - 1
        g1 = g_norm1[l].reshape(1, d)
        g2 = g_norm2[l].reshape(1, d)
        gains = [jnp.tile(na_q_gain[l], NA_HEADS).reshape(1, -1), jnp.tile(na_k_gain[l], NA_HEADS).reshape(1, -1),
                 jnp.tile(wa_q_gain[l], WA_HEADS).reshape(1, -1), jnp.tile(wa_k_gain[l], WA_KV_HEADS).reshape(1, -1)]
        go_na = g_out[l, :NA_WIDTH].reshape(1, -1)
        go_cv = g_out[l, NA_WIDTH:NA_WIDTH + CONV_CH].reshape(1, -1)
        go_wa = g_out[l, NA_WIDTH + CONV_CH:].reshape(1, -1)
        cb = conv_bias[l].reshape(1, -1)

        naq, nak, nav, cvu, cvb, waq, wak, wav = _in_proj(xs, mod, l, lat_row_in, g1, w_in, gains, rope_tabs, tm_in)
        if last:
            cnak, cnav, cwak, cwav = _in_proj(cs, mod, l, ctx_row, g1, w_in, gains, None, tm_ctx, kv_only=True)
        else:
            cnaq, cnak, cnav, ccvu, ccvb, cwaq, cwak, cwav = _in_proj(cs, mod, l, ctx_row, g1, w_in, gains, None, tm_ctx)

        b3 = lambda a, n: a.reshape(batch, n, a.shape[-1])
        na_n, wa_n = _attention(wa_sink[l], b3(naq, seq), b3(nak, seq), b3(nav, seq), b3(cnak, lc), b3(cnav, lc),
                                rpb_bias, l, go_na, b3(waq, seq), b3(wak, seq), b3(wav, seq), b3(cwak, lc),
                                b3(cwav, lc), go_wa)
        xs = _out_mlp(xs, mod, l, lat_row, na_n.reshape(batch * seq, -1), wa_n.reshape(batch * seq, -1),
                      cvu, cvb, conv_w[l], cb, go_cv, w_o, g2, w1_bf, w2_bf, seq, tm)
        if not last:
            cna_n, cwa_n = _ctx_attention(wa_sink[l], b3(cnaq, lc), b3(cnak, lc), b3(cnav, lc),
                                          b3(cwaq, lc), b3(cwak, lc), b3(cwav, lc), go_na, go_wa)
            cs = _out_mlp(cs, mod, l, ctx_row, cna_n.reshape(batch * lc, -1), cwa_n.reshape(batch * lc, -1),
                          ccvu, ccvb, conv_w[l], cb, go_cv, w_o, g2, w1_bf, w2_bf, lc, tm_ctx)
    return xs.reshape(batch, seq, d)
```

```python
import functools

import jax
import jax.numpy as jnp
from jax import lax
from jax.experimental import pallas as pl
from jax.experimental.pallas import tpu as pltpu

D_MODEL = 1024
GRID_W = 64
HEAD_DIM = 64
NA_HEADS = 4
NA_WIDTH = NA_HEADS * HEAD_DIM
CONV_CH = 256
WA_HEADS = 8
WA_KV_HEADS = 2
WA_GROUP = WA_HEADS // WA_KV_HEADS
WA_QW = WA_HEADS * HEAD_DIM
WA_KVW = WA_KV_HEADS * HEAD_DIM
WA_K2W = 2 * WA_KVW
WA_V3W = 2 * WA_KVW
NA_WIN_ROWS = 8
NA_WIN_COLS = 16
NA_DROWS = 2 * NA_WIN_ROWS - 1
NA_DCOLS = 2 * NA_WIN_COLS - 1
WA_WINDOW = 128
WA_BLOCK = 128
WA_BAND = 3 * WA_BLOCK
FFN_DIM = 4 * D_MODEL
FFN_CHUNK = 1024
ROPE_BASE = 10000.0
EPS = 1e-6
NEG_INF = -1e30
IN_WIDTH = 2304
OFF_NA_Q, OFF_NA_K, OFF_NA_V = 0, 256, 512
OFF_CV_X, OFF_CV_B, OFF_CV_C = 768, 1024, 1280
OFF_WA_Q, OFF_WA_K, OFF_WA_V = 1536, 2048, 2176
MOD_ROWS = 16
MOD_TN = 1024
IN_SUB = 256
MLP_SUB = 256
LANES = 128
MXU_DIM = 256
VMEM_LIMIT = 60 * 1024 * 1024
LOG2E = 1.4426950408889634
Q_SCALE = LOG2E * HEAD_DIM ** -0.5

F32 = jnp.float32
BF16 = jnp.bfloat16


def _dot(a, b):
    return jnp.dot(a, b, preferred_element_type=F32)


def _dot_nt(a, b):
    return lax.dot_general(a, b, (((1,), (1,)), ((), ())), preferred_element_type=F32)


def _split_bf16(a):
    hi = a.astype(BF16)
    lo = (a - hi.astype(F32)).astype(BF16)
    return hi, lo


def _iota(shape, dim):
    return lax.broadcasted_iota(jnp.int32, shape, dim)


def _params(**kw):
    return pltpu.CompilerParams(vmem_limit_bytes=VMEM_LIMIT, **kw)


def _mod_kernel(cond_ref, w_ref, b_ref, o_ref):
    a = cond_ref[...]
    a = a * (1.0 / (1.0 + jnp.exp(-a)))
    ah, al = _split_bf16(a)
    wh, wl = _split_bf16(w_ref[...])
    o_ref[...] = _dot(ah, wh) + _dot(al, wh) + _dot(ah, wl) + b_ref[...]


def _modulation(cond, w_mod, b_mod):
    depth = w_mod.shape[0]
    n_out = w_mod.shape[2]
    return pl.pallas_call(
        _mod_kernel,
        out_shape=jax.ShapeDtypeStruct((depth, MOD_ROWS, n_out), F32),
        grid=(depth, n_out // MOD_TN),
        in_specs=[
            pl.BlockSpec((MOD_ROWS, D_MODEL), lambda l, j: (0, 0)),
            pl.BlockSpec((None, D_MODEL, MOD_TN), lambda l, j: (l, 0, j)),
            pl.BlockSpec((None, 1, MOD_TN), lambda l, j: (l, 0, j)),
        ],
        out_specs=pl.BlockSpec((None, MOD_ROWS, MOD_TN), lambda l, j: (l, 0, j)),
        compiler_params=_params(),
        name="modulation",
    )(cond, w_mod, b_mod.reshape(depth, 1, n_out))


def _rpb_kernel(rpb_ref, o_ref):
    l = pl.program_id(0)
    shape = (GRID_W, LANES)
    q = _iota(shape, 0)
    lane = _iota(shape, 1)
    k = lane % GRID_W
    left = lane < GRID_W
    dc = jnp.clip(k - q, -(NA_WIN_COLS - 1), NA_WIN_COLS - 1) + (NA_WIN_COLS - 1)
    col_start = jnp.clip(q - NA_WIN_COLS // 2, 0, GRID_W - NA_WIN_COLS)
    col_ok = (k >= col_start) & (k < col_start + NA_WIN_COLS)
    base_l = l * (NA_HEADS * NA_DROWS * NA_DCOLS)

    def body(d, carry):
        for h in range(NA_HEADS):
            base = base_l + (h * NA_DROWS + d) * NA_DCOLS
            t = jnp.zeros(shape, F32)
            for i in range(NA_DCOLS):
                val = jnp.where(left, rpb_ref[base + i], rpb_ref[base + NA_DCOLS + i])
                t = jnp.where(dc == i, val, t)
            o_ref[d, h * GRID_W:(h + 1) * GRID_W, :] = jnp.where(col_ok, t * LOG2E, NEG_INF)
        return carry

    lax.fori_loop(0, NA_DROWS - 1, body, 0)


def _rpb_tables(na_rpb):
    depth = na_rpb.shape[0]
    return pl.pallas_call(
        _rpb_kernel,
        out_shape=jax.ShapeDtypeStruct((depth, NA_DROWS - 1, NA_HEADS * GRID_W, LANES), F32),
        grid=(depth,),
        in_specs=[pl.BlockSpec(memory_space=pltpu.SMEM)],
        out_specs=pl.BlockSpec((None, NA_DROWS - 1, NA_HEADS * GRID_W, LANES), lambda l: (l, 0, 0, 0)),
        compiler_params=_params(),
        name="rpb_tables",
    )(na_rpb.reshape(-1))


def _head_rms(t, gain):
    width = t.shape[1]
    sq = (t * t).astype(BF16)
    blk = min(width, MXU_DIM)
    ones = jnp.where(_iota((blk, blk), 0) // HEAD_DIM == _iota((blk, blk), 1) // HEAD_DIM, 1.0, 0.0).astype(BF16)
    sums = [_dot(sq[:, c:c + blk], ones) for c in range(0, width, blk)]
    ss = sums[0] if len(sums) == 1 else jnp.concatenate(sums, axis=1)
    return t * lax.rsqrt(ss * (1.0 / HEAD_DIM) + EPS) * gain


def _rope(t, cos, sin):
    lane = _iota((t.shape[0], LANES), 1)
    first = (lane % (HEAD_DIM // 2)) < (HEAD_DIM // 4)
    out = []
    for c in range(0, t.shape[1], LANES):
        u = t[:, c:c + LANES]
        partner = jnp.where(first, pltpu.roll(u, LANES - HEAD_DIM // 4, axis=1), pltpu.roll(u, HEAD_DIM // 4, axis=1))
        out.append(u * cos + partner * sin)
    return out[0] if len(out) == 1 else jnp.concatenate(out, axis=1)


def _dup_kv(t):
    lane = _iota(t.shape, 1)
    swapped = pltpu.roll(t, HEAD_DIM, axis=1)
    left = lane < HEAD_DIM
    return jnp.concatenate([jnp.where(left, t, swapped), jnp.where(left, swapped, t)], axis=1)


def _inproj_kernel(*refs, rope, kv_only, n_w, sub):
    x_ref, mod_ref, g1_ref = refs[:3]
    w_refs = refs[3:3 + n_w]
    gnq_ref, gnk_ref, gwq_ref, gwk_ref = refs[3 + n_w:7 + n_w]
    rest = refs[7 + n_w:]
    if rope:
        cos_ref, sin_ref = rest[:2]
        rest = rest[2:]
    out_refs = rest
    if kv_only:
        nak_ref, nav_ref, wak_ref, wav_ref = out_refs
        o_nk, o_nv, o_wk, o_wv, o_end = 0, NA_WIDTH, 2 * NA_WIDTH, 2 * NA_WIDTH + WA_KVW, 2 * NA_WIDTH + 2 * WA_KVW
    else:
        naq_ref, nak_ref, nav_ref, cvu_ref, cvb_ref, waq_ref, wak_ref, wav_ref = out_refs
        o_nk, o_nv, o_wk, o_wv, o_end = OFF_NA_K, OFF_NA_V, OFF_WA_K, OFF_WA_V, IN_WIDTH
    for r0 in range(0, x_ref.shape[0], sub):
        rows = slice(r0, r0 + sub)
        x = x_ref[rows, :]
        ms = jnp.mean(x * x, axis=-1, keepdims=True)
        h = x * lax.rsqrt(ms + EPS) * g1_ref[...]
        h = h * (1.0 + mod_ref[1]) + mod_ref[0]
        hb = h.astype(BF16)
        p = jnp.concatenate([_dot(hb, w_ref[...]) for w_ref in w_refs], axis=1)
        nak_ref[rows, :] = _head_rms(p[:, o_nk:o_nk + NA_WIDTH], gnk_ref[...]).astype(BF16)
        nav_ref[rows, :] = p[:, o_nv:o_nv + NA_WIDTH].astype(BF16)
        wk = _head_rms(p[:, o_wk:o_wv], gwk_ref[...])
        if rope:
            wk = _rope(wk, cos_ref[rows, :], sin_ref[rows, :])
        wak_ref[rows, :] = _dup_kv(wk).astype(BF16)
        vv = p[:, o_wv:o_end]
        left = _iota(vv.shape, 1) < HEAD_DIM
        wav_ref[rows, :] = jnp.concatenate([jnp.where(left, vv, 1.0), jnp.where(left, pltpu.roll(vv, HEAD_DIM, axis=1), 1.0)],
                                           axis=1).astype(BF16)
        if kv_only:
            continue
        naq_ref[rows, :] = (_head_rms(p[:, OFF_NA_Q:OFF_NA_K], gnq_ref[...]) * Q_SCALE).astype(BF16)
        cvu_ref[rows, :] = p[:, OFF_CV_C:OFF_WA_Q] * p[:, OFF_CV_X:OFF_CV_B]
        cvb_ref[rows, :] = p[:, OFF_CV_B:OFF_CV_C]
        wq = _head_rms(p[:, OFF_WA_Q:OFF_WA_K], gwq_ref[...])
        if rope:
            wq = _rope(wq, cos_ref[rows, :], sin_ref[rows, :])
        waq_ref[rows, :] = (wq * Q_SCALE).astype(BF16)


def _in_proj(x2d, mod, layer, mod_row, g1, w_in, gains, rope_tabs, tm, kv_only=False):
    rows = x2d.shape[0]
    rope = rope_tabs is not None
    row_spec = lambda w: pl.BlockSpec((tm, w), lambda i: (i, 0))
    const = lambda shape: pl.BlockSpec(shape, lambda i: (0,) * len(shape))
    once = pl.Buffered(1)
    if kv_only:
        col_blocks = [OFF_NA_K // MXU_DIM, OFF_NA_V // MXU_DIM, OFF_WA_K // MXU_DIM]
        w_specs = [pl.BlockSpec((None, D_MODEL, MXU_DIM), functools.partial(lambda j, i: (layer, 0, j), j),
                                pipeline_mode=once) for j in col_blocks]
        widths = [(NA_WIDTH, BF16), (NA_WIDTH, BF16), (WA_K2W, BF16), (WA_V3W, BF16)]
    else:
        w_specs = [pl.BlockSpec((None, D_MODEL, IN_WIDTH), lambda i: (layer, 0, 0), pipeline_mode=once)]
        widths = [(NA_WIDTH, BF16), (NA_WIDTH, BF16), (NA_WIDTH, BF16), (CONV_CH, F32), (CONV_CH, F32),
                  (WA_QW, BF16), (WA_K2W, BF16), (WA_V3W, BF16)]
    n_w = len(w_specs)
    in_specs = [
        row_spec(D_MODEL),
        pl.BlockSpec((None, None, 6, 1, D_MODEL), lambda i: (layer, mod_row(i), 0, 0, 0)),
        const((1, D_MODEL)),
        *w_specs,
        const((1, NA_WIDTH)), const((1, NA_WIDTH)), const((1, WA_QW)), const((1, WA_KVW)),
    ]
    args = [x2d, mod, g1, *([w_in] * n_w), *gains]
    if rope:
        seq_tiles = rope_tabs[0].shape[0] // tm
        in_specs += [pl.BlockSpec((tm, LANES), lambda i: (i % seq_tiles, 0))] * 2
        args += list(rope_tabs)
    return pl.pallas_call(
        functools.partial(_inproj_kernel, rope=rope, kv_only=kv_only, n_w=n_w, sub=min(tm, IN_SUB)),
        out_shape=[jax.ShapeDtypeStruct((rows, w), dt) for w, dt in widths],
        grid=(rows // tm,),
        in_specs=in_specs,
        out_specs=[row_spec(w) for w, _ in widths],
        compiler_params=_params(),
        name="in_proj_rope" if rope else ("in_proj_ctx_kv" if kv_only else "in_proj_ctx"),
    )(*args)


def _lane_chunks(s):
    return [s[:, c:c + LANES] for c in range(0, s.shape[1], LANES)]


def _row_max(scores, extra=None):
    chunks = [ch for s in scores for ch in _lane_chunks(s)]
    if extra is not None:
        chunks.append(extra)
    folded = functools.reduce(jnp.maximum, chunks)
    return jnp.broadcast_to(jnp.max(folded, axis=-1, keepdims=True), folded.shape)


def _softmax_pv(parts, extra=None, sums_from_values=False, m=None):
    lane_chunks = _lane_chunks
    if m is None:
        m = _row_max([s for s, _ in parts], extra)
    ps = [jnp.concatenate([jnp.exp2(ch - m) for ch in lane_chunks(s)], axis=1) for s, _ in parts]
    pv = _dot(jnp.concatenate([p.astype(BF16) for p in ps], axis=1),
              jnp.concatenate([v for _, v in parts], axis=0))
    if sums_from_values:
        denom = pv[:, -LANES:]
        pv = pv[:, :-LANES]
        if extra is not None:
            denom = denom + jnp.exp2(extra - m)
        return pv / denom
    psum = functools.reduce(jnp.add, [ch for p in ps for ch in lane_chunks(p)])
    if extra is not None:
        psum = psum + jnp.where(_iota(extra.shape, 1) == 0, jnp.exp2(extra - m), 0.0)
    return pv / jnp.sum(psum, axis=-1, keepdims=True)


def _stack_heads(q, n_heads):
    head = _iota(q.shape, 1) // HEAD_DIM
    zero = jnp.zeros_like(q)
    return jnp.concatenate([jnp.where(head == h, q, zero) for h in range(n_heads)], axis=0)


def _unstack_heads(o, n_heads):
    rows = o.shape[0] // n_heads
    head = _iota((rows, o.shape[1]), 1) // HEAD_DIM
    out = jnp.where(head == 0, o[:rows], 0.0)
    for h in range(1, n_heads):
        out = jnp.where(head == h, o[h * rows:(h + 1) * rows], out)
    return out


def _group_rms(o, gain):
    ms = jnp.mean(o * o, axis=-1, keepdims=True)
    return o * lax.rsqrt(ms + EPS) * gain


SUBLANES = 8


def _wa_scores(qg, keys):
    rows = qg.shape[0]
    left = _iota((rows, LANES), 1) < HEAD_DIM
    zero = jnp.zeros((rows, LANES), qg.dtype)
    blocks = []
    for pair in range(WA_GROUP // 2):
        qp = qg[:, pair * LANES:(pair + 1) * LANES]
        blocks += [jnp.where(left, qp, zero), jnp.where(left, zero, qp)]
    qs = jnp.concatenate(blocks, axis=0)
    scores = []
    for k2, bias in keys:
        s = _dot_nt(k2, qs)
        if bias is not None:
            s = s + jnp.concatenate([bias] * WA_GROUP, axis=1)
        scores.append(s)
    return scores


def _wa_sink_rows(rows, sinks):
    return jnp.concatenate([jnp.full((SUBLANES, rows), sinks[g] * LOG2E, F32) for g in range(WA_GROUP)], axis=1)


def _wa_col_max(scores, sink_rows):
    m = sink_rows[:1]
    for s in scores:
        m = jnp.maximum(m, jnp.max(s, axis=0, keepdims=True))
    return jnp.broadcast_to(m, sink_rows.shape)


def _wa_finish(scores, values, sinks, m=None):
    rows = scores[0].shape[1] // WA_GROUP
    sink_rows = _wa_sink_rows(rows, sinks)
    if m is None:
        m = _wa_col_max(scores, sink_rows)
    p = jnp.concatenate([jnp.exp2(s - m[:1]).astype(BF16) for s in scores], axis=0)
    v = jnp.concatenate(values, axis=0)
    ot = lax.dot_general(v, p, (((0,), (0,)), ((), ())), preferred_element_type=F32)
    denom = ot[HEAD_DIM:HEAD_DIM + SUBLANES] + jnp.exp2(sink_rows - m)
    on = ot[:HEAD_DIM] / denom[:1]
    pairs = [jnp.concatenate([on[:, (2 * pr) * rows:(2 * pr + 1) * rows],
                              on[:, (2 * pr + 1) * rows:(2 * pr + 2) * rows]], axis=0).T
             for pr in range(WA_GROUP // 2)]
    return jnp.concatenate(pairs, axis=1)


def _na_steps(q_ref, k_ref, v_ref, kc_ref, vc_ref, bias_ref, g_ref, o_ref):
    n_rows = q_ref.shape[0] // GRID_W
    win = NA_WIN_ROWS * GRID_W
    lc = kc_ref.shape[0]
    gain = g_ref[...]

    def window(r):
        start = jnp.clip(r - NA_WIN_ROWS // 2, 0, n_rows - NA_WIN_ROWS)
        return pl.multiple_of(start * GRID_W, GRID_W), start - r + (NA_WIN_ROWS - 1)

    def scores(r, s_ref):
        tok0, d0 = window(r)
        q = q_ref[pl.ds(pl.multiple_of(r * GRID_W, GRID_W), GRID_W), :]
        qs = _stack_heads(q, NA_HEADS)
        bias = jnp.concatenate([bias_ref[d0 + 2 * j] for j in range(NA_WIN_ROWS // 2)], axis=1)
        s_ctx = _dot_nt(qs, kc_ref[...])
        s_loc = _dot_nt(qs, k_ref[pl.ds(tok0, win), :]) + bias
        s_ref[:, :lc] = s_ctx
        s_ref[:, lc:lc + win] = s_loc
        s_ref[:, lc + win:] = _row_max([s_ctx, s_loc])

    def finish(r, s_ref):
        tok0, _ = window(r)
        o4 = _softmax_pv([(s_ref[:, :lc], vc_ref[...]), (s_ref[:, lc:lc + win], v_ref[pl.ds(tok0, win), :])],
                         m=s_ref[:, lc + win:])
        o = _unstack_heads(o4, NA_HEADS)
        o_ref[pl.ds(pl.multiple_of(r * GRID_W, GRID_W), GRID_W), :] = _group_rms(o, gain).astype(o_ref.dtype)

    def step(r, cur_ref, nxt_ref):
        scores(jnp.minimum(r + 1, n_rows - 1), nxt_ref)
        finish(r, cur_ref)

    return scores, step


def _wa_steps(sink_ref, q_ref, k_ref, v_ref, kc_ref, vc_ref, g_ref, o_ref):
    seq = q_ref.shape[0]
    n_blocks = seq // WA_BLOCK
    lc = kc_ref.shape[0]
    gain = g_ref[...]
    key = _iota((WA_BAND, WA_BLOCK), 0)
    qry = _iota((WA_BAND, WA_BLOCK), 1)

    def window(n):
        q0 = pl.multiple_of(n * WA_BLOCK, WA_BLOCK)
        k0 = pl.multiple_of(jnp.clip(q0 - WA_BLOCK, 0, seq - WA_BAND), WA_BLOCK)
        return q0, k0

    def scores(n, kh, s_ref):
        q0, k0 = window(n)
        band = jnp.where(jnp.abs((k0 + key) - (q0 + qry)) <= WA_WINDOW, 0.0, NEG_INF)
        lanes = slice(kh * LANES, (kh + 1) * LANES)
        qg = q_ref[pl.ds(q0, WA_BLOCK), kh * WA_GROUP * HEAD_DIM:(kh + 1) * WA_GROUP * HEAD_DIM]
        s_ctx, s_loc = _wa_scores(qg, [(kc_ref[:, lanes], None), (k_ref[pl.ds(k0, WA_BAND), lanes], band)])
        s_ref[kh, :lc, :] = s_ctx
        s_ref[kh, lc:lc + WA_BAND, :] = s_loc
        s_ref[kh, lc + WA_BAND:, :] = _wa_col_max([s_ctx, s_loc], _wa_sink_rows(WA_BLOCK, sinks(kh)))

    def sinks(kh):
        return [sink_ref[kh * WA_GROUP + g] for g in range(WA_GROUP)]

    def finish(n, kh, s_ref):
        _, k0 = window(n)
        lanes = slice(kh * LANES, (kh + 1) * LANES)
        return _wa_finish([s_ref[kh, :lc, :], s_ref[kh, lc:lc + WA_BAND, :]],
                          [vc_ref[:, lanes], v_ref[pl.ds(k0, WA_BAND), lanes]], sinks(kh),
                          m=s_ref[kh, lc + WA_BAND:, :])

    def step(n, cur_ref, nxt_ref):
        nxt = jnp.minimum(n + 1, n_blocks - 1)
        for kh in range(WA_KV_HEADS):
            scores(nxt, kh, nxt_ref)
        o = jnp.concatenate([finish(n, kh, cur_ref) for kh in range(WA_KV_HEADS)], axis=1)
        q0, _ = window(n)
        o_ref[pl.ds(q0, WA_BLOCK), :] = _group_rms(o, gain).astype(o_ref.dtype)

    return scores, step


def _attn_kernel(sink_ref, nq_ref, nk_ref, nv_ref, nkc_ref, nvc_ref, bias_ref, gn_ref,
                 wq_ref, wk_ref, wv_ref, wkc_ref, wvc_ref, gw_ref, on_ref, ow_ref,
                 nsa_ref, nsb_ref, wsa_ref, wsb_ref):
    na_scores, na_step = _na_steps(nq_ref, nk_ref, nv_ref, nkc_ref, nvc_ref, bias_ref, gn_ref, on_ref)
    wa_scores, wa_step = _wa_steps(sink_ref, wq_ref, wk_ref, wv_ref, wkc_ref, wvc_ref, gw_ref, ow_ref)
    n_blocks = wq_ref.shape[0] // WA_BLOCK

    for kh in range(WA_KV_HEADS):
        wa_scores(0, kh, wsa_ref)
    na_scores(0, nsa_ref)

    def body(i, carry):
        for half, (w_cur, w_nxt) in enumerate(((wsa_ref, wsb_ref), (wsb_ref, wsa_ref))):
            n = 2 * i + half
            wa_step(n, w_cur, w_nxt)
            na_step(2 * n, nsa_ref, nsb_ref)
            na_step(2 * n + 1, nsb_ref, nsa_ref)
        return carry

    lax.fori_loop(0, n_blocks // 2, body, 0)


def _attention(sink, nq, nk, nv, nkc, nvc, bias, layer, gain_na, wq, wk2, wv3, wkc2, wvc3, gain_wa):
    batch, seq, _ = nq.shape
    lc = nkc.shape[1]
    assert WA_BLOCK == 2 * GRID_W and (seq // WA_BLOCK) % 2 == 0
    per_b = lambda n, w: pl.BlockSpec((None, n, w), lambda b: (b, 0, 0))
    na_scratch = pltpu.VMEM((NA_HEADS * GRID_W, lc + NA_WIN_ROWS * GRID_W + LANES), F32)
    wa_scratch = pltpu.VMEM((WA_KV_HEADS, lc + WA_BAND + SUBLANES, WA_GROUP * WA_BLOCK), F32)
    return pl.pallas_call(
        _attn_kernel,
        out_shape=[jax.ShapeDtypeStruct((batch, seq, NA_WIDTH), BF16),
                   jax.ShapeDtypeStruct((batch, seq, WA_QW), BF16)],
        grid=(batch,),
        in_specs=[pl.BlockSpec(memory_space=pltpu.SMEM),
                  per_b(seq, NA_WIDTH), per_b(seq, NA_WIDTH), per_b(seq, NA_WIDTH),
                  per_b(lc, NA_WIDTH), per_b(lc, NA_WIDTH),
                  pl.BlockSpec((None,) + bias.shape[1:], lambda b: (layer, 0, 0, 0)),
                  pl.BlockSpec((1, NA_WIDTH), lambda b: (0, 0)),
                  per_b(seq, WA_QW), per_b(seq, WA_K2W), per_b(seq, WA_V3W),
                  per_b(lc, WA_K2W), per_b(lc, WA_V3W),
                  pl.BlockSpec((1, WA_QW), lambda b: (0, 0))],
        out_specs=[per_b(seq, NA_WIDTH), per_b(seq, WA_QW)],
        scratch_shapes=[na_scratch, na_scratch, wa_scratch, wa_scratch],
        compiler_params=_params(),
        name="attention",
    )(sink, nq, nk, nv, nkc, nvc, bias, gain_na, wq, wk2, wv3, wkc2, wvc3, gain_wa)


def _ctx_attn_kernel(sink_ref, qn_ref, kn_ref, vn_ref, qw_ref, kw_ref, vw_ref, gn_ref, gw_ref, on_ref, ow_ref):
    qs = _stack_heads(qn_ref[...], NA_HEADS)
    o4 = _softmax_pv([(_dot_nt(qs, kn_ref[...]), vn_ref[...])])
    on_ref[...] = _group_rms(_unstack_heads(o4, NA_HEADS), gn_ref[...]).astype(on_ref.dtype)
    outs = []
    for kh in range(WA_KV_HEADS):
        lanes = slice(kh * LANES, (kh + 1) * LANES)
        qg = qw_ref[:, kh * WA_GROUP * HEAD_DIM:(kh + 1) * WA_GROUP * HEAD_DIM]
        sinks = [sink_ref[kh * WA_GROUP + g] for g in range(WA_GROUP)]
        outs.append(_wa_finish(_wa_scores(qg, [(kw_ref[:, lanes], None)]), [vw_ref[:, lanes]], sinks))
    ow_ref[...] = _group_rms(jnp.concatenate(outs, axis=1), gw_ref[...]).astype(ow_ref.dtype)


def _ctx_attention(sink, qn, kn, vn, qw, kw2, vw2, gain_na, gain_wa):
    batch, lc, _ = qn.shape
    per_b = lambda w: pl.BlockSpec((None, lc, w), lambda b: (b, 0, 0))
    return pl.pallas_call(
        _ctx_attn_kernel,
        out_shape=[jax.ShapeDtypeStruct((batch, lc, NA_WIDTH), BF16),
                   jax.ShapeDtypeStruct((batch, lc, WA_QW), BF16)],
        grid=(batch,),
        in_specs=[pl.BlockSpec(memory_space=pltpu.SMEM),
                  per_b(NA_WIDTH), per_b(NA_WIDTH), per_b(NA_WIDTH),
                  per_b(WA_QW), per_b(WA_K2W), per_b(WA_V3W),
                  pl.BlockSpec((1, NA_WIDTH), lambda b: (0, 0)),
                  pl.BlockSpec((1, WA_QW), lambda b: (0, 0))],
        out_specs=[per_b(NA_WIDTH), per_b(WA_QW)],
        compiler_params=_params(),
        name="ctx_attention",
    )(sink, qn, kn, vn, qw, kw2, vw2, gain_na, gain_wa)


def _out_mlp_kernel(x_ref, mod_ref, na_ref, wa_ref, u_ref, uprev_ref, unext_ref, cvb_ref, cw_ref, cb_ref, g_ref,
                    g2_ref, wo_ref, *rest, seq_len, sub):
    n_ffn = FFN_DIM // FFN_CHUNK
    w1_refs, w2_refs, (o_ref,) = rest[:n_ffn], rest[n_ffn:2 * n_ffn], rest[2 * n_ffn:]
    tm = x_ref.shape[0]
    i = pl.program_id(0)
    u = u_ref[...]
    at_start = (i * tm) % seq_len == 0
    at_end = ((i + 1) * tm) % seq_len == 0
    prev_row = jnp.where(at_start, 0.0, uprev_ref[7:8, :])
    next_row = jnp.where(at_end, 0.0, unext_ref[0:1, :])
    row = _iota(u.shape, 0)
    up = jnp.where(row == 0, prev_row, pltpu.roll(u, 1, axis=0))
    dn = jnp.where(row == tm - 1, next_row, pltpu.roll(u, tm - 1, axis=0))
    y = cb_ref[...] + cw_ref[0:1, :] * up + cw_ref[1:2, :] * u + cw_ref[2:3, :] * dn
    cv = _group_rms(cvb_ref[...] * y, g_ref[...]).astype(BF16)
    for r0 in range(0, tm, sub):
        rows = slice(r0, r0 + sub)
        mixed = jnp.concatenate([na_ref[rows, :], cv[rows, :], wa_ref[rows, :]], axis=1)
        x = x_ref[rows, :] + mod_ref[2] * _dot(mixed, wo_ref[...])
        ms = jnp.mean(x * x, axis=-1, keepdims=True)
        h = x * lax.rsqrt(ms + EPS) * g2_ref[...]
        h = (h * (1.0 + mod_ref[4]) + mod_ref[3]).astype(BF16)
        acc = None
        for w1_ref, w2_ref in zip(w1_refs, w2_refs):
            a = jnp.maximum(_dot(h, w1_ref[...]), 0.0)
            part = _dot((a * a).astype(BF16), w2_ref[...])
            acc = part if acc is None else acc + part
        o_ref[rows, :] = x + mod_ref[5] * acc


def _out_mlp(x2d, mod, layer, mod_row, na_n, wa_n, cv_u, cv_b, conv_w, conv_b, gain_cv, w_o, g2, w1, w2, seq_len, tm):
    rows = x2d.shape[0]
    halo = 8
    per_tile = tm // halo
    last = rows // halo - 1
    row_spec = lambda w: pl.BlockSpec((tm, w), lambda i: (i, 0))
    const = lambda shape: pl.BlockSpec(shape, lambda i: (0,) * len(shape), pipeline_mode=pl.Buffered(1))
    once = pl.Buffered(1)
    n_ffn = FFN_DIM // FFN_CHUNK
    wo_spec = pl.BlockSpec((None, D_MODEL, D_MODEL), lambda i: (layer, 0, 0), pipeline_mode=once)
    w1_specs = [pl.BlockSpec((None, D_MODEL, FFN_CHUNK), functools.partial(lambda j, i: (layer, 0, j), j),
                             pipeline_mode=once) for j in range(n_ffn)]
    w2_specs = [pl.BlockSpec((None, FFN_CHUNK, D_MODEL), functools.partial(lambda j, i: (layer, j, 0), j),
                             pipeline_mode=once) for j in range(n_ffn)]
    return pl.pallas_call(
        functools.partial(_out_mlp_kernel, seq_len=seq_len, sub=min(tm, MLP_SUB)),
        out_shape=jax.ShapeDtypeStruct((rows, D_MODEL), F32),
        grid=(rows // tm,),
        in_specs=[
            row_spec(D_MODEL),
            pl.BlockSpec((None, None, 6, 1, D_MODEL), lambda i: (layer, mod_row(i), 0, 0, 0)),
            row_spec(NA_WIDTH), row_spec(WA_QW), row_spec(CONV_CH),
            pl.BlockSpec((halo, CONV_CH), lambda i: (jnp.maximum(i * per_tile - 1, 0), 0)),
            pl.BlockSpec((halo, CONV_CH), lambda i: (jnp.minimum((i + 1) * per_tile, last), 0)),
            row_spec(CONV_CH),
            const((3, CONV_CH)), const((1, CONV_CH)), const((1, CONV_CH)),
            const((1, D_MODEL)),
            wo_spec, *w1_specs, *w2_specs,
        ],
        out_specs=row_spec(D_MODEL),
        compiler_params=_params(),
        name="out_mlp",
    )(x2d, mod, na_n, wa_n, cv_u, cv_u, cv_u, cv_b, conv_w, conv_b, gain_cv, g2,
      w_o, *([w1] * n_ffn), *([w2] * n_ffn))


def _rope_tables(seq):
    quarter = HEAD_DIM // 4
    inv = ROPE_BASE ** (-jnp.arange(quarter, dtype=F32) / quarter)
    t = jnp.arange(seq)
    ang_r = (t // GRID_W).astype(F32)[:, None] * inv[None, :]
    ang_c = (t % GRID_W).astype(F32)[:, None] * inv[None, :]
    cos = jnp.concatenate([jnp.cos(ang_r), jnp.cos(ang_r), jnp.cos(ang_c), jnp.cos(ang_c)], axis=1)
    sin = jnp.concatenate([-jnp.sin(ang_r), jnp.sin(ang_r), -jnp.sin(ang_c), jnp.sin(ang_c)], axis=1)
    reps = LANES // HEAD_DIM
    return jnp.tile(cos, (1, reps)), jnp.tile(sin, (1, reps))


def kernel(x, c, ctx, c_ctx, w_mod, b_mod, g_norm1, g_norm2, w_in, na_q_gain, na_k_gain, na_rpb,
           conv_w, conv_bias, wa_q_gain, wa_k_gain, wa_sink, g_out, w_o, w_fc1, w_fc2):
    batch, seq, d = x.shape
    lc = ctx.shape[1]
    depth = w_mod.shape[0]
    assert d == D_MODEL and seq % WA_BLOCK == 0 and seq % GRID_W == 0 and batch < MOD_ROWS
    tm = 1024
    tm_in = 1024
    tm_ctx = 256
    assert seq % tm == 0 and seq % tm_in == 0 and lc % tm_ctx == 0

    cond = jnp.zeros((MOD_ROWS, d), F32).at[:batch].set(c).at[batch].set(c_ctx)
    mod = _modulation(cond, w_mod, b_mod).reshape(depth, MOD_ROWS, 6, 1, d)
    rpb_bias = _rpb_tables(na_rpb)
    rope_tabs = _rope_tables(seq)

    lat_row = lambda i: i // (seq // tm)
    lat_row_in = lambda i: i // (seq // tm_in)
    ctx_row = lambda i: batch

    w_in_bf = w_in.astype(BF16)
    w_o_bf = w_o.astype(BF16)
    w1_bf = w_fc1.astype(BF16)
    w2_bf = w_fc2.astype(BF16)
    xs = x.reshape(batch * seq, d)
    cs = ctx.reshape(batch * lc, d)
    for l in range(depth):
        last = l == depth - 1
        g1 = g_norm1[l].reshape(1, d)
        g2 = g_norm2[l].reshape(1, d)
        gains = [jnp.tile(na_q_gain[l], NA_HEADS).reshape(1, -1), jnp.tile(na_k_gain[l], NA_HEADS).reshape(1, -1),
                 jnp.tile(wa_q_gain[l], WA_HEADS).reshape(1, -1), jnp.tile(wa_k_gain[l], WA_KV_HEADS).reshape(1, -1)]
        go_na = g_out[l, :NA_WIDTH].reshape(1, -1)
        go_cv = g_out[l, NA_WIDTH:NA_WIDTH + CONV_CH].reshape(1, -1)
        go_wa = g_out[l, NA_WIDTH + CONV_CH:].reshape(1, -1)
        cb = conv_bias[l].reshape(1, -1)

        naq, nak, nav, cvu, cvb, waq, wak, wav = _in_proj(xs, mod, l, lat_row_in, g1, w_in_bf, gains, rope_tabs, tm_in)
        if last:
            cnak, cnav, cwak, cwav = _in_proj(cs, mod, l, ctx_row, g1, w_in_bf, gains, None, tm_ctx, kv_only=True)
        else:
            cnaq, cnak, cnav, ccvu, ccvb, cwaq, cwak, cwav = _in_proj(cs, mod, l, ctx_row, g1, w_in_bf, gains, None, tm_ctx)

        b3 = lambda a, n: a.reshape(batch, n, a.shape[-1])
        na_n, wa_n = _attention(wa_sink[l], b3(naq, seq), b3(nak, seq), b3(nav, seq), b3(cnak, lc), b3(cnav, lc),
                                rpb_bias, l, go_na, b3(waq, seq), b3(wak, seq), b3(wav, seq), b3(cwak, lc),
                                b3(cwav, lc), go_wa)
        xs = _out_mlp(xs, mod, l, lat_row, na_n.reshape(batch * seq, -1), wa_n.reshape(batch * seq, -1),
                      cvu, cvb, conv_w[l], cb, go_cv, w_o_bf, g2, w1_bf, w2_bf, seq, tm)
        if not last:
            cna_n, cwa_n = _ctx_attention(wa_sink[l], b3(cnaq, lc), b3(cnak, lc), b3(cnav, lc),
                                          b3(cwaq, lc), b3(cwak, lc), b3(cwav, lc), go_na, go_wa)
            cs = _out_mlp(cs, mod, l, ctx_row, cna_n.reshape(batch * lc, -1), cwa_n.reshape(batch * lc, -1),
                          ccvu, ccvb, conv_w[l], cb, go_cv, w_o_bf, g2, w1_bf, w2_bf, lc, tm_ctx)
    return xs.reshape(batch, seq, d)
```

```python
import functools

import jax
import jax.numpy as jnp
from jax import lax
from jax.experimental import pallas as pl
from jax.experimental.pallas import tpu as pltpu

D_MODEL = 1024
GRID_W = 64
HEAD_DIM = 64
NA_HEADS = 4
NA_WIDTH = NA_HEADS * HEAD_DIM
CONV_CH = 256
WA_HEADS = 8
WA_KV_HEADS = 2
WA_GROUP = WA_HEADS // WA_KV_HEADS
WA_QW = WA_HEADS * HEAD_DIM
WA_KVW = WA_KV_HEADS * HEAD_DIM
WA_K2W = 2 * WA_KVW
WA_V3W = 2 * WA_KVW
NA_WIN_ROWS = 8
NA_WIN_COLS = 16
NA_DROWS = 2 * NA_WIN_ROWS - 1
NA_DCOLS = 2 * NA_WIN_COLS - 1
WA_WINDOW = 128
WA_BLOCK = 128
WA_BAND = 3 * WA_BLOCK
FFN_DIM = 4 * D_MODEL
FFN_CHUNK = 1024
ROPE_BASE = 10000.0
EPS = 1e-6
NEG_INF = -1e30
IN_WIDTH = 2304
OFF_NA_Q, OFF_NA_K, OFF_NA_V = 0, 256, 512
OFF_CV_X, OFF_CV_B, OFF_CV_C = 768, 1024, 1280
OFF_WA_Q, OFF_WA_K, OFF_WA_V = 1536, 2048, 2176
MOD_ROWS = 16
MOD_TN = 1024
IN_SUB = 256
MLP_SUB = 512
LANES = 128
MXU_DIM = 256
VMEM_LIMIT = 60 * 1024 * 1024
LOG2E = 1.4426950408889634
Q_SCALE = LOG2E * HEAD_DIM ** -0.5

F32 = jnp.float32
BF16 = jnp.bfloat16


def _dot(a, b):
    return jnp.dot(a, b, preferred_element_type=F32)


def _dot_nt(a, b):
    return lax.dot_general(a, b, (((1,), (1,)), ((), ())), preferred_element_type=F32)


def _split_bf16(a):
    hi = a.astype(BF16)
    lo = (a - hi.astype(F32)).astype(BF16)
    return hi, lo


def _iota(shape, dim):
    return lax.broadcasted_iota(jnp.int32, shape, dim)


def _params(**kw):
    return pltpu.CompilerParams(vmem_limit_bytes=VMEM_LIMIT, **kw)


def _mod_kernel(cond_ref, w_ref, b_ref, o_ref):
    a = cond_ref[...]
    a = a * (1.0 / (1.0 + jnp.exp(-a)))
    ah, al = _split_bf16(a)
    wh, wl = _split_bf16(w_ref[...])
    o_ref[...] = _dot(ah, wh) + _dot(al, wh) + _dot(ah, wl) + b_ref[...]


def _modulation(cond, w_mod, b_mod):
    depth = w_mod.shape[0]
    n_out = w_mod.shape[2]
    return pl.pallas_call(
        _mod_kernel,
        out_shape=jax.ShapeDtypeStruct((depth, MOD_ROWS, n_out), F32),
        grid=(depth, n_out // MOD_TN),
        in_specs=[
            pl.BlockSpec((MOD_ROWS, D_MODEL), lambda l, j: (0, 0)),
            pl.BlockSpec((None, D_MODEL, MOD_TN), lambda l, j: (l, 0, j)),
            pl.BlockSpec((None, 1, MOD_TN), lambda l, j: (l, 0, j)),
        ],
        out_specs=pl.BlockSpec((None, MOD_ROWS, MOD_TN), lambda l, j: (l, 0, j)),
        compiler_params=_params(),
        name="modulation",
    )(cond, w_mod, b_mod.reshape(depth, 1, n_out))


def _rpb_kernel(rpb_ref, o_ref):
    l = pl.program_id(0)
    shape = (GRID_W, LANES)
    q = _iota(shape, 0)
    lane = _iota(shape, 1)
    k = lane % GRID_W
    left = lane < GRID_W
    dc = jnp.clip(k - q, -(NA_WIN_COLS - 1), NA_WIN_COLS - 1) + (NA_WIN_COLS - 1)
    col_start = jnp.clip(q - NA_WIN_COLS // 2, 0, GRID_W - NA_WIN_COLS)
    col_ok = (k >= col_start) & (k < col_start + NA_WIN_COLS)
    base_l = l * (NA_HEADS * NA_DROWS * NA_DCOLS)

    def body(d, carry):
        for h in range(NA_HEADS):
            base = base_l + (h * NA_DROWS + d) * NA_DCOLS
            t = jnp.zeros(shape, F32)
            for i in range(NA_DCOLS):
                val = jnp.where(left, rpb_ref[base + i], rpb_ref[base + NA_DCOLS + i])
                t = jnp.where(dc == i, val, t)
            o_ref[d, h * GRID_W:(h + 1) * GRID_W, :] = jnp.where(col_ok, t * LOG2E, NEG_INF)
        return carry

    lax.fori_loop(0, NA_DROWS - 1, body, 0)


def _rpb_tables(na_rpb):
    depth = na_rpb.shape[0]
    return pl.pallas_call(
        _rpb_kernel,
        out_shape=jax.ShapeDtypeStruct((depth, NA_DROWS - 1, NA_HEADS * GRID_W, LANES), F32),
        grid=(depth,),
        in_specs=[pl.BlockSpec(memory_space=pltpu.SMEM)],
        out_specs=pl.BlockSpec((None, NA_DROWS - 1, NA_HEADS * GRID_W, LANES), lambda l: (l, 0, 0, 0)),
        compiler_params=_params(),
        name="rpb_tables",
    )(na_rpb.reshape(-1))


def _head_rms(t, gain):
    width = t.shape[1]
    sq = (t * t).astype(BF16)
    blk = min(width, MXU_DIM)
    ones = jnp.where(_iota((blk, blk), 0) // HEAD_DIM == _iota((blk, blk), 1) // HEAD_DIM, 1.0, 0.0).astype(BF16)
    sums = [_dot(sq[:, c:c + blk], ones) for c in range(0, width, blk)]
    ss = sums[0] if len(sums) == 1 else jnp.concatenate(sums, axis=1)
    return t * lax.rsqrt(ss * (1.0 / HEAD_DIM) + EPS) * gain


def _rope(t, cos, sin):
    lane = _iota((t.shape[0], LANES), 1)
    first = (lane % (HEAD_DIM // 2)) < (HEAD_DIM // 4)
    out = []
    for c in range(0, t.shape[1], LANES):
        u = t[:, c:c + LANES]
        partner = jnp.where(first, pltpu.roll(u, LANES - HEAD_DIM // 4, axis=1), pltpu.roll(u, HEAD_DIM // 4, axis=1))
        out.append(u * cos + partner * sin)
    return out[0] if len(out) == 1 else jnp.concatenate(out, axis=1)


def _dup_kv(t):
    lane = _iota(t.shape, 1)
    swapped = pltpu.roll(t, HEAD_DIM, axis=1)
    left = lane < HEAD_DIM
    return jnp.concatenate([jnp.where(left, t, swapped), jnp.where(left, swapped, t)], axis=1)


def _inproj_kernel(*refs, rope, kv_only, n_w, sub):
    x_ref, mod_ref, g1_ref = refs[:3]
    w_refs = refs[3:3 + n_w]
    gnq_ref, gnk_ref, gwq_ref, gwk_ref = refs[3 + n_w:7 + n_w]
    rest = refs[7 + n_w:]
    if rope:
        cos_ref, sin_ref = rest[:2]
        rest = rest[2:]
    out_refs = rest
    if kv_only:
        nak_ref, nav_ref, wak_ref, wav_ref = out_refs
        o_nk, o_nv, o_wk, o_wv, o_end = 0, NA_WIDTH, 2 * NA_WIDTH, 2 * NA_WIDTH + WA_KVW, 2 * NA_WIDTH + 2 * WA_KVW
    else:
        naq_ref, nak_ref, nav_ref, cvu_ref, cvb_ref, waq_ref, wak_ref, wav_ref = out_refs
        o_nk, o_nv, o_wk, o_wv, o_end = OFF_NA_K, OFF_NA_V, OFF_WA_K, OFF_WA_V, IN_WIDTH
    for r0 in range(0, x_ref.shape[0], sub):
        rows = slice(r0, r0 + sub)
        x = x_ref[rows, :]
        ms = jnp.mean(x * x, axis=-1, keepdims=True)
        h = x * lax.rsqrt(ms + EPS) * g1_ref[...]
        h = h * (1.0 + mod_ref[1]) + mod_ref[0]
        hb = h.astype(BF16)
        p = jnp.concatenate([_dot(hb, w_ref[...]) for w_ref in w_refs], axis=1)
        nak_ref[rows, :] = _head_rms(p[:, o_nk:o_nk + NA_WIDTH], gnk_ref[...]).astype(BF16)
        nav_ref[rows, :] = p[:, o_nv:o_nv + NA_WIDTH].astype(BF16)
        wk = _head_rms(p[:, o_wk:o_wv], gwk_ref[...])
        if rope:
            wk = _rope(wk, cos_ref[rows, :], sin_ref[rows, :])
        wak_ref[rows, :] = _dup_kv(wk).astype(BF16)
        vv = p[:, o_wv:o_end]
        left = _iota(vv.shape, 1) < HEAD_DIM
        wav_ref[rows, :] = jnp.concatenate([jnp.where(left, vv, 1.0), jnp.where(left, pltpu.roll(vv, HEAD_DIM, axis=1), 1.0)],
                                           axis=1).astype(BF16)
        if kv_only:
            continue
        naq_ref[rows, :] = (_head_rms(p[:, OFF_NA_Q:OFF_NA_K], gnq_ref[...]) * Q_SCALE).astype(BF16)
        cvu_ref[rows, :] = p[:, OFF_CV_C:OFF_WA_Q] * p[:, OFF_CV_X:OFF_CV_B]
        cvb_ref[rows, :] = p[:, OFF_CV_B:OFF_CV_C]
        wq = _head_rms(p[:, OFF_WA_Q:OFF_WA_K], gwq_ref[...])
        if rope:
            wq = _rope(wq, cos_ref[rows, :], sin_ref[rows, :])
        waq_ref[rows, :] = (wq * Q_SCALE).astype(BF16)


def _in_proj(x2d, mod, layer, mod_row, g1, w_in, gains, rope_tabs, tm, kv_only=False):
    rows = x2d.shape[0]
    rope = rope_tabs is not None
    row_spec = lambda w: pl.BlockSpec((tm, w), lambda i: (i, 0))
    const = lambda shape: pl.BlockSpec(shape, lambda i: (0,) * len(shape))
    once = pl.Buffered(1)
    if kv_only:
        col_blocks = [OFF_NA_K // MXU_DIM, OFF_NA_V // MXU_DIM, OFF_WA_K // MXU_DIM]
        w_specs = [pl.BlockSpec((None, D_MODEL, MXU_DIM), functools.partial(lambda j, i: (layer, 0, j), j),
                                pipeline_mode=once) for j in col_blocks]
        widths = [(NA_WIDTH, BF16), (NA_WIDTH, BF16), (WA_K2W, BF16), (WA_V3W, BF16)]
    else:
        w_specs = [pl.BlockSpec((None, D_MODEL, IN_WIDTH), lambda i: (layer, 0, 0), pipeline_mode=once)]
        widths = [(NA_WIDTH, BF16), (NA_WIDTH, BF16), (NA_WIDTH, BF16), (CONV_CH, F32), (CONV_CH, F32),
                  (WA_QW, BF16), (WA_K2W, BF16), (WA_V3W, BF16)]
    n_w = len(w_specs)
    in_specs = [
        row_spec(D_MODEL),
        pl.BlockSpec((None, None, 6, 1, D_MODEL), lambda i: (layer, mod_row(i), 0, 0, 0)),
        const((1, D_MODEL)),
        *w_specs,
        const((1, NA_WIDTH)), const((1, NA_WIDTH)), const((1, WA_QW)), const((1, WA_KVW)),
    ]
    args = [x2d, mod, g1, *([w_in] * n_w), *gains]
    if rope:
        seq_tiles = rope_tabs[0].shape[0] // tm
        in_specs += [pl.BlockSpec((tm, LANES), lambda i: (i % seq_tiles, 0))] * 2
        args += list(rope_tabs)
    return pl.pallas_call(
        functools.partial(_inproj_kernel, rope=rope, kv_only=kv_only, n_w=n_w, sub=min(tm, IN_SUB)),
        out_shape=[jax.ShapeDtypeStruct((rows, w), dt) for w, dt in widths],
        grid=(rows // tm,),
        in_specs=in_specs,
        out_specs=[row_spec(w) for w, _ in widths],
        compiler_params=_params(),
        name="in_proj_rope" if rope else ("in_proj_ctx_kv" if kv_only else "in_proj_ctx"),
    )(*args)


def _lane_chunks(s):
    return [s[:, c:c + LANES] for c in range(0, s.shape[1], LANES)]


def _row_max(scores, extra=None):
    chunks = [ch for s in scores for ch in _lane_chunks(s)]
    if extra is not None:
        chunks.append(extra)
    folded = functools.reduce(jnp.maximum, chunks)
    return jnp.broadcast_to(jnp.max(folded, axis=-1, keepdims=True), folded.shape)


def _softmax_pv(parts, extra=None, sums_from_values=False, m=None):
    lane_chunks = _lane_chunks
    if m is None:
        m = _row_max([s for s, _ in parts], extra)
    ps = [jnp.concatenate([jnp.exp2(ch - m) for ch in lane_chunks(s)], axis=1) for s, _ in parts]
    pv = _dot(jnp.concatenate([p.astype(BF16) for p in ps], axis=1),
              jnp.concatenate([v for _, v in parts], axis=0))
    if sums_from_values:
        denom = pv[:, -LANES:]
        pv = pv[:, :-LANES]
        if extra is not None:
            denom = denom + jnp.exp2(extra - m)
        return pv / denom
    psum = functools.reduce(jnp.add, [ch for p in ps for ch in lane_chunks(p)])
    if extra is not None:
        psum = psum + jnp.where(_iota(extra.shape, 1) == 0, jnp.exp2(extra - m), 0.0)
    return pv / jnp.sum(psum, axis=-1, keepdims=True)


def _stack_heads(q, n_heads):
    head = _iota(q.shape, 1) // HEAD_DIM
    zero = jnp.zeros_like(q)
    return jnp.concatenate([jnp.where(head == h, q, zero) for h in range(n_heads)], axis=0)


def _unstack_heads(o, n_heads):
    rows = o.shape[0] // n_heads
    head = _iota((rows, o.shape[1]), 1) // HEAD_DIM
    out = jnp.where(head == 0, o[:rows], 0.0)
    for h in range(1, n_heads):
        out = jnp.where(head == h, o[h * rows:(h + 1) * rows], out)
    return out


def _group_rms(o, gain):
    ms = jnp.mean(o * o, axis=-1, keepdims=True)
    return o * lax.rsqrt(ms + EPS) * gain


SUBLANES = 8


def _wa_scores(qg, keys):
    rows = qg.shape[0]
    left = _iota((rows, LANES), 1) < HEAD_DIM
    zero = jnp.zeros((rows, LANES), qg.dtype)
    blocks = []
    for pair in range(WA_GROUP // 2):
        qp = qg[:, pair * LANES:(pair + 1) * LANES]
        blocks += [jnp.where(left, qp, zero), jnp.where(left, zero, qp)]
    qs = jnp.concatenate(blocks, axis=0)
    scores = []
    for k2, bias in keys:
        s = _dot_nt(k2, qs)
        if bias is not None:
            s = s + jnp.concatenate([bias] * WA_GROUP, axis=1)
        scores.append(s)
    return scores


def _wa_sink_rows(rows, sinks):
    return jnp.concatenate([jnp.full((SUBLANES, rows), sinks[g] * LOG2E, F32) for g in range(WA_GROUP)], axis=1)


def _wa_col_max(scores, sink_rows):
    m = sink_rows[:1]
    for s in scores:
        m = jnp.maximum(m, jnp.max(s, axis=0, keepdims=True))
    return jnp.broadcast_to(m, sink_rows.shape)


def _wa_finish(scores, values, sinks, m=None):
    rows = scores[0].shape[1] // WA_GROUP
    sink_rows = _wa_sink_rows(rows, sinks)
    if m is None:
        m = _wa_col_max(scores, sink_rows)
    p = jnp.concatenate([jnp.exp2(s - m[:1]).astype(BF16) for s in scores], axis=0)
    v = jnp.concatenate(values, axis=0)
    ot = lax.dot_general(v, p, (((0,), (0,)), ((), ())), preferred_element_type=F32)
    denom = ot[HEAD_DIM:HEAD_DIM + SUBLANES] + jnp.exp2(sink_rows - m)
    on = ot[:HEAD_DIM] / denom[:1]
    pairs = [jnp.concatenate([on[:, (2 * pr) * rows:(2 * pr + 1) * rows],
                              on[:, (2 * pr + 1) * rows:(2 * pr + 2) * rows]], axis=0).T
             for pr in range(WA_GROUP // 2)]
    return jnp.concatenate(pairs, axis=1)


def _na_steps(q_ref, k_ref, v_ref, kc_ref, vc_ref, bias_ref, g_ref, o_ref):
    n_rows = q_ref.shape[0] // GRID_W
    win = NA_WIN_ROWS * GRID_W
    lc = kc_ref.shape[0]
    gain = g_ref[...]

    def window(r):
        start = jnp.clip(r - NA_WIN_ROWS // 2, 0, n_rows - NA_WIN_ROWS)
        return pl.multiple_of(start * GRID_W, GRID_W), start - r + (NA_WIN_ROWS - 1)

    def scores(r, s_ref):
        tok0, d0 = window(r)
        q = q_ref[pl.ds(pl.multiple_of(r * GRID_W, GRID_W), GRID_W), :]
        qs = _stack_heads(q, NA_HEADS)
        bias = jnp.concatenate([bias_ref[d0 + 2 * j] for j in range(NA_WIN_ROWS // 2)], axis=1)
        s_ctx = _dot_nt(qs, kc_ref[...])
        s_loc = _dot_nt(qs, k_ref[pl.ds(tok0, win), :]) + bias
        s_ref[:, :lc] = s_ctx
        s_ref[:, lc:lc + win] = s_loc
        s_ref[:, lc + win:] = _row_max([s_ctx, s_loc])

    def finish(r, s_ref):
        tok0, _ = window(r)
        o4 = _softmax_pv([(s_ref[:, :lc], vc_ref[...]), (s_ref[:, lc:lc + win], v_ref[pl.ds(tok0, win), :])],
                         m=s_ref[:, lc + win:])
        o = _unstack_heads(o4, NA_HEADS)
        o_ref[pl.ds(pl.multiple_of(r * GRID_W, GRID_W), GRID_W), :] = _group_rms(o, gain).astype(o_ref.dtype)

    def step(r, cur_ref, nxt_ref):
        scores(jnp.minimum(r + 1, n_rows - 1), nxt_ref)
        finish(r, cur_ref)

    return scores, step


def _wa_steps(sink_ref, q_ref, k_ref, v_ref, kc_ref, vc_ref, g_ref, o_ref):
    seq = q_ref.shape[0]
    n_blocks = seq // WA_BLOCK
    lc = kc_ref.shape[0]
    gain = g_ref[...]
    key = _iota((WA_BAND, WA_BLOCK), 0)
    qry = _iota((WA_BAND, WA_BLOCK), 1)

    def window(n):
        q0 = pl.multiple_of(n * WA_BLOCK, WA_BLOCK)
        k0 = pl.multiple_of(jnp.clip(q0 - WA_BLOCK, 0, seq - WA_BAND), WA_BLOCK)
        return q0, k0

    def scores(n, kh, s_ref):
        q0, k0 = window(n)
        band = jnp.where(jnp.abs((k0 + key) - (q0 + qry)) <= WA_WINDOW, 0.0, NEG_INF)
        lanes = slice(kh * LANES, (kh + 1) * LANES)
        qg = q_ref[pl.ds(q0, WA_BLOCK), kh * WA_GROUP * HEAD_DIM:(kh + 1) * WA_GROUP * HEAD_DIM]
        s_ctx, s_loc = _wa_scores(qg, [(kc_ref[:, lanes], None), (k_ref[pl.ds(k0, WA_BAND), lanes], band)])
        s_ref[kh, :lc, :] = s_ctx
        s_ref[kh, lc:lc + WA_BAND, :] = s_loc
        s_ref[kh, lc + WA_BAND:, :] = _wa_col_max([s_ctx, s_loc], _wa_sink_rows(WA_BLOCK, sinks(kh)))

    def sinks(kh):
        return [sink_ref[kh * WA_GROUP + g] for g in range(WA_GROUP)]

    def finish(n, kh, s_ref):
        _, k0 = window(n)
        lanes = slice(kh * LANES, (kh + 1) * LANES)
        return _wa_finish([s_ref[kh, :lc, :], s_ref[kh, lc:lc + WA_BAND, :]],
                          [vc_ref[:, lanes], v_ref[pl.ds(k0, WA_BAND), lanes]], sinks(kh),
                          m=s_ref[kh, lc + WA_BAND:, :])

    def step(n, cur_ref, nxt_ref):
        nxt = jnp.minimum(n + 1, n_blocks - 1)
        for kh in range(WA_KV_HEADS):
            scores(nxt, kh, nxt_ref)
        o = jnp.concatenate([finish(n, kh, cur_ref) for kh in range(WA_KV_HEADS)], axis=1)
        q0, _ = window(n)
        o_ref[pl.ds(q0, WA_BLOCK), :] = _group_rms(o, gain).astype(o_ref.dtype)

    return scores, step


def _attn_kernel(sink_ref, nq_ref, nk_ref, nv_ref, nkc_ref, nvc_ref, bias_ref, gn_ref,
                 wq_ref, wk_ref, wv_ref, wkc_ref, wvc_ref, gw_ref, on_ref, ow_ref,
                 nsa_ref, nsb_ref, wsa_ref, wsb_ref):
    na_scores, na_step = _na_steps(nq_ref, nk_ref, nv_ref, nkc_ref, nvc_ref, bias_ref, gn_ref, on_ref)
    wa_scores, wa_step = _wa_steps(sink_ref, wq_ref, wk_ref, wv_ref, wkc_ref, wvc_ref, gw_ref, ow_ref)
    n_blocks = wq_ref.shape[0] // WA_BLOCK

    for kh in range(WA_KV_HEADS):
        wa_scores(0, kh, wsa_ref)
    na_scores(0, nsa_ref)

    def body(i, carry):
        for half, (w_cur, w_nxt) in enumerate(((wsa_ref, wsb_ref), (wsb_ref, wsa_ref))):
            n = 2 * i + half
            wa_step(n, w_cur, w_nxt)
            na_step(2 * n, nsa_ref, nsb_ref)
            na_step(2 * n + 1, nsb_ref, nsa_ref)
        return carry

    lax.fori_loop(0, n_blocks // 2, body, 0, unroll=2)


def _attention(sink, nq, nk, nv, nkc, nvc, bias, layer, gain_na, wq, wk2, wv3, wkc2, wvc3, gain_wa):
    batch, seq, _ = nq.shape
    lc = nkc.shape[1]
    assert WA_BLOCK == 2 * GRID_W and (seq // WA_BLOCK) % 2 == 0
    per_b = lambda n, w: pl.BlockSpec((None, n, w), lambda b: (b, 0, 0))
    na_scratch = pltpu.VMEM((NA_HEADS * GRID_W, lc + NA_WIN_ROWS * GRID_W + LANES), F32)
    wa_scratch = pltpu.VMEM((WA_KV_HEADS, lc + WA_BAND + SUBLANES, WA_GROUP * WA_BLOCK), F32)
    return pl.pallas_call(
        _attn_kernel,
        out_shape=[jax.ShapeDtypeStruct((batch, seq, NA_WIDTH), BF16),
                   jax.ShapeDtypeStruct((batch, seq, WA_QW), BF16)],
        grid=(batch,),
        in_specs=[pl.BlockSpec(memory_space=pltpu.SMEM),
                  per_b(seq, NA_WIDTH), per_b(seq, NA_WIDTH), per_b(seq, NA_WIDTH),
                  per_b(lc, NA_WIDTH), per_b(lc, NA_WIDTH),
                  pl.BlockSpec((None,) + bias.shape[1:], lambda b: (layer, 0, 0, 0)),
                  pl.BlockSpec((1, NA_WIDTH), lambda b: (0, 0)),
                  per_b(seq, WA_QW), per_b(seq, WA_K2W), per_b(seq, WA_V3W),
                  per_b(lc, WA_K2W), per_b(lc, WA_V3W),
                  pl.BlockSpec((1, WA_QW), lambda b: (0, 0))],
        out_specs=[per_b(seq, NA_WIDTH), per_b(seq, WA_QW)],
        scratch_shapes=[na_scratch, na_scratch, wa_scratch, wa_scratch],
        compiler_params=_params(),
        name="attention",
    )(sink, nq, nk, nv, nkc, nvc, bias, gain_na, wq, wk2, wv3, wkc2, wvc3, gain_wa)


def _ctx_attn_kernel(sink_ref, qn_ref, kn_ref, vn_ref, qw_ref, kw_ref, vw_ref, gn_ref, gw_ref, on_ref, ow_ref):
    qs = _stack_heads(qn_ref[...], NA_HEADS)
    o4 = _softmax_pv([(_dot_nt(qs, kn_ref[...]), vn_ref[...])])
    on_ref[...] = _group_rms(_unstack_heads(o4, NA_HEADS), gn_ref[...]).astype(on_ref.dtype)
    outs = []
    for kh in range(WA_KV_HEADS):
        lanes = slice(kh * LANES, (kh + 1) * LANES)
        qg = qw_ref[:, kh * WA_GROUP * HEAD_DIM:(kh + 1) * WA_GROUP * HEAD_DIM]
        sinks = [sink_ref[kh * WA_GROUP + g] for g in range(WA_GROUP)]
        outs.append(_wa_finish(_wa_scores(qg, [(kw_ref[:, lanes], None)]), [vw_ref[:, lanes]], sinks))
    ow_ref[...] = _group_rms(jnp.concatenate(outs, axis=1), gw_ref[...]).astype(ow_ref.dtype)


def _ctx_attention(sink, qn, kn, vn, qw, kw2, vw2, gain_na, gain_wa):
    batch, lc, _ = qn.shape
    per_b = lambda w: pl.BlockSpec((None, lc, w), lambda b: (b, 0, 0))
    return pl.pallas_call(
        _ctx_attn_kernel,
        out_shape=[jax.ShapeDtypeStruct((batch, lc, NA_WIDTH), BF16),
                   jax.ShapeDtypeStruct((batch, lc, WA_QW), BF16)],
        grid=(batch,),
        in_specs=[pl.BlockSpec(memory_space=pltpu.SMEM),
                  per_b(NA_WIDTH), per_b(NA_WIDTH), per_b(NA_WIDTH),
                  per_b(WA_QW), per_b(WA_K2W), per_b(WA_V3W),
                  pl.BlockSpec((1, NA_WIDTH), lambda b: (0, 0)),
                  pl.BlockSpec((1, WA_QW), lambda b: (0, 0))],
        out_specs=[per_b(NA_WIDTH), per_b(WA_QW)],
        compiler_params=_params(),
        name="ctx_attention",
    )(sink, qn, kn, vn, qw, kw2, vw2, gain_na, gain_wa)


def _out_mlp_kernel(x_ref, mod_ref, na_ref, wa_ref, u_ref, uprev_ref, unext_ref, cvb_ref, cw_ref, cb_ref, g_ref,
                    g2_ref, wo_ref, *rest, seq_len, sub):
    n_ffn = FFN_DIM // FFN_CHUNK
    w1_refs, w2_refs, (o_ref,) = rest[:n_ffn], rest[n_ffn:2 * n_ffn], rest[2 * n_ffn:]
    tm = x_ref.shape[0]
    i = pl.program_id(0)
    u = u_ref[...]
    at_start = (i * tm) % seq_len == 0
    at_end = ((i + 1) * tm) % seq_len == 0
    prev_row = jnp.where(at_start, 0.0, uprev_ref[7:8, :])
    next_row = jnp.where(at_end, 0.0, unext_ref[0:1, :])
    row = _iota(u.shape, 0)
    up = jnp.where(row == 0, prev_row, pltpu.roll(u, 1, axis=0))
    dn = jnp.where(row == tm - 1, next_row, pltpu.roll(u, tm - 1, axis=0))
    y = cb_ref[...] + cw_ref[0:1, :] * up + cw_ref[1:2, :] * u + cw_ref[2:3, :] * dn
    cv = _group_rms(cvb_ref[...] * y, g_ref[...]).astype(BF16)
    for r0 in range(0, tm, sub):
        rows = slice(r0, r0 + sub)
        mixed = jnp.concatenate([na_ref[rows, :], cv[rows, :], wa_ref[rows, :]], axis=1)
        x = x_ref[rows, :] + mod_ref[2] * _dot(mixed, wo_ref[...])
        ms = jnp.mean(x * x, axis=-1, keepdims=True)
        h = x * lax.rsqrt(ms + EPS) * g2_ref[...]
        h = (h * (1.0 + mod_ref[4]) + mod_ref[3]).astype(BF16)
        acc = None
        for w1_ref, w2_ref in zip(w1_refs, w2_refs):
            a = jnp.maximum(_dot(h, w1_ref[...]), 0.0)
            part = _dot((a * a).astype(BF16), w2_ref[...])
            acc = part if acc is None else acc + part
        o_ref[rows, :] = x + mod_ref[5] * acc


def _out_mlp(x2d, mod, layer, mod_row, na_n, wa_n, cv_u, cv_b, conv_w, conv_b, gain_cv, w_o, g2, w1, w2, seq_len, tm):
    rows = x2d.shape[0]
    halo = 8
    per_tile = tm // halo
    last = rows // halo - 1
    row_spec = lambda w: pl.BlockSpec((tm, w), lambda i: (i, 0))
    const = lambda shape: pl.BlockSpec(shape, lambda i: (0,) * len(shape), pipeline_mode=pl.Buffered(1))
    once = pl.Buffered(1)
    n_ffn = FFN_DIM // FFN_CHUNK
    wo_spec = pl.BlockSpec((None, D_MODEL, D_MODEL), lambda i: (layer, 0, 0), pipeline_mode=once)
    w1_specs = [pl.BlockSpec((None, D_MODEL, FFN_CHUNK), functools.partial(lambda j, i: (layer, 0, j), j),
                             pipeline_mode=once) for j in range(n_ffn)]
    w2_specs = [pl.BlockSpec((None, FFN_CHUNK, D_MODEL), functools.partial(lambda j, i: (layer, j, 0), j),
                             pipeline_mode=once) for j in range(n_ffn)]
    return pl.pallas_call(
        functools.partial(_out_mlp_kernel, seq_len=seq_len, sub=min(tm, MLP_SUB)),
        out_shape=jax.ShapeDtypeStruct((rows, D_MODEL), F32),
        grid=(rows // tm,),
        in_specs=[
            row_spec(D_MODEL),
            pl.BlockSpec((None, None, 6, 1, D_MODEL), lambda i: (layer, mod_row(i), 0, 0, 0)),
            row_spec(NA_WIDTH), row_spec(WA_QW), row_spec(CONV_CH),
            pl.BlockSpec((halo, CONV_CH), lambda i: (jnp.maximum(i * per_tile - 1, 0), 0)),
            pl.BlockSpec((halo, CONV_CH), lambda i: (jnp.minimum((i + 1) * per_tile, last), 0)),
            row_spec(CONV_CH),
            const((3, CONV_CH)), const((1, CONV_CH)), const((1, CONV_CH)),
            const((1, D_MODEL)),
            wo_spec, *w1_specs, *w2_specs,
        ],
        out_specs=row_spec(D_MODEL),
        compiler_params=_params(),
        name="out_mlp",
    )(x2d, mod, na_n, wa_n, cv_u, cv_u, cv_u, cv_b, conv_w, conv_b, gain_cv, g2,
      w_o, *([w1] * n_ffn), *([w2] * n_ffn))


def _rope_tables(seq):
    quarter = HEAD_DIM // 4
    inv = ROPE_BASE ** (-jnp.arange(quarter, dtype=F32) / quarter)
    t = jnp.arange(seq)
    ang_r = (t // GRID_W).astype(F32)[:, None] * inv[None, :]
    ang_c = (t % GRID_W).astype(F32)[:, None] * inv[None, :]
    cos = jnp.concatenate([jnp.cos(ang_r), jnp.cos(ang_r), jnp.cos(ang_c), jnp.cos(ang_c)], axis=1)
    sin = jnp.concatenate([-jnp.sin(ang_r), jnp.sin(ang_r), -jnp.sin(ang_c), jnp.sin(ang_c)], axis=1)
    reps = LANES // HEAD_DIM
    return jnp.tile(cos, (1, reps)), jnp.tile(sin, (1, reps))


def kernel(x, c, ctx, c_ctx, w_mod, b_mod, g_norm1, g_norm2, w_in, na_q_gain, na_k_gain, na_rpb,
           conv_w, conv_bias, wa_q_gain, wa_k_gain, wa_sink, g_out, w_o, w_fc1, w_fc2):
    batch, seq, d = x.shape
    lc = ctx.shape[1]
    depth = w_mod.shape[0]
    assert d == D_MODEL and seq % WA_BLOCK == 0 and seq % GRID_W == 0 and batch < MOD_ROWS
    tm = 1024
    tm_in = 1024
    tm_ctx = 256
    assert seq % tm == 0 and seq % tm_in == 0 and lc % tm_ctx == 0

    cond = jnp.zeros((MOD_ROWS, d), F32).at[:batch].set(c).at[batch].set(c_ctx)
    mod = _modulation(cond, w_mod, b_mod).reshape(depth, MOD_ROWS, 6, 1, d)
    rpb_bias = _rpb_tables(na_rpb)
    rope_tabs = _rope_tables(seq)

    lat_row = lambda i: i // (seq // tm)
    lat_row_in = lambda i: i // (seq // tm_in)
    ctx_row = lambda i: batch

    w_in_bf = w_in.astype(BF16)
    w_o_bf = w_o.astype(BF16)
    w1_bf = w_fc1.astype(BF16)
    w2_bf = w_fc2.astype(BF16)
    xs = x.reshape(batch * seq, d)
    cs = ctx.reshape(batch * lc, d)
    for l in range(depth):
        last = l == depth - 1
        g1 = g_norm1[l].reshape(1, d)
        g2 = g_norm2[l].reshape(1, d)
        gains = [jnp.tile(na_q_gain[l], NA_HEADS).reshape(1, -1), jnp.tile(na_k_gain[l], NA_HEADS).reshape(1, -1),
                 jnp.tile(wa_q_gain[l], WA_HEADS).reshape(1, -1), jnp.tile(wa_k_gain[l], WA_KV_HEADS).reshape(1, -1)]
        go_na = g_out[l, :NA_WIDTH].reshape(1, -1)
        go_cv = g_out[l, NA_WIDTH:NA_WIDTH + CONV_CH].reshape(1, -1)
        go_wa = g_out[l, NA_WIDTH + CONV_CH:].reshape(1, -1)
        cb = conv_bias[l].reshape(1, -1)

        naq, nak, nav, cvu, cvb, waq, wak, wav = _in_proj(xs, mod, l, lat_row_in, g1, w_in_bf, gains, rope_tabs, tm_in)
        if last:
            cnak, cnav, cwak, cwav = _in_proj(cs, mod, l, ctx_row, g1, w_in_bf, gains, None, tm_ctx, kv_only=True)
        else:
            cnaq, cnak, cnav, ccvu, ccvb, cwaq, cwak, cwav = _in_proj(cs, mod, l, ctx_row, g1, w_in_bf, gains, None, tm_ctx)

        b3 = lambda a, n: a.reshape(batch, n, a.shape[-1])
        na_n, wa_n = _attention(wa_sink[l], b3(naq, seq), b3(nak, seq), b3(nav, seq), b3(cnak, lc), b3(cnav, lc),
                                rpb_bias, l, go_na, b3(waq, seq), b3(wak, seq), b3(wav, seq), b3(cwak, lc),
                                b3(cwav, lc), go_wa)
        xs = _out_mlp(xs, mod, l, lat_row, na_n.reshape(batch * seq, -1), wa_n.reshape(batch * seq, -1),
                      cvu, cvb, conv_w[l], cb, go_cv, w_o_bf, g2, w1_bf, w2_bf, seq, tm)
        if not last:
            cna_n, cwa_n = _ctx_attention(wa_sink[l], b3(cnaq, lc), b3(cnak, lc), b3(cnav, lc),
                                          b3(cwaq, lc), b3(cwak, lc), b3(cwav, lc), go_na, go_wa)
            cs = _out_mlp(cs, mod, l, ctx_row, cna_n.reshape(batch * lc, -1), cwa_n.reshape(batch * lc, -1),
                          ccvu, ccvb, conv_w[l], cb, go_cv, w_o_bf, g2, w1_bf, w2_bf, lc, tm_ctx)
    return xs.reshape(batch, seq, d)
```

```python
import functools

import jax
import jax.numpy as jnp
from jax import lax
from jax.experimental import pallas as pl
from jax.experimental.pallas import tpu as pltpu

D_MODEL = 1024
GRID_W = 64
HEAD_DIM = 64
NA_HEADS = 4
NA_WIDTH = NA_HEADS * HEAD_DIM
CONV_CH = 256
WA_HEADS = 8
WA_KV_HEADS = 2
WA_GROUP = WA_HEADS // WA_KV_HEADS
WA_QW = WA_HEADS * HEAD_DIM
WA_KVW = WA_KV_HEADS * HEAD_DIM
WA_K2W = 2 * WA_KVW
WA_V3W = 2 * WA_KVW
NA_WIN_ROWS = 8
NA_WIN_COLS = 16
NA_DROWS = 2 * NA_WIN_ROWS - 1
NA_DCOLS = 2 * NA_WIN_COLS - 1
WA_WINDOW = 128
WA_BLOCK = 128
WA_BAND = 3 * WA_BLOCK
FFN_DIM = 4 * D_MODEL
FFN_CHUNK = 1024
ROPE_BASE = 10000.0
EPS = 1e-6
NEG_INF = -1e30
IN_WIDTH = 2304
OFF_NA_Q, OFF_NA_K, OFF_NA_V = 0, 256, 512
OFF_CV_X, OFF_CV_B, OFF_CV_C = 768, 1024, 1280
OFF_WA_Q, OFF_WA_K, OFF_WA_V = 1536, 2048, 2176
MOD_ROWS = 16
MOD_TN = 2048
IN_SUB = 256
MLP_SUB = 512
LANES = 128
MXU_DIM = 256
VMEM_LIMIT = 60 * 1024 * 1024
LOG2E = 1.4426950408889634
Q_SCALE = LOG2E * HEAD_DIM ** -0.5

F32 = jnp.float32
BF16 = jnp.bfloat16


def _dot(a, b):
    return jnp.dot(a, b, preferred_element_type=F32)


def _dot_nt(a, b):
    return lax.dot_general(a, b, (((1,), (1,)), ((), ())), preferred_element_type=F32)


def _split_bf16(a):
    hi = a.astype(BF16)
    lo = (a - hi.astype(F32)).astype(BF16)
    return hi, lo


def _iota(shape, dim):
    return lax.broadcasted_iota(jnp.int32, shape, dim)


def _params(**kw):
    return pltpu.CompilerParams(vmem_limit_bytes=VMEM_LIMIT, **kw)


def _mod_kernel(cond_ref, w_ref, b_ref, o_ref):
    a = cond_ref[...]
    a = a * (1.0 / (1.0 + jnp.exp(-a)))
    ah, al = _split_bf16(a)
    wh, wl = _split_bf16(w_ref[...])
    o_ref[...] = _dot(ah, wh) + _dot(al, wh) + _dot(ah, wl) + b_ref[...]


def _modulation(cond, w_mod, b_mod):
    depth = w_mod.shape[0]
    n_out = w_mod.shape[2]
    return pl.pallas_call(
        _mod_kernel,
        out_shape=jax.ShapeDtypeStruct((depth, MOD_ROWS, n_out), F32),
        grid=(depth, n_out // MOD_TN),
        in_specs=[
            pl.BlockSpec((MOD_ROWS, D_MODEL), lambda l, j: (0, 0)),
            pl.BlockSpec((None, D_MODEL, MOD_TN), lambda l, j: (l, 0, j)),
            pl.BlockSpec((None, 1, MOD_TN), lambda l, j: (l, 0, j)),
        ],
        out_specs=pl.BlockSpec((None, MOD_ROWS, MOD_TN), lambda l, j: (l, 0, j)),
        compiler_params=_params(),
        name="modulation",
    )(cond, w_mod, b_mod.reshape(depth, 1, n_out))


def _rpb_kernel(rpb_ref, o_ref):
    l = pl.program_id(0)
    shape = (GRID_W, LANES)
    q = _iota(shape, 0)
    lane = _iota(shape, 1)
    k = lane % GRID_W
    left = lane < GRID_W
    dc = jnp.clip(k - q, -(NA_WIN_COLS - 1), NA_WIN_COLS - 1) + (NA_WIN_COLS - 1)
    col_start = jnp.clip(q - NA_WIN_COLS // 2, 0, GRID_W - NA_WIN_COLS)
    col_ok = (k >= col_start) & (k < col_start + NA_WIN_COLS)
    base_l = l * (NA_HEADS * NA_DROWS * NA_DCOLS)

    def body(d, carry):
        for h in range(NA_HEADS):
            base = base_l + (h * NA_DROWS + d) * NA_DCOLS
            t = jnp.zeros(shape, F32)
            for i in range(NA_DCOLS):
                val = jnp.where(left, rpb_ref[base + i], rpb_ref[base + NA_DCOLS + i])
                t = jnp.where(dc == i, val, t)
            o_ref[d, h * GRID_W:(h + 1) * GRID_W, :] = jnp.where(col_ok, t * LOG2E, NEG_INF)
        return carry

    lax.fori_loop(0, NA_DROWS - 1, body, 0)


def _rpb_tables(na_rpb):
    depth = na_rpb.shape[0]
    return pl.pallas_call(
        _rpb_kernel,
        out_shape=jax.ShapeDtypeStruct((depth, NA_DROWS - 1, NA_HEADS * GRID_W, LANES), F32),
        grid=(depth,),
        in_specs=[pl.BlockSpec(memory_space=pltpu.SMEM)],
        out_specs=pl.BlockSpec((None, NA_DROWS - 1, NA_HEADS * GRID_W, LANES), lambda l: (l, 0, 0, 0)),
        compiler_params=_params(),
        name="rpb_tables",
    )(na_rpb.reshape(-1))


def _head_rms(t, gain):
    width = t.shape[1]
    sq = (t * t).astype(BF16)
    blk = min(width, MXU_DIM)
    ones = jnp.where(_iota((blk, blk), 0) // HEAD_DIM == _iota((blk, blk), 1) // HEAD_DIM, 1.0, 0.0).astype(BF16)
    sums = [_dot(sq[:, c:c + blk], ones) for c in range(0, width, blk)]
    ss = sums[0] if len(sums) == 1 else jnp.concatenate(sums, axis=1)
    return t * lax.rsqrt(ss * (1.0 / HEAD_DIM) + EPS) * gain


def _rope(t, cos, sin):
    lane = _iota((t.shape[0], LANES), 1)
    first = (lane % (HEAD_DIM // 2)) < (HEAD_DIM // 4)
    out = []
    for c in range(0, t.shape[1], LANES):
        u = t[:, c:c + LANES]
        partner = jnp.where(first, pltpu.roll(u, LANES - HEAD_DIM // 4, axis=1), pltpu.roll(u, HEAD_DIM // 4, axis=1))
        out.append(u * cos + partner * sin)
    return out[0] if len(out) == 1 else jnp.concatenate(out, axis=1)


def _dup_kv(t):
    lane = _iota(t.shape, 1)
    swapped = pltpu.roll(t, HEAD_DIM, axis=1)
    left = lane < HEAD_DIM
    return jnp.concatenate([jnp.where(left, t, swapped), jnp.where(left, swapped, t)], axis=1)


def _inproj_kernel(*refs, rope, kv_only, n_w, sub):
    x_ref, mod_ref, g1_ref = refs[:3]
    w_refs = refs[3:3 + n_w]
    gnq_ref, gnk_ref, gwq_ref, gwk_ref = refs[3 + n_w:7 + n_w]
    rest = refs[7 + n_w:]
    if rope:
        cos_ref, sin_ref = rest[:2]
        rest = rest[2:]
    out_refs = rest
    if kv_only:
        nak_ref, nav_ref, wak_ref, wav_ref = out_refs
        o_nk, o_nv, o_wk, o_wv, o_end = 0, NA_WIDTH, 2 * NA_WIDTH, 2 * NA_WIDTH + WA_KVW, 2 * NA_WIDTH + 2 * WA_KVW
    else:
        naq_ref, nak_ref, nav_ref, cvu_ref, cvb_ref, waq_ref, wak_ref, wav_ref = out_refs
        o_nk, o_nv, o_wk, o_wv, o_end = OFF_NA_K, OFF_NA_V, OFF_WA_K, OFF_WA_V, IN_WIDTH
    for r0 in range(0, x_ref.shape[0], sub):
        rows = slice(r0, r0 + sub)
        x = x_ref[rows, :]
        ms = jnp.mean(x * x, axis=-1, keepdims=True)
        h = x * lax.rsqrt(ms + EPS) * g1_ref[...]
        h = h * (1.0 + mod_ref[1]) + mod_ref[0]
        hb = h.astype(BF16)
        p = jnp.concatenate([_dot(hb, w_ref[...]) for w_ref in w_refs], axis=1)
        nak_ref[rows, :] = _head_rms(p[:, o_nk:o_nk + NA_WIDTH], gnk_ref[...]).astype(BF16)
        nav_ref[rows, :] = p[:, o_nv:o_nv + NA_WIDTH].astype(BF16)
        wk = _head_rms(p[:, o_wk:o_wv], gwk_ref[...])
        if rope:
            wk = _rope(wk, cos_ref[rows, :], sin_ref[rows, :])
        wak_ref[rows, :] = _dup_kv(wk).astype(BF16)
        vv = p[:, o_wv:o_end]
        left = _iota(vv.shape, 1) < HEAD_DIM
        wav_ref[rows, :] = jnp.concatenate([jnp.where(left, vv, 1.0), jnp.where(left, pltpu.roll(vv, HEAD_DIM, axis=1), 1.0)],
                                           axis=1).astype(BF16)
        if kv_only:
            continue
        naq_ref[rows, :] = (_head_rms(p[:, OFF_NA_Q:OFF_NA_K], gnq_ref[...]) * Q_SCALE).astype(BF16)
        cvu_ref[rows, :] = p[:, OFF_CV_C:OFF_WA_Q] * p[:, OFF_CV_X:OFF_CV_B]
        cvb_ref[rows, :] = p[:, OFF_CV_B:OFF_CV_C]
        wq = _head_rms(p[:, OFF_WA_Q:OFF_WA_K], gwq_ref[...])
        if rope:
            wq = _rope(wq, cos_ref[rows, :], sin_ref[rows, :])
        waq_ref[rows, :] = (wq * Q_SCALE).astype(BF16)


def _in_proj(x2d, mod, layer, mod_row, g1, w_in, gains, rope_tabs, tm, kv_only=False):
    rows = x2d.shape[0]
    rope = rope_tabs is not None
    row_spec = lambda w: pl.BlockSpec((tm, w), lambda i: (i, 0))
    const = lambda shape: pl.BlockSpec(shape, lambda i: (0,) * len(shape))
    once = pl.Buffered(1)
    if kv_only:
        col_blocks = [OFF_NA_K // MXU_DIM, OFF_NA_V // MXU_DIM, OFF_WA_K // MXU_DIM]
        w_specs = [pl.BlockSpec((None, D_MODEL, MXU_DIM), functools.partial(lambda j, i: (layer, 0, j), j),
                                pipeline_mode=once) for j in col_blocks]
        widths = [(NA_WIDTH, BF16), (NA_WIDTH, BF16), (WA_K2W, BF16), (WA_V3W, BF16)]
    else:
        w_specs = [pl.BlockSpec((None, D_MODEL, IN_WIDTH), lambda i: (layer, 0, 0), pipeline_mode=once)]
        widths = [(NA_WIDTH, BF16), (NA_WIDTH, BF16), (NA_WIDTH, BF16), (CONV_CH, F32), (CONV_CH, F32),
                  (WA_QW, BF16), (WA_K2W, BF16), (WA_V3W, BF16)]
    n_w = len(w_specs)
    in_specs = [
        row_spec(D_MODEL),
        pl.BlockSpec((None, None, 6, 1, D_MODEL), lambda i: (layer, mod_row(i), 0, 0, 0)),
        const((1, D_MODEL)),
        *w_specs,
        const((1, NA_WIDTH)), const((1, NA_WIDTH)), const((1, WA_QW)), const((1, WA_KVW)),
    ]
    args = [x2d, mod, g1, *([w_in] * n_w), *gains]
    if rope:
        seq_tiles = rope_tabs[0].shape[0] // tm
        in_specs += [pl.BlockSpec((tm, LANES), lambda i: (i % seq_tiles, 0))] * 2
        args += list(rope_tabs)
    return pl.pallas_call(
        functools.partial(_inproj_kernel, rope=rope, kv_only=kv_only, n_w=n_w, sub=min(tm, IN_SUB)),
        out_shape=[jax.ShapeDtypeStruct((rows, w), dt) for w, dt in widths],
        grid=(rows // tm,),
        in_specs=in_specs,
        out_specs=[row_spec(w) for w, _ in widths],
        compiler_params=_params(),
        name="in_proj_rope" if rope else ("in_proj_ctx_kv" if kv_only else "in_proj_ctx"),
    )(*args)


def _lane_chunks(s):
    return [s[:, c:c + LANES] for c in range(0, s.shape[1], LANES)]


def _row_max(scores, extra=None):
    chunks = [ch for s in scores for ch in _lane_chunks(s)]
    if extra is not None:
        chunks.append(extra)
    folded = functools.reduce(jnp.maximum, chunks)
    return jnp.broadcast_to(jnp.max(folded, axis=-1, keepdims=True), folded.shape)


def _softmax_pv(parts, extra=None, sums_from_values=False, m=None):
    lane_chunks = _lane_chunks
    if m is None:
        m = _row_max([s for s, _ in parts], extra)
    ps = [jnp.concatenate([jnp.exp2(ch - m) for ch in lane_chunks(s)], axis=1) for s, _ in parts]
    pv = _dot(jnp.concatenate([p.astype(BF16) for p in ps], axis=1),
              jnp.concatenate([v for _, v in parts], axis=0))
    if sums_from_values:
        denom = pv[:, -LANES:]
        pv = pv[:, :-LANES]
        if extra is not None:
            denom = denom + jnp.exp2(extra - m)
        return pv / denom
    psum = functools.reduce(jnp.add, [ch for p in ps for ch in lane_chunks(p)])
    if extra is not None:
        psum = psum + jnp.where(_iota(extra.shape, 1) == 0, jnp.exp2(extra - m), 0.0)
    return pv / jnp.sum(psum, axis=-1, keepdims=True)


def _stack_heads(q, n_heads):
    head = _iota(q.shape, 1) // HEAD_DIM
    zero = jnp.zeros_like(q)
    return jnp.concatenate([jnp.where(head == h, q, zero) for h in range(n_heads)], axis=0)


def _unstack_heads(o, n_heads):
    rows = o.shape[0] // n_heads
    head = _iota((rows, o.shape[1]), 1) // HEAD_DIM
    out = jnp.where(head == 0, o[:rows], 0.0)
    for h in range(1, n_heads):
        out = jnp.where(head == h, o[h * rows:(h + 1) * rows], out)
    return out


def _group_rms(o, gain):
    ms = jnp.mean(o * o, axis=-1, keepdims=True)
    return o * lax.rsqrt(ms + EPS) * gain


SUBLANES = 8


def _wa_scores(qg, keys):
    rows = qg.shape[0]
    left = _iota((rows, LANES), 1) < HEAD_DIM
    zero = jnp.zeros((rows, LANES), qg.dtype)
    blocks = []
    for pair in range(WA_GROUP // 2):
        qp = qg[:, pair * LANES:(pair + 1) * LANES]
        blocks += [jnp.where(left, qp, zero), jnp.where(left, zero, qp)]
    qs = jnp.concatenate(blocks, axis=0)
    scores = []
    for k2, bias in keys:
        s = _dot_nt(k2, qs)
        if bias is not None:
            s = s + jnp.concatenate([bias] * WA_GROUP, axis=1)
        scores.append(s)
    return scores


def _wa_sink_rows(rows, sinks):
    return jnp.concatenate([jnp.full((SUBLANES, rows), sinks[g] * LOG2E, F32) for g in range(WA_GROUP)], axis=1)


def _wa_col_max(scores, sink_rows):
    m = sink_rows[:1]
    for s in scores:
        m = jnp.maximum(m, jnp.max(s, axis=0, keepdims=True))
    return jnp.broadcast_to(m, sink_rows.shape)


def _wa_finish(scores, values, sinks, m=None):
    rows = scores[0].shape[1] // WA_GROUP
    sink_rows = _wa_sink_rows(rows, sinks)
    if m is None:
        m = _wa_col_max(scores, sink_rows)
    p = jnp.concatenate([jnp.exp2(s - m[:1]).astype(BF16) for s in scores], axis=0)
    v = jnp.concatenate(values, axis=0)
    ot = lax.dot_general(v, p, (((0,), (0,)), ((), ())), preferred_element_type=F32)
    denom = ot[HEAD_DIM:HEAD_DIM + SUBLANES] + jnp.exp2(sink_rows - m)
    on = ot[:HEAD_DIM] / denom[:1]
    pairs = [jnp.concatenate([on[:, (2 * pr) * rows:(2 * pr + 1) * rows],
                              on[:, (2 * pr + 1) * rows:(2 * pr + 2) * rows]], axis=0).T
             for pr in range(WA_GROUP // 2)]
    return jnp.concatenate(pairs, axis=1)


def _na_steps(q_ref, k_ref, v_ref, kc_ref, vc_ref, bias_ref, g_ref, o_ref):
    n_rows = q_ref.shape[0] // GRID_W
    win = NA_WIN_ROWS * GRID_W
    lc = kc_ref.shape[0]
    gain = g_ref[...]

    def window(r):
        start = jnp.clip(r - NA_WIN_ROWS // 2, 0, n_rows - NA_WIN_ROWS)
        return pl.multiple_of(start * GRID_W, GRID_W), start - r + (NA_WIN_ROWS - 1)

    def scores(r, s_ref):
        tok0, d0 = window(r)
        q = q_ref[pl.ds(pl.multiple_of(r * GRID_W, GRID_W), GRID_W), :]
        qs = _stack_heads(q, NA_HEADS)
        bias = jnp.concatenate([bias_ref[d0 + 2 * j] for j in range(NA_WIN_ROWS // 2)], axis=1)
        s_ctx = _dot_nt(qs, kc_ref[...])
        s_loc = _dot_nt(qs, k_ref[pl.ds(tok0, win), :]) + bias
        s_ref[:, :lc] = s_ctx
        s_ref[:, lc:lc + win] = s_loc
        s_ref[:, lc + win:] = _row_max([s_ctx, s_loc])

    def finish(r, s_ref):
        tok0, _ = window(r)
        o4 = _softmax_pv([(s_ref[:, :lc], vc_ref[...]), (s_ref[:, lc:lc + win], v_ref[pl.ds(tok0, win), :])],
                         m=s_ref[:, lc + win:])
        o = _unstack_heads(o4, NA_HEADS)
        o_ref[pl.ds(pl.multiple_of(r * GRID_W, GRID_W), GRID_W), :] = _group_rms(o, gain).astype(o_ref.dtype)

    def step(r, cur_ref, nxt_ref):
        scores(jnp.minimum(r + 1, n_rows - 1), nxt_ref)
        finish(r, cur_ref)

    return scores, step


def _wa_steps(sink_ref, q_ref, k_ref, v_ref, kc_ref, vc_ref, g_ref, o_ref):
    seq = q_ref.shape[0]
    n_blocks = seq // WA_BLOCK
    lc = kc_ref.shape[0]
    gain = g_ref[...]
    key = _iota((WA_BAND, WA_BLOCK), 0)
    qry = _iota((WA_BAND, WA_BLOCK), 1)

    def window(n):
        q0 = pl.multiple_of(n * WA_BLOCK, WA_BLOCK)
        k0 = pl.multiple_of(jnp.clip(q0 - WA_BLOCK, 0, seq - WA_BAND), WA_BLOCK)
        return q0, k0

    def scores(n, kh, s_ref):
        q0, k0 = window(n)
        band = jnp.where(jnp.abs((k0 + key) - (q0 + qry)) <= WA_WINDOW, 0.0, NEG_INF)
        lanes = slice(kh * LANES, (kh + 1) * LANES)
        qg = q_ref[pl.ds(q0, WA_BLOCK), kh * WA_GROUP * HEAD_DIM:(kh + 1) * WA_GROUP * HEAD_DIM]
        s_ctx, s_loc = _wa_scores(qg, [(kc_ref[:, lanes], None), (k_ref[pl.ds(k0, WA_BAND), lanes], band)])
        s_ref[kh, :lc, :] = s_ctx
        s_ref[kh, lc:lc + WA_BAND, :] = s_loc
        s_ref[kh, lc + WA_BAND:, :] = _wa_col_max([s_ctx, s_loc], _wa_sink_rows(WA_BLOCK, sinks(kh)))

    def sinks(kh):
        return [sink_ref[kh * WA_GROUP + g] for g in range(WA_GROUP)]

    def finish(n, kh, s_ref):
        _, k0 = window(n)
        lanes = slice(kh * LANES, (kh + 1) * LANES)
        return _wa_finish([s_ref[kh, :lc, :], s_ref[kh, lc:lc + WA_BAND, :]],
                          [vc_ref[:, lanes], v_ref[pl.ds(k0, WA_BAND), lanes]], sinks(kh),
                          m=s_ref[kh, lc + WA_BAND:, :])

    def step(n, cur_ref, nxt_ref):
        nxt = jnp.minimum(n + 1, n_blocks - 1)
        for kh in range(WA_KV_HEADS):
            scores(nxt, kh, nxt_ref)
        o = jnp.concatenate([finish(n, kh, cur_ref) for kh in range(WA_KV_HEADS)], axis=1)
        q0, _ = window(n)
        o_ref[pl.ds(q0, WA_BLOCK), :] = _group_rms(o, gain).astype(o_ref.dtype)

    return scores, step


def _attn_kernel(sink_ref, nq_ref, nk_ref, nv_ref, nkc_ref, nvc_ref, bias_ref, gn_ref,
                 wq_ref, wk_ref, wv_ref, wkc_ref, wvc_ref, gw_ref, *rest, n_cast):
    f32_refs = rest[:n_cast]
    on_ref, ow_ref = rest[n_cast:n_cast + 2]
    bf16_refs = rest[n_cast + 2:2 * n_cast + 2]
    nsa_ref, nsb_ref, wsa_ref, wsb_ref = rest[2 * n_cast + 2:]
    for src_ref, dst_ref in zip(f32_refs, bf16_refs):
        dst_ref[...] = src_ref[...].astype(dst_ref.dtype)
    na_scores, na_step = _na_steps(nq_ref, nk_ref, nv_ref, nkc_ref, nvc_ref, bias_ref, gn_ref, on_ref)
    wa_scores, wa_step = _wa_steps(sink_ref, wq_ref, wk_ref, wv_ref, wkc_ref, wvc_ref, gw_ref, ow_ref)
    n_blocks = wq_ref.shape[0] // WA_BLOCK

    for kh in range(WA_KV_HEADS):
        wa_scores(0, kh, wsa_ref)
    na_scores(0, nsa_ref)

    def body(i, carry):
        for half, (w_cur, w_nxt) in enumerate(((wsa_ref, wsb_ref), (wsb_ref, wsa_ref))):
            n = 2 * i + half
            wa_step(n, w_cur, w_nxt)
            na_step(2 * n, nsa_ref, nsb_ref)
            na_step(2 * n + 1, nsb_ref, nsa_ref)
        return carry

    lax.fori_loop(0, n_blocks // 2, body, 0, unroll=2)


def _attention(sink, nq, nk, nv, nkc, nvc, bias, layer, gain_na, wq, wk2, wv3, wkc2, wvc3, gain_wa, to_bf16):
    batch, seq, _ = nq.shape
    slab_in = lambda w: pl.BlockSpec((None, w.shape[1] // batch, w.shape[2]), lambda b: (layer, b, 0))
    slab_out = lambda w: pl.BlockSpec((w.shape[1] // batch, w.shape[2]), lambda b: (b, 0))
    lc = nkc.shape[1]
    assert WA_BLOCK == 2 * GRID_W and (seq // WA_BLOCK) % 2 == 0
    per_b = lambda n, w: pl.BlockSpec((None, n, w), lambda b: (b, 0, 0))
    na_scratch = pltpu.VMEM((NA_HEADS * GRID_W, lc + NA_WIN_ROWS * GRID_W + LANES), F32)
    wa_scratch = pltpu.VMEM((WA_KV_HEADS, lc + WA_BAND + SUBLANES, WA_GROUP * WA_BLOCK), F32)
    outs = pl.pallas_call(
        functools.partial(_attn_kernel, n_cast=len(to_bf16)),
        out_shape=[jax.ShapeDtypeStruct((batch, seq, NA_WIDTH), BF16),
                   jax.ShapeDtypeStruct((batch, seq, WA_QW), BF16),
                   *[jax.ShapeDtypeStruct(w.shape[1:], BF16) for w in to_bf16]],
        grid=(batch,),
        in_specs=[pl.BlockSpec(memory_space=pltpu.SMEM),
                  per_b(seq, NA_WIDTH), per_b(seq, NA_WIDTH), per_b(seq, NA_WIDTH),
                  per_b(lc, NA_WIDTH), per_b(lc, NA_WIDTH),
                  pl.BlockSpec((None,) + bias.shape[1:], lambda b: (layer, 0, 0, 0)),
                  pl.BlockSpec((1, NA_WIDTH), lambda b: (0, 0)),
                  per_b(seq, WA_QW), per_b(seq, WA_K2W), per_b(seq, WA_V3W),
                  per_b(lc, WA_K2W), per_b(lc, WA_V3W),
                  pl.BlockSpec((1, WA_QW), lambda b: (0, 0)),
                  *[slab_in(w) for w in to_bf16]],
        out_specs=[per_b(seq, NA_WIDTH), per_b(seq, WA_QW), *[slab_out(w) for w in to_bf16]],
        scratch_shapes=[na_scratch, na_scratch, wa_scratch, wa_scratch],
        compiler_params=_params(),
        name="attention",
    )(sink, nq, nk, nv, nkc, nvc, bias, gain_na, wq, wk2, wv3, wkc2, wvc3, gain_wa, *to_bf16)
    return outs[0], outs[1], outs[2:]


def _ctx_attn_kernel(sink_ref, qn_ref, kn_ref, vn_ref, qw_ref, kw_ref, vw_ref, gn_ref, gw_ref, on_ref, ow_ref):
    qs = _stack_heads(qn_ref[...], NA_HEADS)
    o4 = _softmax_pv([(_dot_nt(qs, kn_ref[...]), vn_ref[...])])
    on_ref[...] = _group_rms(_unstack_heads(o4, NA_HEADS), gn_ref[...]).astype(on_ref.dtype)
    outs = []
    for kh in range(WA_KV_HEADS):
        lanes = slice(kh * LANES, (kh + 1) * LANES)
        qg = qw_ref[:, kh * WA_GROUP * HEAD_DIM:(kh + 1) * WA_GROUP * HEAD_DIM]
        sinks = [sink_ref[kh * WA_GROUP + g] for g in range(WA_GROUP)]
        outs.append(_wa_finish(_wa_scores(qg, [(kw_ref[:, lanes], None)]), [vw_ref[:, lanes]], sinks))
    ow_ref[...] = _group_rms(jnp.concatenate(outs, axis=1), gw_ref[...]).astype(ow_ref.dtype)


def _ctx_attention(sink, qn, kn, vn, qw, kw2, vw2, gain_na, gain_wa):
    batch, lc, _ = qn.shape
    per_b = lambda w: pl.BlockSpec((None, lc, w), lambda b: (b, 0, 0))
    return pl.pallas_call(
        _ctx_attn_kernel,
        out_shape=[jax.ShapeDtypeStruct((batch, lc, NA_WIDTH), BF16),
                   jax.ShapeDtypeStruct((batch, lc, WA_QW), BF16)],
        grid=(batch,),
        in_specs=[pl.BlockSpec(memory_space=pltpu.SMEM),
                  per_b(NA_WIDTH), per_b(NA_WIDTH), per_b(NA_WIDTH),
                  per_b(WA_QW), per_b(WA_K2W), per_b(WA_V3W),
                  pl.BlockSpec((1, NA_WIDTH), lambda b: (0, 0)),
                  pl.BlockSpec((1, WA_QW), lambda b: (0, 0))],
        out_specs=[per_b(NA_WIDTH), per_b(WA_QW)],
        compiler_params=_params(),
        name="ctx_attention",
    )(sink, qn, kn, vn, qw, kw2, vw2, gain_na, gain_wa)


def _out_mlp_kernel(x_ref, mod_ref, na_ref, wa_ref, u_ref, uprev_ref, unext_ref, cvb_ref, cw_ref, cb_ref, g_ref,
                    g2_ref, wo_ref, *rest, seq_len, sub):
    n_ffn = FFN_DIM // FFN_CHUNK
    w1_refs, w2_refs, (o_ref,) = rest[:n_ffn], rest[n_ffn:2 * n_ffn], rest[2 * n_ffn:]
    tm = x_ref.shape[0]
    i = pl.program_id(0)
    u = u_ref[...]
    at_start = (i * tm) % seq_len == 0
    at_end = ((i + 1) * tm) % seq_len == 0
    prev_row = jnp.where(at_start, 0.0, uprev_ref[7:8, :])
    next_row = jnp.where(at_end, 0.0, unext_ref[0:1, :])
    row = _iota(u.shape, 0)
    up = jnp.where(row == 0, prev_row, pltpu.roll(u, 1, axis=0))
    dn = jnp.where(row == tm - 1, next_row, pltpu.roll(u, tm - 1, axis=0))
    y = cb_ref[...] + cw_ref[0:1, :] * up + cw_ref[1:2, :] * u + cw_ref[2:3, :] * dn
    cv = _group_rms(cvb_ref[...] * y, g_ref[...]).astype(BF16)
    for r0 in range(0, tm, sub):
        rows = slice(r0, r0 + sub)
        mixed = jnp.concatenate([na_ref[rows, :], cv[rows, :], wa_ref[rows, :]], axis=1)
        x = x_ref[rows, :] + mod_ref[2] * _dot(mixed, wo_ref[...])
        ms = jnp.mean(x * x, axis=-1, keepdims=True)
        h = x * lax.rsqrt(ms + EPS) * g2_ref[...]
        h = (h * (1.0 + mod_ref[4]) + mod_ref[3]).astype(BF16)
        acc = None
        for w1_ref, w2_ref in zip(w1_refs, w2_refs):
            a = jnp.maximum(_dot(h, w1_ref[...]), 0.0)
            part = _dot((a * a).astype(BF16), w2_ref[...])
            acc = part if acc is None else acc + part
        o_ref[rows, :] = x + mod_ref[5] * acc


def _out_mlp(x2d, mod, layer, mod_row, na_n, wa_n, cv_u, cv_b, conv_w, conv_b, gain_cv, w_o, g2, w1, w2, seq_len, tm):
    rows = x2d.shape[0]
    halo = 8
    per_tile = tm // halo
    last = rows // halo - 1
    row_spec = lambda w: pl.BlockSpec((tm, w), lambda i: (i, 0))
    const = lambda shape: pl.BlockSpec(shape, lambda i: (0,) * len(shape), pipeline_mode=pl.Buffered(1))
    once = pl.Buffered(1)
    n_ffn = FFN_DIM // FFN_CHUNK
    wo_spec = pl.BlockSpec((D_MODEL, D_MODEL), lambda i: (0, 0), pipeline_mode=once)
    w1_specs = [pl.BlockSpec((D_MODEL, FFN_CHUNK), functools.partial(lambda j, i: (0, j), j),
                             pipeline_mode=once) for j in range(n_ffn)]
    w2_specs = [pl.BlockSpec((FFN_CHUNK, D_MODEL), functools.partial(lambda j, i: (j, 0), j),
                             pipeline_mode=once) for j in range(n_ffn)]
    return pl.pallas_call(
        functools.partial(_out_mlp_kernel, seq_len=seq_len, sub=min(tm, MLP_SUB)),
        out_shape=jax.ShapeDtypeStruct((rows, D_MODEL), F32),
        grid=(rows // tm,),
        in_specs=[
            row_spec(D_MODEL),
            pl.BlockSpec((None, None, 6, 1, D_MODEL), lambda i: (layer, mod_row(i), 0, 0, 0)),
            row_spec(NA_WIDTH), row_spec(WA_QW), row_spec(CONV_CH),
            pl.BlockSpec((halo, CONV_CH), lambda i: (jnp.maximum(i * per_tile - 1, 0), 0)),
            pl.BlockSpec((halo, CONV_CH), lambda i: (jnp.minimum((i + 1) * per_tile, last), 0)),
            row_spec(CONV_CH),
            const((3, CONV_CH)), const((1, CONV_CH)), const((1, CONV_CH)),
            const((1, D_MODEL)),
            wo_spec, *w1_specs, *w2_specs,
        ],
        out_specs=row_spec(D_MODEL),
        compiler_params=_params(),
        name="out_mlp",
    )(x2d, mod, na_n, wa_n, cv_u, cv_u, cv_u, cv_b, conv_w, conv_b, gain_cv, g2,
      w_o, *([w1] * n_ffn), *([w2] * n_ffn))


def _rope_tables(seq):
    quarter = HEAD_DIM // 4
    inv = ROPE_BASE ** (-jnp.arange(quarter, dtype=F32) / quarter)
    t = jnp.arange(seq)
    ang_r = (t // GRID_W).astype(F32)[:, None] * inv[None, :]
    ang_c = (t % GRID_W).astype(F32)[:, None] * inv[None, :]
    cos = jnp.concatenate([jnp.cos(ang_r), jnp.cos(ang_r), jnp.cos(ang_c), jnp.cos(ang_c)], axis=1)
    sin = jnp.concatenate([-jnp.sin(ang_r), jnp.sin(ang_r), -jnp.sin(ang_c), jnp.sin(ang_c)], axis=1)
    reps = LANES // HEAD_DIM
    return jnp.tile(cos, (1, reps)), jnp.tile(sin, (1, reps))


def kernel(x, c, ctx, c_ctx, w_mod, b_mod, g_norm1, g_norm2, w_in, na_q_gain, na_k_gain, na_rpb,
           conv_w, conv_bias, wa_q_gain, wa_k_gain, wa_sink, g_out, w_o, w_fc1, w_fc2):
    batch, seq, d = x.shape
    lc = ctx.shape[1]
    depth = w_mod.shape[0]
    assert d == D_MODEL and seq % WA_BLOCK == 0 and seq % GRID_W == 0 and batch < MOD_ROWS
    tm = 1024
    tm_in = 1024
    tm_ctx = 256
    assert seq % tm == 0 and seq % tm_in == 0 and lc % tm_ctx == 0

    cond = jnp.zeros((MOD_ROWS, d), F32).at[:batch].set(c).at[batch].set(c_ctx)
    mod = _modulation(cond, w_mod, b_mod).reshape(depth, MOD_ROWS, 6, 1, d)
    rpb_bias = _rpb_tables(na_rpb)
    rope_tabs = _rope_tables(seq)

    lat_row = lambda i: i // (seq // tm)
    lat_row_in = lambda i: i // (seq // tm_in)
    ctx_row = lambda i: batch

    w_in_bf = w_in.astype(BF16)
    xs = x.reshape(batch * seq, d)
    cs = ctx.reshape(batch * lc, d)
    for l in range(depth):
        last = l == depth - 1
        g1 = g_norm1[l].reshape(1, d)
        g2 = g_norm2[l].reshape(1, d)
        gains = [jnp.tile(na_q_gain[l], NA_HEADS).reshape(1, -1), jnp.tile(na_k_gain[l], NA_HEADS).reshape(1, -1),
                 jnp.tile(wa_q_gain[l], WA_HEADS).reshape(1, -1), jnp.tile(wa_k_gain[l], WA_KV_HEADS).reshape(1, -1)]
        go_na = g_out[l, :NA_WIDTH].reshape(1, -1)
        go_cv = g_out[l, NA_WIDTH:NA_WIDTH + CONV_CH].reshape(1, -1)
        go_wa = g_out[l, NA_WIDTH + CONV_CH:].reshape(1, -1)
        cb = conv_bias[l].reshape(1, -1)

        naq, nak, nav, cvu, cvb, waq, wak, wav = _in_proj(xs, mod, l, lat_row_in, g1, w_in_bf, gains, rope_tabs, tm_in)
        if last:
            cnak, cnav, cwak, cwav = _in_proj(cs, mod, l, ctx_row, g1, w_in_bf, gains, None, tm_ctx, kv_only=True)
        else:
            cnaq, cnak, cnav, ccvu, ccvb, cwaq, cwak, cwav = _in_proj(cs, mod, l, ctx_row, g1, w_in_bf, gains, None, tm_ctx)

        b3 = lambda a, n: a.reshape(batch, n, a.shape[-1])
        na_n, wa_n, (w_o_bf, w1_bf, w2_bf) = _attention(
            wa_sink[l], b3(naq, seq), b3(nak, seq), b3(nav, seq), b3(cnak, lc), b3(cnav, lc), rpb_bias, l, go_na,
            b3(waq, seq), b3(wak, seq), b3(wav, seq), b3(cwak, lc), b3(cwav, lc), go_wa, [w_o, w_fc1, w_fc2])
        xs = _out_mlp(xs, mod, l, lat_row, na_n.reshape(batch * seq, -1), wa_n.reshape(batch * seq, -1),
                      cvu, cvb, conv_w[l], cb, go_cv, w_o_bf, g2, w1_bf, w2_bf, seq, tm)
        if not last:
            cna_n, cwa_n = _ctx_attention(wa_sink[l], b3(cnaq, lc), b3(cnak, lc), b3(cnav, lc),
                                          b3(cwaq, lc), b3(cwak, lc), b3(cwav, lc), go_na, go_wa)
            cs = _out_mlp(cs, mod, l, ctx_row, cna_n.reshape(batch * lc, -1), cwa_n.reshape(batch * lc, -1),
                          ccvu, ccvb, conv_w[l], cb, go_cv, w_o_bf, g2, w1_bf, w2_bf, lc, tm_ctx)
    return xs.reshape(batch, seq, d)
```

```python
import functools

import jax
import jax.numpy as jnp
from jax import lax
from jax.experimental import pallas as pl
from jax.experimental.pallas import tpu as pltpu

D_MODEL = 1024
GRID_W = 64
HEAD_DIM = 64
NA_HEADS = 4
NA_WIDTH = NA_HEADS * HEAD_DIM
CONV_CH = 256
WA_HEADS = 8
WA_KV_HEADS = 2
WA_GROUP = WA_HEADS // WA_KV_HEADS
WA_QW = WA_HEADS * HEAD_DIM
WA_KVW = WA_KV_HEADS * HEAD_DIM
WA_K2W = 2 * WA_KVW
WA_V3W = 2 * WA_KVW
NA_WIN_ROWS = 8
NA_WIN_COLS = 16
NA_DROWS = 2 * NA_WIN_ROWS - 1
NA_DCOLS = 2 * NA_WIN_COLS - 1
WA_WINDOW = 128
WA_BLOCK = 128
WA_BAND = 3 * WA_BLOCK
FFN_DIM = 4 * D_MODEL
FFN_CHUNK = 1024
ROPE_BASE = 10000.0
EPS = 1e-6
NEG_INF = -1e30
IN_WIDTH = 2304
OFF_NA_Q, OFF_NA_K, OFF_NA_V = 0, 256, 512
OFF_CV_X, OFF_CV_B, OFF_CV_C = 768, 1024, 1280
OFF_WA_Q, OFF_WA_K, OFF_WA_V = 1536, 2048, 2176
MOD_ROWS = 16
MOD_TN = 2048
IN_SUB = 256
MLP_SUB = 512
LANES = 128
MXU_DIM = 256
VMEM_LIMIT = 60 * 1024 * 1024
LOG2E = 1.4426950408889634
Q_SCALE = LOG2E * HEAD_DIM ** -0.5

F32 = jnp.float32
BF16 = jnp.bfloat16


def _dot(a, b):
    return jnp.dot(a, b, preferred_element_type=F32)


def _dot_nt(a, b):
    return lax.dot_general(a, b, (((1,), (1,)), ((), ())), preferred_element_type=F32)


def _split_bf16(a):
    hi = a.astype(BF16)
    lo = (a - hi.astype(F32)).astype(BF16)
    return hi, lo


def _iota(shape, dim):
    return lax.broadcasted_iota(jnp.int32, shape, dim)


def _params(**kw):
    return pltpu.CompilerParams(vmem_limit_bytes=VMEM_LIMIT, **kw)


def _mod_kernel(cond_ref, w_ref, b_ref, o_ref):
    a = cond_ref[...]
    a = a * (1.0 / (1.0 + jnp.exp(-a)))
    ah, al = _split_bf16(a)
    wh, wl = _split_bf16(w_ref[...])
    res = _dot(ah, wh) + _dot(al, wh) + _dot(ah, wl) + b_ref[...]
    for c in range(o_ref.shape[1]):
        o_ref[:, c, 0, :] = res[:, c * D_MODEL:(c + 1) * D_MODEL]


def _modulation(cond, w_mod, b_mod):
    depth = w_mod.shape[0]
    n_out = w_mod.shape[2]
    per_tile = MOD_TN // D_MODEL
    return pl.pallas_call(
        _mod_kernel,
        out_shape=jax.ShapeDtypeStruct((depth, MOD_ROWS, n_out // D_MODEL, 1, D_MODEL), F32),
        grid=(depth, n_out // MOD_TN),
        in_specs=[
            pl.BlockSpec((MOD_ROWS, D_MODEL), lambda l, j: (0, 0)),
            pl.BlockSpec((None, D_MODEL, MOD_TN), lambda l, j: (l, 0, j)),
            pl.BlockSpec((None, 1, MOD_TN), lambda l, j: (l, 0, j)),
        ],
        out_specs=pl.BlockSpec((None, MOD_ROWS, per_tile, 1, D_MODEL), lambda l, j: (l, 0, j, 0, 0)),
        compiler_params=_params(),
        name="modulation",
    )(cond, w_mod, b_mod.reshape(depth, 1, n_out))


def _rpb_kernel(rpb_ref, o_ref):
    shape = (GRID_W, LANES)
    q = _iota(shape, 0)
    lane = _iota(shape, 1)
    k = lane % GRID_W
    left = lane < GRID_W
    col_start = jnp.clip(q - NA_WIN_COLS // 2, 0, GRID_W - NA_WIN_COLS)
    col_ok = (k >= col_start) & (k < col_start + NA_WIN_COLS)
    centre = NA_WIN_COLS - 1
    for d in range(NA_DROWS - 1):
        for h in range(NA_HEADS):
            t1 = pltpu.roll(jnp.broadcast_to(rpb_ref[h, d:d + 1, :], shape), LANES - centre, axis=1,
                            stride=1, stride_axis=0)
            t2 = pltpu.roll(jnp.broadcast_to(rpb_ref[h, d + 1:d + 2, :], shape), GRID_W - centre, axis=1,
                            stride=1, stride_axis=0)
            t = jnp.where(left, t1, t2)
            o_ref[d, h * GRID_W:(h + 1) * GRID_W, :] = jnp.where(col_ok, t * LOG2E, NEG_INF)


def _rpb_tables(na_rpb):
    depth = na_rpb.shape[0]
    padded = jnp.pad(na_rpb, ((0, 0), (0, 0), (0, 1), (0, LANES - NA_DCOLS)))
    return pl.pallas_call(
        _rpb_kernel,
        out_shape=jax.ShapeDtypeStruct((depth, NA_DROWS - 1, NA_HEADS * GRID_W, LANES), F32),
        grid=(depth,),
        in_specs=[pl.BlockSpec((None,) + padded.shape[1:], lambda l: (l, 0, 0, 0))],
        out_specs=pl.BlockSpec((None, NA_DROWS - 1, NA_HEADS * GRID_W, LANES), lambda l: (l, 0, 0, 0)),
        compiler_params=_params(),
        name="rpb_tables",
    )(padded)


def _head_rms(t, gain):
    width = t.shape[1]
    sq = (t * t).astype(BF16)
    blk = min(width, MXU_DIM)
    ones = jnp.where(_iota((blk, blk), 0) // HEAD_DIM == _iota((blk, blk), 1) // HEAD_DIM, 1.0, 0.0).astype(BF16)
    sums = [_dot(sq[:, c:c + blk], ones) for c in range(0, width, blk)]
    ss = sums[0] if len(sums) == 1 else jnp.concatenate(sums, axis=1)
    return t * lax.rsqrt(ss * (1.0 / HEAD_DIM) + EPS) * gain


def _rope(t, cos, sin):
    lane = _iota((t.shape[0], LANES), 1)
    first = (lane % (HEAD_DIM // 2)) < (HEAD_DIM // 4)
    out = []
    for c in range(0, t.shape[1], LANES):
        u = t[:, c:c + LANES]
        partner = jnp.where(first, pltpu.roll(u, LANES - HEAD_DIM // 4, axis=1), pltpu.roll(u, HEAD_DIM // 4, axis=1))
        out.append(u * cos + partner * sin)
    return out[0] if len(out) == 1 else jnp.concatenate(out, axis=1)


def _dup_kv(t):
    lane = _iota(t.shape, 1)
    swapped = pltpu.roll(t, HEAD_DIM, axis=1)
    left = lane < HEAD_DIM
    return jnp.concatenate([jnp.where(left, t, swapped), jnp.where(left, swapped, t)], axis=1)


def _inproj_kernel(*refs, rope, kv_only, n_w, sub):
    x_ref, mod_ref, g1_ref = refs[:3]
    w_refs = refs[3:3 + n_w]
    gnq_ref, gnk_ref, gwq_ref, gwk_ref = refs[3 + n_w:7 + n_w]
    rest = refs[7 + n_w:]
    if rope:
        cos_ref, sin_ref = rest[:2]
        rest = rest[2:]
    out_refs = rest
    if kv_only:
        nak_ref, nav_ref, wak_ref, wav_ref = out_refs
        o_nk, o_nv, o_wk, o_wv, o_end = 0, NA_WIDTH, 2 * NA_WIDTH, 2 * NA_WIDTH + WA_KVW, 2 * NA_WIDTH + 2 * WA_KVW
    else:
        naq_ref, nak_ref, nav_ref, cvu_ref, cvb_ref, waq_ref, wak_ref, wav_ref = out_refs
        o_nk, o_nv, o_wk, o_wv, o_end = OFF_NA_K, OFF_NA_V, OFF_WA_K, OFF_WA_V, IN_WIDTH
    def project(rows):
        x = x_ref[rows, :]
        ms = jnp.mean(x * x, axis=-1, keepdims=True)
        h = x * lax.rsqrt(ms + EPS) * g1_ref[...]
        h = h * (1.0 + mod_ref[1]) + mod_ref[0]
        hb = h.astype(BF16)
        return jnp.concatenate([_dot(hb, w_ref[...]) for w_ref in w_refs], axis=1)

    def finish(rows, p):
        nak_ref[rows, :] = _head_rms(p[:, o_nk:o_nk + NA_WIDTH], gnk_ref[...]).astype(BF16)
        nav_ref[rows, :] = p[:, o_nv:o_nv + NA_WIDTH].astype(BF16)
        wk = _head_rms(p[:, o_wk:o_wv], gwk_ref[...])
        if rope:
            wk = _rope(wk, cos_ref[rows, :], sin_ref[rows, :])
        wak_ref[rows, :] = _dup_kv(wk).astype(BF16)
        vv = p[:, o_wv:o_end]
        left = _iota(vv.shape, 1) < HEAD_DIM
        wav_ref[rows, :] = jnp.concatenate([jnp.where(left, vv, 1.0), jnp.where(left, pltpu.roll(vv, HEAD_DIM, axis=1), 1.0)],
                                           axis=1).astype(BF16)
        if kv_only:
            return
        naq_ref[rows, :] = (_head_rms(p[:, OFF_NA_Q:OFF_NA_K], gnq_ref[...]) * Q_SCALE).astype(BF16)
        cvu_ref[rows, :] = p[:, OFF_CV_C:OFF_WA_Q] * p[:, OFF_CV_X:OFF_CV_B]
        cvb_ref[rows, :] = p[:, OFF_CV_B:OFF_CV_C]
        wq = _head_rms(p[:, OFF_WA_Q:OFF_WA_K], gwq_ref[...])
        if rope:
            wq = _rope(wq, cos_ref[rows, :], sin_ref[rows, :])
        waq_ref[rows, :] = (wq * Q_SCALE).astype(BF16)

    for r0 in range(0, x_ref.shape[0], sub):
        rows = slice(r0, r0 + sub)
        finish(rows, project(rows))


def _in_proj(x2d, mod, layer, mod_row, g1, w_in, gains, rope_tabs, tm, kv_only=False):
    rows = x2d.shape[0]
    rope = rope_tabs is not None
    row_spec = lambda w: pl.BlockSpec((tm, w), lambda i: (i, 0))
    const = lambda shape: pl.BlockSpec(shape, lambda i: (0,) * len(shape))
    once = pl.Buffered(1)
    if kv_only:
        col_blocks = [OFF_NA_K // MXU_DIM, OFF_NA_V // MXU_DIM, OFF_WA_K // MXU_DIM]
        w_specs = [pl.BlockSpec((None, D_MODEL, MXU_DIM), functools.partial(lambda j, i: (layer, 0, j), j),
                                pipeline_mode=once) for j in col_blocks]
        widths = [(NA_WIDTH, BF16), (NA_WIDTH, BF16), (WA_K2W, BF16), (WA_V3W, BF16)]
    else:
        w_specs = [pl.BlockSpec((None, D_MODEL, IN_WIDTH), lambda i: (layer, 0, 0), pipeline_mode=once)]
        widths = [(NA_WIDTH, BF16), (NA_WIDTH, BF16), (NA_WIDTH, BF16), (CONV_CH, F32), (CONV_CH, F32),
                  (WA_QW, BF16), (WA_K2W, BF16), (WA_V3W, BF16)]
    n_w = len(w_specs)
    in_specs = [
        row_spec(D_MODEL),
        pl.BlockSpec((None, None, 6, 1, D_MODEL), lambda i: (layer, mod_row(i), 0, 0, 0)),
        const((1, D_MODEL)),
        *w_specs,
        const((1, NA_WIDTH)), const((1, NA_WIDTH)), const((1, WA_QW)), const((1, WA_KVW)),
    ]
    args = [x2d, mod, g1, *([w_in] * n_w), *gains]
    if rope:
        seq_tiles = rope_tabs[0].shape[0] // tm
        in_specs += [pl.BlockSpec((tm, LANES), lambda i: (i % seq_tiles, 0))] * 2
        args += list(rope_tabs)
    return pl.pallas_call(
        functools.partial(_inproj_kernel, rope=rope, kv_only=kv_only, n_w=n_w, sub=min(tm, IN_SUB)),
        out_shape=[jax.ShapeDtypeStruct((rows, w), dt) for w, dt in widths],
        grid=(rows // tm,),
        in_specs=in_specs,
        out_specs=[row_spec(w) for w, _ in widths],
        compiler_params=_params(),
        name="in_proj_rope" if rope else ("in_proj_ctx_kv" if kv_only else "in_proj_ctx"),
    )(*args)


def _lane_chunks(s):
    return [s[:, c:c + LANES] for c in range(0, s.shape[1], LANES)]


def _row_max(scores, extra=None):
    chunks = [ch for s in scores for ch in _lane_chunks(s)]
    if extra is not None:
        chunks.append(extra)
    folded = functools.reduce(jnp.maximum, chunks)
    return jnp.broadcast_to(jnp.max(folded, axis=-1, keepdims=True), folded.shape)


def _softmax_pv(parts, extra=None, sums_from_values=False, m=None):
    lane_chunks = _lane_chunks
    if m is None:
        m = _row_max([s for s, _ in parts], extra)
    ps = [jnp.concatenate([jnp.exp2(ch - m) for ch in lane_chunks(s)], axis=1) for s, _ in parts]
    pv = _dot(jnp.concatenate([p.astype(BF16) for p in ps], axis=1),
              jnp.concatenate([v for _, v in parts], axis=0))
    if sums_from_values:
        denom = pv[:, -LANES:]
        pv = pv[:, :-LANES]
        if extra is not None:
            denom = denom + jnp.exp2(extra - m)
        return pv / denom
    psum = functools.reduce(jnp.add, [ch for p in ps for ch in lane_chunks(p)])
    if extra is not None:
        psum = psum + jnp.where(_iota(extra.shape, 1) == 0, jnp.exp2(extra - m), 0.0)
    return pv / jnp.sum(psum, axis=-1, keepdims=True)


def _stack_heads(q, n_heads):
    head = _iota(q.shape, 1) // HEAD_DIM
    zero = jnp.zeros_like(q)
    return jnp.concatenate([jnp.where(head == h, q, zero) for h in range(n_heads)], axis=0)


def _unstack_heads(o, n_heads):
    rows = o.shape[0] // n_heads
    head = _iota((rows, o.shape[1]), 1) // HEAD_DIM
    out = jnp.where(head == 0, o[:rows], 0.0)
    for h in range(1, n_heads):
        out = jnp.where(head == h, o[h * rows:(h + 1) * rows], out)
    return out


def _group_rms(o, gain):
    ms = jnp.mean(o * o, axis=-1, keepdims=True)
    return o * lax.rsqrt(ms + EPS) * gain


SUBLANES = 8


def _wa_scores(qg, keys):
    rows = qg.shape[0]
    left = _iota((rows, LANES), 1) < HEAD_DIM
    zero = jnp.zeros((rows, LANES), qg.dtype)
    blocks = []
    for pair in range(WA_GROUP // 2):
        qp = qg[:, pair * LANES:(pair + 1) * LANES]
        blocks += [jnp.where(left, qp, zero), jnp.where(left, zero, qp)]
    qs = jnp.concatenate(blocks, axis=0)
    scores = []
    for k2, bias in keys:
        s = _dot_nt(k2, qs)
        if bias is not None:
            s = s + jnp.concatenate([bias] * WA_GROUP, axis=1)
        scores.append(s)
    return scores


def _wa_sink_rows(rows, sinks):
    return jnp.concatenate([jnp.full((SUBLANES, rows), sinks[g] * LOG2E, F32) for g in range(WA_GROUP)], axis=1)


def _wa_col_max(scores, sink_rows):
    m = sink_rows[:1]
    for s in scores:
        m = jnp.maximum(m, jnp.max(s, axis=0, keepdims=True))
    return jnp.broadcast_to(m, sink_rows.shape)


def _wa_finish(scores, values, sinks, m=None):
    rows = scores[0].shape[1] // WA_GROUP
    sink_rows = _wa_sink_rows(rows, sinks)
    if m is None:
        m = _wa_col_max(scores, sink_rows)
    p = jnp.concatenate([jnp.exp2(s - m[:1]).astype(BF16) for s in scores], axis=0)
    v = jnp.concatenate(values, axis=0)
    ot = lax.dot_general(v, p, (((0,), (0,)), ((), ())), preferred_element_type=F32)
    denom = ot[HEAD_DIM:HEAD_DIM + SUBLANES] + jnp.exp2(sink_rows - m)
    on = ot[:HEAD_DIM] / denom[:1]
    pairs = [jnp.concatenate([on[:, (2 * pr) * rows:(2 * pr + 1) * rows],
                              on[:, (2 * pr + 1) * rows:(2 * pr + 2) * rows]], axis=0).T
             for pr in range(WA_GROUP // 2)]
    return jnp.concatenate(pairs, axis=1)


def _na_steps(q_ref, k_ref, v_ref, kc_ref, vc_ref, bias_ref, g_ref, o_ref):
    n_rows = q_ref.shape[0] // GRID_W
    win = NA_WIN_ROWS * GRID_W
    lc = kc_ref.shape[0]
    gain = g_ref[...]

    def window(r):
        start = jnp.clip(r - NA_WIN_ROWS // 2, 0, n_rows - NA_WIN_ROWS)
        return pl.multiple_of(start * GRID_W, GRID_W), start - r + (NA_WIN_ROWS - 1)

    def scores(r, s_ref):
        tok0, d0 = window(r)
        q = q_ref[pl.ds(pl.multiple_of(r * GRID_W, GRID_W), GRID_W), :]
        qs = _stack_heads(q, NA_HEADS)
        bias = jnp.concatenate([bias_ref[d0 + 2 * j] for j in range(NA_WIN_ROWS // 2)], axis=1)
        s_ctx = _dot_nt(qs, kc_ref[...])
        s_loc = _dot_nt(qs, k_ref[pl.ds(tok0, win), :]) + bias
        s_ref[:, :lc] = s_ctx
        s_ref[:, lc:lc + win] = s_loc
        s_ref[:, lc + win:] = _row_max([s_ctx, s_loc])

    def finish(r, s_ref):
        tok0, _ = window(r)
        o4 = _softmax_pv([(s_ref[:, :lc], vc_ref[...]), (s_ref[:, lc:lc + win], v_ref[pl.ds(tok0, win), :])],
                         m=s_ref[:, lc + win:])
        o = _unstack_heads(o4, NA_HEADS)
        o_ref[pl.ds(pl.multiple_of(r * GRID_W, GRID_W), GRID_W), :] = _group_rms(o, gain).astype(o_ref.dtype)

    def step(r, cur_ref, nxt_ref):
        scores(jnp.minimum(r + 1, n_rows - 1), nxt_ref)
        finish(r, cur_ref)

    return scores, step


def _wa_steps(sink_ref, q_ref, k_ref, v_ref, kc_ref, vc_ref, g_ref, o_ref):
    seq = q_ref.shape[0]
    n_blocks = seq // WA_BLOCK
    lc = kc_ref.shape[0]
    gain = g_ref[...]
    key = _iota((WA_BAND, WA_BLOCK), 0)
    qry = _iota((WA_BAND, WA_BLOCK), 1)

    def window(n):
        q0 = pl.multiple_of(n * WA_BLOCK, WA_BLOCK)
        k0 = pl.multiple_of(jnp.clip(q0 - WA_BLOCK, 0, seq - WA_BAND), WA_BLOCK)
        return q0, k0

    def scores(n, kh, s_ref):
        q0, k0 = window(n)
        band = jnp.where(jnp.abs((k0 + key) - (q0 + qry)) <= WA_WINDOW, 0.0, NEG_INF)
        lanes = slice(kh * LANES, (kh + 1) * LANES)
        qg = q_ref[pl.ds(q0, WA_BLOCK), kh * WA_GROUP * HEAD_DIM:(kh + 1) * WA_GROUP * HEAD_DIM]
        s_ctx, s_loc = _wa_scores(qg, [(kc_ref[:, lanes], None), (k_ref[pl.ds(k0, WA_BAND), lanes], band)])
        s_ref[kh, :lc, :] = s_ctx
        s_ref[kh, lc:lc + WA_BAND, :] = s_loc
        s_ref[kh, lc + WA_BAND:, :] = _wa_col_max([s_ctx, s_loc], _wa_sink_rows(WA_BLOCK, sinks(kh)))

    def sinks(kh):
        return [sink_ref[kh * WA_GROUP + g] for g in range(WA_GROUP)]

    def finish(n, kh, s_ref):
        _, k0 = window(n)
        lanes = slice(kh * LANES, (kh + 1) * LANES)
        return _wa_finish([s_ref[kh, :lc, :], s_ref[kh, lc:lc + WA_BAND, :]],
                          [vc_ref[:, lanes], v_ref[pl.ds(k0, WA_BAND), lanes]], sinks(kh),
                          m=s_ref[kh, lc + WA_BAND:, :])

    def step(n, cur_ref, nxt_ref):
        nxt = jnp.minimum(n + 1, n_blocks - 1)
        for kh in range(WA_KV_HEADS):
            scores(nxt, kh, nxt_ref)
        o = jnp.concatenate([finish(n, kh, cur_ref) for kh in range(WA_KV_HEADS)], axis=1)
        q0, _ = window(n)
        o_ref[pl.ds(q0, WA_BLOCK), :] = _group_rms(o, gain).astype(o_ref.dtype)

    return scores, step


def _attn_kernel(sink_ref, nq_ref, nk_ref, nv_ref, nkc_ref, nvc_ref, bias_ref, gn_ref,
                 wq_ref, wk_ref, wv_ref, wkc_ref, wvc_ref, gw_ref, *rest, n_cast):
    f32_refs = rest[:n_cast]
    on_ref, ow_ref = rest[n_cast:n_cast + 2]
    bf16_refs = rest[n_cast + 2:2 * n_cast + 2]
    nsa_ref, nsb_ref, wsa_ref, wsb_ref = rest[2 * n_cast + 2:]
    for src_ref, dst_ref in zip(f32_refs, bf16_refs):
        dst_ref[...] = src_ref[...].astype(dst_ref.dtype)
    na_scores, na_step = _na_steps(nq_ref, nk_ref, nv_ref, nkc_ref, nvc_ref, bias_ref, gn_ref, on_ref)
    wa_scores, wa_step = _wa_steps(sink_ref, wq_ref, wk_ref, wv_ref, wkc_ref, wvc_ref, gw_ref, ow_ref)
    n_blocks = wq_ref.shape[0] // WA_BLOCK

    for kh in range(WA_KV_HEADS):
        wa_scores(0, kh, wsa_ref)
    na_scores(0, nsa_ref)

    def body(i, carry):
        for half, (w_cur, w_nxt) in enumerate(((wsa_ref, wsb_ref), (wsb_ref, wsa_ref))):
            n = 2 * i + half
            wa_step(n, w_cur, w_nxt)
            na_step(2 * n, nsa_ref, nsb_ref)
            na_step(2 * n + 1, nsb_ref, nsa_ref)
        return carry

    lax.fori_loop(0, n_blocks // 2, body, 0, unroll=2)


def _attention(sink, nq, nk, nv, nkc, nvc, bias, layer, gain_na, wq, wk2, wv3, wkc2, wvc3, gain_wa, to_bf16):
    batch, seq, _ = nq.shape
    slab_in = lambda w: pl.BlockSpec((None, w.shape[1] // batch, w.shape[2]), lambda b: (layer, b, 0))
    slab_out = lambda w: pl.BlockSpec((w.shape[1] // batch, w.shape[2]), lambda b: (b, 0))
    lc = nkc.shape[1]
    assert WA_BLOCK == 2 * GRID_W and (seq // WA_BLOCK) % 2 == 0
    per_b = lambda n, w: pl.BlockSpec((None, n, w), lambda b: (b, 0, 0))
    na_scratch = pltpu.VMEM((NA_HEADS * GRID_W, lc + NA_WIN_ROWS * GRID_W + LANES), F32)
    wa_scratch = pltpu.VMEM((WA_KV_HEADS, lc + WA_BAND + SUBLANES, WA_GROUP * WA_BLOCK), F32)
    outs = pl.pallas_call(
        functools.partial(_attn_kernel, n_cast=len(to_bf16)),
        out_shape=[jax.ShapeDtypeStruct((batch, seq, NA_WIDTH), BF16),
                   jax.ShapeDtypeStruct((batch, seq, WA_QW), BF16),
                   *[jax.ShapeDtypeStruct(w.shape[1:], BF16) for w in to_bf16]],
        grid=(batch,),
        in_specs=[pl.BlockSpec(memory_space=pltpu.SMEM),
                  per_b(seq, NA_WIDTH), per_b(seq, NA_WIDTH), per_b(seq, NA_WIDTH),
                  per_b(lc, NA_WIDTH), per_b(lc, NA_WIDTH),
                  pl.BlockSpec((None,) + bias.shape[1:], lambda b: (layer, 0, 0, 0)),
                  pl.BlockSpec((1, NA_WIDTH), lambda b: (0, 0)),
                  per_b(seq, WA_QW), per_b(seq, WA_K2W), per_b(seq, WA_V3W),
                  per_b(lc, WA_K2W), per_b(lc, WA_V3W),
                  pl.BlockSpec((1, WA_QW), lambda b: (0, 0)),
                  *[slab_in(w) for w in to_bf16]],
        out_specs=[per_b(seq, NA_WIDTH), per_b(seq, WA_QW), *[slab_out(w) for w in to_bf16]],
        scratch_shapes=[na_scratch, na_scratch, wa_scratch, wa_scratch],
        compiler_params=_params(),
        name="attention",
    )(sink, nq, nk, nv, nkc, nvc, bias, gain_na, wq, wk2, wv3, wkc2, wvc3, gain_wa, *to_bf16)
    return outs[0], outs[1], outs[2:]


def _ctx_attn_kernel(sink_ref, qn_ref, kn_ref, vn_ref, qw_ref, kw_ref, vw_ref, gn_ref, gw_ref, on_ref, ow_ref):
    qs = _stack_heads(qn_ref[...], NA_HEADS)
    o4 = _softmax_pv([(_dot_nt(qs, kn_ref[...]), vn_ref[...])])
    on_ref[...] = _group_rms(_unstack_heads(o4, NA_HEADS), gn_ref[...]).astype(on_ref.dtype)
    outs = []
    for kh in range(WA_KV_HEADS):
        lanes = slice(kh * LANES, (kh + 1) * LANES)
        qg = qw_ref[:, kh * WA_GROUP * HEAD_DIM:(kh + 1) * WA_GROUP * HEAD_DIM]
        sinks = [sink_ref[kh * WA_GROUP + g] for g in range(WA_GROUP)]
        outs.append(_wa_finish(_wa_scores(qg, [(kw_ref[:, lanes], None)]), [vw_ref[:, lanes]], sinks))
    ow_ref[...] = _group_rms(jnp.concatenate(outs, axis=1), gw_ref[...]).astype(ow_ref.dtype)


def _ctx_attention(sink, qn, kn, vn, qw, kw2, vw2, gain_na, gain_wa):
    batch, lc, _ = qn.shape
    per_b = lambda w: pl.BlockSpec((None, lc, w), lambda b: (b, 0, 0))
    return pl.pallas_call(
        _ctx_attn_kernel,
        out_shape=[jax.ShapeDtypeStruct((batch, lc, NA_WIDTH), BF16),
                   jax.ShapeDtypeStruct((batch, lc, WA_QW), BF16)],
        grid=(batch,),
        in_specs=[pl.BlockSpec(memory_space=pltpu.SMEM),
                  per_b(NA_WIDTH), per_b(NA_WIDTH), per_b(NA_WIDTH),
                  per_b(WA_QW), per_b(WA_K2W), per_b(WA_V3W),
                  pl.BlockSpec((1, NA_WIDTH), lambda b: (0, 0)),
                  pl.BlockSpec((1, WA_QW), lambda b: (0, 0))],
        out_specs=[per_b(NA_WIDTH), per_b(WA_QW)],
        compiler_params=_params(),
        name="ctx_attention",
    )(sink, qn, kn, vn, qw, kw2, vw2, gain_na, gain_wa)


def _out_mlp_kernel(x_ref, mod_ref, na_ref, wa_ref, u_ref, uprev_ref, unext_ref, cvb_ref, cw_ref, cb_ref, g_ref,
                    g2_ref, wo_ref, *rest, seq_len, sub):
    n_ffn = FFN_DIM // FFN_CHUNK
    w1_refs, w2_refs, (o_ref,) = rest[:n_ffn], rest[n_ffn:2 * n_ffn], rest[2 * n_ffn:]
    tm = x_ref.shape[0]
    i = pl.program_id(0)
    u = u_ref[...]
    at_start = (i * tm) % seq_len == 0
    at_end = ((i + 1) * tm) % seq_len == 0
    prev_row = jnp.where(at_start, 0.0, uprev_ref[7:8, :])
    next_row = jnp.where(at_end, 0.0, unext_ref[0:1, :])
    row = _iota(u.shape, 0)
    up = jnp.where(row == 0, prev_row, pltpu.roll(u, 1, axis=0))
    dn = jnp.where(row == tm - 1, next_row, pltpu.roll(u, tm - 1, axis=0))
    y = cb_ref[...] + cw_ref[0:1, :] * up + cw_ref[1:2, :] * u + cw_ref[2:3, :] * dn
    cv = _group_rms(cvb_ref[...] * y, g_ref[...]).astype(BF16)
    for r0 in range(0, tm, sub):
        rows = slice(r0, r0 + sub)
        mixed = jnp.concatenate([na_ref[rows, :], cv[rows, :], wa_ref[rows, :]], axis=1)
        x = x_ref[rows, :] + mod_ref[2] * _dot(mixed, wo_ref[...])
        ms = jnp.mean(x * x, axis=-1, keepdims=True)
        h = x * lax.rsqrt(ms + EPS) * g2_ref[...]
        h = (h * (1.0 + mod_ref[4]) + mod_ref[3]).astype(BF16)
        acc = None
        for w1_ref, w2_ref in zip(w1_refs, w2_refs):
            a = jnp.maximum(_dot(h, w1_ref[...]), 0.0)
            part = _dot((a * a).astype(BF16), w2_ref[...])
            acc = part if acc is None else acc + part
        o_ref[rows, :] = x + mod_ref[5] * acc


def _out_mlp(x2d, mod, layer, mod_row, na_n, wa_n, cv_u, cv_b, conv_w, conv_b, gain_cv, w_o, g2, w1, w2, seq_len, tm):
    rows = x2d.shape[0]
    halo = 8
    per_tile = tm // halo
    last = rows // halo - 1
    row_spec = lambda w: pl.BlockSpec((tm, w), lambda i: (i, 0))
    const = lambda shape: pl.BlockSpec(shape, lambda i: (0,) * len(shape), pipeline_mode=pl.Buffered(1))
    once = pl.Buffered(1)
    n_ffn = FFN_DIM // FFN_CHUNK
    wo_spec = pl.BlockSpec((D_MODEL, D_MODEL), lambda i: (0, 0), pipeline_mode=once)
    w1_specs = [pl.BlockSpec((D_MODEL, FFN_CHUNK), functools.partial(lambda j, i: (0, j), j),
                             pipeline_mode=once) for j in range(n_ffn)]
    w2_specs = [pl.BlockSpec((FFN_CHUNK, D_MODEL), functools.partial(lambda j, i: (j, 0), j),
                             pipeline_mode=once) for j in range(n_ffn)]
    return pl.pallas_call(
        functools.partial(_out_mlp_kernel, seq_len=seq_len, sub=min(tm, MLP_SUB)),
        out_shape=jax.ShapeDtypeStruct((rows, D_MODEL), F32),
        grid=(rows // tm,),
        in_specs=[
            row_spec(D_MODEL),
            pl.BlockSpec((None, None, 6, 1, D_MODEL), lambda i: (layer, mod_row(i), 0, 0, 0)),
            row_spec(NA_WIDTH), row_spec(WA_QW), row_spec(CONV_CH),
            pl.BlockSpec((halo, CONV_CH), lambda i: (jnp.maximum(i * per_tile - 1, 0), 0)),
            pl.BlockSpec((halo, CONV_CH), lambda i: (jnp.minimum((i + 1) * per_tile, last), 0)),
            row_spec(CONV_CH),
            const((3, CONV_CH)), const((1, CONV_CH)), const((1, CONV_CH)),
            const((1, D_MODEL)),
            wo_spec, *w1_specs, *w2_specs,
        ],
        out_specs=row_spec(D_MODEL),
        compiler_params=_params(),
        name="out_mlp",
    )(x2d, mod, na_n, wa_n, cv_u, cv_u, cv_u, cv_b, conv_w, conv_b, gain_cv, g2,
      w_o, *([w1] * n_ffn), *([w2] * n_ffn))


def _rope_tables(seq):
    quarter = HEAD_DIM // 4
    inv = ROPE_BASE ** (-jnp.arange(quarter, dtype=F32) / quarter)
    t = jnp.arange(seq)
    ang_r = (t // GRID_W).astype(F32)[:, None] * inv[None, :]
    ang_c = (t % GRID_W).astype(F32)[:, None] * inv[None, :]
    cos = jnp.concatenate([jnp.cos(ang_r), jnp.cos(ang_r), jnp.cos(ang_c), jnp.cos(ang_c)], axis=1)
    sin = jnp.concatenate([-jnp.sin(ang_r), jnp.sin(ang_r), -jnp.sin(ang_c), jnp.sin(ang_c)], axis=1)
    reps = LANES // HEAD_DIM
    return jnp.tile(cos, (1, reps)), jnp.tile(sin, (1, reps))


def kernel(x, c, ctx, c_ctx, w_mod, b_mod, g_norm1, g_norm2, w_in, na_q_gain, na_k_gain, na_rpb,
           conv_w, conv_bias, wa_q_gain, wa_k_gain, wa_sink, g_out, w_o, w_fc1, w_fc2):
    batch, seq, d = x.shape
    lc = ctx.shape[1]
    depth = w_mod.shape[0]
    assert d == D_MODEL and seq % WA_BLOCK == 0 and seq % GRID_W == 0 and batch < MOD_ROWS
    tm = 1024
    tm_in = 1024
    tm_ctx = 256
    assert seq % tm == 0 and seq % tm_in == 0 and lc % tm_ctx == 0

    cond = jnp.zeros((MOD_ROWS, d), F32).at[:batch].set(c).at[batch].set(c_ctx)
    mod = _modulation(cond, w_mod, b_mod)
    rpb_bias = _rpb_tables(na_rpb)
    rope_tabs = _rope_tables(seq)

    lat_row = lambda i: i // (seq // tm)
    lat_row_in = lambda i: i // (seq // tm_in)
    ctx_row = lambda i: batch

    w_in_bf = w_in.astype(BF16)
    xs = x.reshape(batch * seq, d)
    cs = ctx.reshape(batch * lc, d)
    for l in range(depth):
        last = l == depth - 1
        g1 = g_norm1[l].reshape(1, d)
        g2 = g_norm2[l].reshape(1, d)
        gains = [jnp.tile(na_q_gain[l], NA_HEADS).reshape(1, -1), jnp.tile(na_k_gain[l], NA_HEADS).reshape(1, -1),
                 jnp.tile(wa_q_gain[l], WA_HEADS).reshape(1, -1), jnp.tile(wa_k_gain[l], WA_KV_HEADS).reshape(1, -1)]
        go_na = g_out[l, :NA_WIDTH].reshape(1, -1)
        go_cv = g_out[l, NA_WIDTH:NA_WIDTH + CONV_CH].reshape(1, -1)
        go_wa = g_out[l, NA_WIDTH + CONV_CH:].reshape(1, -1)
        cb = conv_bias[l].reshape(1, -1)

        naq, nak, nav, cvu, cvb, waq, wak, wav = _in_proj(xs, mod, l, lat_row_in, g1, w_in_bf, gains, rope_tabs, tm_in)
        if last:
            cnak, cnav, cwak, cwav = _in_proj(cs, mod, l, ctx_row, g1, w_in_bf, gains, None, tm_ctx, kv_only=True)
        else:
            cnaq, cnak, cnav, ccvu, ccvb, cwaq, cwak, cwav = _in_proj(cs, mod, l, ctx_row, g1, w_in_bf, gains, None, tm_ctx)

        b3 = lambda a, n: a.reshape(batch, n, a.shape[-1])
        na_n, wa_n, (w_o_bf, w1_bf, w2_bf) = _attention(
            wa_sink[l], b3(naq, seq), b3(nak, seq), b3(nav, seq), b3(cnak, lc), b3(cnav, lc), rpb_bias, l, go_na,
            b3(waq, seq), b3(wak, seq), b3(wav, seq), b3(cwak, lc), b3(cwav, lc), go_wa, [w_o, w_fc1, w_fc2])
        xs = _out_mlp(xs, mod, l, lat_row, na_n.reshape(batch * seq, -1), wa_n.reshape(batch * seq, -1),
                      cvu, cvb, conv_w[l], cb, go_cv, w_o_bf, g2, w1_bf, w2_bf, seq, tm)
        if not last:
            cna_n, cwa_n = _ctx_attention(wa_sink[l], b3(cnaq, lc), b3(cnak, lc), b3(cnav, lc),
                                          b3(cwaq, lc), b3(cwak, lc), b3(cwav, lc), go_na, go_wa)
            cs = _out_mlp(cs, mod, l, ctx_row, cna_n.reshape(batch * lc, -1), cwa_n.reshape(batch * lc, -1),
                          ccvu, ccvb, conv_w[l], cb, go_cv, w_o_bf, g2, w1_bf, w2_bf, lc, tm_ctx)
    return xs.reshape(batch, seq, d)
```

```python
import functools

import jax
import jax.numpy as jnp
from jax import lax
from jax.experimental import pallas as pl
from jax.experimental.pallas import tpu as pltpu

D_MODEL = 1024
GRID_W = 64
HEAD_DIM = 64
NA_HEADS = 4
NA_WIDTH = NA_HEADS * HEAD_DIM
CONV_CH = 256
WA_HEADS = 8
WA_KV_HEADS = 2
WA_GROUP = WA_HEADS // WA_KV_HEADS
WA_QW = WA_HEADS * HEAD_DIM
WA_KVW = WA_KV_HEADS * HEAD_DIM
WA_K2W = 2 * WA_KVW
WA_V3W = 2 * WA_KVW
NA_WIN_ROWS = 8
NA_WIN_COLS = 16
NA_DROWS = 2 * NA_WIN_ROWS - 1
NA_DCOLS = 2 * NA_WIN_COLS - 1
WA_WINDOW = 128
WA_BLOCK = 128
WA_BAND = 3 * WA_BLOCK
FFN_DIM = 4 * D_MODEL
FFN_CHUNK = 1024
ROPE_BASE = 10000.0
EPS = 1e-6
NEG_INF = -1e30
IN_WIDTH = 2304
OFF_NA_Q, OFF_NA_K, OFF_NA_V = 0, 256, 512
OFF_CV_X, OFF_CV_B, OFF_CV_C = 768, 1024, 1280
OFF_WA_Q, OFF_WA_K, OFF_WA_V = 1536, 2048, 2176
MOD_ROWS = 16
MOD_TN = 2048
IN_SUB = 256
MLP_SUB = 512
LANES = 128
MXU_DIM = 256
VMEM_LIMIT = 60 * 1024 * 1024
LOG2E = 1.4426950408889634
Q_SCALE = LOG2E * HEAD_DIM ** -0.5

F32 = jnp.float32
BF16 = jnp.bfloat16


def _dot(a, b):
    return jnp.dot(a, b, preferred_element_type=F32)


def _dot_nt(a, b):
    return lax.dot_general(a, b, (((1,), (1,)), ((), ())), preferred_element_type=F32)


def _split_bf16(a):
    hi = a.astype(BF16)
    lo = (a - hi.astype(F32)).astype(BF16)
    return hi, lo


def _iota(shape, dim):
    return lax.broadcasted_iota(jnp.int32, shape, dim)


def _params(**kw):
    return pltpu.CompilerParams(vmem_limit_bytes=VMEM_LIMIT, **kw)


def _mod_kernel(cond_ref, w_ref, b_ref, o_ref):
    a = cond_ref[...]
    a = a * (1.0 / (1.0 + jnp.exp(-a)))
    ah, al = _split_bf16(a)
    wh, wl = _split_bf16(w_ref[...])
    res = _dot(ah, wh) + _dot(al, wh) + _dot(ah, wl) + b_ref[...]
    for c in range(o_ref.shape[1]):
        o_ref[:, c, 0, :] = res[:, c * D_MODEL:(c + 1) * D_MODEL]


def _modulation(cond, w_mod, b_mod):
    depth = w_mod.shape[0]
    n_out = w_mod.shape[2]
    per_tile = MOD_TN // D_MODEL
    return pl.pallas_call(
        _mod_kernel,
        out_shape=jax.ShapeDtypeStruct((depth, MOD_ROWS, n_out // D_MODEL, 1, D_MODEL), F32),
        grid=(depth, n_out // MOD_TN),
        in_specs=[
            pl.BlockSpec((MOD_ROWS, D_MODEL), lambda l, j: (0, 0)),
            pl.BlockSpec((None, D_MODEL, MOD_TN), lambda l, j: (l, 0, j)),
            pl.BlockSpec((None, 1, MOD_TN), lambda l, j: (l, 0, j)),
        ],
        out_specs=pl.BlockSpec((None, MOD_ROWS, per_tile, 1, D_MODEL), lambda l, j: (l, 0, j, 0, 0)),
        compiler_params=_params(),
        name="modulation",
    )(cond, w_mod, b_mod.reshape(depth, 1, n_out))


def _rpb_kernel(rpb_ref, o_ref):
    shape = (GRID_W, LANES)
    q = _iota(shape, 0)
    lane = _iota(shape, 1)
    k = lane % GRID_W
    left = lane < GRID_W
    col_start = jnp.clip(q - NA_WIN_COLS // 2, 0, GRID_W - NA_WIN_COLS)
    col_ok = (k >= col_start) & (k < col_start + NA_WIN_COLS)
    centre = NA_WIN_COLS - 1
    for d in range(NA_DROWS - 1):
        for h in range(NA_HEADS):
            t1 = pltpu.roll(jnp.broadcast_to(rpb_ref[h, d:d + 1, :], shape), LANES - centre, axis=1,
                            stride=1, stride_axis=0)
            t2 = pltpu.roll(jnp.broadcast_to(rpb_ref[h, d + 1:d + 2, :], shape), GRID_W - centre, axis=1,
                            stride=1, stride_axis=0)
            t = jnp.where(left, t1, t2)
            o_ref[d, h * GRID_W:(h + 1) * GRID_W, :] = jnp.where(col_ok, t * LOG2E, NEG_INF)


def _rpb_tables(na_rpb):
    depth = na_rpb.shape[0]
    padded = jnp.pad(na_rpb, ((0, 0), (0, 0), (0, 1), (0, LANES - NA_DCOLS)))
    return pl.pallas_call(
        _rpb_kernel,
        out_shape=jax.ShapeDtypeStruct((depth, NA_DROWS - 1, NA_HEADS * GRID_W, LANES), F32),
        grid=(depth,),
        in_specs=[pl.BlockSpec((None,) + padded.shape[1:], lambda l: (l, 0, 0, 0))],
        out_specs=pl.BlockSpec((None, NA_DROWS - 1, NA_HEADS * GRID_W, LANES), lambda l: (l, 0, 0, 0)),
        compiler_params=_params(),
        name="rpb_tables",
    )(padded)


def _head_rms(t, gain):
    width = t.shape[1]
    sq = (t * t).astype(BF16)
    blk = min(width, MXU_DIM)
    ones = jnp.where(_iota((blk, blk), 0) // HEAD_DIM == _iota((blk, blk), 1) // HEAD_DIM, 1.0, 0.0).astype(BF16)
    sums = [_dot(sq[:, c:c + blk], ones) for c in range(0, width, blk)]
    ss = sums[0] if len(sums) == 1 else jnp.concatenate(sums, axis=1)
    return t * lax.rsqrt(ss * (1.0 / HEAD_DIM) + EPS) * gain


def _rope(t, cos, sin):
    lane = _iota((t.shape[0], LANES), 1)
    first = (lane % (HEAD_DIM // 2)) < (HEAD_DIM // 4)
    out = []
    for c in range(0, t.shape[1], LANES):
        u = t[:, c:c + LANES]
        partner = jnp.where(first, pltpu.roll(u, LANES - HEAD_DIM // 4, axis=1), pltpu.roll(u, HEAD_DIM // 4, axis=1))
        out.append(u * cos + partner * sin)
    return out[0] if len(out) == 1 else jnp.concatenate(out, axis=1)


def _dup_kv(t):
    lane = _iota(t.shape, 1)
    swapped = pltpu.roll(t, HEAD_DIM, axis=1)
    left = lane < HEAD_DIM
    return jnp.concatenate([jnp.where(left, t, swapped), jnp.where(left, swapped, t)], axis=1)


def _inproj_kernel(*refs, rope, kv_only, n_w, sub):
    x_ref, mod_ref, g1_ref = refs[:3]
    w_refs = refs[3:3 + n_w]
    gnq_ref, gnk_ref, gwq_ref, gwk_ref = refs[3 + n_w:7 + n_w]
    rest = refs[7 + n_w:]
    if rope:
        cos_ref, sin_ref = rest[:2]
        rest = rest[2:]
    out_refs = rest
    if kv_only:
        nak_ref, nav_ref, wak_ref, wav_ref = out_refs
        o_nk, o_nv, o_wk, o_wv, o_end = 0, NA_WIDTH, 2 * NA_WIDTH, 2 * NA_WIDTH + WA_KVW, 2 * NA_WIDTH + 2 * WA_KVW
    else:
        naq_ref, nak_ref, nav_ref, cvu_ref, cvb_ref, waq_ref, wak_ref, wav_ref = out_refs
        o_nk, o_nv, o_wk, o_wv, o_end = OFF_NA_K, OFF_NA_V, OFF_WA_K, OFF_WA_V, IN_WIDTH
    def project(rows):
        x = x_ref[rows, :]
        ms = jnp.mean(x * x, axis=-1, keepdims=True)
        h = x * lax.rsqrt(ms + EPS) * g1_ref[...]
        h = h * (1.0 + mod_ref[1]) + mod_ref[0]
        hb = h.astype(BF16)
        return jnp.concatenate([_dot(hb, w_ref[...]) for w_ref in w_refs], axis=1)

    def finish(rows, p):
        nak_ref[rows, :] = _head_rms(p[:, o_nk:o_nk + NA_WIDTH], gnk_ref[...]).astype(BF16)
        nav_ref[rows, :] = p[:, o_nv:o_nv + NA_WIDTH].astype(BF16)
        wk = _head_rms(p[:, o_wk:o_wv], gwk_ref[...])
        if rope:
            wk = _rope(wk, cos_ref[rows, :], sin_ref[rows, :])
        wak_ref[rows, :] = _dup_kv(wk).astype(BF16)
        vv = p[:, o_wv:o_end]
        left = _iota(vv.shape, 1) < HEAD_DIM
        wav_ref[rows, :] = jnp.concatenate([jnp.where(left, vv, 1.0), jnp.where(left, pltpu.roll(vv, HEAD_DIM, axis=1), 1.0)],
                                           axis=1).astype(BF16)
        if kv_only:
            return
        naq_ref[rows, :] = (_head_rms(p[:, OFF_NA_Q:OFF_NA_K], gnq_ref[...]) * Q_SCALE).astype(BF16)
        cvu_ref[rows, :] = p[:, OFF_CV_C:OFF_WA_Q] * p[:, OFF_CV_X:OFF_CV_B]
        cvb_ref[rows, :] = p[:, OFF_CV_B:OFF_CV_C]
        wq = _head_rms(p[:, OFF_WA_Q:OFF_WA_K], gwq_ref[...])
        if rope:
            wq = _rope(wq, cos_ref[rows, :], sin_ref[rows, :])
        waq_ref[rows, :] = (wq * Q_SCALE).astype(BF16)

    for r0 in range(0, x_ref.shape[0], sub):
        rows = slice(r0, r0 + sub)
        finish(rows, project(rows))


def _in_proj(x2d, mod, layer, mod_row, g1, w_in, gains, rope_tabs, tm, kv_only=False):
    rows = x2d.shape[0]
    rope = rope_tabs is not None
    row_spec = lambda w: pl.BlockSpec((tm, w), lambda i: (i, 0))
    const = lambda shape: pl.BlockSpec(shape, lambda i: (0,) * len(shape))
    once = pl.Buffered(1)
    if kv_only:
        col_blocks = [OFF_NA_K // MXU_DIM, OFF_NA_V // MXU_DIM, OFF_WA_K // MXU_DIM]
        w_specs = [pl.BlockSpec((D_MODEL, MXU_DIM), functools.partial(lambda j, i: (0, j), j),
                                pipeline_mode=once) for j in col_blocks]
        widths = [(NA_WIDTH, BF16), (NA_WIDTH, BF16), (WA_K2W, BF16), (WA_V3W, BF16)]
    else:
        w_specs = [pl.BlockSpec((D_MODEL, IN_WIDTH), lambda i: (0, 0), pipeline_mode=once)]
        widths = [(NA_WIDTH, BF16), (NA_WIDTH, BF16), (NA_WIDTH, BF16), (CONV_CH, F32), (CONV_CH, F32),
                  (WA_QW, BF16), (WA_K2W, BF16), (WA_V3W, BF16)]
    n_w = len(w_specs)
    in_specs = [
        row_spec(D_MODEL),
        pl.BlockSpec((None, None, 6, 1, D_MODEL), lambda i: (layer, mod_row(i), 0, 0, 0)),
        const((1, D_MODEL)),
        *w_specs,
        const((1, NA_WIDTH)), const((1, NA_WIDTH)), const((1, WA_QW)), const((1, WA_KVW)),
    ]
    args = [x2d, mod, g1, *([w_in] * n_w), *gains]
    if rope:
        seq_tiles = rope_tabs[0].shape[0] // tm
        in_specs += [pl.BlockSpec((tm, LANES), lambda i: (i % seq_tiles, 0))] * 2
        args += list(rope_tabs)
    return pl.pallas_call(
        functools.partial(_inproj_kernel, rope=rope, kv_only=kv_only, n_w=n_w, sub=min(tm, IN_SUB)),
        out_shape=[jax.ShapeDtypeStruct((rows, w), dt) for w, dt in widths],
        grid=(rows // tm,),
        in_specs=in_specs,
        out_specs=[row_spec(w) for w, _ in widths],
        compiler_params=_params(),
        name="in_proj_rope" if rope else ("in_proj_ctx_kv" if kv_only else "in_proj_ctx"),
    )(*args)


def _lane_chunks(s):
    return [s[:, c:c + LANES] for c in range(0, s.shape[1], LANES)]


def _row_max(scores, extra=None):
    chunks = [ch for s in scores for ch in _lane_chunks(s)]
    if extra is not None:
        chunks.append(extra)
    folded = functools.reduce(jnp.maximum, chunks)
    return jnp.broadcast_to(jnp.max(folded, axis=-1, keepdims=True), folded.shape)


def _softmax_pv(parts, extra=None, sums_from_values=False, m=None):
    lane_chunks = _lane_chunks
    if m is None:
        m = _row_max([s for s, _ in parts], extra)
    ps = [jnp.concatenate([jnp.exp2(ch - m) for ch in lane_chunks(s)], axis=1) for s, _ in parts]
    pv = _dot(jnp.concatenate([p.astype(BF16) for p in ps], axis=1),
              jnp.concatenate([v for _, v in parts], axis=0))
    if sums_from_values:
        denom = pv[:, -LANES:]
        pv = pv[:, :-LANES]
        if extra is not None:
            denom = denom + jnp.exp2(extra - m)
        return pv / denom
    psum = functools.reduce(jnp.add, [ch for p in ps for ch in lane_chunks(p)])
    if extra is not None:
        psum = psum + jnp.where(_iota(extra.shape, 1) == 0, jnp.exp2(extra - m), 0.0)
    return pv / jnp.sum(psum, axis=-1, keepdims=True)


def _stack_heads(q, n_heads):
    head = _iota(q.shape, 1) // HEAD_DIM
    zero = jnp.zeros_like(q)
    return jnp.concatenate([jnp.where(head == h, q, zero) for h in range(n_heads)], axis=0)


def _unstack_heads(o, n_heads):
    rows = o.shape[0] // n_heads
    head = _iota((rows, o.shape[1]), 1) // HEAD_DIM
    out = jnp.where(head == 0, o[:rows], 0.0)
    for h in range(1, n_heads):
        out = jnp.where(head == h, o[h * rows:(h + 1) * rows], out)
    return out


def _group_rms(o, gain):
    ms = jnp.mean(o * o, axis=-1, keepdims=True)
    return o * lax.rsqrt(ms + EPS) * gain


SUBLANES = 8


def _wa_scores(qg, keys):
    rows = qg.shape[0]
    left = _iota((rows, LANES), 1) < HEAD_DIM
    zero = jnp.zeros((rows, LANES), qg.dtype)
    blocks = []
    for pair in range(WA_GROUP // 2):
        qp = qg[:, pair * LANES:(pair + 1) * LANES]
        blocks += [jnp.where(left, qp, zero), jnp.where(left, zero, qp)]
    qs = jnp.concatenate(blocks, axis=0)
    scores = []
    for k2, bias in keys:
        s = _dot_nt(k2, qs)
        if bias is not None:
            s = s + jnp.concatenate([bias] * WA_GROUP, axis=1)
        scores.append(s)
    return scores


def _wa_sink_rows(rows, sinks):
    return jnp.concatenate([jnp.full((SUBLANES, rows), sinks[g] * LOG2E, F32) for g in range(WA_GROUP)], axis=1)


def _wa_col_max(scores, sink_rows):
    m = sink_rows[:1]
    for s in scores:
        m = jnp.maximum(m, jnp.max(s, axis=0, keepdims=True))
    return jnp.broadcast_to(m, sink_rows.shape)


def _wa_finish(scores, values, sinks, m=None):
    rows = scores[0].shape[1] // WA_GROUP
    sink_rows = _wa_sink_rows(rows, sinks)
    if m is None:
        m = _wa_col_max(scores, sink_rows)
    p = jnp.concatenate([jnp.exp2(s - m[:1]).astype(BF16) for s in scores], axis=0)
    v = jnp.concatenate(values, axis=0)
    ot = lax.dot_general(v, p, (((0,), (0,)), ((), ())), preferred_element_type=F32)
    denom = ot[HEAD_DIM:HEAD_DIM + SUBLANES] + jnp.exp2(sink_rows - m)
    on = ot[:HEAD_DIM] / denom[:1]
    pairs = [jnp.concatenate([on[:, (2 * pr) * rows:(2 * pr + 1) * rows],
                              on[:, (2 * pr + 1) * rows:(2 * pr + 2) * rows]], axis=0).T
             for pr in range(WA_GROUP // 2)]
    return jnp.concatenate(pairs, axis=1)


def _na_steps(q_ref, k_ref, v_ref, kc_ref, vc_ref, bias_ref, g_ref, o_ref):
    n_rows = q_ref.shape[0] // GRID_W
    win = NA_WIN_ROWS * GRID_W
    lc = kc_ref.shape[0]
    gain = g_ref[...]

    def window(r):
        start = jnp.clip(r - NA_WIN_ROWS // 2, 0, n_rows - NA_WIN_ROWS)
        return pl.multiple_of(start * GRID_W, GRID_W), start - r + (NA_WIN_ROWS - 1)

    def scores(r, s_ref):
        tok0, d0 = window(r)
        q = q_ref[pl.ds(pl.multiple_of(r * GRID_W, GRID_W), GRID_W), :]
        qs = _stack_heads(q, NA_HEADS)
        bias = jnp.concatenate([bias_ref[d0 + 2 * j] for j in range(NA_WIN_ROWS // 2)], axis=1)
        s_ctx = _dot_nt(qs, kc_ref[...])
        s_loc = _dot_nt(qs, k_ref[pl.ds(tok0, win), :]) + bias
        s_ref[:, :lc] = s_ctx
        s_ref[:, lc:lc + win] = s_loc
        s_ref[:, lc + win:] = _row_max([s_ctx, s_loc])

    def finish(r, s_ref):
        tok0, _ = window(r)
        o4 = _softmax_pv([(s_ref[:, :lc], vc_ref[...]), (s_ref[:, lc:lc + win], v_ref[pl.ds(tok0, win), :])],
                         m=s_ref[:, lc + win:])
        o = _unstack_heads(o4, NA_HEADS)
        o_ref[pl.ds(pl.multiple_of(r * GRID_W, GRID_W), GRID_W), :] = _group_rms(o, gain).astype(o_ref.dtype)

    def step(r, cur_ref, nxt_ref):
        scores(jnp.minimum(r + 1, n_rows - 1), nxt_ref)
        finish(r, cur_ref)

    return scores, step


def _wa_steps(sink_ref, q_ref, k_ref, v_ref, kc_ref, vc_ref, g_ref, o_ref):
    seq = q_ref.shape[0]
    n_blocks = seq // WA_BLOCK
    lc = kc_ref.shape[0]
    gain = g_ref[...]
    key = _iota((WA_BAND, WA_BLOCK), 0)
    qry = _iota((WA_BAND, WA_BLOCK), 1)

    def window(n):
        q0 = pl.multiple_of(n * WA_BLOCK, WA_BLOCK)
        k0 = pl.multiple_of(jnp.clip(q0 - WA_BLOCK, 0, seq - WA_BAND), WA_BLOCK)
        return q0, k0

    def scores(n, kh, s_ref):
        q0, k0 = window(n)
        band = jnp.where(jnp.abs((k0 + key) - (q0 + qry)) <= WA_WINDOW, 0.0, NEG_INF)
        lanes = slice(kh * LANES, (kh + 1) * LANES)
        qg = q_ref[pl.ds(q0, WA_BLOCK), kh * WA_GROUP * HEAD_DIM:(kh + 1) * WA_GROUP * HEAD_DIM]
        s_ctx, s_loc = _wa_scores(qg, [(kc_ref[:, lanes], None), (k_ref[pl.ds(k0, WA_BAND), lanes], band)])
        s_ref[kh, :lc, :] = s_ctx
        s_ref[kh, lc:lc + WA_BAND, :] = s_loc
        s_ref[kh, lc + WA_BAND:, :] = _wa_col_max([s_ctx, s_loc], _wa_sink_rows(WA_BLOCK, sinks(kh)))

    def sinks(kh):
        return [sink_ref[kh * WA_GROUP + g] for g in range(WA_GROUP)]

    def finish(n, kh, s_ref):
        _, k0 = window(n)
        lanes = slice(kh * LANES, (kh + 1) * LANES)
        return _wa_finish([s_ref[kh, :lc, :], s_ref[kh, lc:lc + WA_BAND, :]],
                          [vc_ref[:, lanes], v_ref[pl.ds(k0, WA_BAND), lanes]], sinks(kh),
                          m=s_ref[kh, lc + WA_BAND:, :])

    def step(n, cur_ref, nxt_ref):
        nxt = jnp.minimum(n + 1, n_blocks - 1)
        for kh in range(WA_KV_HEADS):
            scores(nxt, kh, nxt_ref)
        o = jnp.concatenate([finish(n, kh, cur_ref) for kh in range(WA_KV_HEADS)], axis=1)
        q0, _ = window(n)
        o_ref[pl.ds(q0, WA_BLOCK), :] = _group_rms(o, gain).astype(o_ref.dtype)

    return scores, step


def _attn_kernel(sink_ref, nq_ref, nk_ref, nv_ref, nkc_ref, nvc_ref, bias_ref, gn_ref,
                 wq_ref, wk_ref, wv_ref, wkc_ref, wvc_ref, gw_ref, *rest, n_cast):
    f32_refs = rest[:n_cast]
    on_ref, ow_ref = rest[n_cast:n_cast + 2]
    bf16_refs = rest[n_cast + 2:2 * n_cast + 2]
    nsa_ref, nsb_ref, wsa_ref, wsb_ref = rest[2 * n_cast + 2:]
    for src_ref, dst_ref in zip(f32_refs, bf16_refs):
        dst_ref[...] = src_ref[...].astype(dst_ref.dtype)
    na_scores, na_step = _na_steps(nq_ref, nk_ref, nv_ref, nkc_ref, nvc_ref, bias_ref, gn_ref, on_ref)
    wa_scores, wa_step = _wa_steps(sink_ref, wq_ref, wk_ref, wv_ref, wkc_ref, wvc_ref, gw_ref, ow_ref)
    n_blocks = wq_ref.shape[0] // WA_BLOCK

    for kh in range(WA_KV_HEADS):
        wa_scores(0, kh, wsa_ref)
    na_scores(0, nsa_ref)

    def body(i, carry):
        for half, (w_cur, w_nxt) in enumerate(((wsa_ref, wsb_ref), (wsb_ref, wsa_ref))):
            n = 2 * i + half
            wa_step(n, w_cur, w_nxt)
            na_step(2 * n, nsa_ref, nsb_ref)
            na_step(2 * n + 1, nsb_ref, nsa_ref)
        return carry

    lax.fori_loop(0, n_blocks // 2, body, 0, unroll=2)


def _attention(sink, nq, nk, nv, nkc, nvc, bias, layer, gain_na, wq, wk2, wv3, wkc2, wvc3, gain_wa, to_bf16):
    batch, seq, _ = nq.shape
    slab_in = lambda w, j: pl.BlockSpec((None, w.shape[1] // batch, w.shape[2]), lambda b: (j, b, 0))
    slab_out = lambda w: pl.BlockSpec((w.shape[1] // batch, w.shape[2]), lambda b: (b, 0))
    lc = nkc.shape[1]
    assert WA_BLOCK == 2 * GRID_W and (seq // WA_BLOCK) % 2 == 0
    per_b = lambda n, w: pl.BlockSpec((None, n, w), lambda b: (b, 0, 0))
    na_scratch = pltpu.VMEM((NA_HEADS * GRID_W, lc + NA_WIN_ROWS * GRID_W + LANES), F32)
    wa_scratch = pltpu.VMEM((WA_KV_HEADS, lc + WA_BAND + SUBLANES, WA_GROUP * WA_BLOCK), F32)
    outs = pl.pallas_call(
        functools.partial(_attn_kernel, n_cast=len(to_bf16)),
        out_shape=[jax.ShapeDtypeStruct((batch, seq, NA_WIDTH), BF16),
                   jax.ShapeDtypeStruct((batch, seq, WA_QW), BF16),
                   *[jax.ShapeDtypeStruct(w.shape[1:], BF16) for w, _ in to_bf16]],
        grid=(batch,),
        in_specs=[pl.BlockSpec(memory_space=pltpu.SMEM),
                  per_b(seq, NA_WIDTH), per_b(seq, NA_WIDTH), per_b(seq, NA_WIDTH),
                  per_b(lc, NA_WIDTH), per_b(lc, NA_WIDTH),
                  pl.BlockSpec((None,) + bias.shape[1:], lambda b: (layer, 0, 0, 0)),
                  pl.BlockSpec((1, NA_WIDTH), lambda b: (0, 0)),
                  per_b(seq, WA_QW), per_b(seq, WA_K2W), per_b(seq, WA_V3W),
                  per_b(lc, WA_K2W), per_b(lc, WA_V3W),
                  pl.BlockSpec((1, WA_QW), lambda b: (0, 0)),
                  *[slab_in(w, j) for w, j in to_bf16]],
        out_specs=[per_b(seq, NA_WIDTH), per_b(seq, WA_QW), *[slab_out(w) for w, _ in to_bf16]],
        scratch_shapes=[na_scratch, na_scratch, wa_scratch, wa_scratch],
        compiler_params=_params(),
        name="attention",
    )(sink, nq, nk, nv, nkc, nvc, bias, gain_na, wq, wk2, wv3, wkc2, wvc3, gain_wa, *[w for w, _ in to_bf16])
    return outs[0], outs[1], outs[2:]


def _ctx_attn_kernel(sink_ref, qn_ref, kn_ref, vn_ref, qw_ref, kw_ref, vw_ref, gn_ref, gw_ref, on_ref, ow_ref):
    qs = _stack_heads(qn_ref[...], NA_HEADS)
    o4 = _softmax_pv([(_dot_nt(qs, kn_ref[...]), vn_ref[...])])
    on_ref[...] = _group_rms(_unstack_heads(o4, NA_HEADS), gn_ref[...]).astype(on_ref.dtype)
    outs = []
    for kh in range(WA_KV_HEADS):
        lanes = slice(kh * LANES, (kh + 1) * LANES)
        qg = qw_ref[:, kh * WA_GROUP * HEAD_DIM:(kh + 1) * WA_GROUP * HEAD_DIM]
        sinks = [sink_ref[kh * WA_GROUP + g] for g in range(WA_GROUP)]
        outs.append(_wa_finish(_wa_scores(qg, [(kw_ref[:, lanes], None)]), [vw_ref[:, lanes]], sinks))
    ow_ref[...] = _group_rms(jnp.concatenate(outs, axis=1), gw_ref[...]).astype(ow_ref.dtype)


def _ctx_attention(sink, qn, kn, vn, qw, kw2, vw2, gain_na, gain_wa):
    batch, lc, _ = qn.shape
    per_b = lambda w: pl.BlockSpec((None, lc, w), lambda b: (b, 0, 0))
    return pl.pallas_call(
        _ctx_attn_kernel,
        out_shape=[jax.ShapeDtypeStruct((batch, lc, NA_WIDTH), BF16),
                   jax.ShapeDtypeStruct((batch, lc, WA_QW), BF16)],
        grid=(batch,),
        in_specs=[pl.BlockSpec(memory_space=pltpu.SMEM),
                  per_b(NA_WIDTH), per_b(NA_WIDTH), per_b(NA_WIDTH),
                  per_b(WA_QW), per_b(WA_K2W), per_b(WA_V3W),
                  pl.BlockSpec((1, NA_WIDTH), lambda b: (0, 0)),
                  pl.BlockSpec((1, WA_QW), lambda b: (0, 0))],
        out_specs=[per_b(NA_WIDTH), per_b(WA_QW)],
        compiler_params=_params(),
        name="ctx_attention",
    )(sink, qn, kn, vn, qw, kw2, vw2, gain_na, gain_wa)


def _out_mlp_kernel(x_ref, mod_ref, na_ref, wa_ref, u_ref, uprev_ref, unext_ref, cvb_ref, cw_ref, cb_ref, g_ref,
                    g2_ref, wo_ref, *rest, seq_len, sub):
    n_ffn = FFN_DIM // FFN_CHUNK
    w1_refs, w2_refs, (o_ref,) = rest[:n_ffn], rest[n_ffn:2 * n_ffn], rest[2 * n_ffn:]
    tm = x_ref.shape[0]
    i = pl.program_id(0)
    u = u_ref[...]
    row = _iota(u.shape, 0)
    pos = (i * tm + row) % seq_len
    up = jnp.where(row == 0, uprev_ref[7:8, :], pltpu.roll(u, 1, axis=0))
    dn = jnp.where(row == tm - 1, unext_ref[0:1, :], pltpu.roll(u, tm - 1, axis=0))
    up = jnp.where(pos == 0, 0.0, up)
    dn = jnp.where(pos == seq_len - 1, 0.0, dn)
    y = cb_ref[...] + cw_ref[0:1, :] * up + cw_ref[1:2, :] * u + cw_ref[2:3, :] * dn
    cv = _group_rms(cvb_ref[...] * y, g_ref[...]).astype(BF16)
    for r0 in range(0, tm, sub):
        rows = slice(r0, r0 + sub)
        mixed = jnp.concatenate([na_ref[rows, :], cv[rows, :], wa_ref[rows, :]], axis=1)
        x = x_ref[rows, :] + mod_ref[2] * _dot(mixed, wo_ref[...])
        ms = jnp.mean(x * x, axis=-1, keepdims=True)
        h = x * lax.rsqrt(ms + EPS) * g2_ref[...]
        h = (h * (1.0 + mod_ref[4]) + mod_ref[3]).astype(BF16)
        acc = None
        for w1_ref, w2_ref in zip(w1_refs, w2_refs):
            a = jnp.maximum(_dot(h, w1_ref[...]), 0.0)
            part = _dot((a * a).astype(BF16), w2_ref[...])
            acc = part if acc is None else acc + part
        o_ref[rows, :] = x + mod_ref[5] * acc


def _out_mlp(x2d, mod, layer, mod_row, na_n, wa_n, cv_u, cv_b, conv_w, conv_b, gain_cv, w_o, g2, w1, w2, seq_len, tm):
    rows = x2d.shape[0]
    halo = 8
    per_tile = tm // halo
    last = rows // halo - 1
    row_spec = lambda w: pl.BlockSpec((tm, w), lambda i: (i, 0))
    const = lambda shape: pl.BlockSpec(shape, lambda i: (0,) * len(shape), pipeline_mode=pl.Buffered(1))
    once = pl.Buffered(1)
    n_ffn = FFN_DIM // FFN_CHUNK
    wo_spec = pl.BlockSpec((D_MODEL, D_MODEL), lambda i: (0, 0), pipeline_mode=once)
    w1_specs = [pl.BlockSpec((D_MODEL, FFN_CHUNK), functools.partial(lambda j, i: (0, j), j),
                             pipeline_mode=once) for j in range(n_ffn)]
    w2_specs = [pl.BlockSpec((FFN_CHUNK, D_MODEL), functools.partial(lambda j, i: (j, 0), j),
                             pipeline_mode=once) for j in range(n_ffn)]
    return pl.pallas_call(
        functools.partial(_out_mlp_kernel, seq_len=seq_len, sub=min(tm, MLP_SUB)),
        out_shape=jax.ShapeDtypeStruct((rows, D_MODEL), F32),
        grid=(rows // tm,),
        in_specs=[
            row_spec(D_MODEL),
            pl.BlockSpec((None, None, 6, 1, D_MODEL), lambda i: (layer, mod_row(i), 0, 0, 0)),
            row_spec(NA_WIDTH), row_spec(WA_QW), row_spec(CONV_CH),
            pl.BlockSpec((halo, CONV_CH), lambda i: (jnp.maximum(i * per_tile - 1, 0), 0)),
            pl.BlockSpec((halo, CONV_CH), lambda i: (jnp.minimum((i + 1) * per_tile, last), 0)),
            row_spec(CONV_CH),
            const((3, CONV_CH)), const((1, CONV_CH)), const((1, CONV_CH)),
            const((1, D_MODEL)),
            wo_spec, *w1_specs, *w2_specs,
        ],
        out_specs=row_spec(D_MODEL),
        compiler_params=_params(),
        name="out_mlp",
    )(x2d, mod, na_n, wa_n, cv_u, cv_u, cv_u, cv_b, conv_w, conv_b, gain_cv, g2,
      w_o, *([w1] * n_ffn), *([w2] * n_ffn))


def _rope_tables(seq):
    quarter = HEAD_DIM // 4
    inv = ROPE_BASE ** (-jnp.arange(quarter, dtype=F32) / quarter)
    t = jnp.arange(seq)
    ang_r = (t // GRID_W).astype(F32)[:, None] * inv[None, :]
    ang_c = (t % GRID_W).astype(F32)[:, None] * inv[None, :]
    cos = jnp.concatenate([jnp.cos(ang_r), jnp.cos(ang_r), jnp.cos(ang_c), jnp.cos(ang_c)], axis=1)
    sin = jnp.concatenate([-jnp.sin(ang_r), jnp.sin(ang_r), -jnp.sin(ang_c), jnp.sin(ang_c)], axis=1)
    reps = LANES // HEAD_DIM
    return jnp.tile(cos, (1, reps)), jnp.tile(sin, (1, reps))


def kernel(x, c, ctx, c_ctx, w_mod, b_mod, g_norm1, g_norm2, w_in, na_q_gain, na_k_gain, na_rpb,
           conv_w, conv_bias, wa_q_gain, wa_k_gain, wa_sink, g_out, w_o, w_fc1, w_fc2):
    batch, seq, d = x.shape
    lc = ctx.shape[1]
    depth = w_mod.shape[0]
    assert d == D_MODEL and seq % WA_BLOCK == 0 and seq % GRID_W == 0 and batch < MOD_ROWS
    tm = 1024
    tm_in = 1024
    tm_ctx = 1024
    assert seq % tm == 0 and seq % tm_in == 0 and (batch * lc) % tm_ctx == 0

    cond = jnp.zeros((MOD_ROWS, d), F32).at[:batch].set(c).at[batch].set(c_ctx)
    mod = _modulation(cond, w_mod, b_mod)
    rpb_bias = _rpb_tables(na_rpb)
    rope_tabs = _rope_tables(seq)

    lat_row = lambda i: i // (seq // tm)
    lat_row_in = lambda i: i // (seq // tm_in)
    ctx_row = lambda i: batch

    w_in_bf = w_in[0].astype(BF16)
    xs = x.reshape(batch * seq, d)
    cs = ctx.reshape(batch * lc, d)
    for l in range(depth):
        last = l == depth - 1
        g1 = g_norm1[l].reshape(1, d)
        g2 = g_norm2[l].reshape(1, d)
        gains = [jnp.tile(na_q_gain[l], NA_HEADS).reshape(1, -1), jnp.tile(na_k_gain[l], NA_HEADS).reshape(1, -1),
                 jnp.tile(wa_q_gain[l], WA_HEADS).reshape(1, -1), jnp.tile(wa_k_gain[l], WA_KV_HEADS).reshape(1, -1)]
        go_na = g_out[l, :NA_WIDTH].reshape(1, -1)
        go_cv = g_out[l, NA_WIDTH:NA_WIDTH + CONV_CH].reshape(1, -1)
        go_wa = g_out[l, NA_WIDTH + CONV_CH:].reshape(1, -1)
        cb = conv_bias[l].reshape(1, -1)

        naq, nak, nav, cvu, cvb, waq, wak, wav = _in_proj(xs, mod, l, lat_row_in, g1, w_in_bf, gains, rope_tabs, tm_in)
        if last:
            cnak, cnav, cwak, cwav = _in_proj(cs, mod, l, ctx_row, g1, w_in_bf, gains, None, tm_ctx, kv_only=True)
        else:
            cnaq, cnak, cnav, ccvu, ccvb, cwaq, cwak, cwav = _in_proj(cs, mod, l, ctx_row, g1, w_in_bf, gains, None, tm_ctx)

        b3 = lambda a, n: a.reshape(batch, n, a.shape[-1])
        casts = [(w_o, l), (w_fc1, l), (w_fc2, l)] + ([] if last else [(w_in, l + 1)])
        na_n, wa_n, cast_out = _attention(
            wa_sink[l], b3(naq, seq), b3(nak, seq), b3(nav, seq), b3(cnak, lc), b3(cnav, lc), rpb_bias, l, go_na,
            b3(waq, seq), b3(wak, seq), b3(wav, seq), b3(cwak, lc), b3(cwav, lc), go_wa, casts)
        w_o_bf, w1_bf, w2_bf = cast_out[:3]
        if not last:
            w_in_bf = cast_out[3]
        xs = _out_mlp(xs, mod, l, lat_row, na_n.reshape(batch * seq, -1), wa_n.reshape(batch * seq, -1),
                      cvu, cvb, conv_w[l], cb, go_cv, w_o_bf, g2, w1_bf, w2_bf, seq, tm)
        if not last:
            cna_n, cwa_n = _ctx_attention(wa_sink[l], b3(cnaq, lc), b3(cnak, lc), b3(cnav, lc),
                                          b3(cwaq, lc), b3(cwak, lc), b3(cwav, lc), go_na, go_wa)
            cs = _out_mlp(cs, mod, l, ctx_row, cna_n.reshape(batch * lc, -1), cwa_n.reshape(batch * lc, -1),
                          ccvu, ccvb, conv_w[l], cb, go_cv, w_o_bf, g2, w1_bf, w2_bf, lc, tm_ctx)
    return xs.reshape(batch, seq, d)
```

```python
import functools

import jax
import jax.numpy as jnp
from jax import lax
from jax.experimental import pallas as pl
from jax.experimental.pallas import tpu as pltpu

D_MODEL = 1024
GRID_W = 64
HEAD_DIM = 64
NA_HEADS = 4
NA_WIDTH = NA_HEADS * HEAD_DIM
CONV_CH = 256
WA_HEADS = 8
WA_KV_HEADS = 2
WA_GROUP = WA_HEADS // WA_KV_HEADS
WA_QW = WA_HEADS * HEAD_DIM
WA_KVW = WA_KV_HEADS * HEAD_DIM
WA_K2W = 2 * WA_KVW
WA_V3W = 2 * WA_KVW
NA_WIN_ROWS = 8
NA_WIN_COLS = 16
NA_DROWS = 2 * NA_WIN_ROWS - 1
NA_DCOLS = 2 * NA_WIN_COLS - 1
WA_WINDOW = 128
WA_BLOCK = 128
WA_BAND = 3 * WA_BLOCK
FFN_DIM = 4 * D_MODEL
FFN_CHUNK = 1024
ROPE_BASE = 10000.0
EPS = 1e-6
NEG_INF = -1e30
IN_WIDTH = 2304
OFF_NA_Q, OFF_NA_K, OFF_NA_V = 0, 256, 512
OFF_CV_X, OFF_CV_B, OFF_CV_C = 768, 1024, 1280
OFF_WA_Q, OFF_WA_K, OFF_WA_V = 1536, 2048, 2176
MOD_ROWS = 16
MOD_TN = 2048
IN_SUB = 256
MLP_SUB = 512
LANES = 128
MXU_DIM = 256
VMEM_LIMIT = 60 * 1024 * 1024
LOG2E = 1.4426950408889634
Q_SCALE = LOG2E * HEAD_DIM ** -0.5

F32 = jnp.float32
BF16 = jnp.bfloat16


def _dot(a, b):
    return jnp.dot(a, b, preferred_element_type=F32)


def _dot_nt(a, b):
    return lax.dot_general(a, b, (((1,), (1,)), ((), ())), preferred_element_type=F32)


def _split_bf16(a):
    hi = a.astype(BF16)
    lo = (a - hi.astype(F32)).astype(BF16)
    return hi, lo


def _iota(shape, dim):
    return lax.broadcasted_iota(jnp.int32, shape, dim)


def _params(**kw):
    return pltpu.CompilerParams(vmem_limit_bytes=VMEM_LIMIT, **kw)


def _mod_kernel(cond_ref, w_ref, b_ref, o_ref):
    a = cond_ref[...]
    a = a * (1.0 / (1.0 + jnp.exp(-a)))
    ah, al = _split_bf16(a)
    wh, wl = _split_bf16(w_ref[...])
    res = _dot(ah, wh) + _dot(al, wh) + _dot(ah, wl) + b_ref[...]
    for c in range(o_ref.shape[1]):
        o_ref[:, c, 0, :] = res[:, c * D_MODEL:(c + 1) * D_MODEL]


def _modulation(cond, w_mod, b_mod):
    depth = w_mod.shape[0]
    n_out = w_mod.shape[2]
    per_tile = MOD_TN // D_MODEL
    return pl.pallas_call(
        _mod_kernel,
        out_shape=jax.ShapeDtypeStruct((depth, MOD_ROWS, n_out // D_MODEL, 1, D_MODEL), F32),
        grid=(depth, n_out // MOD_TN),
        in_specs=[
            pl.BlockSpec((MOD_ROWS, D_MODEL), lambda l, j: (0, 0)),
            pl.BlockSpec((None, D_MODEL, MOD_TN), lambda l, j: (l, 0, j)),
            pl.BlockSpec((None, 1, MOD_TN), lambda l, j: (l, 0, j)),
        ],
        out_specs=pl.BlockSpec((None, MOD_ROWS, per_tile, 1, D_MODEL), lambda l, j: (l, 0, j, 0, 0)),
        compiler_params=_params(),
        name="modulation",
    )(cond, w_mod, b_mod.reshape(depth, 1, n_out))


def _rpb_kernel(rpb_ref, o_ref):
    shape = (GRID_W, LANES)
    q = _iota(shape, 0)
    lane = _iota(shape, 1)
    k = lane % GRID_W
    left = lane < GRID_W
    col_start = jnp.clip(q - NA_WIN_COLS // 2, 0, GRID_W - NA_WIN_COLS)
    col_ok = (k >= col_start) & (k < col_start + NA_WIN_COLS)
    centre = NA_WIN_COLS - 1
    for d in range(NA_DROWS - 1):
        tiles = []
        for h in range(NA_HEADS):
            t1 = pltpu.roll(jnp.broadcast_to(rpb_ref[h, d:d + 1, :], shape), LANES - centre, axis=1,
                            stride=1, stride_axis=0)
            t2 = pltpu.roll(jnp.broadcast_to(rpb_ref[h, d + 1:d + 2, :], shape), GRID_W - centre, axis=1,
                            stride=1, stride_axis=0)
            t = jnp.where(left, t1, t2)
            tiles.append(jnp.where(col_ok, t * LOG2E, NEG_INF))
        o_ref[d] = jnp.concatenate(tiles, axis=0).T


def _rpb_tables(na_rpb):
    depth = na_rpb.shape[0]
    padded = jnp.pad(na_rpb, ((0, 0), (0, 0), (0, 1), (0, LANES - NA_DCOLS)))
    return pl.pallas_call(
        _rpb_kernel,
        out_shape=jax.ShapeDtypeStruct((depth, NA_DROWS - 1, LANES, NA_HEADS * GRID_W), F32),
        grid=(depth,),
        in_specs=[pl.BlockSpec((None,) + padded.shape[1:], lambda l: (l, 0, 0, 0))],
        out_specs=pl.BlockSpec((None, NA_DROWS - 1, LANES, NA_HEADS * GRID_W), lambda l: (l, 0, 0, 0)),
        compiler_params=_params(),
        name="rpb_tables",
    )(padded)


def _head_rms(t, gain):
    width = t.shape[1]
    sq = (t * t).astype(BF16)
    blk = min(width, MXU_DIM)
    ones = jnp.where(_iota((blk, blk), 0) // HEAD_DIM == _iota((blk, blk), 1) // HEAD_DIM, 1.0, 0.0).astype(BF16)
    sums = [_dot(sq[:, c:c + blk], ones) for c in range(0, width, blk)]
    ss = sums[0] if len(sums) == 1 else jnp.concatenate(sums, axis=1)
    return t * lax.rsqrt(ss * (1.0 / HEAD_DIM) + EPS) * gain


def _rope(t, cos, sin):
    lane = _iota((t.shape[0], LANES), 1)
    first = (lane % (HEAD_DIM // 2)) < (HEAD_DIM // 4)
    out = []
    for c in range(0, t.shape[1], LANES):
        u = t[:, c:c + LANES]
        partner = jnp.where(first, pltpu.roll(u, LANES - HEAD_DIM // 4, axis=1), pltpu.roll(u, HEAD_DIM // 4, axis=1))
        out.append(u * cos + partner * sin)
    return out[0] if len(out) == 1 else jnp.concatenate(out, axis=1)


def _dup_kv(t):
    lane = _iota(t.shape, 1)
    swapped = pltpu.roll(t, HEAD_DIM, axis=1)
    left = lane < HEAD_DIM
    return jnp.concatenate([jnp.where(left, t, swapped), jnp.where(left, swapped, t)], axis=1)


def _inproj_kernel(*refs, rope, kv_only, n_w, sub):
    x_ref, mod_ref, g1_ref = refs[:3]
    w_refs = refs[3:3 + n_w]
    gnq_ref, gnk_ref, gwq_ref, gwk_ref = refs[3 + n_w:7 + n_w]
    rest = refs[7 + n_w:]
    if rope:
        cos_ref, sin_ref = rest[:2]
        rest = rest[2:]
    out_refs = rest
    if kv_only:
        nak_ref, nav_ref, wak_ref, wav_ref = out_refs
        o_nk, o_nv, o_wk, o_wv, o_end = 0, NA_WIDTH, 2 * NA_WIDTH, 2 * NA_WIDTH + WA_KVW, 2 * NA_WIDTH + 2 * WA_KVW
    else:
        naq_ref, nak_ref, nav_ref, cvu_ref, cvb_ref, waq_ref, wak_ref, wav_ref = out_refs
        o_nk, o_nv, o_wk, o_wv, o_end = OFF_NA_K, OFF_NA_V, OFF_WA_K, OFF_WA_V, IN_WIDTH
    def project(rows):
        x = x_ref[rows, :]
        ms = jnp.mean(x * x, axis=-1, keepdims=True)
        h = x * lax.rsqrt(ms + EPS) * g1_ref[...]
        h = h * (1.0 + mod_ref[1]) + mod_ref[0]
        hb = h.astype(BF16)
        return jnp.concatenate([_dot(hb, w_ref[...]) for w_ref in w_refs], axis=1)

    def finish(rows, p):
        nak_ref[rows, :] = _head_rms(p[:, o_nk:o_nk + NA_WIDTH], gnk_ref[...]).astype(BF16)
        nav_ref[rows, :] = p[:, o_nv:o_nv + NA_WIDTH].astype(BF16)
        wk = _head_rms(p[:, o_wk:o_wv], gwk_ref[...])
        if rope:
            wk = _rope(wk, cos_ref[rows, :], sin_ref[rows, :])
        wak_ref[rows, :] = _dup_kv(wk).astype(BF16)
        vv = p[:, o_wv:o_end]
        left = _iota(vv.shape, 1) < HEAD_DIM
        wav_ref[rows, :] = jnp.concatenate([jnp.where(left, vv, 1.0), jnp.where(left, pltpu.roll(vv, HEAD_DIM, axis=1), 1.0)],
                                           axis=1).astype(BF16)
        if kv_only:
            return
        naq_ref[rows, :] = (_head_rms(p[:, OFF_NA_Q:OFF_NA_K], gnq_ref[...]) * Q_SCALE).astype(BF16)
        cvu_ref[rows, :] = p[:, OFF_CV_C:OFF_WA_Q] * p[:, OFF_CV_X:OFF_CV_B]
        cvb_ref[rows, :] = p[:, OFF_CV_B:OFF_CV_C]
        wq = _head_rms(p[:, OFF_WA_Q:OFF_WA_K], gwq_ref[...])
        if rope:
            wq = _rope(wq, cos_ref[rows, :], sin_ref[rows, :])
        waq_ref[rows, :] = (wq * Q_SCALE).astype(BF16)

    for r0 in range(0, x_ref.shape[0], sub):
        rows = slice(r0, r0 + sub)
        finish(rows, project(rows))


def _in_proj(x2d, mod, layer, mod_row, g1, w_in, gains, rope_tabs, tm, kv_only=False):
    rows = x2d.shape[0]
    rope = rope_tabs is not None
    row_spec = lambda w: pl.BlockSpec((tm, w), lambda i: (i, 0))
    const = lambda shape: pl.BlockSpec(shape, lambda i: (0,) * len(shape))
    once = pl.Buffered(1)
    if kv_only:
        col_blocks = [OFF_NA_K // MXU_DIM, OFF_NA_V // MXU_DIM, OFF_WA_K // MXU_DIM]
        w_specs = [pl.BlockSpec((D_MODEL, MXU_DIM), functools.partial(lambda j, i: (0, j), j),
                                pipeline_mode=once) for j in col_blocks]
        widths = [(NA_WIDTH, BF16), (NA_WIDTH, BF16), (WA_K2W, BF16), (WA_V3W, BF16)]
    else:
        w_specs = [pl.BlockSpec((D_MODEL, IN_WIDTH), lambda i: (0, 0), pipeline_mode=once)]
        widths = [(NA_WIDTH, BF16), (NA_WIDTH, BF16), (NA_WIDTH, BF16), (CONV_CH, F32), (CONV_CH, F32),
                  (WA_QW, BF16), (WA_K2W, BF16), (WA_V3W, BF16)]
    n_w = len(w_specs)
    in_specs = [
        row_spec(D_MODEL),
        pl.BlockSpec((None, None, 6, 1, D_MODEL), lambda i: (layer, mod_row(i), 0, 0, 0)),
        const((1, D_MODEL)),
        *w_specs,
        const((1, NA_WIDTH)), const((1, NA_WIDTH)), const((1, WA_QW)), const((1, WA_KVW)),
    ]
    args = [x2d, mod, g1, *([w_in] * n_w), *gains]
    if rope:
        seq_tiles = rope_tabs[0].shape[0] // tm
        in_specs += [pl.BlockSpec((tm, LANES), lambda i: (i % seq_tiles, 0))] * 2
        args += list(rope_tabs)
    return pl.pallas_call(
        functools.partial(_inproj_kernel, rope=rope, kv_only=kv_only, n_w=n_w, sub=min(tm, IN_SUB)),
        out_shape=[jax.ShapeDtypeStruct((rows, w), dt) for w, dt in widths],
        grid=(rows // tm,),
        in_specs=in_specs,
        out_specs=[row_spec(w) for w, _ in widths],
        compiler_params=_params(),
        name="in_proj_rope" if rope else ("in_proj_ctx_kv" if kv_only else "in_proj_ctx"),
    )(*args)


def _lane_chunks(s):
    return [s[:, c:c + LANES] for c in range(0, s.shape[1], LANES)]


def _row_max(scores, extra=None):
    chunks = [ch for s in scores for ch in _lane_chunks(s)]
    if extra is not None:
        chunks.append(extra)
    folded = functools.reduce(jnp.maximum, chunks)
    return jnp.broadcast_to(jnp.max(folded, axis=-1, keepdims=True), folded.shape)


def _softmax_pv(parts, extra=None, sums_from_values=False, m=None):
    lane_chunks = _lane_chunks
    if m is None:
        m = _row_max([s for s, _ in parts], extra)
    ps = [jnp.concatenate([jnp.exp2(ch - m) for ch in lane_chunks(s)], axis=1) for s, _ in parts]
    pv = _dot(jnp.concatenate([p.astype(BF16) for p in ps], axis=1),
              jnp.concatenate([v for _, v in parts], axis=0))
    if sums_from_values:
        denom = pv[:, -LANES:]
        pv = pv[:, :-LANES]
        if extra is not None:
            denom = denom + jnp.exp2(extra - m)
        return pv / denom
    psum = functools.reduce(jnp.add, [ch for p in ps for ch in lane_chunks(p)])
    if extra is not None:
        psum = psum + jnp.where(_iota(extra.shape, 1) == 0, jnp.exp2(extra - m), 0.0)
    return pv / jnp.sum(psum, axis=-1, keepdims=True)


def _stack_heads(q, n_heads):
    head = _iota(q.shape, 1) // HEAD_DIM
    zero = jnp.zeros_like(q)
    return jnp.concatenate([jnp.where(head == h, q, zero) for h in range(n_heads)], axis=0)


def _unstack_heads(o, n_heads):
    rows = o.shape[0] // n_heads
    head = _iota((rows, o.shape[1]), 1) // HEAD_DIM
    out = jnp.where(head == 0, o[:rows], 0.0)
    for h in range(1, n_heads):
        out = jnp.where(head == h, o[h * rows:(h + 1) * rows], out)
    return out


def _group_rms(o, gain):
    ms = jnp.mean(o * o, axis=-1, keepdims=True)
    return o * lax.rsqrt(ms + EPS) * gain


SUBLANES = 8


def _wa_scores(qg, keys):
    rows = qg.shape[0]
    left = _iota((rows, LANES), 1) < HEAD_DIM
    zero = jnp.zeros((rows, LANES), qg.dtype)
    blocks = []
    for pair in range(WA_GROUP // 2):
        qp = qg[:, pair * LANES:(pair + 1) * LANES]
        blocks += [jnp.where(left, qp, zero), jnp.where(left, zero, qp)]
    qs = jnp.concatenate(blocks, axis=0)
    scores = []
    for k2, bias in keys:
        s = _dot_nt(k2, qs)
        if bias is not None:
            s = s + jnp.concatenate([bias] * WA_GROUP, axis=1)
        scores.append(s)
    return scores


def _wa_sink_rows(rows, sinks):
    return jnp.concatenate([jnp.full((SUBLANES, rows), sinks[g] * LOG2E, F32) for g in range(WA_GROUP)], axis=1)


def _wa_col_max(scores, sink_rows):
    m = sink_rows[:1]
    for s in scores:
        m = jnp.maximum(m, jnp.max(s, axis=0, keepdims=True))
    return jnp.broadcast_to(m, sink_rows.shape)


def _wa_finish(scores, values, sinks, m=None):
    rows = scores[0].shape[1] // WA_GROUP
    sink_rows = _wa_sink_rows(rows, sinks)
    if m is None:
        m = _wa_col_max(scores, sink_rows)
    p = jnp.concatenate([jnp.exp2(s - m[:1]).astype(BF16) for s in scores], axis=0)
    v = jnp.concatenate(values, axis=0)
    ot = lax.dot_general(v, p, (((0,), (0,)), ((), ())), preferred_element_type=F32)
    denom = ot[HEAD_DIM:HEAD_DIM + SUBLANES] + jnp.exp2(sink_rows - m)
    on = ot[:HEAD_DIM] / denom[:1]
    pairs = [jnp.concatenate([on[:, (2 * pr) * rows:(2 * pr + 1) * rows],
                              on[:, (2 * pr + 1) * rows:(2 * pr + 2) * rows]], axis=0).T
             for pr in range(WA_GROUP // 2)]
    return jnp.concatenate(pairs, axis=1)


def _na_steps(q_ref, k_ref, v_ref, kc_ref, vc_ref, bias_ref, g_ref, o_ref, vt_ref, vct_ref):
    seq = q_ref.shape[0]
    n_rows = seq // GRID_W
    win = NA_WIN_ROWS * GRID_W
    lc = kc_ref.shape[0]
    gain = g_ref[...]

    def prepare():
        vt_ref[0] = v_ref[...].astype(F32).T.astype(BF16)
        vt_ref[1, :, :seq - GRID_W] = v_ref[GRID_W:, :].astype(F32).T.astype(BF16)
        vct_ref[...] = vc_ref[...].astype(F32).T.astype(BF16)

    def window(r):
        start = jnp.clip(r - NA_WIN_ROWS // 2, 0, n_rows - NA_WIN_ROWS)
        return start, start - r + (NA_WIN_ROWS - 1)

    def scores(r, s_ref):
        start, d0 = window(r)
        tok0 = pl.multiple_of(start * GRID_W, GRID_W)
        q = q_ref[pl.ds(pl.multiple_of(r * GRID_W, GRID_W), GRID_W), :]
        qs = _stack_heads(q, NA_HEADS)
        bias = jnp.concatenate([bias_ref[d0 + 2 * j] for j in range(NA_WIN_ROWS // 2)], axis=0)
        s_ctx = _dot_nt(kc_ref[...], qs)
        s_loc = _dot_nt(k_ref[pl.ds(tok0, win), :], qs) + bias
        m = jnp.maximum(jnp.max(s_ctx, axis=0, keepdims=True), jnp.max(s_loc, axis=0, keepdims=True))
        s_ref[:lc, :] = s_ctx
        s_ref[lc:lc + win, :] = s_loc
        s_ref[lc + win:, :] = jnp.broadcast_to(m, (SUBLANES, m.shape[1]))

    def finish(r, s_ref):
        start, _ = window(r)
        odd = start % 2
        lane0 = pl.multiple_of((start - odd) * GRID_W, LANES)
        m = s_ref[lc + win:lc + win + 1, :]
        p_ctx = jnp.exp2(s_ref[:lc, :] - m)
        p_loc = jnp.exp2(s_ref[lc:lc + win, :] - m)
        denom = jnp.sum(p_ctx, axis=0, keepdims=True) + jnp.sum(p_loc, axis=0, keepdims=True)
        ot = _dot(vct_ref[...], p_ctx.astype(BF16)) + _dot(vt_ref[odd, :, pl.ds(lane0, win)], p_loc.astype(BF16))
        o4 = (ot * (1.0 / denom)).T
        o = _unstack_heads(o4, NA_HEADS)
        o_ref[pl.ds(pl.multiple_of(r * GRID_W, GRID_W), GRID_W), :] = _group_rms(o, gain).astype(o_ref.dtype)

    def step(r, cur_ref, nxt_ref):
        scores(jnp.minimum(r + 1, n_rows - 1), nxt_ref)
        finish(r, cur_ref)

    return prepare, scores, step


def _wa_steps(sink_ref, q_ref, k_ref, v_ref, kc_ref, vc_ref, g_ref, o_ref):
    seq = q_ref.shape[0]
    n_blocks = seq // WA_BLOCK
    lc = kc_ref.shape[0]
    gain = g_ref[...]
    key = _iota((WA_BAND, WA_BLOCK), 0)
    qry = _iota((WA_BAND, WA_BLOCK), 1)

    def window(n):
        q0 = pl.multiple_of(n * WA_BLOCK, WA_BLOCK)
        k0 = pl.multiple_of(jnp.clip(q0 - WA_BLOCK, 0, seq - WA_BAND), WA_BLOCK)
        return q0, k0

    def scores(n, kh, s_ref):
        q0, k0 = window(n)
        band = jnp.where(jnp.abs((k0 + key) - (q0 + qry)) <= WA_WINDOW, 0.0, NEG_INF)
        lanes = slice(kh * LANES, (kh + 1) * LANES)
        qg = q_ref[pl.ds(q0, WA_BLOCK), kh * WA_GROUP * HEAD_DIM:(kh + 1) * WA_GROUP * HEAD_DIM]
        s_ctx, s_loc = _wa_scores(qg, [(kc_ref[:, lanes], None), (k_ref[pl.ds(k0, WA_BAND), lanes], band)])
        s_ref[kh, :lc, :] = s_ctx
        s_ref[kh, lc:lc + WA_BAND, :] = s_loc
        s_ref[kh, lc + WA_BAND:, :] = _wa_col_max([s_ctx, s_loc], _wa_sink_rows(WA_BLOCK, sinks(kh)))

    def sinks(kh):
        return [sink_ref[kh * WA_GROUP + g] for g in range(WA_GROUP)]

    def finish(n, kh, s_ref):
        _, k0 = window(n)
        lanes = slice(kh * LANES, (kh + 1) * LANES)
        return _wa_finish([s_ref[kh, :lc, :], s_ref[kh, lc:lc + WA_BAND, :]],
                          [vc_ref[:, lanes], v_ref[pl.ds(k0, WA_BAND), lanes]], sinks(kh),
                          m=s_ref[kh, lc + WA_BAND:, :])

    def step(n, cur_ref, nxt_ref):
        nxt = jnp.minimum(n + 1, n_blocks - 1)
        for kh in range(WA_KV_HEADS):
            scores(nxt, kh, nxt_ref)
        o = jnp.concatenate([finish(n, kh, cur_ref) for kh in range(WA_KV_HEADS)], axis=1)
        q0, _ = window(n)
        o_ref[pl.ds(q0, WA_BLOCK), :] = _group_rms(o, gain).astype(o_ref.dtype)

    return scores, step


def _attn_kernel(sink_ref, nq_ref, nk_ref, nv_ref, nkc_ref, nvc_ref, bias_ref, gn_ref,
                 wq_ref, wk_ref, wv_ref, wkc_ref, wvc_ref, gw_ref, *rest, n_cast):
    f32_refs = rest[:n_cast]
    on_ref, ow_ref = rest[n_cast:n_cast + 2]
    bf16_refs = rest[n_cast + 2:2 * n_cast + 2]
    nsa_ref, nsb_ref, wsa_ref, wsb_ref, vt_ref, vct_ref = rest[2 * n_cast + 2:]
    for src_ref, dst_ref in zip(f32_refs, bf16_refs):
        dst_ref[...] = src_ref[...].astype(dst_ref.dtype)
    na_prepare, na_scores, na_step = _na_steps(nq_ref, nk_ref, nv_ref, nkc_ref, nvc_ref, bias_ref, gn_ref, on_ref,
                                               vt_ref, vct_ref)
    na_prepare()
    wa_scores, wa_step = _wa_steps(sink_ref, wq_ref, wk_ref, wv_ref, wkc_ref, wvc_ref, gw_ref, ow_ref)
    n_blocks = wq_ref.shape[0] // WA_BLOCK

    for kh in range(WA_KV_HEADS):
        wa_scores(0, kh, wsa_ref)
    na_scores(0, nsa_ref)

    def body(i, carry):
        for half, (w_cur, w_nxt) in enumerate(((wsa_ref, wsb_ref), (wsb_ref, wsa_ref))):
            n = 2 * i + half
            wa_step(n, w_cur, w_nxt)
            na_step(2 * n, nsa_ref, nsb_ref)
            na_step(2 * n + 1, nsb_ref, nsa_ref)
        return carry

    lax.fori_loop(0, n_blocks // 2, body, 0, unroll=2)


def _attention(sink, nq, nk, nv, nkc, nvc, bias, layer, gain_na, wq, wk2, wv3, wkc2, wvc3, gain_wa, to_bf16):
    batch, seq, _ = nq.shape
    slab_in = lambda w, j: pl.BlockSpec((None, w.shape[1] // batch, w.shape[2]), lambda b: (j, b, 0))
    slab_out = lambda w: pl.BlockSpec((w.shape[1] // batch, w.shape[2]), lambda b: (b, 0))
    lc = nkc.shape[1]
    assert WA_BLOCK == 2 * GRID_W and (seq // WA_BLOCK) % 2 == 0
    per_b = lambda n, w: pl.BlockSpec((None, n, w), lambda b: (b, 0, 0))
    na_scratch = pltpu.VMEM((lc + NA_WIN_ROWS * GRID_W + SUBLANES, NA_HEADS * GRID_W), F32)
    vt_scratch = pltpu.VMEM((2, NA_WIDTH, seq), BF16)
    vct_scratch = pltpu.VMEM((NA_WIDTH, lc), BF16)
    wa_scratch = pltpu.VMEM((WA_KV_HEADS, lc + WA_BAND + SUBLANES, WA_GROUP * WA_BLOCK), F32)
    outs = pl.pallas_call(
        functools.partial(_attn_kernel, n_cast=len(to_bf16)),
        out_shape=[jax.ShapeDtypeStruct((batch, seq, NA_WIDTH), BF16),
                   jax.ShapeDtypeStruct((batch, seq, WA_QW), BF16),
                   *[jax.ShapeDtypeStruct(w.shape[1:], BF16) for w, _ in to_bf16]],
        grid=(batch,),
        in_specs=[pl.BlockSpec(memory_space=pltpu.SMEM),
                  per_b(seq, NA_WIDTH), per_b(seq, NA_WIDTH), per_b(seq, NA_WIDTH),
                  per_b(lc, NA_WIDTH), per_b(lc, NA_WIDTH),
                  pl.BlockSpec((None,) + bias.shape[1:], lambda b: (layer, 0, 0, 0)),
                  pl.BlockSpec((1, NA_WIDTH), lambda b: (0, 0)),
                  per_b(seq, WA_QW), per_b(seq, WA_K2W), per_b(seq, WA_V3W),
                  per_b(lc, WA_K2W), per_b(lc, WA_V3W),
                  pl.BlockSpec((1, WA_QW), lambda b: (0, 0)),
                  *[slab_in(w, j) for w, j in to_bf16]],
        out_specs=[per_b(seq, NA_WIDTH), per_b(seq, WA_QW), *[slab_out(w) for w, _ in to_bf16]],
        scratch_shapes=[na_scratch, na_scratch, wa_scratch, wa_scratch, vt_scratch, vct_scratch],
        compiler_params=_params(),
        name="attention",
    )(sink, nq, nk, nv, nkc, nvc, bias, gain_na, wq, wk2, wv3, wkc2, wvc3, gain_wa, *[w for w, _ in to_bf16])
    return outs[0], outs[1], outs[2:]


def _ctx_attn_kernel(sink_ref, qn_ref, kn_ref, vn_ref, qw_ref, kw_ref, vw_ref, gn_ref, gw_ref, on_ref, ow_ref):
    qs = _stack_heads(qn_ref[...], NA_HEADS)
    o4 = _softmax_pv([(_dot_nt(qs, kn_ref[...]), vn_ref[...])])
    on_ref[...] = _group_rms(_unstack_heads(o4, NA_HEADS), gn_ref[...]).astype(on_ref.dtype)
    outs = []
    for kh in range(WA_KV_HEADS):
        lanes = slice(kh * LANES, (kh + 1) * LANES)
        qg = qw_ref[:, kh * WA_GROUP * HEAD_DIM:(kh + 1) * WA_GROUP * HEAD_DIM]
        sinks = [sink_ref[kh * WA_GROUP + g] for g in range(WA_GROUP)]
        outs.append(_wa_finish(_wa_scores(qg, [(kw_ref[:, lanes], None)]), [vw_ref[:, lanes]], sinks))
    ow_ref[...] = _group_rms(jnp.concatenate(outs, axis=1), gw_ref[...]).astype(ow_ref.dtype)


def _ctx_attention(sink, qn, kn, vn, qw, kw2, vw2, gain_na, gain_wa):
    batch, lc, _ = qn.shape
    per_b = lambda w: pl.BlockSpec((None, lc, w), lambda b: (b, 0, 0))
    return pl.pallas_call(
        _ctx_attn_kernel,
        out_shape=[jax.ShapeDtypeStruct((batch, lc, NA_WIDTH), BF16),
                   jax.ShapeDtypeStruct((batch, lc, WA_QW), BF16)],
        grid=(batch,),
        in_specs=[pl.BlockSpec(memory_space=pltpu.SMEM),
                  per_b(NA_WIDTH), per_b(NA_WIDTH), per_b(NA_WIDTH),
                  per_b(WA_QW), per_b(WA_K2W), per_b(WA_V3W),
                  pl.BlockSpec((1, NA_WIDTH), lambda b: (0, 0)),
                  pl.BlockSpec((1, WA_QW), lambda b: (0, 0))],
        out_specs=[per_b(NA_WIDTH), per_b(WA_QW)],
        compiler_params=_params(),
        name="ctx_attention",
    )(sink, qn, kn, vn, qw, kw2, vw2, gain_na, gain_wa)


def _out_mlp_kernel(x_ref, mod_ref, na_ref, wa_ref, u_ref, uprev_ref, unext_ref, cvb_ref, cw_ref, cb_ref, g_ref,
                    g2_ref, wo_ref, *rest, seq_len, sub):
    n_ffn = FFN_DIM // FFN_CHUNK
    w1_refs, w2_refs, (o_ref,) = rest[:n_ffn], rest[n_ffn:2 * n_ffn], rest[2 * n_ffn:]
    tm = x_ref.shape[0]
    i = pl.program_id(0)
    u = u_ref[...]
    at_start = (i * tm) % seq_len == 0
    at_end = ((i + 1) * tm) % seq_len == 0
    prev_row = jnp.where(at_start, 0.0, uprev_ref[7:8, :])
    next_row = jnp.where(at_end, 0.0, unext_ref[0:1, :])
    row = _iota(u.shape, 0)
    up = jnp.where(row == 0, prev_row, pltpu.roll(u, 1, axis=0))
    dn = jnp.where(row == tm - 1, next_row, pltpu.roll(u, tm - 1, axis=0))
    y = cb_ref[...] + cw_ref[0:1, :] * up + cw_ref[1:2, :] * u + cw_ref[2:3, :] * dn
    cv = _group_rms(cvb_ref[...] * y, g_ref[...]).astype(BF16)
    for r0 in range(0, tm, sub):
        rows = slice(r0, r0 + sub)
        mixed = jnp.concatenate([na_ref[rows, :], cv[rows, :], wa_ref[rows, :]], axis=1)
        x = x_ref[rows, :] + mod_ref[2] * _dot(mixed, wo_ref[...])
        ms = jnp.mean(x * x, axis=-1, keepdims=True)
        h = x * lax.rsqrt(ms + EPS) * g2_ref[...]
        h = (h * (1.0 + mod_ref[4]) + mod_ref[3]).astype(BF16)
        acc = None
        for w1_ref, w2_ref in zip(w1_refs, w2_refs):
            a = jnp.maximum(_dot(h, w1_ref[...]), 0.0)
            part = _dot((a * a).astype(BF16), w2_ref[...])
            acc = part if acc is None else acc + part
        o_ref[rows, :] = x + mod_ref[5] * acc


def _out_mlp(x2d, mod, layer, mod_row, na_n, wa_n, cv_u, cv_b, conv_w, conv_b, gain_cv, w_o, g2, w1, w2, seq_len, tm):
    rows = x2d.shape[0]
    halo = 8
    per_tile = tm // halo
    last = rows // halo - 1
    row_spec = lambda w: pl.BlockSpec((tm, w), lambda i: (i, 0))
    const = lambda shape: pl.BlockSpec(shape, lambda i: (0,) * len(shape), pipeline_mode=pl.Buffered(1))
    once = pl.Buffered(1)
    n_ffn = FFN_DIM // FFN_CHUNK
    wo_spec = pl.BlockSpec((D_MODEL, D_MODEL), lambda i: (0, 0), pipeline_mode=once)
    w1_specs = [pl.BlockSpec((D_MODEL, FFN_CHUNK), functools.partial(lambda j, i: (0, j), j),
                             pipeline_mode=once) for j in range(n_ffn)]
    w2_specs = [pl.BlockSpec((FFN_CHUNK, D_MODEL), functools.partial(lambda j, i: (j, 0), j),
                             pipeline_mode=once) for j in range(n_ffn)]
    return pl.pallas_call(
        functools.partial(_out_mlp_kernel, seq_len=seq_len, sub=min(tm, MLP_SUB)),
        out_shape=jax.ShapeDtypeStruct((rows, D_MODEL), F32),
        grid=(rows // tm,),
        in_specs=[
            row_spec(D_MODEL),
            pl.BlockSpec((None, None, 6, 1, D_MODEL), lambda i: (layer, mod_row(i), 0, 0, 0)),
            row_spec(NA_WIDTH), row_spec(WA_QW), row_spec(CONV_CH),
            pl.BlockSpec((halo, CONV_CH), lambda i: (jnp.maximum(i * per_tile - 1, 0), 0)),
            pl.BlockSpec((halo, CONV_CH), lambda i: (jnp.minimum((i + 1) * per_tile, last), 0)),
            row_spec(CONV_CH),
            const((3, CONV_CH)), const((1, CONV_CH)), const((1, CONV_CH)),
            const((1, D_MODEL)),
            wo_spec, *w1_specs, *w2_specs,
        ],
        out_specs=row_spec(D_MODEL),
        compiler_params=_params(),
        name="out_mlp",
    )(x2d, mod, na_n, wa_n, cv_u, cv_u, cv_u, cv_b, conv_w, conv_b, gain_cv, g2,
      w_o, *([w1] * n_ffn), *([w2] * n_ffn))


def _rope_tables(seq):
    quarter = HEAD_DIM // 4
    inv = ROPE_BASE ** (-jnp.arange(quarter, dtype=F32) / quarter)
    t = jnp.arange(seq)
    ang_r = (t // GRID_W).astype(F32)[:, None] * inv[None, :]
    ang_c = (t % GRID_W).astype(F32)[:, None] * inv[None, :]
    cos = jnp.concatenate([jnp.cos(ang_r), jnp.cos(ang_r), jnp.cos(ang_c), jnp.cos(ang_c)], axis=1)
    sin = jnp.concatenate([-jnp.sin(ang_r), jnp.sin(ang_r), -jnp.sin(ang_c), jnp.sin(ang_c)], axis=1)
    reps = LANES // HEAD_DIM
    return jnp.tile(cos, (1, reps)), jnp.tile(sin, (1, reps))


def kernel(x, c, ctx, c_ctx, w_mod, b_mod, g_norm1, g_norm2, w_in, na_q_gain, na_k_gain, na_rpb,
           conv_w, conv_bias, wa_q_gain, wa_k_gain, wa_sink, g_out, w_o, w_fc1, w_fc2):
    batch, seq, d = x.shape
    lc = ctx.shape[1]
    depth = w_mod.shape[0]
    assert d == D_MODEL and seq % WA_BLOCK == 0 and seq % GRID_W == 0 and batch < MOD_ROWS
    tm = 1024
    tm_in = 1024
    tm_ctx = 256
    assert seq % tm == 0 and seq % tm_in == 0 and lc % tm_ctx == 0

    cond = jnp.zeros((MOD_ROWS, d), F32).at[:batch].set(c).at[batch].set(c_ctx)
    mod = _modulation(cond, w_mod, b_mod)
    rpb_bias = _rpb_tables(na_rpb)
    rope_tabs = _rope_tables(seq)

    lat_row = lambda i: i // (seq // tm)
    lat_row_in = lambda i: i // (seq // tm_in)
    ctx_row = lambda i: batch

    w_in_bf = w_in[0].astype(BF16)
    xs = x.reshape(batch * seq, d)
    cs = ctx.reshape(batch * lc, d)
    for l in range(depth):
        last = l == depth - 1
        g1 = g_norm1[l].reshape(1, d)
        g2 = g_norm2[l].reshape(1, d)
        gains = [jnp.tile(na_q_gain[l], NA_HEADS).reshape(1, -1), jnp.tile(na_k_gain[l], NA_HEADS).reshape(1, -1),
                 jnp.tile(wa_q_gain[l], WA_HEADS).reshape(1, -1), jnp.tile(wa_k_gain[l], WA_KV_HEADS).reshape(1, -1)]
        go_na = g_out[l, :NA_WIDTH].reshape(1, -1)
        go_cv = g_out[l, NA_WIDTH:NA_WIDTH + CONV_CH].reshape(1, -1)
        go_wa = g_out[l, NA_WIDTH + CONV_CH:].reshape(1, -1)
        cb = conv_bias[l].reshape(1, -1)

        naq, nak, nav, cvu, cvb, waq, wak, wav = _in_proj(xs, mod, l, lat_row_in, g1, w_in_bf, gains, rope_tabs, tm_in)
        if last:
            cnak, cnav, cwak, cwav = _in_proj(cs, mod, l, ctx_row, g1, w_in_bf, gains, None, tm_ctx, kv_only=True)
        else:
            cnaq, cnak, cnav, ccvu, ccvb, cwaq, cwak, cwav = _in_proj(cs, mod, l, ctx_row, g1, w_in_bf, gains, None, tm_ctx)

        b3 = lambda a, n: a.reshape(batch, n, a.shape[-1])
        casts = [(w_o, l), (w_fc1, l), (w_fc2, l)] + ([] if last else [(w_in, l + 1)])
        na_n, wa_n, cast_out = _attention(
            wa_sink[l], b3(naq, seq), b3(nak, seq), b3(nav, seq), b3(cnak, lc), b3(cnav, lc), rpb_bias, l, go_na,
            b3(waq, seq), b3(wak, seq), b3(wav, seq), b3(cwak, lc), b3(cwav, lc), go_wa, casts)
        w_o_bf, w1_bf, w2_bf = cast_out[:3]
        if not last:
            w_in_bf = cast_out[3]
        xs = _out_mlp(xs, mod, l, lat_row, na_n.reshape(batch * seq, -1), wa_n.reshape(batch * seq, -1),
                      cvu, cvb, conv_w[l], cb, go_cv, w_o_bf, g2, w1_bf, w2_bf, seq, tm)
        if not last:
            cna_n, cwa_n = _ctx_attention(wa_sink[l], b3(cnaq, lc), b3(cnak, lc), b3(cnav, lc),
                                          b3(cwaq, lc), b3(cwak, lc), b3(cwav, lc), go_na, go_wa)
            cs = _out_mlp(cs, mod, l, ctx_row, cna_n.reshape(batch * lc, -1), cwa_n.reshape(batch * lc, -1),
                          ccvu, ccvb, conv_w[l], cb, go_cv, w_o_bf, g2, w1_bf, w2_bf, lc, tm_ctx)
    return xs.reshape(batch, seq, d)
```

```python
import functools

import jax
import jax.numpy as jnp
from jax import lax
from jax.experimental import pallas as pl
from jax.experimental.pallas import tpu as pltpu

D_MODEL = 1024
GRID_W = 64
HEAD_DIM = 64
NA_HEADS = 4
NA_WIDTH = NA_HEADS * HEAD_DIM
CONV_CH = 256
WA_HEADS = 8
WA_KV_HEADS = 2
WA_GROUP = WA_HEADS // WA_KV_HEADS
WA_QW = WA_HEADS * HEAD_DIM
WA_KVW = WA_KV_HEADS * HEAD_DIM
WA_K2W = 2 * WA_KVW
WA_V3W = 2 * WA_KVW
NA_WIN_ROWS = 8
NA_WIN_COLS = 16
NA_DROWS = 2 * NA_WIN_ROWS - 1
NA_DCOLS = 2 * NA_WIN_COLS - 1
WA_WINDOW = 128
WA_BLOCK = 128
WA_BAND = 3 * WA_BLOCK
FFN_DIM = 4 * D_MODEL
FFN_CHUNK = 1024
ROPE_BASE = 10000.0
EPS = 1e-6
NEG_INF = -1e30
IN_WIDTH = 2304
OFF_NA_Q, OFF_NA_K, OFF_NA_V = 0, 256, 512
OFF_CV_X, OFF_CV_B, OFF_CV_C = 768, 1024, 1280
OFF_WA_Q, OFF_WA_K, OFF_WA_V = 1536, 2048, 2176
MOD_ROWS = 16
MOD_TN = 2048
IN_SUB = 256
MLP_SUB = 512
LANES = 128
MXU_DIM = 256
VMEM_LIMIT = 60 * 1024 * 1024
LOG2E = 1.4426950408889634
Q_SCALE = LOG2E * HEAD_DIM ** -0.5

F32 = jnp.float32
BF16 = jnp.bfloat16


def _dot(a, b):
    return jnp.dot(a, b, preferred_element_type=F32)


def _dot_nt(a, b):
    return lax.dot_general(a, b, (((1,), (1,)), ((), ())), preferred_element_type=F32)


def _split_bf16(a):
    hi = a.astype(BF16)
    lo = (a - hi.astype(F32)).astype(BF16)
    return hi, lo


def _iota(shape, dim):
    return lax.broadcasted_iota(jnp.int32, shape, dim)


def _params(**kw):
    return pltpu.CompilerParams(vmem_limit_bytes=VMEM_LIMIT, **kw)


def _mod_kernel(cond_ref, w_ref, b_ref, o_ref):
    a = cond_ref[...]
    a = a * (1.0 / (1.0 + jnp.exp(-a)))
    ah, al = _split_bf16(a)
    wh, wl = _split_bf16(w_ref[...])
    res = _dot(ah, wh) + _dot(al, wh) + _dot(ah, wl) + b_ref[...]
    for c in range(o_ref.shape[1]):
        o_ref[:, c, 0, :] = res[:, c * D_MODEL:(c + 1) * D_MODEL]


def _modulation(cond, w_mod, b_mod):
    depth = w_mod.shape[0]
    n_out = w_mod.shape[2]
    per_tile = MOD_TN // D_MODEL
    return pl.pallas_call(
        _mod_kernel,
        out_shape=jax.ShapeDtypeStruct((depth, MOD_ROWS, n_out // D_MODEL, 1, D_MODEL), F32),
        grid=(depth, n_out // MOD_TN),
        in_specs=[
            pl.BlockSpec((MOD_ROWS, D_MODEL), lambda l, j: (0, 0)),
            pl.BlockSpec((None, D_MODEL, MOD_TN), lambda l, j: (l, 0, j)),
            pl.BlockSpec((None, 1, MOD_TN), lambda l, j: (l, 0, j)),
        ],
        out_specs=pl.BlockSpec((None, MOD_ROWS, per_tile, 1, D_MODEL), lambda l, j: (l, 0, j, 0, 0)),
        compiler_params=_params(),
        name="modulation",
    )(cond, w_mod, b_mod.reshape(depth, 1, n_out))


def _rpb_kernel(rpb_ref, o_ref):
    shape = (GRID_W, LANES)
    q = _iota(shape, 0)
    lane = _iota(shape, 1)
    k = lane % GRID_W
    left = lane < GRID_W
    col_start = jnp.clip(q - NA_WIN_COLS // 2, 0, GRID_W - NA_WIN_COLS)
    col_ok = (k >= col_start) & (k < col_start + NA_WIN_COLS)
    centre = NA_WIN_COLS - 1
    for d in range(NA_DROWS - 1):
        tiles = []
        for h in range(NA_HEADS):
            t1 = pltpu.roll(jnp.broadcast_to(rpb_ref[h, d:d + 1, :], shape), LANES - centre, axis=1,
                            stride=1, stride_axis=0)
            t2 = pltpu.roll(jnp.broadcast_to(rpb_ref[h, d + 1:d + 2, :], shape), GRID_W - centre, axis=1,
                            stride=1, stride_axis=0)
            t = jnp.where(left, t1, t2)
            tiles.append(jnp.where(col_ok, t * LOG2E, NEG_INF))
        o_ref[d] = jnp.concatenate(tiles, axis=0).T


def _rpb_tables(na_rpb):
    depth = na_rpb.shape[0]
    padded = jnp.pad(na_rpb, ((0, 0), (0, 0), (0, 1), (0, LANES - NA_DCOLS)))
    return pl.pallas_call(
        _rpb_kernel,
        out_shape=jax.ShapeDtypeStruct((depth, NA_DROWS - 1, LANES, NA_HEADS * GRID_W), F32),
        grid=(depth,),
        in_specs=[pl.BlockSpec((None,) + padded.shape[1:], lambda l: (l, 0, 0, 0))],
        out_specs=pl.BlockSpec((None, NA_DROWS - 1, LANES, NA_HEADS * GRID_W), lambda l: (l, 0, 0, 0)),
        compiler_params=_params(),
        name="rpb_tables",
    )(padded)


def _head_rms(t, gain):
    width = t.shape[1]
    sq = (t * t).astype(BF16)
    blk = min(width, MXU_DIM)
    ones = jnp.where(_iota((blk, blk), 0) // HEAD_DIM == _iota((blk, blk), 1) // HEAD_DIM, 1.0, 0.0).astype(BF16)
    sums = [_dot(sq[:, c:c + blk], ones) for c in range(0, width, blk)]
    ss = sums[0] if len(sums) == 1 else jnp.concatenate(sums, axis=1)
    return t * lax.rsqrt(ss * (1.0 / HEAD_DIM) + EPS) * gain


def _rope(t, cos, sin):
    lane = _iota((t.shape[0], LANES), 1)
    first = (lane % (HEAD_DIM // 2)) < (HEAD_DIM // 4)
    out = []
    for c in range(0, t.shape[1], LANES):
        u = t[:, c:c + LANES]
        partner = jnp.where(first, pltpu.roll(u, LANES - HEAD_DIM // 4, axis=1), pltpu.roll(u, HEAD_DIM // 4, axis=1))
        out.append(u * cos + partner * sin)
    return out[0] if len(out) == 1 else jnp.concatenate(out, axis=1)


def _dup_kv(t):
    lane = _iota(t.shape, 1)
    swapped = pltpu.roll(t, HEAD_DIM, axis=1)
    left = lane < HEAD_DIM
    return jnp.concatenate([jnp.where(left, t, swapped), jnp.where(left, swapped, t)], axis=1)


def _inproj_kernel(*refs, rope, kv_only, n_w, sub):
    x_ref, mod_ref, g1_ref = refs[:3]
    w_refs = refs[3:3 + n_w]
    gnq_ref, gnk_ref, gwq_ref, gwk_ref = refs[3 + n_w:7 + n_w]
    rest = refs[7 + n_w:]
    if rope:
        cos_ref, sin_ref = rest[:2]
        rest = rest[2:]
    out_refs = rest
    if kv_only:
        nak_ref, nav_ref, wak_ref, wav_ref = out_refs
        o_nk, o_nv, o_wk, o_wv, o_end = 0, NA_WIDTH, 2 * NA_WIDTH, 2 * NA_WIDTH + WA_KVW, 2 * NA_WIDTH + 2 * WA_KVW
    else:
        naq_ref, nak_ref, nav_ref, cvu_ref, cvb_ref, waq_ref, wak_ref, wav_ref = out_refs
        o_nk, o_nv, o_wk, o_wv, o_end = OFF_NA_K, OFF_NA_V, OFF_WA_K, OFF_WA_V, IN_WIDTH
    def project(rows):
        x = x_ref[rows, :]
        ms = jnp.mean(x * x, axis=-1, keepdims=True)
        h = x * lax.rsqrt(ms + EPS) * g1_ref[...]
        h = h * (1.0 + mod_ref[1]) + mod_ref[0]
        hb = h.astype(BF16)
        return jnp.concatenate([_dot(hb, w_ref[...]) for w_ref in w_refs], axis=1)

    def finish(rows, p):
        nak_ref[rows, :] = _head_rms(p[:, o_nk:o_nk + NA_WIDTH], gnk_ref[...]).astype(BF16)
        nav_ref[rows, :] = p[:, o_nv:o_nv + NA_WIDTH].astype(BF16)
        wk = _head_rms(p[:, o_wk:o_wv], gwk_ref[...])
        if rope:
            wk = _rope(wk, cos_ref[rows, :], sin_ref[rows, :])
        wak_ref[rows, :] = _dup_kv(wk).astype(BF16)
        vv = p[:, o_wv:o_end]
        left = _iota(vv.shape, 1) < HEAD_DIM
        wav_ref[rows, :] = jnp.concatenate([jnp.where(left, vv, 1.0), jnp.where(left, pltpu.roll(vv, HEAD_DIM, axis=1), 1.0)],
                                           axis=1).astype(BF16)
        if kv_only:
            return
        naq_ref[rows, :] = (_head_rms(p[:, OFF_NA_Q:OFF_NA_K], gnq_ref[...]) * Q_SCALE).astype(BF16)
        cvu_ref[rows, :] = p[:, OFF_CV_C:OFF_WA_Q] * p[:, OFF_CV_X:OFF_CV_B]
        cvb_ref[rows, :] = p[:, OFF_CV_B:OFF_CV_C]
        wq = _head_rms(p[:, OFF_WA_Q:OFF_WA_K], gwq_ref[...])
        if rope:
            wq = _rope(wq, cos_ref[rows, :], sin_ref[rows, :])
        waq_ref[rows, :] = (wq * Q_SCALE).astype(BF16)

    for r0 in range(0, x_ref.shape[0], sub):
        rows = slice(r0, r0 + sub)
        finish(rows, project(rows))


def _in_proj(x2d, mod, layer, mod_row, g1, w_in, gains, rope_tabs, tm, kv_only=False):
    rows = x2d.shape[0]
    rope = rope_tabs is not None
    row_spec = lambda w: pl.BlockSpec((tm, w), lambda i: (i, 0))
    const = lambda shape: pl.BlockSpec(shape, lambda i: (0,) * len(shape))
    once = pl.Buffered(1)
    if kv_only:
        col_blocks = [OFF_NA_K // MXU_DIM, OFF_NA_V // MXU_DIM, OFF_WA_K // MXU_DIM]
        w_specs = [pl.BlockSpec((D_MODEL, MXU_DIM), functools.partial(lambda j, i: (0, j), j),
                                pipeline_mode=once) for j in col_blocks]
        widths = [(NA_WIDTH, BF16), (NA_WIDTH, BF16), (WA_K2W, BF16), (WA_V3W, BF16)]
    else:
        w_specs = [pl.BlockSpec((D_MODEL, IN_WIDTH), lambda i: (0, 0), pipeline_mode=once)]
        widths = [(NA_WIDTH, BF16), (NA_WIDTH, BF16), (NA_WIDTH, BF16), (CONV_CH, F32), (CONV_CH, F32),
                  (WA_QW, BF16), (WA_K2W, BF16), (WA_V3W, BF16)]
    n_w = len(w_specs)
    in_specs = [
        row_spec(D_MODEL),
        pl.BlockSpec((None, None, 6, 1, D_MODEL), lambda i: (layer, mod_row(i), 0, 0, 0)),
        const((1, D_MODEL)),
        *w_specs,
        const((1, NA_WIDTH)), const((1, NA_WIDTH)), const((1, WA_QW)), const((1, WA_KVW)),
    ]
    args = [x2d, mod, g1, *([w_in] * n_w), *gains]
    if rope:
        seq_tiles = rope_tabs[0].shape[0] // tm
        in_specs += [pl.BlockSpec((tm, LANES), lambda i: (i % seq_tiles, 0))] * 2
        args += list(rope_tabs)
    return pl.pallas_call(
        functools.partial(_inproj_kernel, rope=rope, kv_only=kv_only, n_w=n_w, sub=min(tm, IN_SUB)),
        out_shape=[jax.ShapeDtypeStruct((rows, w), dt) for w, dt in widths],
        grid=(rows // tm,),
        in_specs=in_specs,
        out_specs=[row_spec(w) for w, _ in widths],
        compiler_params=_params(),
        name="in_proj_rope" if rope else ("in_proj_ctx_kv" if kv_only else "in_proj_ctx"),
    )(*args)


def _lane_chunks(s):
    return [s[:, c:c + LANES] for c in range(0, s.shape[1], LANES)]


def _row_max(scores, extra=None):
    chunks = [ch for s in scores for ch in _lane_chunks(s)]
    if extra is not None:
        chunks.append(extra)
    folded = functools.reduce(jnp.maximum, chunks)
    return jnp.broadcast_to(jnp.max(folded, axis=-1, keepdims=True), folded.shape)


def _softmax_pv(parts, extra=None, sums_from_values=False, m=None):
    lane_chunks = _lane_chunks
    if m is None:
        m = _row_max([s for s, _ in parts], extra)
    ps = [jnp.concatenate([jnp.exp2(ch - m) for ch in lane_chunks(s)], axis=1) for s, _ in parts]
    pv = _dot(jnp.concatenate([p.astype(BF16) for p in ps], axis=1),
              jnp.concatenate([v for _, v in parts], axis=0))
    if sums_from_values:
        denom = pv[:, -LANES:]
        pv = pv[:, :-LANES]
        if extra is not None:
            denom = denom + jnp.exp2(extra - m)
        return pv / denom
    psum = functools.reduce(jnp.add, [ch for p in ps for ch in lane_chunks(p)])
    if extra is not None:
        psum = psum + jnp.where(_iota(extra.shape, 1) == 0, jnp.exp2(extra - m), 0.0)
    return pv / jnp.sum(psum, axis=-1, keepdims=True)


def _stack_heads(q, n_heads):
    head = _iota(q.shape, 1) // HEAD_DIM
    zero = jnp.zeros_like(q)
    return jnp.concatenate([jnp.where(head == h, q, zero) for h in range(n_heads)], axis=0)


def _unstack_heads(o, n_heads):
    rows = o.shape[0] // n_heads
    head = _iota((rows, o.shape[1]), 1) // HEAD_DIM
    out = jnp.where(head == 0, o[:rows], 0.0)
    for h in range(1, n_heads):
        out = jnp.where(head == h, o[h * rows:(h + 1) * rows], out)
    return out


def _group_rms(o, gain):
    ms = jnp.mean(o * o, axis=-1, keepdims=True)
    return o * lax.rsqrt(ms + EPS) * gain


SUBLANES = 8


def _wa_scores(qg, keys):
    rows = qg.shape[0]
    left = _iota((rows, LANES), 1) < HEAD_DIM
    zero = jnp.zeros((rows, LANES), qg.dtype)
    blocks = []
    for pair in range(WA_GROUP // 2):
        qp = qg[:, pair * LANES:(pair + 1) * LANES]
        blocks += [jnp.where(left, qp, zero), jnp.where(left, zero, qp)]
    qs = jnp.concatenate(blocks, axis=0)
    scores = []
    for k2, bias in keys:
        s = _dot_nt(k2, qs)
        if bias is not None:
            s = s + jnp.concatenate([bias] * WA_GROUP, axis=1)
        scores.append(s)
    return scores


def _wa_sink_rows(rows, sinks):
    return jnp.concatenate([jnp.full((SUBLANES, rows), sinks[g] * LOG2E, F32) for g in range(WA_GROUP)], axis=1)


def _wa_col_max(scores, sink_rows):
    m = sink_rows[:1]
    for s in scores:
        m = jnp.maximum(m, jnp.max(s, axis=0, keepdims=True))
    return jnp.broadcast_to(m, sink_rows.shape)


def _wa_finish(scores, values, sinks, m=None):
    rows = scores[0].shape[1] // WA_GROUP
    sink_rows = _wa_sink_rows(rows, sinks)
    if m is None:
        m = _wa_col_max(scores, sink_rows)
    p = jnp.concatenate([jnp.exp2(s - m[:1]).astype(BF16) for s in scores], axis=0)
    v = jnp.concatenate(values, axis=0)
    ot = lax.dot_general(v, p, (((0,), (0,)), ((), ())), preferred_element_type=F32)
    denom = ot[HEAD_DIM:HEAD_DIM + SUBLANES] + jnp.exp2(sink_rows - m)
    on = ot[:HEAD_DIM] / denom[:1]
    pairs = [jnp.concatenate([on[:, (2 * pr) * rows:(2 * pr + 1) * rows],
                              on[:, (2 * pr + 1) * rows:(2 * pr + 2) * rows]], axis=0).T
             for pr in range(WA_GROUP // 2)]
    return jnp.concatenate(pairs, axis=1)


def _na_steps(q_ref, k_ref, v_ref, kc_ref, vc_ref, bias_ref, g_ref, o_ref, vt_ref, vct_ref):
    seq = q_ref.shape[0]
    n_rows = seq // GRID_W
    win = NA_WIN_ROWS * GRID_W
    lc = kc_ref.shape[0]
    gain = g_ref[...]

    def prepare():
        vt_ref[0] = v_ref[...].astype(F32).T.astype(BF16)
        vt_ref[1, :, :seq - GRID_W] = v_ref[GRID_W:, :].astype(F32).T.astype(BF16)
        vct_ref[...] = vc_ref[...].astype(F32).T.astype(BF16)

    def window(r):
        start = jnp.clip(r - NA_WIN_ROWS // 2, 0, n_rows - NA_WIN_ROWS)
        return start, start - r + (NA_WIN_ROWS - 1)

    def scores(r, s_ref):
        start, d0 = window(r)
        tok0 = pl.multiple_of(start * GRID_W, GRID_W)
        q = q_ref[pl.ds(pl.multiple_of(r * GRID_W, GRID_W), GRID_W), :]
        qs = _stack_heads(q, NA_HEADS)
        bias = jnp.concatenate([bias_ref[d0 + 2 * j] for j in range(NA_WIN_ROWS // 2)], axis=0)
        s_ctx = _dot_nt(kc_ref[...], qs)
        s_loc = _dot_nt(k_ref[pl.ds(tok0, win), :], qs) + bias
        m = jnp.maximum(jnp.max(s_ctx, axis=0, keepdims=True), jnp.max(s_loc, axis=0, keepdims=True))
        s_ref[:lc, :] = s_ctx
        s_ref[lc:lc + win, :] = s_loc
        s_ref[lc + win:, :] = jnp.broadcast_to(m, (SUBLANES, m.shape[1]))

    def finish(r, s_ref):
        start, _ = window(r)
        odd = start % 2
        lane0 = pl.multiple_of((start - odd) * GRID_W, LANES)
        m = s_ref[lc + win:lc + win + 1, :]
        p_ctx = jnp.exp2(s_ref[:lc, :] - m)
        p_loc = jnp.exp2(s_ref[lc:lc + win, :] - m)
        denom = jnp.sum(p_ctx, axis=0, keepdims=True) + jnp.sum(p_loc, axis=0, keepdims=True)
        ot = _dot(vct_ref[...], p_ctx.astype(BF16)) + _dot(vt_ref[odd, :, pl.ds(lane0, win)], p_loc.astype(BF16))
        o4 = (ot * (1.0 / denom)).T
        o = _unstack_heads(o4, NA_HEADS)
        o_ref[pl.ds(pl.multiple_of(r * GRID_W, GRID_W), GRID_W), :] = _group_rms(o, gain).astype(o_ref.dtype)

    def step(r, cur_ref, nxt_ref):
        scores(jnp.minimum(r + 1, n_rows - 1), nxt_ref)
        finish(r, cur_ref)

    return prepare, scores, step


def _wa_steps(sink_ref, q_ref, k_ref, v_ref, kc_ref, vc_ref, g_ref, o_ref):
    seq = q_ref.shape[0]
    n_blocks = seq // WA_BLOCK
    lc = kc_ref.shape[0]
    gain = g_ref[...]
    key = _iota((WA_BAND, WA_BLOCK), 0)
    qry = _iota((WA_BAND, WA_BLOCK), 1)

    def window(n):
        q0 = pl.multiple_of(n * WA_BLOCK, WA_BLOCK)
        k0 = pl.multiple_of(jnp.clip(q0 - WA_BLOCK, 0, seq - WA_BAND), WA_BLOCK)
        return q0, k0

    def scores(n, kh, s_ref):
        q0, k0 = window(n)
        band = jnp.where(jnp.abs((k0 + key) - (q0 + qry)) <= WA_WINDOW, 0.0, NEG_INF)
        lanes = slice(kh * LANES, (kh + 1) * LANES)
        qg = q_ref[pl.ds(q0, WA_BLOCK), kh * WA_GROUP * HEAD_DIM:(kh + 1) * WA_GROUP * HEAD_DIM]
        s_ctx, s_loc = _wa_scores(qg, [(kc_ref[:, lanes], None), (k_ref[pl.ds(k0, WA_BAND), lanes], band)])
        s_ref[kh, :lc, :] = s_ctx
        s_ref[kh, lc:lc + WA_BAND, :] = s_loc
        s_ref[kh, lc + WA_BAND:, :] = _wa_col_max([s_ctx, s_loc], _wa_sink_rows(WA_BLOCK, sinks(kh)))

    def sinks(kh):
        return [sink_ref[kh * WA_GROUP + g] for g in range(WA_GROUP)]

    def finish(n, kh, s_ref):
        _, k0 = window(n)
        lanes = slice(kh * LANES, (kh + 1) * LANES)
        return _wa_finish([s_ref[kh, :lc, :], s_ref[kh, lc:lc + WA_BAND, :]],
                          [vc_ref[:, lanes], v_ref[pl.ds(k0, WA_BAND), lanes]], sinks(kh),
                          m=s_ref[kh, lc + WA_BAND:, :])

    def step(n, cur_ref, nxt_ref):
        nxt = jnp.minimum(n + 1, n_blocks - 1)
        for kh in range(WA_KV_HEADS):
            scores(nxt, kh, nxt_ref)
        o = jnp.concatenate([finish(n, kh, cur_ref) for kh in range(WA_KV_HEADS)], axis=1)
        q0, _ = window(n)
        o_ref[pl.ds(q0, WA_BLOCK), :] = _group_rms(o, gain).astype(o_ref.dtype)

    return scores, step


def _attn_kernel(sink_ref, nq_ref, nk_ref, nv_ref, nkc_ref, nvc_ref, bias_ref, gn_ref,
                 wq_ref, wk_ref, wv_ref, wkc_ref, wvc_ref, gw_ref, *rest, n_cast):
    f32_refs = rest[:n_cast]
    on_ref, ow_ref = rest[n_cast:n_cast + 2]
    bf16_refs = rest[n_cast + 2:2 * n_cast + 2]
    nsa_ref, nsb_ref, wsa_ref, wsb_ref, vt_ref, vct_ref = rest[2 * n_cast + 2:]
    for src_ref, dst_ref in zip(f32_refs, bf16_refs):
        dst_ref[...] = src_ref[...].astype(dst_ref.dtype)
    na_prepare, na_scores, na_step = _na_steps(nq_ref, nk_ref, nv_ref, nkc_ref, nvc_ref, bias_ref, gn_ref, on_ref,
                                               vt_ref, vct_ref)
    na_prepare()
    wa_scores, wa_step = _wa_steps(sink_ref, wq_ref, wk_ref, wv_ref, wkc_ref, wvc_ref, gw_ref, ow_ref)
    n_blocks = wq_ref.shape[0] // WA_BLOCK

    for kh in range(WA_KV_HEADS):
        wa_scores(0, kh, wsa_ref)
    na_scores(0, nsa_ref)

    def body(i, carry):
        for half, (w_cur, w_nxt) in enumerate(((wsa_ref, wsb_ref), (wsb_ref, wsa_ref))):
            n = 2 * i + half
            wa_step(n, w_cur, w_nxt)
            na_step(2 * n, nsa_ref, nsb_ref)
            na_step(2 * n + 1, nsb_ref, nsa_ref)
        return carry

    lax.fori_loop(0, n_blocks // 2, body, 0, unroll=2)


def _attention(sink, nq, nk, nv, nkc, nvc, bias, layer, gain_na, wq, wk2, wv3, wkc2, wvc3, gain_wa, to_bf16):
    batch, seq, _ = nq.shape
    slab_in = lambda w, j: pl.BlockSpec((None, w.shape[1] // batch, w.shape[2]), lambda b: (j, b, 0))
    slab_out = lambda w: pl.BlockSpec((w.shape[1] // batch, w.shape[2]), lambda b: (b, 0))
    lc = nkc.shape[1]
    assert WA_BLOCK == 2 * GRID_W and (seq // WA_BLOCK) % 2 == 0
    per_b = lambda n, w: pl.BlockSpec((None, n, w), lambda b: (b, 0, 0))
    na_scratch = pltpu.VMEM((lc + NA_WIN_ROWS * GRID_W + SUBLANES, NA_HEADS * GRID_W), F32)
    vt_scratch = pltpu.VMEM((2, NA_WIDTH, seq), BF16)
    vct_scratch = pltpu.VMEM((NA_WIDTH, lc), BF16)
    wa_scratch = pltpu.VMEM((WA_KV_HEADS, lc + WA_BAND + SUBLANES, WA_GROUP * WA_BLOCK), F32)
    outs = pl.pallas_call(
        functools.partial(_attn_kernel, n_cast=len(to_bf16)),
        out_shape=[jax.ShapeDtypeStruct((batch, seq, NA_WIDTH), BF16),
                   jax.ShapeDtypeStruct((batch, seq, WA_QW), BF16),
                   *[jax.ShapeDtypeStruct(w.shape[1:], BF16) for w, _ in to_bf16]],
        grid=(batch,),
        in_specs=[pl.BlockSpec(memory_space=pltpu.SMEM),
                  per_b(seq, NA_WIDTH), per_b(seq, NA_WIDTH), per_b(seq, NA_WIDTH),
                  per_b(lc, NA_WIDTH), per_b(lc, NA_WIDTH),
                  pl.BlockSpec((None,) + bias.shape[1:], lambda b: (layer, 0, 0, 0)),
                  pl.BlockSpec((1, NA_WIDTH), lambda b: (0, 0)),
                  per_b(seq, WA_QW), per_b(seq, WA_K2W), per_b(seq, WA_V3W),
                  per_b(lc, WA_K2W), per_b(lc, WA_V3W),
                  pl.BlockSpec((1, WA_QW), lambda b: (0, 0)),
                  *[slab_in(w, j) for w, j in to_bf16]],
        out_specs=[per_b(seq, NA_WIDTH), per_b(seq, WA_QW), *[slab_out(w) for w, _ in to_bf16]],
        scratch_shapes=[na_scratch, na_scratch, wa_scratch, wa_scratch, vt_scratch, vct_scratch],
        compiler_params=_params(),
        name="attention",
    )(sink, nq, nk, nv, nkc, nvc, bias, gain_na, wq, wk2, wv3, wkc2, wvc3, gain_wa, *[w for w, _ in to_bf16])
    return outs[0], outs[1], outs[2:]


def _ctx_attn(sink_ref, qn_ref, kn_ref, vn_ref, qw_ref, kw_ref, vw_ref, gn_ref, gw_ref):
    qs = _stack_heads(qn_ref[...], NA_HEADS)
    o4 = _softmax_pv([(_dot_nt(qs, kn_ref[...]), vn_ref[...])])
    na_n = _group_rms(_unstack_heads(o4, NA_HEADS), gn_ref[...]).astype(BF16)
    outs = []
    for kh in range(WA_KV_HEADS):
        lanes = slice(kh * LANES, (kh + 1) * LANES)
        qg = qw_ref[:, kh * WA_GROUP * HEAD_DIM:(kh + 1) * WA_GROUP * HEAD_DIM]
        sinks = [sink_ref[kh * WA_GROUP + g] for g in range(WA_GROUP)]
        outs.append(_wa_finish(_wa_scores(qg, [(kw_ref[:, lanes], None)]), [vw_ref[:, lanes]], sinks))
    return na_n, _group_rms(jnp.concatenate(outs, axis=1), gw_ref[...]).astype(BF16)


def _out_mlp_kernel(*refs, seq_len, sub, ctx_attn):
    x_ref, mod_ref = refs[:2]
    n_attn = 9 if ctx_attn else 2
    attn_refs = refs[2:2 + n_attn]
    u_ref, uprev_ref, unext_ref, cvb_ref, cw_ref, cb_ref, g_ref, g2_ref, wo_ref = refs[2 + n_attn:11 + n_attn]
    rest = refs[11 + n_attn:]
    n_ffn = FFN_DIM // FFN_CHUNK
    w1_refs, w2_refs, (o_ref,) = rest[:n_ffn], rest[n_ffn:2 * n_ffn], rest[2 * n_ffn:]
    tm = x_ref.shape[0]
    i = pl.program_id(0)
    if ctx_attn:
        na_t, wa_t = _ctx_attn(*attn_refs)
    else:
        na_t, wa_t = attn_refs
    u = u_ref[...]
    at_start = (i * tm) % seq_len == 0
    at_end = ((i + 1) * tm) % seq_len == 0
    prev_row = jnp.where(at_start, 0.0, uprev_ref[7:8, :])
    next_row = jnp.where(at_end, 0.0, unext_ref[0:1, :])
    row = _iota(u.shape, 0)
    up = jnp.where(row == 0, prev_row, pltpu.roll(u, 1, axis=0))
    dn = jnp.where(row == tm - 1, next_row, pltpu.roll(u, tm - 1, axis=0))
    y = cb_ref[...] + cw_ref[0:1, :] * up + cw_ref[1:2, :] * u + cw_ref[2:3, :] * dn
    cv = _group_rms(cvb_ref[...] * y, g_ref[...]).astype(BF16)
    for r0 in range(0, tm, sub):
        rows = slice(r0, r0 + sub)
        mixed = jnp.concatenate([na_t[rows, :], cv[rows, :], wa_t[rows, :]], axis=1)
        x = x_ref[rows, :] + mod_ref[2] * _dot(mixed, wo_ref[...])
        ms = jnp.mean(x * x, axis=-1, keepdims=True)
        h = x * lax.rsqrt(ms + EPS) * g2_ref[...]
        h = (h * (1.0 + mod_ref[4]) + mod_ref[3]).astype(BF16)
        acc = None
        for w1_ref, w2_ref in zip(w1_refs, w2_refs):
            a = jnp.maximum(_dot(h, w1_ref[...]), 0.0)
            part = _dot((a * a).astype(BF16), w2_ref[...])
            acc = part if acc is None else acc + part
        o_ref[rows, :] = x + mod_ref[5] * acc


def _out_mlp(x2d, mod, layer, mod_row, attn, cv_u, cv_b, conv_w, conv_b, gain_cv, w_o, g2, w1, w2, seq_len, tm):
    rows = x2d.shape[0]
    ctx_attn = len(attn) == 9
    assert not ctx_attn or tm == seq_len
    halo = 8
    per_tile = tm // halo
    last = rows // halo - 1
    row_spec = lambda w: pl.BlockSpec((tm, w), lambda i: (i, 0))
    const = lambda shape: pl.BlockSpec(shape, lambda i: (0,) * len(shape), pipeline_mode=pl.Buffered(1))
    once = pl.Buffered(1)
    n_ffn = FFN_DIM // FFN_CHUNK
    wo_spec = pl.BlockSpec((D_MODEL, D_MODEL), lambda i: (0, 0), pipeline_mode=once)
    w1_specs = [pl.BlockSpec((D_MODEL, FFN_CHUNK), functools.partial(lambda j, i: (0, j), j),
                             pipeline_mode=once) for j in range(n_ffn)]
    w2_specs = [pl.BlockSpec((FFN_CHUNK, D_MODEL), functools.partial(lambda j, i: (j, 0), j),
                             pipeline_mode=once) for j in range(n_ffn)]
    small = lambda a: pl.BlockSpec(a.shape, lambda i: (0,) * a.ndim)
    if ctx_attn:
        attn_specs = [pl.BlockSpec(memory_space=pltpu.SMEM), *[row_spec(a.shape[1]) for a in attn[1:7]],
                      small(attn[7]), small(attn[8])]
    else:
        attn_specs = [row_spec(NA_WIDTH), row_spec(WA_QW)]
    return pl.pallas_call(
        functools.partial(_out_mlp_kernel, seq_len=seq_len, sub=min(tm, MLP_SUB), ctx_attn=ctx_attn),
        out_shape=jax.ShapeDtypeStruct((rows, D_MODEL), F32),
        grid=(rows // tm,),
        in_specs=[
            row_spec(D_MODEL),
            pl.BlockSpec((None, None, 6, 1, D_MODEL), lambda i: (layer, mod_row(i), 0, 0, 0)),
            *attn_specs, row_spec(CONV_CH),
            pl.BlockSpec((halo, CONV_CH), lambda i: (jnp.maximum(i * per_tile - 1, 0), 0)),
            pl.BlockSpec((halo, CONV_CH), lambda i: (jnp.minimum((i + 1) * per_tile, last), 0)),
            row_spec(CONV_CH),
            const((3, CONV_CH)), const((1, CONV_CH)), const((1, CONV_CH)),
            const((1, D_MODEL)),
            wo_spec, *w1_specs, *w2_specs,
        ],
        out_specs=row_spec(D_MODEL),
        compiler_params=_params(),
        name="out_mlp",
    )(x2d, mod, *attn, cv_u, cv_u, cv_u, cv_b, conv_w, conv_b, gain_cv, g2,
      w_o, *([w1] * n_ffn), *([w2] * n_ffn))


def _rope_tables(seq):
    quarter = HEAD_DIM // 4
    inv = ROPE_BASE ** (-jnp.arange(quarter, dtype=F32) / quarter)
    t = jnp.arange(seq)
    ang_r = (t // GRID_W).astype(F32)[:, None] * inv[None, :]
    ang_c = (t % GRID_W).astype(F32)[:, None] * inv[None, :]
    cos = jnp.concatenate([jnp.cos(ang_r), jnp.cos(ang_r), jnp.cos(ang_c), jnp.cos(ang_c)], axis=1)
    sin = jnp.concatenate([-jnp.sin(ang_r), jnp.sin(ang_r), -jnp.sin(ang_c), jnp.sin(ang_c)], axis=1)
    reps = LANES // HEAD_DIM
    return jnp.tile(cos, (1, reps)), jnp.tile(sin, (1, reps))


def kernel(x, c, ctx, c_ctx, w_mod, b_mod, g_norm1, g_norm2, w_in, na_q_gain, na_k_gain, na_rpb,
           conv_w, conv_bias, wa_q_gain, wa_k_gain, wa_sink, g_out, w_o, w_fc1, w_fc2):
    batch, seq, d = x.shape
    lc = ctx.shape[1]
    depth = w_mod.shape[0]
    assert d == D_MODEL and seq % WA_BLOCK == 0 and seq % GRID_W == 0 and batch < MOD_ROWS
    tm = 1024
    tm_in = 1024
    tm_ctx = 256
    assert seq % tm == 0 and seq % tm_in == 0 and lc % tm_ctx == 0

    cond = jnp.zeros((MOD_ROWS, d), F32).at[:batch].set(c).at[batch].set(c_ctx)
    mod = _modulation(cond, w_mod, b_mod)
    rpb_bias = _rpb_tables(na_rpb)
    rope_tabs = _rope_tables(seq)

    lat_row = lambda i: i // (seq // tm)
    lat_row_in = lambda i: i // (seq // tm_in)
    ctx_row = lambda i: batch

    w_in_bf = w_in[0].astype(BF16)
    xs = x.reshape(batch * seq, d)
    cs = ctx.reshape(batch * lc, d)
    for l in range(depth):
        last = l == depth - 1
        g1 = g_norm1[l].reshape(1, d)
        g2 = g_norm2[l].reshape(1, d)
        gains = [jnp.tile(na_q_gain[l], NA_HEADS).reshape(1, -1), jnp.tile(na_k_gain[l], NA_HEADS).reshape(1, -1),
                 jnp.tile(wa_q_gain[l], WA_HEADS).reshape(1, -1), jnp.tile(wa_k_gain[l], WA_KV_HEADS).reshape(1, -1)]
        go_na = g_out[l, :NA_WIDTH].reshape(1, -1)
        go_cv = g_out[l, NA_WIDTH:NA_WIDTH + CONV_CH].reshape(1, -1)
        go_wa = g_out[l, NA_WIDTH + CONV_CH:].reshape(1, -1)
        cb = conv_bias[l].reshape(1, -1)

        naq, nak, nav, cvu, cvb, waq, wak, wav = _in_proj(xs, mod, l, lat_row_in, g1, w_in_bf, gains, rope_tabs, tm_in)
        if last:
            cnak, cnav, cwak, cwav = _in_proj(cs, mod, l, ctx_row, g1, w_in_bf, gains, None, tm_ctx, kv_only=True)
        else:
            cnaq, cnak, cnav, ccvu, ccvb, cwaq, cwak, cwav = _in_proj(cs, mod, l, ctx_row, g1, w_in_bf, gains, None, tm_ctx)

        b3 = lambda a, n: a.reshape(batch, n, a.shape[-1])
        casts = [(w_o, l), (w_fc1, l), (w_fc2, l)] + ([] if last else [(w_in, l + 1)])
        na_n, wa_n, cast_out = _attention(
            wa_sink[l], b3(naq, seq), b3(nak, seq), b3(nav, seq), b3(cnak, lc), b3(cnav, lc), rpb_bias, l, go_na,
            b3(waq, seq), b3(wak, seq), b3(wav, seq), b3(cwak, lc), b3(cwav, lc), go_wa, casts)
        w_o_bf, w1_bf, w2_bf = cast_out[:3]
        if not last:
            w_in_bf = cast_out[3]
        xs = _out_mlp(xs, mod, l, lat_row, (na_n.reshape(batch * seq, -1), wa_n.reshape(batch * seq, -1)),
                      cvu, cvb, conv_w[l], cb, go_cv, w_o_bf, g2, w1_bf, w2_bf, seq, tm)
        if not last:
            ctx_attn = (wa_sink[l], cnaq, cnak, cnav, cwaq, cwak, cwav, go_na, go_wa)
            cs = _out_mlp(cs, mod, l, ctx_row, ctx_attn, ccvu, ccvb, conv_w[l], cb, go_cv,
                          w_o_bf, g2, w1_bf, w2_bf, lc, lc)
    return xs.reshape(batch, seq, d)
```

```python
import functools

import jax
import jax.numpy as jnp
from jax import lax
from jax.experimental import pallas as pl
from jax.experimental.pallas import tpu as pltpu

D_MODEL = 1024
GRID_W = 64
HEAD_DIM = 64
NA_HEADS = 4
NA_WIDTH = NA_HEADS * HEAD_DIM
CONV_CH = 256
WA_HEADS = 8
WA_KV_HEADS = 2
WA_GROUP = WA_HEADS // WA_KV_HEADS
WA_QW = WA_HEADS * HEAD_DIM
WA_KVW = WA_KV_HEADS * HEAD_DIM
WA_K2W = 2 * WA_KVW
WA_V3W = 2 * WA_KVW
NA_WIN_ROWS = 8
NA_WIN_COLS = 16
NA_DROWS = 2 * NA_WIN_ROWS - 1
NA_DCOLS = 2 * NA_WIN_COLS - 1
WA_WINDOW = 128
WA_BLOCK = 128
WA_BAND = 3 * WA_BLOCK
FFN_DIM = 4 * D_MODEL
FFN_CHUNK = 1024
ROPE_BASE = 10000.0
EPS = 1e-6
NEG_INF = -1e30
IN_WIDTH = 2304
OFF_NA_Q, OFF_NA_K, OFF_NA_V = 0, 256, 512
OFF_CV_X, OFF_CV_B, OFF_CV_C = 768, 1024, 1280
OFF_WA_Q, OFF_WA_K, OFF_WA_V = 1536, 2048, 2176
MOD_ROWS = 16
MOD_TN = 2048
IN_SUB = 256
MLP_SUB = 1024
LANES = 128
MXU_DIM = 256
VMEM_LIMIT = 60 * 1024 * 1024
LOG2E = 1.4426950408889634
Q_SCALE = LOG2E * HEAD_DIM ** -0.5

F32 = jnp.float32
BF16 = jnp.bfloat16


def _dot(a, b):
    return jnp.dot(a, b, preferred_element_type=F32)


def _dot_nt(a, b):
    return lax.dot_general(a, b, (((1,), (1,)), ((), ())), preferred_element_type=F32)


def _split_bf16(a):
    hi = a.astype(BF16)
    lo = (a - hi.astype(F32)).astype(BF16)
    return hi, lo


def _iota(shape, dim):
    return lax.broadcasted_iota(jnp.int32, shape, dim)


def _params(**kw):
    return pltpu.CompilerParams(vmem_limit_bytes=VMEM_LIMIT, **kw)


def _mod_kernel(cond_ref, w_ref, b_ref, o_ref):
    a = cond_ref[...]
    a = a * (1.0 / (1.0 + jnp.exp(-a)))
    ah, al = _split_bf16(a)
    wh, wl = _split_bf16(w_ref[...])
    res = _dot(ah, wh) + _dot(al, wh) + _dot(ah, wl) + b_ref[...]
    for c in range(o_ref.shape[1]):
        o_ref[:, c, 0, :] = res[:, c * D_MODEL:(c + 1) * D_MODEL]


def _modulation(cond, w_mod, b_mod):
    depth = w_mod.shape[0]
    n_out = w_mod.shape[2]
    per_tile = MOD_TN // D_MODEL
    return pl.pallas_call(
        _mod_kernel,
        out_shape=jax.ShapeDtypeStruct((depth, MOD_ROWS, n_out // D_MODEL, 1, D_MODEL), F32),
        grid=(depth, n_out // MOD_TN),
        in_specs=[
            pl.BlockSpec((MOD_ROWS, D_MODEL), lambda l, j: (0, 0)),
            pl.BlockSpec((None, D_MODEL, MOD_TN), lambda l, j: (l, 0, j)),
            pl.BlockSpec((None, 1, MOD_TN), lambda l, j: (l, 0, j)),
        ],
        out_specs=pl.BlockSpec((None, MOD_ROWS, per_tile, 1, D_MODEL), lambda l, j: (l, 0, j, 0, 0)),
        compiler_params=_params(),
        name="modulation",
    )(cond, w_mod, b_mod.reshape(depth, 1, n_out))


def _rpb_kernel(rpb_ref, o_ref):
    shape = (GRID_W, LANES)
    q = _iota(shape, 0)
    lane = _iota(shape, 1)
    k = lane % GRID_W
    left = lane < GRID_W
    col_start = jnp.clip(q - NA_WIN_COLS // 2, 0, GRID_W - NA_WIN_COLS)
    col_ok = (k >= col_start) & (k < col_start + NA_WIN_COLS)
    centre = NA_WIN_COLS - 1
    for d in range(NA_DROWS - 1):
        tiles = []
        for h in range(NA_HEADS):
            t1 = pltpu.roll(jnp.broadcast_to(rpb_ref[h, d:d + 1, :], shape), LANES - centre, axis=1,
                            stride=1, stride_axis=0)
            t2 = pltpu.roll(jnp.broadcast_to(rpb_ref[h, d + 1:d + 2, :], shape), GRID_W - centre, axis=1,
                            stride=1, stride_axis=0)
            t = jnp.where(left, t1, t2)
            tiles.append(jnp.where(col_ok, t * LOG2E, NEG_INF))
        o_ref[d] = jnp.concatenate(tiles, axis=0).T


def _rpb_tables(na_rpb):
    depth = na_rpb.shape[0]
    padded = jnp.pad(na_rpb, ((0, 0), (0, 0), (0, 1), (0, LANES - NA_DCOLS)))
    return pl.pallas_call(
        _rpb_kernel,
        out_shape=jax.ShapeDtypeStruct((depth, NA_DROWS - 1, LANES, NA_HEADS * GRID_W), F32),
        grid=(depth,),
        in_specs=[pl.BlockSpec((None,) + padded.shape[1:], lambda l: (l, 0, 0, 0))],
        out_specs=pl.BlockSpec((None, NA_DROWS - 1, LANES, NA_HEADS * GRID_W), lambda l: (l, 0, 0, 0)),
        compiler_params=_params(),
        name="rpb_tables",
    )(padded)


def _head_rms(t, gain):
    width = t.shape[1]
    sq = (t * t).astype(BF16)
    blk = min(width, MXU_DIM)
    ones = jnp.where(_iota((blk, blk), 0) // HEAD_DIM == _iota((blk, blk), 1) // HEAD_DIM, 1.0, 0.0).astype(BF16)
    sums = [_dot(sq[:, c:c + blk], ones) for c in range(0, width, blk)]
    ss = sums[0] if len(sums) == 1 else jnp.concatenate(sums, axis=1)
    return t * lax.rsqrt(ss * (1.0 / HEAD_DIM) + EPS) * gain


def _rope(t, cos, sin):
    lane = _iota((t.shape[0], LANES), 1)
    first = (lane % (HEAD_DIM // 2)) < (HEAD_DIM // 4)
    out = []
    for c in range(0, t.shape[1], LANES):
        u = t[:, c:c + LANES]
        partner = jnp.where(first, pltpu.roll(u, LANES - HEAD_DIM // 4, axis=1), pltpu.roll(u, HEAD_DIM // 4, axis=1))
        out.append(u * cos + partner * sin)
    return out[0] if len(out) == 1 else jnp.concatenate(out, axis=1)


def _dup_kv(t):
    lane = _iota(t.shape, 1)
    swapped = pltpu.roll(t, HEAD_DIM, axis=1)
    left = lane < HEAD_DIM
    return jnp.concatenate([jnp.where(left, t, swapped), jnp.where(left, swapped, t)], axis=1)


def _inproj_kernel(*refs, rope, kv_only, n_w, sub):
    x_ref, mod_ref, g1_ref = refs[:3]
    w_refs = refs[3:3 + n_w]
    gnq_ref, gnk_ref, gwq_ref, gwk_ref = refs[3 + n_w:7 + n_w]
    rest = refs[7 + n_w:]
    if rope:
        cos_ref, sin_ref = rest[:2]
        rest = rest[2:]
    out_refs = rest
    if kv_only:
        nak_ref, nav_ref, wak_ref, wav_ref = out_refs
        o_nk, o_nv, o_wk, o_wv, o_end = 0, NA_WIDTH, 2 * NA_WIDTH, 2 * NA_WIDTH + WA_KVW, 2 * NA_WIDTH + 2 * WA_KVW
    else:
        naq_ref, nak_ref, nav_ref, cvu_ref, cvb_ref, waq_ref, wak_ref, wav_ref = out_refs
        o_nk, o_nv, o_wk, o_wv, o_end = OFF_NA_K, OFF_NA_V, OFF_WA_K, OFF_WA_V, IN_WIDTH
    def project(rows):
        x = x_ref[rows, :]
        ms = jnp.mean(x * x, axis=-1, keepdims=True)
        h = x * lax.rsqrt(ms + EPS) * g1_ref[...]
        h = h * (1.0 + mod_ref[1]) + mod_ref[0]
        hb = h.astype(BF16)
        return jnp.concatenate([_dot(hb, w_ref[...]) for w_ref in w_refs], axis=1)

    def finish(rows, p):
        nak_ref[rows, :] = _head_rms(p[:, o_nk:o_nk + NA_WIDTH], gnk_ref[...]).astype(BF16)
        nav_ref[rows, :] = p[:, o_nv:o_nv + NA_WIDTH].astype(BF16)
        wk = _head_rms(p[:, o_wk:o_wv], gwk_ref[...])
        if rope:
            wk = _rope(wk, cos_ref[rows, :], sin_ref[rows, :])
        wak_ref[rows, :] = _dup_kv(wk).astype(BF16)
        vv = p[:, o_wv:o_end]
        left = _iota(vv.shape, 1) < HEAD_DIM
        wav_ref[rows, :] = jnp.concatenate([jnp.where(left, vv, 1.0), jnp.where(left, pltpu.roll(vv, HEAD_DIM, axis=1), 1.0)],
                                           axis=1).astype(BF16)
        if kv_only:
            return
        naq_ref[rows, :] = (_head_rms(p[:, OFF_NA_Q:OFF_NA_K], gnq_ref[...]) * Q_SCALE).astype(BF16)
        cvu_ref[rows, :] = p[:, OFF_CV_C:OFF_WA_Q] * p[:, OFF_CV_X:OFF_CV_B]
        cvb_ref[rows, :] = p[:, OFF_CV_B:OFF_CV_C]
        wq = _head_rms(p[:, OFF_WA_Q:OFF_WA_K], gwq_ref[...])
        if rope:
            wq = _rope(wq, cos_ref[rows, :], sin_ref[rows, :])
        waq_ref[rows, :] = (wq * Q_SCALE).astype(BF16)

    for r0 in range(0, x_ref.shape[0], sub):
        rows = slice(r0, r0 + sub)
        finish(rows, project(rows))


def _in_proj(x2d, mod, layer, mod_row, g1, w_in, gains, rope_tabs, tm, kv_only=False):
    rows = x2d.shape[0]
    rope = rope_tabs is not None
    row_spec = lambda w: pl.BlockSpec((tm, w), lambda i: (i, 0))
    const = lambda shape: pl.BlockSpec(shape, lambda i: (0,) * len(shape))
    once = pl.Buffered(1)
    if kv_only:
        col_blocks = [OFF_NA_K // MXU_DIM, OFF_NA_V // MXU_DIM, OFF_WA_K // MXU_DIM]
        w_specs = [pl.BlockSpec((D_MODEL, MXU_DIM), functools.partial(lambda j, i: (0, j), j),
                                pipeline_mode=once) for j in col_blocks]
        widths = [(NA_WIDTH, BF16), (NA_WIDTH, BF16), (WA_K2W, BF16), (WA_V3W, BF16)]
    else:
        w_specs = [pl.BlockSpec((D_MODEL, IN_WIDTH), lambda i: (0, 0), pipeline_mode=once)]
        widths = [(NA_WIDTH, BF16), (NA_WIDTH, BF16), (NA_WIDTH, BF16), (CONV_CH, F32), (CONV_CH, F32),
                  (WA_QW, BF16), (WA_K2W, BF16), (WA_V3W, BF16)]
    n_w = len(w_specs)
    in_specs = [
        row_spec(D_MODEL),
        pl.BlockSpec((None, None, 6, 1, D_MODEL), lambda i: (layer, mod_row(i), 0, 0, 0)),
        const((1, D_MODEL)),
        *w_specs,
        const((1, NA_WIDTH)), const((1, NA_WIDTH)), const((1, WA_QW)), const((1, WA_KVW)),
    ]
    args = [x2d, mod, g1, *([w_in] * n_w), *gains]
    if rope:
        seq_tiles = rope_tabs[0].shape[0] // tm
        in_specs += [pl.BlockSpec((tm, LANES), lambda i: (i % seq_tiles, 0))] * 2
        args += list(rope_tabs)
    return pl.pallas_call(
        functools.partial(_inproj_kernel, rope=rope, kv_only=kv_only, n_w=n_w, sub=min(tm, IN_SUB)),
        out_shape=[jax.ShapeDtypeStruct((rows, w), dt) for w, dt in widths],
        grid=(rows // tm,),
        in_specs=in_specs,
        out_specs=[row_spec(w) for w, _ in widths],
        compiler_params=_params(),
        name="in_proj_rope" if rope else ("in_proj_ctx_kv" if kv_only else "in_proj_ctx"),
    )(*args)


def _lane_chunks(s):
    return [s[:, c:c + LANES] for c in range(0, s.shape[1], LANES)]


def _row_max(scores, extra=None):
    chunks = [ch for s in scores for ch in _lane_chunks(s)]
    if extra is not None:
        chunks.append(extra)
    folded = functools.reduce(jnp.maximum, chunks)
    return jnp.broadcast_to(jnp.max(folded, axis=-1, keepdims=True), folded.shape)


def _softmax_pv(parts, extra=None, sums_from_values=False, m=None):
    lane_chunks = _lane_chunks
    if m is None:
        m = _row_max([s for s, _ in parts], extra)
    ps = [jnp.concatenate([jnp.exp2(ch - m) for ch in lane_chunks(s)], axis=1) for s, _ in parts]
    pv = _dot(jnp.concatenate([p.astype(BF16) for p in ps], axis=1),
              jnp.concatenate([v for _, v in parts], axis=0))
    if sums_from_values:
        denom = pv[:, -LANES:]
        pv = pv[:, :-LANES]
        if extra is not None:
            denom = denom + jnp.exp2(extra - m)
        return pv / denom
    psum = functools.reduce(jnp.add, [ch for p in ps for ch in lane_chunks(p)])
    if extra is not None:
        psum = psum + jnp.where(_iota(extra.shape, 1) == 0, jnp.exp2(extra - m), 0.0)
    return pv / jnp.sum(psum, axis=-1, keepdims=True)


def _stack_heads(q, n_heads):
    head = _iota(q.shape, 1) // HEAD_DIM
    zero = jnp.zeros_like(q)
    return jnp.concatenate([jnp.where(head == h, q, zero) for h in range(n_heads)], axis=0)


def _unstack_heads(o, n_heads):
    rows = o.shape[0] // n_heads
    head = _iota((rows, o.shape[1]), 1) // HEAD_DIM
    out = jnp.where(head == 0, o[:rows], 0.0)
    for h in range(1, n_heads):
        out = jnp.where(head == h, o[h * rows:(h + 1) * rows], out)
    return out


def _group_rms(o, gain):
    ms = jnp.mean(o * o, axis=-1, keepdims=True)
    return o * lax.rsqrt(ms + EPS) * gain


SUBLANES = 8


def _wa_scores(qg, keys):
    rows = qg.shape[0]
    left = _iota((rows, LANES), 1) < HEAD_DIM
    zero = jnp.zeros((rows, LANES), qg.dtype)
    blocks = []
    for pair in range(WA_GROUP // 2):
        qp = qg[:, pair * LANES:(pair + 1) * LANES]
        blocks += [jnp.where(left, qp, zero), jnp.where(left, zero, qp)]
    qs = jnp.concatenate(blocks, axis=0)
    scores = []
    for k2, bias in keys:
        s = _dot_nt(k2, qs)
        if bias is not None:
            s = s + jnp.concatenate([bias] * WA_GROUP, axis=1)
        scores.append(s)
    return scores


def _wa_sink_rows(rows, sinks):
    return jnp.concatenate([jnp.full((SUBLANES, rows), sinks[g] * LOG2E, F32) for g in range(WA_GROUP)], axis=1)


def _wa_col_max(scores, sink_rows):
    m = sink_rows[:1]
    for s in scores:
        m = jnp.maximum(m, jnp.max(s, axis=0, keepdims=True))
    return jnp.broadcast_to(m, sink_rows.shape)


def _wa_finish(scores, values, sinks, m=None):
    rows = scores[0].shape[1] // WA_GROUP
    sink_rows = _wa_sink_rows(rows, sinks)
    if m is None:
        m = _wa_col_max(scores, sink_rows)
    p = jnp.concatenate([jnp.exp2(s - m[:1]).astype(BF16) for s in scores], axis=0)
    v = jnp.concatenate(values, axis=0)
    ot = lax.dot_general(v, p, (((0,), (0,)), ((), ())), preferred_element_type=F32)
    denom = ot[HEAD_DIM:HEAD_DIM + SUBLANES] + jnp.exp2(sink_rows - m)
    on = ot[:HEAD_DIM] / denom[:1]
    pairs = [jnp.concatenate([on[:, (2 * pr) * rows:(2 * pr + 1) * rows],
                              on[:, (2 * pr + 1) * rows:(2 * pr + 2) * rows]], axis=0).T
             for pr in range(WA_GROUP // 2)]
    return jnp.concatenate(pairs, axis=1)


def _na_steps(q_ref, k_ref, v_ref, kc_ref, vc_ref, bias_ref, g_ref, o_ref, vt_ref, vct_ref):
    seq = q_ref.shape[0]
    n_rows = seq // GRID_W
    win = NA_WIN_ROWS * GRID_W
    lc = kc_ref.shape[0]
    gain = g_ref[...]

    def prepare():
        vt_ref[0] = v_ref[...].astype(F32).T.astype(BF16)
        vt_ref[1, :, :seq - GRID_W] = v_ref[GRID_W:, :].astype(F32).T.astype(BF16)
        vct_ref[...] = vc_ref[...].astype(F32).T.astype(BF16)

    def window(r):
        start = jnp.clip(r - NA_WIN_ROWS // 2, 0, n_rows - NA_WIN_ROWS)
        return start, start - r + (NA_WIN_ROWS - 1)

    def scores(r, s_ref):
        start, d0 = window(r)
        tok0 = pl.multiple_of(start * GRID_W, GRID_W)
        q = q_ref[pl.ds(pl.multiple_of(r * GRID_W, GRID_W), GRID_W), :]
        qs = _stack_heads(q, NA_HEADS)
        bias = jnp.concatenate([bias_ref[d0 + 2 * j] for j in range(NA_WIN_ROWS // 2)], axis=0)
        s_ctx = _dot_nt(kc_ref[...], qs)
        s_loc = _dot_nt(k_ref[pl.ds(tok0, win), :], qs) + bias
        m = jnp.maximum(jnp.max(s_ctx, axis=0, keepdims=True), jnp.max(s_loc, axis=0, keepdims=True))
        s_ref[:lc, :] = s_ctx
        s_ref[lc:lc + win, :] = s_loc
        s_ref[lc + win:, :] = jnp.broadcast_to(m, (SUBLANES, m.shape[1]))

    def finish(r, s_ref):
        start, _ = window(r)
        odd = start % 2
        lane0 = pl.multiple_of((start - odd) * GRID_W, LANES)
        m = s_ref[lc + win:lc + win + 1, :]
        p_ctx = jnp.exp2(s_ref[:lc, :] - m)
        p_loc = jnp.exp2(s_ref[lc:lc + win, :] - m)
        denom = jnp.sum(p_ctx, axis=0, keepdims=True) + jnp.sum(p_loc, axis=0, keepdims=True)
        ot = _dot(vct_ref[...], p_ctx.astype(BF16)) + _dot(vt_ref[odd, :, pl.ds(lane0, win)], p_loc.astype(BF16))
        o4 = (ot * (1.0 / denom)).T
        o = _unstack_heads(o4, NA_HEADS)
        o_ref[pl.ds(pl.multiple_of(r * GRID_W, GRID_W), GRID_W), :] = _group_rms(o, gain).astype(o_ref.dtype)

    def step(r, cur_ref, nxt_ref):
        scores(jnp.minimum(r + 1, n_rows - 1), nxt_ref)
        finish(r, cur_ref)

    return prepare, scores, step


def _wa_steps(sink_ref, q_ref, k_ref, v_ref, kc_ref, vc_ref, g_ref, o_ref):
    seq = q_ref.shape[0]
    n_blocks = seq // WA_BLOCK
    lc = kc_ref.shape[0]
    gain = g_ref[...]
    key = _iota((WA_BAND, WA_BLOCK), 0)
    qry = _iota((WA_BAND, WA_BLOCK), 1)

    def window(n):
        q0 = pl.multiple_of(n * WA_BLOCK, WA_BLOCK)
        k0 = pl.multiple_of(jnp.clip(q0 - WA_BLOCK, 0, seq - WA_BAND), WA_BLOCK)
        return q0, k0

    def scores(n, kh, s_ref):
        q0, k0 = window(n)
        band = jnp.where(jnp.abs((k0 + key) - (q0 + qry)) <= WA_WINDOW, 0.0, NEG_INF)
        lanes = slice(kh * LANES, (kh + 1) * LANES)
        qg = q_ref[pl.ds(q0, WA_BLOCK), kh * WA_GROUP * HEAD_DIM:(kh + 1) * WA_GROUP * HEAD_DIM]
        s_ctx, s_loc = _wa_scores(qg, [(kc_ref[:, lanes], None), (k_ref[pl.ds(k0, WA_BAND), lanes], band)])
        s_ref[kh, :lc, :] = s_ctx
        s_ref[kh, lc:lc + WA_BAND, :] = s_loc
        s_ref[kh, lc + WA_BAND:, :] = _wa_col_max([s_ctx, s_loc], _wa_sink_rows(WA_BLOCK, sinks(kh)))

    def sinks(kh):
        return [sink_ref[kh * WA_GROUP + g] for g in range(WA_GROUP)]

    def finish(n, kh, s_ref):
        _, k0 = window(n)
        lanes = slice(kh * LANES, (kh + 1) * LANES)
        return _wa_finish([s_ref[kh, :lc, :], s_ref[kh, lc:lc + WA_BAND, :]],
                          [vc_ref[:, lanes], v_ref[pl.ds(k0, WA_BAND), lanes]], sinks(kh),
                          m=s_ref[kh, lc + WA_BAND:, :])

    def step(n, cur_ref, nxt_ref):
        nxt = jnp.minimum(n + 1, n_blocks - 1)
        for kh in range(WA_KV_HEADS):
            scores(nxt, kh, nxt_ref)
        o = jnp.concatenate([finish(n, kh, cur_ref) for kh in range(WA_KV_HEADS)], axis=1)
        q0, _ = window(n)
        o_ref[pl.ds(q0, WA_BLOCK), :] = _group_rms(o, gain).astype(o_ref.dtype)

    return scores, step


def _attn_kernel(sink_ref, nq_ref, nk_ref, nv_ref, nkc_ref, nvc_ref, bias_ref, gn_ref,
                 wq_ref, wk_ref, wv_ref, wkc_ref, wvc_ref, gw_ref, *rest, n_cast):
    f32_refs = rest[:n_cast]
    on_ref, ow_ref = rest[n_cast:n_cast + 2]
    bf16_refs = rest[n_cast + 2:2 * n_cast + 2]
    nsa_ref, nsb_ref, wsa_ref, wsb_ref, vt_ref, vct_ref = rest[2 * n_cast + 2:]
    for src_ref, dst_ref in zip(f32_refs, bf16_refs):
        dst_ref[...] = src_ref[...].astype(dst_ref.dtype)
    na_prepare, na_scores, na_step = _na_steps(nq_ref, nk_ref, nv_ref, nkc_ref, nvc_ref, bias_ref, gn_ref, on_ref,
                                               vt_ref, vct_ref)
    na_prepare()
    wa_scores, wa_step = _wa_steps(sink_ref, wq_ref, wk_ref, wv_ref, wkc_ref, wvc_ref, gw_ref, ow_ref)
    n_blocks = wq_ref.shape[0] // WA_BLOCK

    for kh in range(WA_KV_HEADS):
        wa_scores(0, kh, wsa_ref)
    na_scores(0, nsa_ref)

    def body(i, carry):
        for half, (w_cur, w_nxt) in enumerate(((wsa_ref, wsb_ref), (wsb_ref, wsa_ref))):
            n = 2 * i + half
            wa_step(n, w_cur, w_nxt)
            na_step(2 * n, nsa_ref, nsb_ref)
            na_step(2 * n + 1, nsb_ref, nsa_ref)
        return carry

    lax.fori_loop(0, n_blocks // 2, body, 0, unroll=2)


def _attention(sink, nq, nk, nv, nkc, nvc, bias, layer, gain_na, wq, wk2, wv3, wkc2, wvc3, gain_wa, to_bf16):
    batch, seq, _ = nq.shape
    slab_in = lambda w, j: pl.BlockSpec((None, w.shape[1] // batch, w.shape[2]), lambda b: (j, b, 0))
    slab_out = lambda w: pl.BlockSpec((w.shape[1] // batch, w.shape[2]), lambda b: (b, 0))
    lc = nkc.shape[1]
    assert WA_BLOCK == 2 * GRID_W and (seq // WA_BLOCK) % 2 == 0
    per_b = lambda n, w: pl.BlockSpec((None, n, w), lambda b: (b, 0, 0))
    na_scratch = pltpu.VMEM((lc + NA_WIN_ROWS * GRID_W + SUBLANES, NA_HEADS * GRID_W), F32)
    vt_scratch = pltpu.VMEM((2, NA_WIDTH, seq), BF16)
    vct_scratch = pltpu.VMEM((NA_WIDTH, lc), BF16)
    wa_scratch = pltpu.VMEM((WA_KV_HEADS, lc + WA_BAND + SUBLANES, WA_GROUP * WA_BLOCK), F32)
    outs = pl.pallas_call(
        functools.partial(_attn_kernel, n_cast=len(to_bf16)),
        out_shape=[jax.ShapeDtypeStruct((batch, seq, NA_WIDTH), BF16),
                   jax.ShapeDtypeStruct((batch, seq, WA_QW), BF16),
                   *[jax.ShapeDtypeStruct(w.shape[1:], BF16) for w, _ in to_bf16]],
        grid=(batch,),
        in_specs=[pl.BlockSpec(memory_space=pltpu.SMEM),
                  per_b(seq, NA_WIDTH), per_b(seq, NA_WIDTH), per_b(seq, NA_WIDTH),
                  per_b(lc, NA_WIDTH), per_b(lc, NA_WIDTH),
                  pl.BlockSpec((None,) + bias.shape[1:], lambda b: (layer, 0, 0, 0)),
                  pl.BlockSpec((1, NA_WIDTH), lambda b: (0, 0)),
                  per_b(seq, WA_QW), per_b(seq, WA_K2W), per_b(seq, WA_V3W),
                  per_b(lc, WA_K2W), per_b(lc, WA_V3W),
                  pl.BlockSpec((1, WA_QW), lambda b: (0, 0)),
                  *[slab_in(w, j) for w, j in to_bf16]],
        out_specs=[per_b(seq, NA_WIDTH), per_b(seq, WA_QW), *[slab_out(w) for w, _ in to_bf16]],
        scratch_shapes=[na_scratch, na_scratch, wa_scratch, wa_scratch, vt_scratch, vct_scratch],
        compiler_params=_params(),
        name="attention",
    )(sink, nq, nk, nv, nkc, nvc, bias, gain_na, wq, wk2, wv3, wkc2, wvc3, gain_wa, *[w for w, _ in to_bf16])
    return outs[0], outs[1], outs[2:]


def _ctx_attn(sink_ref, qn_ref, kn_ref, vn_ref, qw_ref, kw_ref, vw_ref, gn_ref, gw_ref):
    qs = _stack_heads(qn_ref[...], NA_HEADS)
    o4 = _softmax_pv([(_dot_nt(qs, kn_ref[...]), vn_ref[...])])
    na_n = _group_rms(_unstack_heads(o4, NA_HEADS), gn_ref[...]).astype(BF16)
    outs = []
    for kh in range(WA_KV_HEADS):
        lanes = slice(kh * LANES, (kh + 1) * LANES)
        qg = qw_ref[:, kh * WA_GROUP * HEAD_DIM:(kh + 1) * WA_GROUP * HEAD_DIM]
        sinks = [sink_ref[kh * WA_GROUP + g] for g in range(WA_GROUP)]
        outs.append(_wa_finish(_wa_scores(qg, [(kw_ref[:, lanes], None)]), [vw_ref[:, lanes]], sinks))
    return na_n, _group_rms(jnp.concatenate(outs, axis=1), gw_ref[...]).astype(BF16)


def _out_mlp_kernel(*refs, seq_len, sub, ctx_attn):
    x_ref, mod_ref = refs[:2]
    n_attn = 9 if ctx_attn else 2
    attn_refs = refs[2:2 + n_attn]
    u_ref, uprev_ref, unext_ref, cvb_ref, cw_ref, cb_ref, g_ref, g2_ref, wo_ref = refs[2 + n_attn:11 + n_attn]
    rest = refs[11 + n_attn:]
    n_ffn = FFN_DIM // FFN_CHUNK
    w1_refs, w2_refs, (o_ref,) = rest[:n_ffn], rest[n_ffn:2 * n_ffn], rest[2 * n_ffn:]
    tm = x_ref.shape[0]
    i = pl.program_id(0)
    if ctx_attn:
        na_t, wa_t = _ctx_attn(*attn_refs)
    else:
        na_t, wa_t = attn_refs
    u = u_ref[...]
    at_start = (i * tm) % seq_len == 0
    at_end = ((i + 1) * tm) % seq_len == 0
    prev_row = jnp.where(at_start, 0.0, uprev_ref[7:8, :])
    next_row = jnp.where(at_end, 0.0, unext_ref[0:1, :])
    row = _iota(u.shape, 0)
    up = jnp.where(row == 0, prev_row, pltpu.roll(u, 1, axis=0))
    dn = jnp.where(row == tm - 1, next_row, pltpu.roll(u, tm - 1, axis=0))
    y = cb_ref[...] + cw_ref[0:1, :] * up + cw_ref[1:2, :] * u + cw_ref[2:3, :] * dn
    cv = _group_rms(cvb_ref[...] * y, g_ref[...]).astype(BF16)
    for r0 in range(0, tm, sub):
        rows = slice(r0, r0 + sub)
        mixed = jnp.concatenate([na_t[rows, :], cv[rows, :], wa_t[rows, :]], axis=1)
        x = x_ref[rows, :] + mod_ref[2] * _dot(mixed, wo_ref[...])
        ms = jnp.mean(x * x, axis=-1, keepdims=True)
        h = x * lax.rsqrt(ms + EPS) * g2_ref[...]
        h = (h * (1.0 + mod_ref[4]) + mod_ref[3]).astype(BF16)
        acc = None
        for w1_ref, w2_ref in zip(w1_refs, w2_refs):
            a = jnp.maximum(_dot(h, w1_ref[...]), 0.0)
            part = _dot((a * a).astype(BF16), w2_ref[...])
            acc = part if acc is None else acc + part
        o_ref[rows, :] = x + mod_ref[5] * acc


def _out_mlp(x2d, mod, layer, mod_row, attn, cv_u, cv_b, conv_w, conv_b, gain_cv, w_o, g2, w1, w2, seq_len, tm):
    rows = x2d.shape[0]
    ctx_attn = len(attn) == 9
    assert not ctx_attn or tm == seq_len
    halo = 8
    per_tile = tm // halo
    last = rows // halo - 1
    row_spec = lambda w: pl.BlockSpec((tm, w), lambda i: (i, 0))
    const = lambda shape: pl.BlockSpec(shape, lambda i: (0,) * len(shape), pipeline_mode=pl.Buffered(1))
    once = pl.Buffered(1)
    n_ffn = FFN_DIM // FFN_CHUNK
    wo_spec = pl.BlockSpec((D_MODEL, D_MODEL), lambda i: (0, 0), pipeline_mode=once)
    w1_specs = [pl.BlockSpec((D_MODEL, FFN_CHUNK), functools.partial(lambda j, i: (0, j), j),
                             pipeline_mode=once) for j in range(n_ffn)]
    w2_specs = [pl.BlockSpec((FFN_CHUNK, D_MODEL), functools.partial(lambda j, i: (j, 0), j),
                             pipeline_mode=once) for j in range(n_ffn)]
    small = lambda a: pl.BlockSpec(a.shape, lambda i: (0,) * a.ndim)
    if ctx_attn:
        attn_specs = [pl.BlockSpec(memory_space=pltpu.SMEM), *[row_spec(a.shape[1]) for a in attn[1:7]],
                      small(attn[7]), small(attn[8])]
    else:
        attn_specs = [row_spec(NA_WIDTH), row_spec(WA_QW)]
    return pl.pallas_call(
        functools.partial(_out_mlp_kernel, seq_len=seq_len, sub=min(tm, MLP_SUB), ctx_attn=ctx_attn),
        out_shape=jax.ShapeDtypeStruct((rows, D_MODEL), F32),
        grid=(rows // tm,),
        in_specs=[
            row_spec(D_MODEL),
            pl.BlockSpec((None, None, 6, 1, D_MODEL), lambda i: (layer, mod_row(i), 0, 0, 0)),
            *attn_specs, row_spec(CONV_CH),
            pl.BlockSpec((halo, CONV_CH), lambda i: (jnp.maximum(i * per_tile - 1, 0), 0)),
            pl.BlockSpec((halo, CONV_CH), lambda i: (jnp.minimum((i + 1) * per_tile, last), 0)),
            row_spec(CONV_CH),
            const((3, CONV_CH)), const((1, CONV_CH)), const((1, CONV_CH)),
            const((1, D_MODEL)),
            wo_spec, *w1_specs, *w2_specs,
        ],
        out_specs=row_spec(D_MODEL),
        compiler_params=_params(),
        name="out_mlp",
    )(x2d, mod, *attn, cv_u, cv_u, cv_u, cv_b, conv_w, conv_b, gain_cv, g2,
      w_o, *([w1] * n_ffn), *([w2] * n_ffn))


def _rope_tables(seq):
    quarter = HEAD_DIM // 4
    inv = ROPE_BASE ** (-jnp.arange(quarter, dtype=F32) / quarter)
    t = jnp.arange(seq)
    ang_r = (t // GRID_W).astype(F32)[:, None] * inv[None, :]
    ang_c = (t % GRID_W).astype(F32)[:, None] * inv[None, :]
    cos = jnp.concatenate([jnp.cos(ang_r), jnp.cos(ang_r), jnp.cos(ang_c), jnp.cos(ang_c)], axis=1)
    sin = jnp.concatenate([-jnp.sin(ang_r), jnp.sin(ang_r), -jnp.sin(ang_c), jnp.sin(ang_c)], axis=1)
    reps = LANES // HEAD_DIM
    return jnp.tile(cos, (1, reps)), jnp.tile(sin, (1, reps))


def kernel(x, c, ctx, c_ctx, w_mod, b_mod, g_norm1, g_norm2, w_in, na_q_gain, na_k_gain, na_rpb,
           conv_w, conv_bias, wa_q_gain, wa_k_gain, wa_sink, g_out, w_o, w_fc1, w_fc2):
    batch, seq, d = x.shape
    lc = ctx.shape[1]
    depth = w_mod.shape[0]
    assert d == D_MODEL and seq % WA_BLOCK == 0 and seq % GRID_W == 0 and batch < MOD_ROWS
    tm = 1024
    tm_in = 1024
    tm_ctx = 256
    assert seq % tm == 0 and seq % tm_in == 0 and lc % tm_ctx == 0

    cond = jnp.zeros((MOD_ROWS, d), F32).at[:batch].set(c).at[batch].set(c_ctx)
    mod = _modulation(cond, w_mod, b_mod)
    rpb_bias = _rpb_tables(na_rpb)
    rope_tabs = _rope_tables(seq)

    lat_row = lambda i: i // (seq // tm)
    lat_row_in = lambda i: i // (seq // tm_in)
    ctx_row = lambda i: batch

    w_in_bf = w_in[0].astype(BF16)
    xs = x.reshape(batch * seq, d)
    cs = ctx.reshape(batch * lc, d)
    for l in range(depth):
        last = l == depth - 1
        g1 = g_norm1[l].reshape(1, d)
        g2 = g_norm2[l].reshape(1, d)
        gains = [jnp.tile(na_q_gain[l], NA_HEADS).reshape(1, -1), jnp.tile(na_k_gain[l], NA_HEADS).reshape(1, -1),
                 jnp.tile(wa_q_gain[l], WA_HEADS).reshape(1, -1), jnp.tile(wa_k_gain[l], WA_KV_HEADS).reshape(1, -1)]
        go_na = g_out[l, :NA_WIDTH].reshape(1, -1)
        go_cv = g_out[l, NA_WIDTH:NA_WIDTH + CONV_CH].reshape(1, -1)
        go_wa = g_out[l, NA_WIDTH + CONV_CH:].reshape(1, -1)
        cb = conv_bias[l].reshape(1, -1)

        naq, nak, nav, cvu, cvb, waq, wak, wav = _in_proj(xs, mod, l, lat_row_in, g1, w_in_bf, gains, rope_tabs, tm_in)
        if last:
            cnak, cnav, cwak, cwav = _in_proj(cs, mod, l, ctx_row, g1, w_in_bf, gains, None, tm_ctx, kv_only=True)
        else:
            cnaq, cnak, cnav, ccvu, ccvb, cwaq, cwak, cwav = _in_proj(cs, mod, l, ctx_row, g1, w_in_bf, gains, None, tm_ctx)

        b3 = lambda a, n: a.reshape(batch, n, a.shape[-1])
        casts = [(w_o, l), (w_fc1, l), (w_fc2, l)] + ([] if last else [(w_in, l + 1)])
        na_n, wa_n, cast_out = _attention(
            wa_sink[l], b3(naq, seq), b3(nak, seq), b3(nav, seq), b3(cnak, lc), b3(cnav, lc), rpb_bias, l, go_na,
            b3(waq, seq), b3(wak, seq), b3(wav, seq), b3(cwak, lc), b3(cwav, lc), go_wa, casts)
        w_o_bf, w1_bf, w2_bf = cast_out[:3]
        if not last:
            w_in_bf = cast_out[3]
        xs = _out_mlp(xs, mod, l, lat_row, (na_n.reshape(batch * seq, -1), wa_n.reshape(batch * seq, -1)),
                      cvu, cvb, conv_w[l], cb, go_cv, w_o_bf, g2, w1_bf, w2_bf, seq, tm)
        if not last:
            ctx_attn = (wa_sink[l], cnaq, cnak, cnav, cwaq, cwak, cwav, go_na, go_wa)
            cs = _out_mlp(cs, mod, l, ctx_row, ctx_attn, ccvu, ccvb, conv_w[l], cb, go_cv,
                          w_o_bf, g2, w1_bf, w2_bf, lc, lc)
    return xs.reshape(batch, seq, d)
```

```python
import functools

import jax
import jax.numpy as jnp
from jax import lax
from jax.experimental import pallas as pl
from jax.experimental.pallas import tpu as pltpu

D_MODEL = 1024
GRID_W = 64
HEAD_DIM = 64
NA_HEADS = 4
NA_WIDTH = NA_HEADS * HEAD_DIM
CONV_CH = 256
WA_HEADS = 8
WA_KV_HEADS = 2
WA_GROUP = WA_HEADS // WA_KV_HEADS
WA_QW = WA_HEADS * HEAD_DIM
WA_KVW = WA_KV_HEADS * HEAD_DIM
WA_K2W = 2 * WA_KVW
WA_V3W = 2 * WA_KVW
NA_WIN_ROWS = 8
NA_WIN_COLS = 16
NA_DROWS = 2 * NA_WIN_ROWS - 1
NA_DCOLS = 2 * NA_WIN_COLS - 1
WA_WINDOW = 128
WA_BLOCK = 128
WA_BAND = 3 * WA_BLOCK
FFN_DIM = 4 * D_MODEL
FFN_CHUNK = 1024
ROPE_BASE = 10000.0
EPS = 1e-6
NEG_INF = -1e30
IN_WIDTH = 2304
OFF_NA_Q, OFF_NA_K, OFF_NA_V = 0, 256, 512
OFF_CV_X, OFF_CV_B, OFF_CV_C = 768, 1024, 1280
OFF_WA_Q, OFF_WA_K, OFF_WA_V = 1536, 2048, 2176
MOD_ROWS = 16
MOD_TN = 2048
IN_SUB = 256
MLP_SUB = 1024
LANES = 128
MXU_DIM = 256
VMEM_LIMIT = 60 * 1024 * 1024
LOG2E = 1.4426950408889634
Q_SCALE = LOG2E * HEAD_DIM ** -0.5

F32 = jnp.float32
BF16 = jnp.bfloat16


def _dot(a, b):
    return jnp.dot(a, b, preferred_element_type=F32)


def _dot_nt(a, b):
    return lax.dot_general(a, b, (((1,), (1,)), ((), ())), preferred_element_type=F32)


def _split_bf16(a):
    hi = a.astype(BF16)
    lo = (a - hi.astype(F32)).astype(BF16)
    return hi, lo


def _iota(shape, dim):
    return lax.broadcasted_iota(jnp.int32, shape, dim)


def _params(**kw):
    return pltpu.CompilerParams(vmem_limit_bytes=VMEM_LIMIT, **kw)


def _mod_kernel(cond_ref, w_ref, b_ref, o_ref):
    a = cond_ref[...]
    a = a * (1.0 / (1.0 + jnp.exp(-a)))
    ah, al = _split_bf16(a)
    wh, wl = _split_bf16(w_ref[...])
    res = _dot(ah, wh) + _dot(al, wh) + _dot(ah, wl) + b_ref[...]
    for c in range(o_ref.shape[1]):
        o_ref[:, c, 0, :] = res[:, c * D_MODEL:(c + 1) * D_MODEL]


def _modulation(cond, w_mod, b_mod):
    depth = w_mod.shape[0]
    n_out = w_mod.shape[2]
    per_tile = MOD_TN // D_MODEL
    return pl.pallas_call(
        _mod_kernel,
        out_shape=jax.ShapeDtypeStruct((depth, MOD_ROWS, n_out // D_MODEL, 1, D_MODEL), F32),
        grid=(depth, n_out // MOD_TN),
        in_specs=[
            pl.BlockSpec((MOD_ROWS, D_MODEL), lambda l, j: (0, 0)),
            pl.BlockSpec((None, D_MODEL, MOD_TN), lambda l, j: (l, 0, j)),
            pl.BlockSpec((None, 1, MOD_TN), lambda l, j: (l, 0, j)),
        ],
        out_specs=pl.BlockSpec((None, MOD_ROWS, per_tile, 1, D_MODEL), lambda l, j: (l, 0, j, 0, 0)),
        compiler_params=_params(),
        name="modulation",
    )(cond, w_mod, b_mod.reshape(depth, 1, n_out))


def _rpb_kernel(rpb_ref, o_ref):
    shape = (GRID_W, LANES)
    q = _iota(shape, 0)
    lane = _iota(shape, 1)
    k = lane % GRID_W
    left = lane < GRID_W
    col_start = jnp.clip(q - NA_WIN_COLS // 2, 0, GRID_W - NA_WIN_COLS)
    col_ok = (k >= col_start) & (k < col_start + NA_WIN_COLS)
    centre = NA_WIN_COLS - 1
    for d in range(NA_DROWS - 1):
        tiles = []
        for h in range(NA_HEADS):
            t1 = pltpu.roll(jnp.broadcast_to(rpb_ref[h, d:d + 1, :], shape), LANES - centre, axis=1,
                            stride=1, stride_axis=0)
            t2 = pltpu.roll(jnp.broadcast_to(rpb_ref[h, d + 1:d + 2, :], shape), GRID_W - centre, axis=1,
                            stride=1, stride_axis=0)
            t = jnp.where(left, t1, t2)
            tiles.append(jnp.where(col_ok, t * LOG2E, NEG_INF))
        o_ref[d] = jnp.concatenate(tiles, axis=0).T


def _rpb_tables(na_rpb):
    depth = na_rpb.shape[0]
    padded = jnp.pad(na_rpb, ((0, 0), (0, 0), (0, 1), (0, LANES - NA_DCOLS)))
    return pl.pallas_call(
        _rpb_kernel,
        out_shape=jax.ShapeDtypeStruct((depth, NA_DROWS - 1, LANES, NA_HEADS * GRID_W), F32),
        grid=(depth,),
        in_specs=[pl.BlockSpec((None,) + padded.shape[1:], lambda l: (l, 0, 0, 0))],
        out_specs=pl.BlockSpec((None, NA_DROWS - 1, LANES, NA_HEADS * GRID_W), lambda l: (l, 0, 0, 0)),
        compiler_params=_params(),
        name="rpb_tables",
    )(padded)


def _head_rms(t, gain):
    width = t.shape[1]
    sq = (t * t).astype(BF16)
    blk = min(width, MXU_DIM)
    ones = jnp.where(_iota((blk, blk), 0) // HEAD_DIM == _iota((blk, blk), 1) // HEAD_DIM, 1.0, 0.0).astype(BF16)
    sums = [_dot(sq[:, c:c + blk], ones) for c in range(0, width, blk)]
    ss = sums[0] if len(sums) == 1 else jnp.concatenate(sums, axis=1)
    return t * lax.rsqrt(ss * (1.0 / HEAD_DIM) + EPS) * gain


def _rope(t, cos, sin):
    lane = _iota((t.shape[0], LANES), 1)
    first = (lane % (HEAD_DIM // 2)) < (HEAD_DIM // 4)
    out = []
    for c in range(0, t.shape[1], LANES):
        u = t[:, c:c + LANES]
        partner = jnp.where(first, pltpu.roll(u, LANES - HEAD_DIM // 4, axis=1), pltpu.roll(u, HEAD_DIM // 4, axis=1))
        out.append(u * cos + partner * sin)
    return out[0] if len(out) == 1 else jnp.concatenate(out, axis=1)


def _dup_kv(t):
    lane = _iota(t.shape, 1)
    swapped = pltpu.roll(t, HEAD_DIM, axis=1)
    left = lane < HEAD_DIM
    return jnp.concatenate([jnp.where(left, t, swapped), jnp.where(left, swapped, t)], axis=1)


def _inproj_kernel(*refs, rope, kv_only, n_w, sub):
    x_ref, mod_ref, g1_ref = refs[:3]
    w_refs = refs[3:3 + n_w]
    gnq_ref, gnk_ref, gwq_ref, gwk_ref = refs[3 + n_w:7 + n_w]
    rest = refs[7 + n_w:]
    if rope:
        cos_ref, sin_ref = rest[:2]
        rest = rest[2:]
    out_refs = rest
    if kv_only:
        nak_ref, nav_ref, wak_ref, wav_ref = out_refs
        o_nk, o_nv, o_wk, o_wv, o_end = 0, NA_WIDTH, 2 * NA_WIDTH, 2 * NA_WIDTH + WA_KVW, 2 * NA_WIDTH + 2 * WA_KVW
    else:
        naq_ref, nak_ref, nav_ref, cvu_ref, cvb_ref, waq_ref, wak_ref, wav_ref = out_refs
        o_nk, o_nv, o_wk, o_wv, o_end = OFF_NA_K, OFF_NA_V, OFF_WA_K, OFF_WA_V, IN_WIDTH
    def project(rows):
        x = x_ref[rows, :]
        ms = jnp.mean(x * x, axis=-1, keepdims=True)
        h = x * lax.rsqrt(ms + EPS) * g1_ref[...]
        h = h * (1.0 + mod_ref[1]) + mod_ref[0]
        hb = h.astype(BF16)
        return jnp.concatenate([_dot(hb, w_ref[...]) for w_ref in w_refs], axis=1)

    def finish(rows, p):
        nak_ref[rows, :] = _head_rms(p[:, o_nk:o_nk + NA_WIDTH], gnk_ref[...]).astype(BF16)
        nav_ref[rows, :] = p[:, o_nv:o_nv + NA_WIDTH].astype(BF16)
        wk = _head_rms(p[:, o_wk:o_wv], gwk_ref[...])
        if rope:
            wk = _rope(wk, cos_ref[rows, :], sin_ref[rows, :])
        wak_ref[rows, :] = _dup_kv(wk).astype(BF16)
        vv = p[:, o_wv:o_end]
        left = _iota(vv.shape, 1) < HEAD_DIM
        wav_ref[rows, :] = jnp.concatenate([jnp.where(left, vv, 1.0), jnp.where(left, pltpu.roll(vv, HEAD_DIM, axis=1), 1.0)],
                                           axis=1).astype(BF16)
        if kv_only:
            return
        naq_ref[rows, :] = (_head_rms(p[:, OFF_NA_Q:OFF_NA_K], gnq_ref[...]) * Q_SCALE).astype(BF16)
        cvu_ref[rows, :] = p[:, OFF_CV_C:OFF_WA_Q] * p[:, OFF_CV_X:OFF_CV_B]
        cvb_ref[rows, :] = p[:, OFF_CV_B:OFF_CV_C]
        wq = _head_rms(p[:, OFF_WA_Q:OFF_WA_K], gwq_ref[...])
        if rope:
            wq = _rope(wq, cos_ref[rows, :], sin_ref[rows, :])
        waq_ref[rows, :] = (wq * Q_SCALE).astype(BF16)

    for r0 in range(0, x_ref.shape[0], sub):
        rows = slice(r0, r0 + sub)
        finish(rows, project(rows))


def _in_proj(x2d, mod, layer, mod_row, g1, w_in, gains, rope_tabs, tm, kv_only=False):
    rows = x2d.shape[0]
    rope = rope_tabs is not None
    row_spec = lambda w: pl.BlockSpec((tm, w), lambda i: (i, 0))
    const = lambda shape: pl.BlockSpec(shape, lambda i: (0,) * len(shape))
    once = pl.Buffered(1)
    if kv_only:
        col_blocks = [OFF_NA_K // MXU_DIM, OFF_NA_V // MXU_DIM, OFF_WA_K // MXU_DIM]
        w_specs = [pl.BlockSpec((D_MODEL, MXU_DIM), functools.partial(lambda j, i: (0, j), j),
                                pipeline_mode=once) for j in col_blocks]
        widths = [(NA_WIDTH, BF16), (NA_WIDTH, BF16), (WA_K2W, BF16), (WA_V3W, BF16)]
    else:
        w_specs = [pl.BlockSpec((D_MODEL, IN_WIDTH), lambda i: (0, 0), pipeline_mode=once)]
        widths = [(NA_WIDTH, BF16), (NA_WIDTH, BF16), (NA_WIDTH, BF16), (CONV_CH, F32), (CONV_CH, F32),
                  (WA_QW, BF16), (WA_K2W, BF16), (WA_V3W, BF16)]
    n_w = len(w_specs)
    in_specs = [
        row_spec(D_MODEL),
        pl.BlockSpec((None, None, 6, 1, D_MODEL), lambda i: (layer, mod_row(i), 0, 0, 0)),
        const((1, D_MODEL)),
        *w_specs,
        const((1, NA_WIDTH)), const((1, NA_WIDTH)), const((1, WA_QW)), const((1, WA_KVW)),
    ]
    args = [x2d, mod, g1, *([w_in] * n_w), *gains]
    if rope:
        seq_tiles = rope_tabs[0].shape[0] // tm
        in_specs += [pl.BlockSpec((tm, LANES), lambda i: (i % seq_tiles, 0))] * 2
        args += list(rope_tabs)
    return pl.pallas_call(
        functools.partial(_inproj_kernel, rope=rope, kv_only=kv_only, n_w=n_w, sub=min(tm, IN_SUB)),
        out_shape=[jax.ShapeDtypeStruct((rows, w), dt) for w, dt in widths],
        grid=(rows // tm,),
        in_specs=in_specs,
        out_specs=[row_spec(w) for w, _ in widths],
        compiler_params=_params(),
        name="in_proj_rope" if rope else ("in_proj_ctx_kv" if kv_only else "in_proj_ctx"),
    )(*args)


def _lane_chunks(s):
    return [s[:, c:c + LANES] for c in range(0, s.shape[1], LANES)]


def _row_max(scores, extra=None):
    chunks = [ch for s in scores for ch in _lane_chunks(s)]
    if extra is not None:
        chunks.append(extra)
    folded = functools.reduce(jnp.maximum, chunks)
    return jnp.broadcast_to(jnp.max(folded, axis=-1, keepdims=True), folded.shape)


def _softmax_pv(parts, extra=None, sums_from_values=False, m=None):
    lane_chunks = _lane_chunks
    if m is None:
        m = _row_max([s for s, _ in parts], extra)
    ps = [jnp.concatenate([jnp.exp2(ch - m) for ch in lane_chunks(s)], axis=1) for s, _ in parts]
    pv = _dot(jnp.concatenate([p.astype(BF16) for p in ps], axis=1),
              jnp.concatenate([v for _, v in parts], axis=0))
    if sums_from_values:
        denom = pv[:, -LANES:]
        pv = pv[:, :-LANES]
        if extra is not None:
            denom = denom + jnp.exp2(extra - m)
        return pv / denom
    psum = functools.reduce(jnp.add, [ch for p in ps for ch in lane_chunks(p)])
    if extra is not None:
        psum = psum + jnp.where(_iota(extra.shape, 1) == 0, jnp.exp2(extra - m), 0.0)
    return pv / jnp.sum(psum, axis=-1, keepdims=True)


def _stack_heads(q, n_heads):
    head = _iota(q.shape, 1) // HEAD_DIM
    zero = jnp.zeros_like(q)
    return jnp.concatenate([jnp.where(head == h, q, zero) for h in range(n_heads)], axis=0)


def _unstack_heads(o, n_heads):
    rows = o.shape[0] // n_heads
    head = _iota((rows, o.shape[1]), 1) // HEAD_DIM
    out = jnp.where(head == 0, o[:rows], 0.0)
    for h in range(1, n_heads):
        out = jnp.where(head == h, o[h * rows:(h + 1) * rows], out)
    return out


def _group_rms(o, gain):
    ms = jnp.mean(o * o, axis=-1, keepdims=True)
    return o * lax.rsqrt(ms + EPS) * gain


SUBLANES = 8


def _wa_scores(qg, keys):
    rows = qg.shape[0]
    left = _iota((rows, LANES), 1) < HEAD_DIM
    zero = jnp.zeros((rows, LANES), qg.dtype)
    blocks = []
    for pair in range(WA_GROUP // 2):
        qp = qg[:, pair * LANES:(pair + 1) * LANES]
        blocks += [jnp.where(left, qp, zero), jnp.where(left, zero, qp)]
    qs = jnp.concatenate(blocks, axis=0)
    scores = []
    for k2, bias in keys:
        s = _dot_nt(k2, qs)
        if bias is not None:
            s = s + jnp.concatenate([bias] * WA_GROUP, axis=1)
        scores.append(s)
    return scores


def _wa_sink_rows(rows, sinks):
    return jnp.concatenate([jnp.full((SUBLANES, rows), sinks[g] * LOG2E, F32) for g in range(WA_GROUP)], axis=1)


def _wa_col_max(scores, sink_rows):
    m = sink_rows[:1]
    for s in scores:
        m = jnp.maximum(m, jnp.max(s, axis=0, keepdims=True))
    return jnp.broadcast_to(m, sink_rows.shape)


def _wa_finish(scores, values, sinks, m=None):
    rows = scores[0].shape[1] // WA_GROUP
    sink_rows = _wa_sink_rows(rows, sinks)
    if m is None:
        m = _wa_col_max(scores, sink_rows)
    p = jnp.concatenate([jnp.exp2(s - m[:1]).astype(BF16) for s in scores], axis=0)
    v = jnp.concatenate(values, axis=0)
    ot = lax.dot_general(v, p, (((0,), (0,)), ((), ())), preferred_element_type=F32)
    denom = ot[HEAD_DIM:HEAD_DIM + SUBLANES] + jnp.exp2(sink_rows - m)
    on = ot[:HEAD_DIM] / denom[:1]
    pairs = [jnp.concatenate([on[:, (2 * pr) * rows:(2 * pr + 1) * rows],
                              on[:, (2 * pr + 1) * rows:(2 * pr + 2) * rows]], axis=0).T
             for pr in range(WA_GROUP // 2)]
    return jnp.concatenate(pairs, axis=1)


def _na_steps(q_ref, k_ref, v_ref, kc_ref, vc_ref, bias_ref, g_ref, o_ref, vt_ref, vct_ref):
    seq = q_ref.shape[0]
    n_rows = seq // GRID_W
    win = NA_WIN_ROWS * GRID_W
    lc = kc_ref.shape[0]
    gain = g_ref[...]

    def prepare():
        vt_ref[0] = v_ref[...].astype(F32).T.astype(BF16)
        vt_ref[1, :, :seq - GRID_W] = v_ref[GRID_W:, :].astype(F32).T.astype(BF16)
        vct_ref[...] = vc_ref[...].astype(F32).T.astype(BF16)

    def window(r):
        start = jnp.clip(r - NA_WIN_ROWS // 2, 0, n_rows - NA_WIN_ROWS)
        return start, start - r + (NA_WIN_ROWS - 1)

    def scores(r, s_ref):
        start, d0 = window(r)
        tok0 = pl.multiple_of(start * GRID_W, GRID_W)
        q = q_ref[pl.ds(pl.multiple_of(r * GRID_W, GRID_W), GRID_W), :]
        qs = _stack_heads(q, NA_HEADS)
        bias = jnp.concatenate([bias_ref[d0 + 2 * j] for j in range(NA_WIN_ROWS // 2)], axis=0)
        s_ctx = _dot_nt(kc_ref[...], qs)
        s_loc = _dot_nt(k_ref[pl.ds(tok0, win), :], qs) + bias
        m = jnp.maximum(jnp.max(s_ctx, axis=0, keepdims=True), jnp.max(s_loc, axis=0, keepdims=True))
        s_ref[:lc, :] = s_ctx
        s_ref[lc:lc + win, :] = s_loc
        s_ref[lc + win:, :] = jnp.broadcast_to(m, (SUBLANES, m.shape[1]))

    def finish(r, s_ref):
        start, _ = window(r)
        odd = start % 2
        lane0 = pl.multiple_of((start - odd) * GRID_W, LANES)
        m = s_ref[lc + win:lc + win + 1, :]
        p_ctx = jnp.exp2(s_ref[:lc, :] - m)
        p_loc = jnp.exp2(s_ref[lc:lc + win, :] - m)
        denom = jnp.sum(p_ctx, axis=0, keepdims=True) + jnp.sum(p_loc, axis=0, keepdims=True)
        ot = _dot(vct_ref[...], p_ctx.astype(BF16)) + _dot(vt_ref[odd, :, pl.ds(lane0, win)], p_loc.astype(BF16))
        o4 = (ot * (1.0 / denom)).T
        o = _unstack_heads(o4, NA_HEADS)
        o_ref[pl.ds(pl.multiple_of(r * GRID_W, GRID_W), GRID_W), :] = _group_rms(o, gain).astype(o_ref.dtype)

    def step(r, cur_ref, nxt_ref):
        scores(jnp.minimum(r + 1, n_rows - 1), nxt_ref)
        finish(r, cur_ref)

    return prepare, scores, step


def _wa_steps(sink_ref, q_ref, k_ref, v_ref, kc_ref, vc_ref, g_ref, o_ref):
    seq = q_ref.shape[0]
    n_blocks = seq // WA_BLOCK
    lc = kc_ref.shape[0]
    gain = g_ref[...]
    key = _iota((WA_BAND, WA_BLOCK), 0)
    qry = _iota((WA_BAND, WA_BLOCK), 1)

    def window(n):
        q0 = pl.multiple_of(n * WA_BLOCK, WA_BLOCK)
        k0 = pl.multiple_of(jnp.clip(q0 - WA_BLOCK, 0, seq - WA_BAND), WA_BLOCK)
        return q0, k0

    def scores(n, kh, s_ref):
        q0, k0 = window(n)
        band = jnp.where(jnp.abs((k0 + key) - (q0 + qry)) <= WA_WINDOW, 0.0, NEG_INF)
        lanes = slice(kh * LANES, (kh + 1) * LANES)
        qg = q_ref[pl.ds(q0, WA_BLOCK), kh * WA_GROUP * HEAD_DIM:(kh + 1) * WA_GROUP * HEAD_DIM]
        s_ctx, s_loc = _wa_scores(qg, [(kc_ref[:, lanes], None), (k_ref[pl.ds(k0, WA_BAND), lanes], band)])
        s_ref[kh, :lc, :] = s_ctx
        s_ref[kh, lc:lc + WA_BAND, :] = s_loc
        s_ref[kh, lc + WA_BAND:, :] = _wa_col_max([s_ctx, s_loc], _wa_sink_rows(WA_BLOCK, sinks(kh)))

    def sinks(kh):
        return [sink_ref[kh * WA_GROUP + g] for g in range(WA_GROUP)]

    def finish(n, kh, s_ref):
        _, k0 = window(n)
        lanes = slice(kh * LANES, (kh + 1) * LANES)
        return _wa_finish([s_ref[kh, :lc, :], s_ref[kh, lc:lc + WA_BAND, :]],
                          [vc_ref[:, lanes], v_ref[pl.ds(k0, WA_BAND), lanes]], sinks(kh),
                          m=s_ref[kh, lc + WA_BAND:, :])

    def step(n, cur_ref, nxt_ref):
        nxt = jnp.minimum(n + 1, n_blocks - 1)
        for kh in range(WA_KV_HEADS):
            scores(nxt, kh, nxt_ref)
        o = jnp.concatenate([finish(n, kh, cur_ref) for kh in range(WA_KV_HEADS)], axis=1)
        q0, _ = window(n)
        o_ref[pl.ds(q0, WA_BLOCK), :] = _group_rms(o, gain).astype(o_ref.dtype)

    return scores, step


def _attn_kernel(sink_ref, nq_ref, nk_ref, nv_ref, nkc_ref, nvc_ref, bias_ref, gn_ref,
                 wq_ref, wk_ref, wv_ref, wkc_ref, wvc_ref, gw_ref, *rest, n_cast):
    f32_refs = rest[:n_cast]
    on_ref, ow_ref = rest[n_cast:n_cast + 2]
    bf16_refs = rest[n_cast + 2:2 * n_cast + 2]
    nsa_ref, nsb_ref, wsa_ref, wsb_ref, vt_ref, vct_ref = rest[2 * n_cast + 2:]
    for src_ref, dst_ref in zip(f32_refs, bf16_refs):
        dst_ref[...] = src_ref[...].astype(dst_ref.dtype)
    na_prepare, na_scores, na_step = _na_steps(nq_ref, nk_ref, nv_ref, nkc_ref, nvc_ref, bias_ref, gn_ref, on_ref,
                                               vt_ref, vct_ref)
    na_prepare()
    wa_scores, wa_step = _wa_steps(sink_ref, wq_ref, wk_ref, wv_ref, wkc_ref, wvc_ref, gw_ref, ow_ref)
    n_blocks = wq_ref.shape[0] // WA_BLOCK

    for kh in range(WA_KV_HEADS):
        wa_scores(0, kh, wsa_ref)
    na_scores(0, nsa_ref)

    def body(i, carry):
        for half, (w_cur, w_nxt) in enumerate(((wsa_ref, wsb_ref), (wsb_ref, wsa_ref))):
            n = 2 * i + half
            wa_step(n, w_cur, w_nxt)
            na_step(2 * n, nsa_ref, nsb_ref)
            na_step(2 * n + 1, nsb_ref, nsa_ref)
        return carry

    lax.fori_loop(0, n_blocks // 2, body, 0, unroll=2)


def _attention(sink, nq, nk, nv, nkc, nvc, bias, layer, gain_na, wq, wk2, wv3, wkc2, wvc3, gain_wa, to_bf16):
    batch, seq, _ = nq.shape
    slab_in = lambda w, j: pl.BlockSpec((None, w.shape[1] // batch, w.shape[2]), lambda b: (j, b, 0))
    slab_out = lambda w: pl.BlockSpec((w.shape[1] // batch, w.shape[2]), lambda b: (b, 0))
    lc = nkc.shape[1]
    assert WA_BLOCK == 2 * GRID_W and (seq // WA_BLOCK) % 2 == 0
    per_b = lambda n, w: pl.BlockSpec((None, n, w), lambda b: (b, 0, 0))
    na_scratch = pltpu.VMEM((lc + NA_WIN_ROWS * GRID_W + SUBLANES, NA_HEADS * GRID_W), F32)
    vt_scratch = pltpu.VMEM((2, NA_WIDTH, seq), BF16)
    vct_scratch = pltpu.VMEM((NA_WIDTH, lc), BF16)
    wa_scratch = pltpu.VMEM((WA_KV_HEADS, lc + WA_BAND + SUBLANES, WA_GROUP * WA_BLOCK), F32)
    outs = pl.pallas_call(
        functools.partial(_attn_kernel, n_cast=len(to_bf16)),
        out_shape=[jax.ShapeDtypeStruct((batch, seq, NA_WIDTH), BF16),
                   jax.ShapeDtypeStruct((batch, seq, WA_QW), BF16),
                   *[jax.ShapeDtypeStruct(w.shape[1:], BF16) for w, _ in to_bf16]],
        grid=(batch,),
        in_specs=[pl.BlockSpec(memory_space=pltpu.SMEM),
                  per_b(seq, NA_WIDTH), per_b(seq, NA_WIDTH), per_b(seq, NA_WIDTH),
                  per_b(lc, NA_WIDTH), per_b(lc, NA_WIDTH),
                  pl.BlockSpec((None,) + bias.shape[1:], lambda b: (layer, 0, 0, 0)),
                  pl.BlockSpec((1, NA_WIDTH), lambda b: (0, 0)),
                  per_b(seq, WA_QW), per_b(seq, WA_K2W), per_b(seq, WA_V3W),
                  per_b(lc, WA_K2W), per_b(lc, WA_V3W),
                  pl.BlockSpec((1, WA_QW), lambda b: (0, 0)),
                  *[slab_in(w, j) for w, j in to_bf16]],
        out_specs=[per_b(seq, NA_WIDTH), per_b(seq, WA_QW), *[slab_out(w) for w, _ in to_bf16]],
        scratch_shapes=[na_scratch, na_scratch, wa_scratch, wa_scratch, vt_scratch, vct_scratch],
        compiler_params=_params(),
        name="attention",
    )(sink, nq, nk, nv, nkc, nvc, bias, gain_na, wq, wk2, wv3, wkc2, wvc3, gain_wa, *[w for w, _ in to_bf16])
    return outs[0], outs[1], outs[2:]


def _ctx_attn(sink_ref, qn_ref, kn_ref, vn_ref, qw_ref, kw_ref, vw_ref, gn_ref, gw_ref):
    qs = _stack_heads(qn_ref[...], NA_HEADS)
    o4 = _softmax_pv([(_dot_nt(qs, kn_ref[...]), vn_ref[...])])
    na_n = _group_rms(_unstack_heads(o4, NA_HEADS), gn_ref[...]).astype(BF16)
    outs = []
    for kh in range(WA_KV_HEADS):
        lanes = slice(kh * LANES, (kh + 1) * LANES)
        qg = qw_ref[:, kh * WA_GROUP * HEAD_DIM:(kh + 1) * WA_GROUP * HEAD_DIM]
        sinks = [sink_ref[kh * WA_GROUP + g] for g in range(WA_GROUP)]
        outs.append(_wa_finish(_wa_scores(qg, [(kw_ref[:, lanes], None)]), [vw_ref[:, lanes]], sinks))
    return na_n, _group_rms(jnp.concatenate(outs, axis=1), gw_ref[...]).astype(BF16)


def _out_mlp_kernel(*refs, seq_len, sub, ctx_attn):
    x_ref, mod_ref = refs[:2]
    n_attn = 9 if ctx_attn else 2
    attn_refs = refs[2:2 + n_attn]
    u_ref, uprev_ref, unext_ref, cvb_ref, cw_ref, cb_ref, g_ref, g2_ref, wo_ref = refs[2 + n_attn:11 + n_attn]
    rest = refs[11 + n_attn:]
    n_ffn = FFN_DIM // FFN_CHUNK
    w1_refs, w2_refs, (o_ref,) = rest[:n_ffn], rest[n_ffn:2 * n_ffn], rest[2 * n_ffn:]
    tm = x_ref.shape[0]
    i = pl.program_id(0)
    if ctx_attn:
        na_t, wa_t = _ctx_attn(*attn_refs)
    else:
        na_t, wa_t = attn_refs
    u = u_ref[...]
    at_start = (i * tm) % seq_len == 0
    at_end = ((i + 1) * tm) % seq_len == 0
    prev_row = jnp.where(at_start, 0.0, uprev_ref[7:8, :])
    next_row = jnp.where(at_end, 0.0, unext_ref[0:1, :])
    row = _iota(u.shape, 0)
    up = jnp.where(row == 0, prev_row, pltpu.roll(u, 1, axis=0))
    dn = jnp.where(row == tm - 1, next_row, pltpu.roll(u, tm - 1, axis=0))
    y = cb_ref[...] + cw_ref[0:1, :] * up + cw_ref[1:2, :] * u + cw_ref[2:3, :] * dn
    cv = _group_rms(cvb_ref[...] * y, g_ref[...]).astype(BF16)
    for r0 in range(0, tm, sub):
        rows = slice(r0, r0 + sub)
        mixed = jnp.concatenate([na_t[rows, :], cv[rows, :], wa_t[rows, :]], axis=1)
        x = x_ref[rows, :] + mod_ref[2] * _dot(mixed, wo_ref[...])
        ms = jnp.mean(x * x, axis=-1, keepdims=True)
        h = x * lax.rsqrt(ms + EPS) * g2_ref[...]
        h = (h * (1.0 + mod_ref[4]) + mod_ref[3]).astype(BF16)
        acc = None
        for w1_ref, w2_ref in zip(w1_refs, w2_refs):
            a = jnp.maximum(_dot(h, w1_ref[...]), 0.0)
            part = _dot((a * a).astype(BF16), w2_ref[...])
            acc = part if acc is None else acc + part
        o_ref[rows, :] = x + mod_ref[5] * acc


def _out_mlp(x2d, mod, layer, mod_row, attn, cv_u, cv_b, conv_w, conv_b, gain_cv, w_o, g2, w1, w2, seq_len, tm):
    rows = x2d.shape[0]
    ctx_attn = len(attn) == 9
    assert not ctx_attn or tm == seq_len
    halo = 8
    per_tile = tm // halo
    last = rows // halo - 1
    row_spec = lambda w: pl.BlockSpec((tm, w), lambda i: (i, 0))
    const = lambda shape: pl.BlockSpec(shape, lambda i: (0,) * len(shape), pipeline_mode=pl.Buffered(1))
    once = pl.Buffered(1)
    n_ffn = FFN_DIM // FFN_CHUNK
    wo_spec = pl.BlockSpec((D_MODEL, D_MODEL), lambda i: (0, 0), pipeline_mode=once)
    w1_specs = [pl.BlockSpec((D_MODEL, FFN_CHUNK), functools.partial(lambda j, i: (0, j), j),
                             pipeline_mode=once) for j in range(n_ffn)]
    w2_specs = [pl.BlockSpec((FFN_CHUNK, D_MODEL), functools.partial(lambda j, i: (j, 0), j),
                             pipeline_mode=once) for j in range(n_ffn)]
    small = lambda a: pl.BlockSpec(a.shape, lambda i: (0,) * a.ndim)
    if ctx_attn:
        attn_specs = [pl.BlockSpec(memory_space=pltpu.SMEM), *[row_spec(a.shape[1]) for a in attn[1:7]],
                      small(attn[7]), small(attn[8])]
    else:
        attn_specs = [row_spec(NA_WIDTH), row_spec(WA_QW)]
    return pl.pallas_call(
        functools.partial(_out_mlp_kernel, seq_len=seq_len, sub=min(tm, MLP_SUB), ctx_attn=ctx_attn),
        out_shape=jax.ShapeDtypeStruct((rows, D_MODEL), F32),
        grid=(rows // tm,),
        in_specs=[
            row_spec(D_MODEL),
            pl.BlockSpec((None, None, 6, 1, D_MODEL), lambda i: (layer, mod_row(i), 0, 0, 0)),
            *attn_specs, row_spec(CONV_CH),
            pl.BlockSpec((halo, CONV_CH), lambda i: (jnp.maximum(i * per_tile - 1, 0), 0)),
            pl.BlockSpec((halo, CONV_CH), lambda i: (jnp.minimum((i + 1) * per_tile, last), 0)),
            row_spec(CONV_CH),
            const((3, CONV_CH)), const((1, CONV_CH)), const((1, CONV_CH)),
            const((1, D_MODEL)),
            wo_spec, *w1_specs, *w2_specs,
        ],
        out_specs=row_spec(D_MODEL),
        compiler_params=_params(),
        name="out_mlp",
    )(x2d, mod, *attn, cv_u, cv_u, cv_u, cv_b, conv_w, conv_b, gain_cv, g2,
      w_o, *([w1] * n_ffn), *([w2] * n_ffn))


def _rope_tables(seq):
    quarter = HEAD_DIM // 4
    inv = ROPE_BASE ** (-jnp.arange(quarter, dtype=F32) / quarter)
    t = jnp.arange(seq)
    ang_r = (t // GRID_W).astype(F32)[:, None] * inv[None, :]
    ang_c = (t % GRID_W).astype(F32)[:, None] * inv[None, :]
    cos = jnp.concatenate([jnp.cos(ang_r), jnp.cos(ang_r), jnp.cos(ang_c), jnp.cos(ang_c)], axis=1)
    sin = jnp.concatenate([-jnp.sin(ang_r), jnp.sin(ang_r), -jnp.sin(ang_c), jnp.sin(ang_c)], axis=1)
    reps = LANES // HEAD_DIM
    return jnp.tile(cos, (1, reps)), jnp.tile(sin, (1, reps))


def kernel(x, c, ctx, c_ctx, w_mod, b_mod, g_norm1, g_norm2, w_in, na_q_gain, na_k_gain, na_rpb,
           conv_w, conv_bias, wa_q_gain, wa_k_gain, wa_sink, g_out, w_o, w_fc1, w_fc2):
    batch, seq, d = x.shape
    lc = ctx.shape[1]
    depth = w_mod.shape[0]
    assert d == D_MODEL and seq % WA_BLOCK == 0 and seq % GRID_W == 0 and batch < MOD_ROWS
    tm = 1024
    tm_in = 2048
    tm_ctx = 512
    assert seq % tm == 0 and seq % tm_in == 0 and (batch * lc) % tm_ctx == 0

    cond = jnp.zeros((MOD_ROWS, d), F32).at[:batch].set(c).at[batch].set(c_ctx)
    mod = _modulation(cond, w_mod, b_mod)
    rpb_bias = _rpb_tables(na_rpb)
    rope_tabs = _rope_tables(seq)

    lat_row = lambda i: i // (seq // tm)
    lat_row_in = lambda i: i // (seq // tm_in)
    ctx_row = lambda i: batch

    w_in_bf = w_in[0].astype(BF16)
    xs = x.reshape(batch * seq, d)
    cs = ctx.reshape(batch * lc, d)
    for l in range(depth):
        last = l == depth - 1
        g1 = g_norm1[l].reshape(1, d)
        g2 = g_norm2[l].reshape(1, d)
        gains = [jnp.tile(na_q_gain[l], NA_HEADS).reshape(1, -1), jnp.tile(na_k_gain[l], NA_HEADS).reshape(1, -1),
                 jnp.tile(wa_q_gain[l], WA_HEADS).reshape(1, -1), jnp.tile(wa_k_gain[l], WA_KV_HEADS).reshape(1, -1)]
        go_na = g_out[l, :NA_WIDTH].reshape(1, -1)
        go_cv = g_out[l, NA_WIDTH:NA_WIDTH + CONV_CH].reshape(1, -1)
        go_wa = g_out[l, NA_WIDTH + CONV_CH:].reshape(1, -1)
        cb = conv_bias[l].reshape(1, -1)

        naq, nak, nav, cvu, cvb, waq, wak, wav = _in_proj(xs, mod, l, lat_row_in, g1, w_in_bf, gains, rope_tabs, tm_in)
        if last:
            cnak, cnav, cwak, cwav = _in_proj(cs, mod, l, ctx_row, g1, w_in_bf, gains, None, tm_ctx, kv_only=True)
        else:
            cnaq, cnak, cnav, ccvu, ccvb, cwaq, cwak, cwav = _in_proj(cs, mod, l, ctx_row, g1, w_in_bf, gains, None, tm_ctx)

        b3 = lambda a, n: a.reshape(batch, n, a.shape[-1])
        casts = [(w_o, l), (w_fc1, l), (w_fc2, l)] + ([] if last else [(w_in, l + 1)])
        na_n, wa_n, cast_out = _attention(
            wa_sink[l], b3(naq, seq), b3(nak, seq), b3(nav, seq), b3(cnak, lc), b3(cnav, lc), rpb_bias, l, go_na,
            b3(waq, seq), b3(wak, seq), b3(wav, seq), b3(cwak, lc), b3(cwav, lc), go_wa, casts)
        w_o_bf, w1_bf, w2_bf = cast_out[:3]
        if not last:
            w_in_bf = cast_out[3]
        xs = _out_mlp(xs, mod, l, lat_row, (na_n.reshape(batch * seq, -1), wa_n.reshape(batch * seq, -1)),
                      cvu, cvb, conv_w[l], cb, go_cv, w_o_bf, g2, w1_bf, w2_bf, seq, tm)
        if not last:
            ctx_attn = (wa_sink[l], cnaq, cnak, cnav, cwaq, cwak, cwav, go_na, go_wa)
            cs = _out_mlp(cs, mod, l, ctx_row, ctx_attn, ccvu, ccvb, conv_w[l], cb, go_cv,
                          w_o_bf, g2, w1_bf, w2_bf, lc, lc)
    return xs.reshape(batch, seq, d)
```

```python
import functools

import jax
import jax.numpy as jnp
from jax import lax
from jax.experimental import pallas as pl
from jax.experimental.pallas import tpu as pltpu

D_MODEL = 1024
GRID_W = 64
HEAD_DIM = 64
NA_HEADS = 4
NA_WIDTH = NA_HEADS * HEAD_DIM
CONV_CH = 256
WA_HEADS = 8
WA_KV_HEADS = 2
WA_GROUP = WA_HEADS // WA_KV_HEADS
WA_QW = WA_HEADS * HEAD_DIM
WA_KVW = WA_KV_HEADS * HEAD_DIM
WA_K2W = 2 * WA_KVW
WA_V3W = 2 * WA_KVW
NA_WIN_ROWS = 8
NA_WIN_COLS = 16
NA_DROWS = 2 * NA_WIN_ROWS - 1
NA_DCOLS = 2 * NA_WIN_COLS - 1
WA_WINDOW = 128
WA_BLOCK = 128
WA_BAND = 3 * WA_BLOCK
FFN_DIM = 4 * D_MODEL
FFN_CHUNK = 1024
ROPE_BASE = 10000.0
EPS = 1e-6
NEG_INF = -1e30
IN_WIDTH = 2304
OFF_NA_Q, OFF_NA_K, OFF_NA_V = 0, 256, 512
OFF_CV_X, OFF_CV_B, OFF_CV_C = 768, 1024, 1280
OFF_WA_Q, OFF_WA_K, OFF_WA_V = 1536, 2048, 2176
MOD_ROWS = 16
MOD_TN = 2048
IN_SUB = 256
MLP_SUB = 1024
LANES = 128
MXU_DIM = 256
VMEM_LIMIT = 60 * 1024 * 1024
LOG2E = 1.4426950408889634
Q_SCALE = LOG2E * HEAD_DIM ** -0.5

F32 = jnp.float32
BF16 = jnp.bfloat16


def _dot(a, b):
    return jnp.dot(a, b, preferred_element_type=F32)


def _dot_nt(a, b):
    return lax.dot_general(a, b, (((1,), (1,)), ((), ())), preferred_element_type=F32)


def _split_bf16(a):
    hi = a.astype(BF16)
    lo = (a - hi.astype(F32)).astype(BF16)
    return hi, lo


def _iota(shape, dim):
    return lax.broadcasted_iota(jnp.int32, shape, dim)


def _params(**kw):
    return pltpu.CompilerParams(vmem_limit_bytes=VMEM_LIMIT, **kw)


def _mod_kernel(cond_ref, w_ref, b_ref, o_ref):
    a = cond_ref[...]
    a = a * (1.0 / (1.0 + jnp.exp(-a)))
    ah, al = _split_bf16(a)
    wh, wl = _split_bf16(w_ref[...])
    rows = a.shape[0]
    both = _dot(jnp.concatenate([ah, al], axis=0), wh)
    res = both[:rows] + both[rows:] + _dot(ah, wl) + b_ref[...]
    for c in range(o_ref.shape[1]):
        o_ref[:, c, 0, :] = res[:, c * D_MODEL:(c + 1) * D_MODEL]


def _modulation(cond, w_mod, b_mod):
    depth = w_mod.shape[0]
    n_out = w_mod.shape[2]
    per_tile = MOD_TN // D_MODEL
    return pl.pallas_call(
        _mod_kernel,
        out_shape=jax.ShapeDtypeStruct((depth, MOD_ROWS, n_out // D_MODEL, 1, D_MODEL), F32),
        grid=(depth, n_out // MOD_TN),
        in_specs=[
            pl.BlockSpec((MOD_ROWS, D_MODEL), lambda l, j: (0, 0)),
            pl.BlockSpec((None, D_MODEL, MOD_TN), lambda l, j: (l, 0, j)),
            pl.BlockSpec((None, 1, MOD_TN), lambda l, j: (l, 0, j)),
        ],
        out_specs=pl.BlockSpec((None, MOD_ROWS, per_tile, 1, D_MODEL), lambda l, j: (l, 0, j, 0, 0)),
        compiler_params=_params(),
        name="modulation",
    )(cond, w_mod, b_mod.reshape(depth, 1, n_out))


def _rpb_kernel(rpb_ref, w_ref, o_ref, wbf_ref):
    wbf_ref[...] = w_ref[...].astype(BF16)
    shape = (GRID_W, LANES)
    q = _iota(shape, 0)
    lane = _iota(shape, 1)
    k = lane % GRID_W
    left = lane < GRID_W
    col_start = jnp.clip(q - NA_WIN_COLS // 2, 0, GRID_W - NA_WIN_COLS)
    col_ok = (k >= col_start) & (k < col_start + NA_WIN_COLS)
    centre = NA_WIN_COLS - 1
    for d in range(NA_DROWS - 1):
        tiles = []
        for h in range(NA_HEADS):
            t1 = pltpu.roll(jnp.broadcast_to(rpb_ref[h, d:d + 1, :], shape), LANES - centre, axis=1,
                            stride=1, stride_axis=0)
            t2 = pltpu.roll(jnp.broadcast_to(rpb_ref[h, d + 1:d + 2, :], shape), GRID_W - centre, axis=1,
                            stride=1, stride_axis=0)
            t = jnp.where(left, t1, t2)
            tiles.append(jnp.where(col_ok, t * LOG2E, NEG_INF))
        o_ref[d] = jnp.concatenate(tiles, axis=0).T


def _rpb_tables(na_rpb, w_in):
    depth = na_rpb.shape[0]
    padded = jnp.pad(na_rpb, ((0, 0), (0, 0), (0, 1), (0, LANES - NA_DCOLS)))
    slab = w_in.shape[1] // depth
    return pl.pallas_call(
        _rpb_kernel,
        out_shape=[jax.ShapeDtypeStruct((depth, NA_DROWS - 1, LANES, NA_HEADS * GRID_W), F32),
                   jax.ShapeDtypeStruct(w_in.shape[1:], BF16)],
        grid=(depth,),
        in_specs=[pl.BlockSpec((None,) + padded.shape[1:], lambda l: (l, 0, 0, 0)),
                  pl.BlockSpec((None, slab, w_in.shape[2]), lambda l: (0, l, 0))],
        out_specs=[pl.BlockSpec((None, NA_DROWS - 1, LANES, NA_HEADS * GRID_W), lambda l: (l, 0, 0, 0)),
                   pl.BlockSpec((slab, w_in.shape[2]), lambda l: (l, 0))],
        compiler_params=_params(),
        name="rpb_tables",
    )(padded, w_in)


def _head_rms(t, gain):
    width = t.shape[1]
    sq = (t * t).astype(BF16)
    blk = min(width, MXU_DIM)
    ones = jnp.where(_iota((blk, blk), 0) // HEAD_DIM == _iota((blk, blk), 1) // HEAD_DIM, 1.0, 0.0).astype(BF16)
    sums = [_dot(sq[:, c:c + blk], ones) for c in range(0, width, blk)]
    ss = sums[0] if len(sums) == 1 else jnp.concatenate(sums, axis=1)
    return t * lax.rsqrt(ss * (1.0 / HEAD_DIM) + EPS) * gain


def _rope(t, cos, sin):
    lane = _iota((t.shape[0], LANES), 1)
    first = (lane % (HEAD_DIM // 2)) < (HEAD_DIM // 4)
    out = []
    for c in range(0, t.shape[1], LANES):
        u = t[:, c:c + LANES]
        partner = jnp.where(first, pltpu.roll(u, LANES - HEAD_DIM // 4, axis=1), pltpu.roll(u, HEAD_DIM // 4, axis=1))
        out.append(u * cos + partner * sin)
    return out[0] if len(out) == 1 else jnp.concatenate(out, axis=1)


def _dup_kv(t):
    lane = _iota(t.shape, 1)
    swapped = pltpu.roll(t, HEAD_DIM, axis=1)
    left = lane < HEAD_DIM
    return jnp.concatenate([jnp.where(left, t, swapped), jnp.where(left, swapped, t)], axis=1)


def _inproj_kernel(*refs, rope, kv_only, n_w, sub):
    x_ref, mod_ref, g1_ref = refs[:3]
    w_refs = refs[3:3 + n_w]
    gnq_ref, gnk_ref, gwq_ref, gwk_ref = refs[3 + n_w:7 + n_w]
    rest = refs[7 + n_w:]
    if rope:
        cos_ref, sin_ref = rest[:2]
        rest = rest[2:]
    out_refs = rest
    if kv_only:
        nak_ref, nav_ref, wak_ref, wav_ref = out_refs
        o_nk, o_nv, o_wk, o_wv, o_end = 0, NA_WIDTH, 2 * NA_WIDTH, 2 * NA_WIDTH + WA_KVW, 2 * NA_WIDTH + 2 * WA_KVW
    else:
        naq_ref, nak_ref, nav_ref, cvu_ref, cvb_ref, waq_ref, wak_ref, wav_ref = out_refs
        o_nk, o_nv, o_wk, o_wv, o_end = OFF_NA_K, OFF_NA_V, OFF_WA_K, OFF_WA_V, IN_WIDTH
    def project(rows):
        x = x_ref[rows, :]
        ms = jnp.mean(x * x, axis=-1, keepdims=True)
        h = x * lax.rsqrt(ms + EPS) * g1_ref[...]
        h = h * (1.0 + mod_ref[1]) + mod_ref[0]
        hb = h.astype(BF16)
        return jnp.concatenate([_dot(hb, w_ref[...]) for w_ref in w_refs], axis=1)

    def finish(rows, p):
        nak_ref[rows, :] = _head_rms(p[:, o_nk:o_nk + NA_WIDTH], gnk_ref[...]).astype(BF16)
        nav_ref[rows, :] = p[:, o_nv:o_nv + NA_WIDTH].astype(BF16)
        wk = _head_rms(p[:, o_wk:o_wv], gwk_ref[...])
        if rope:
            wk = _rope(wk, cos_ref[rows, :], sin_ref[rows, :])
        wak_ref[rows, :] = _dup_kv(wk).astype(BF16)
        vv = p[:, o_wv:o_end]
        left = _iota(vv.shape, 1) < HEAD_DIM
        wav_ref[rows, :] = jnp.concatenate([jnp.where(left, vv, 1.0), jnp.where(left, pltpu.roll(vv, HEAD_DIM, axis=1), 1.0)],
                                           axis=1).astype(BF16)
        if kv_only:
            return
        naq_ref[rows, :] = (_head_rms(p[:, OFF_NA_Q:OFF_NA_K], gnq_ref[...]) * Q_SCALE).astype(BF16)
        cvu_ref[rows, :] = p[:, OFF_CV_C:OFF_WA_Q] * p[:, OFF_CV_X:OFF_CV_B]
        cvb_ref[rows, :] = p[:, OFF_CV_B:OFF_CV_C]
        wq = _head_rms(p[:, OFF_WA_Q:OFF_WA_K], gwq_ref[...])
        if rope:
            wq = _rope(wq, cos_ref[rows, :], sin_ref[rows, :])
        waq_ref[rows, :] = (wq * Q_SCALE).astype(BF16)

    for r0 in range(0, x_ref.shape[0], sub):
        rows = slice(r0, r0 + sub)
        finish(rows, project(rows))


def _in_proj(x2d, mod, layer, mod_row, g1, w_in, gains, rope_tabs, tm, kv_only=False):
    rows = x2d.shape[0]
    rope = rope_tabs is not None
    row_spec = lambda w: pl.BlockSpec((tm, w), lambda i: (i, 0))
    const = lambda shape: pl.BlockSpec(shape, lambda i: (0,) * len(shape))
    once = pl.Buffered(1)
    if kv_only:
        col_blocks = [OFF_NA_K // MXU_DIM, OFF_NA_V // MXU_DIM, OFF_WA_K // MXU_DIM]
        w_specs = [pl.BlockSpec((D_MODEL, MXU_DIM), functools.partial(lambda j, i: (0, j), j),
                                pipeline_mode=once) for j in col_blocks]
        widths = [(NA_WIDTH, BF16), (NA_WIDTH, BF16), (WA_K2W, BF16), (WA_V3W, BF16)]
    else:
        w_specs = [pl.BlockSpec((D_MODEL, IN_WIDTH), lambda i: (0, 0), pipeline_mode=once)]
        widths = [(NA_WIDTH, BF16), (NA_WIDTH, BF16), (NA_WIDTH, BF16), (CONV_CH, F32), (CONV_CH, F32),
                  (WA_QW, BF16), (WA_K2W, BF16), (WA_V3W, BF16)]
    n_w = len(w_specs)
    in_specs = [
        row_spec(D_MODEL),
        pl.BlockSpec((None, None, 6, 1, D_MODEL), lambda i: (layer, mod_row(i), 0, 0, 0)),
        const((1, D_MODEL)),
        *w_specs,
        const((1, NA_WIDTH)), const((1, NA_WIDTH)), const((1, WA_QW)), const((1, WA_KVW)),
    ]
    args = [x2d, mod, g1, *([w_in] * n_w), *gains]
    if rope:
        seq_tiles = rope_tabs[0].shape[0] // tm
        in_specs += [pl.BlockSpec((tm, LANES), lambda i: (i % seq_tiles, 0))] * 2
        args += list(rope_tabs)
    return pl.pallas_call(
        functools.partial(_inproj_kernel, rope=rope, kv_only=kv_only, n_w=n_w, sub=min(tm, IN_SUB)),
        out_shape=[jax.ShapeDtypeStruct((rows, w), dt) for w, dt in widths],
        grid=(rows // tm,),
        in_specs=in_specs,
        out_specs=[row_spec(w) for w, _ in widths],
        compiler_params=_params(),
        name="in_proj_rope" if rope else ("in_proj_ctx_kv" if kv_only else "in_proj_ctx"),
    )(*args)


def _lane_chunks(s):
    return [s[:, c:c + LANES] for c in range(0, s.shape[1], LANES)]


def _row_max(scores, extra=None):
    chunks = [ch for s in scores for ch in _lane_chunks(s)]
    if extra is not None:
        chunks.append(extra)
    folded = functools.reduce(jnp.maximum, chunks)
    return jnp.broadcast_to(jnp.max(folded, axis=-1, keepdims=True), folded.shape)


def _softmax_pv(parts, extra=None, sums_from_values=False, m=None):
    lane_chunks = _lane_chunks
    if m is None:
        m = _row_max([s for s, _ in parts], extra)
    ps = [jnp.concatenate([jnp.exp2(ch - m) for ch in lane_chunks(s)], axis=1) for s, _ in parts]
    pv = _dot(jnp.concatenate([p.astype(BF16) for p in ps], axis=1),
              jnp.concatenate([v for _, v in parts], axis=0))
    if sums_from_values:
        denom = pv[:, -LANES:]
        pv = pv[:, :-LANES]
        if extra is not None:
            denom = denom + jnp.exp2(extra - m)
        return pv / denom
    psum = functools.reduce(jnp.add, [ch for p in ps for ch in lane_chunks(p)])
    if extra is not None:
        psum = psum + jnp.where(_iota(extra.shape, 1) == 0, jnp.exp2(extra - m), 0.0)
    return pv / jnp.sum(psum, axis=-1, keepdims=True)


def _stack_heads(q, n_heads):
    head = _iota(q.shape, 1) // HEAD_DIM
    zero = jnp.zeros_like(q)
    return jnp.concatenate([jnp.where(head == h, q, zero) for h in range(n_heads)], axis=0)


def _unstack_heads(o, n_heads):
    rows = o.shape[0] // n_heads
    head = _iota((rows, o.shape[1]), 1) // HEAD_DIM
    out = jnp.where(head == 0, o[:rows], 0.0)
    for h in range(1, n_heads):
        out = jnp.where(head == h, o[h * rows:(h + 1) * rows], out)
    return out


def _group_rms(o, gain):
    ms = jnp.mean(o * o, axis=-1, keepdims=True)
    return o * lax.rsqrt(ms + EPS) * gain


SUBLANES = 8


def _wa_scores(qg, keys):
    rows = qg.shape[0]
    left = _iota((rows, LANES), 1) < HEAD_DIM
    zero = jnp.zeros((rows, LANES), qg.dtype)
    blocks = []
    for pair in range(WA_GROUP // 2):
        qp = qg[:, pair * LANES:(pair + 1) * LANES]
        blocks += [jnp.where(left, qp, zero), jnp.where(left, zero, qp)]
    qs = jnp.concatenate(blocks, axis=0)
    scores = []
    for k2, bias in keys:
        s = _dot_nt(k2, qs)
        if bias is not None:
            s = s + jnp.concatenate([bias] * WA_GROUP, axis=1)
        scores.append(s)
    return scores


def _wa_sink_rows(rows, sinks):
    return jnp.concatenate([jnp.full((SUBLANES, rows), sinks[g] * LOG2E, F32) for g in range(WA_GROUP)], axis=1)


def _wa_col_max(scores, sink_rows):
    m = sink_rows[:1]
    for s in scores:
        m = jnp.maximum(m, jnp.max(s, axis=0, keepdims=True))
    return jnp.broadcast_to(m, sink_rows.shape)


def _wa_finish(scores, values, sinks, m=None):
    rows = scores[0].shape[1] // WA_GROUP
    sink_rows = _wa_sink_rows(rows, sinks)
    if m is None:
        m = _wa_col_max(scores, sink_rows)
    p = jnp.concatenate([jnp.exp2(s - m[:1]).astype(BF16) for s in scores], axis=0)
    v = jnp.concatenate(values, axis=0)
    ot = lax.dot_general(v, p, (((0,), (0,)), ((), ())), preferred_element_type=F32)
    denom = ot[HEAD_DIM:HEAD_DIM + SUBLANES] + jnp.exp2(sink_rows - m)
    on = ot[:HEAD_DIM] / denom[:1]
    pairs = [jnp.concatenate([on[:, (2 * pr) * rows:(2 * pr + 1) * rows],
                              on[:, (2 * pr + 1) * rows:(2 * pr + 2) * rows]], axis=0).T
             for pr in range(WA_GROUP // 2)]
    return jnp.concatenate(pairs, axis=1)


def _na_steps(q_ref, k_ref, v_ref, kc_ref, vc_ref, bias_ref, g_ref, o_ref, vt_ref, vct_ref):
    seq = q_ref.shape[0]
    n_rows = seq // GRID_W
    win = NA_WIN_ROWS * GRID_W
    lc = kc_ref.shape[0]
    gain = g_ref[...]

    def prepare():
        vt_ref[0] = v_ref[...].astype(F32).T.astype(BF16)
        vt_ref[1, :, :seq - GRID_W] = v_ref[GRID_W:, :].astype(F32).T.astype(BF16)
        vct_ref[...] = vc_ref[...].astype(F32).T.astype(BF16)

    def window(r):
        start = jnp.clip(r - NA_WIN_ROWS // 2, 0, n_rows - NA_WIN_ROWS)
        return start, start - r + (NA_WIN_ROWS - 1)

    def scores(r, s_ref):
        start, d0 = window(r)
        tok0 = pl.multiple_of(start * GRID_W, GRID_W)
        q = q_ref[pl.ds(pl.multiple_of(r * GRID_W, GRID_W), GRID_W), :]
        qs = _stack_heads(q, NA_HEADS)
        bias = jnp.concatenate([bias_ref[d0 + 2 * j] for j in range(NA_WIN_ROWS // 2)], axis=0)
        s_ctx = _dot_nt(kc_ref[...], qs)
        s_loc = _dot_nt(k_ref[pl.ds(tok0, win), :], qs) + bias
        m = jnp.maximum(jnp.max(s_ctx, axis=0, keepdims=True), jnp.max(s_loc, axis=0, keepdims=True))
        s_ref[:lc, :] = s_ctx
        s_ref[lc:lc + win, :] = s_loc
        s_ref[lc + win:, :] = jnp.broadcast_to(m, (SUBLANES, m.shape[1]))

    def finish(r, s_ref):
        start, _ = window(r)
        odd = start % 2
        lane0 = pl.multiple_of((start - odd) * GRID_W, LANES)
        m = s_ref[lc + win:lc + win + 1, :]
        p_ctx = jnp.exp2(s_ref[:lc, :] - m)
        p_loc = jnp.exp2(s_ref[lc:lc + win, :] - m)
        denom = jnp.sum(p_ctx, axis=0, keepdims=True) + jnp.sum(p_loc, axis=0, keepdims=True)
        ot = _dot(vct_ref[...], p_ctx.astype(BF16)) + _dot(vt_ref[odd, :, pl.ds(lane0, win)], p_loc.astype(BF16))
        o4 = (ot * (1.0 / denom)).T
        o = _unstack_heads(o4, NA_HEADS)
        o_ref[pl.ds(pl.multiple_of(r * GRID_W, GRID_W), GRID_W), :] = _group_rms(o, gain).astype(o_ref.dtype)

    def step(r, cur_ref, nxt_ref):
        scores(jnp.minimum(r + 1, n_rows - 1), nxt_ref)
        finish(r, cur_ref)

    return prepare, scores, step


def _wa_steps(sink_ref, q_ref, k_ref, v_ref, kc_ref, vc_ref, g_ref, o_ref):
    seq = q_ref.shape[0]
    n_blocks = seq // WA_BLOCK
    lc = kc_ref.shape[0]
    gain = g_ref[...]
    key = _iota((WA_BAND, WA_BLOCK), 0)
    qry = _iota((WA_BAND, WA_BLOCK), 1)

    def window(n):
        q0 = pl.multiple_of(n * WA_BLOCK, WA_BLOCK)
        k0 = pl.multiple_of(jnp.clip(q0 - WA_BLOCK, 0, seq - WA_BAND), WA_BLOCK)
        return q0, k0

    def scores(n, kh, s_ref):
        q0, k0 = window(n)
        band = jnp.where(jnp.abs((k0 + key) - (q0 + qry)) <= WA_WINDOW, 0.0, NEG_INF)
        lanes = slice(kh * LANES, (kh + 1) * LANES)
        qg = q_ref[pl.ds(q0, WA_BLOCK), kh * WA_GROUP * HEAD_DIM:(kh + 1) * WA_GROUP * HEAD_DIM]
        s_ctx, s_loc = _wa_scores(qg, [(kc_ref[:, lanes], None), (k_ref[pl.ds(k0, WA_BAND), lanes], band)])
        s_ref[kh, :lc, :] = s_ctx
        s_ref[kh, lc:lc + WA_BAND, :] = s_loc
        s_ref[kh, lc + WA_BAND:, :] = _wa_col_max([s_ctx, s_loc], _wa_sink_rows(WA_BLOCK, sinks(kh)))

    def sinks(kh):
        return [sink_ref[kh * WA_GROUP + g] for g in range(WA_GROUP)]

    def finish(n, kh, s_ref):
        _, k0 = window(n)
        lanes = slice(kh * LANES, (kh + 1) * LANES)
        return _wa_finish([s_ref[kh, :lc, :], s_ref[kh, lc:lc + WA_BAND, :]],
                          [vc_ref[:, lanes], v_ref[pl.ds(k0, WA_BAND), lanes]], sinks(kh),
                          m=s_ref[kh, lc + WA_BAND:, :])

    def step(n, cur_ref, nxt_ref):
        nxt = jnp.minimum(n + 1, n_blocks - 1)
        for kh in range(WA_KV_HEADS):
            scores(nxt, kh, nxt_ref)
        o = jnp.concatenate([finish(n, kh, cur_ref) for kh in range(WA_KV_HEADS)], axis=1)
        q0, _ = window(n)
        o_ref[pl.ds(q0, WA_BLOCK), :] = _group_rms(o, gain).astype(o_ref.dtype)

    return scores, step


def _attn_kernel(sink_ref, nq_ref, nk_ref, nv_ref, nkc_ref, nvc_ref, bias_ref, gn_ref,
                 wq_ref, wk_ref, wv_ref, wkc_ref, wvc_ref, gw_ref, *rest, n_cast):
    f32_refs = rest[:n_cast]
    on_ref, ow_ref = rest[n_cast:n_cast + 2]
    bf16_refs = rest[n_cast + 2:2 * n_cast + 2]
    nsa_ref, nsb_ref, wsa_ref, wsb_ref, vt_ref, vct_ref = rest[2 * n_cast + 2:]
    for src_ref, dst_ref in zip(f32_refs, bf16_refs):
        dst_ref[...] = src_ref[...].astype(dst_ref.dtype)
    na_prepare, na_scores, na_step = _na_steps(nq_ref, nk_ref, nv_ref, nkc_ref, nvc_ref, bias_ref, gn_ref, on_ref,
                                               vt_ref, vct_ref)
    na_prepare()
    wa_scores, wa_step = _wa_steps(sink_ref, wq_ref, wk_ref, wv_ref, wkc_ref, wvc_ref, gw_ref, ow_ref)
    n_blocks = wq_ref.shape[0] // WA_BLOCK

    for kh in range(WA_KV_HEADS):
        wa_scores(0, kh, wsa_ref)
    na_scores(0, nsa_ref)

    def body(i, carry):
        for half, (w_cur, w_nxt) in enumerate(((wsa_ref, wsb_ref), (wsb_ref, wsa_ref))):
            n = 2 * i + half
            wa_step(n, w_cur, w_nxt)
            na_step(2 * n, nsa_ref, nsb_ref)
            na_step(2 * n + 1, nsb_ref, nsa_ref)
        return carry

    lax.fori_loop(0, n_blocks // 2, body, 0, unroll=2)


def _attention(sink, nq, nk, nv, nkc, nvc, bias, layer, gain_na, wq, wk2, wv3, wkc2, wvc3, gain_wa, to_bf16):
    batch, seq, _ = nq.shape
    slab_in = lambda w, j: pl.BlockSpec((None, w.shape[1] // batch, w.shape[2]), lambda b: (j, b, 0))
    slab_out = lambda w: pl.BlockSpec((w.shape[1] // batch, w.shape[2]), lambda b: (b, 0))
    lc = nkc.shape[1]
    assert WA_BLOCK == 2 * GRID_W and (seq // WA_BLOCK) % 2 == 0
    per_b = lambda n, w: pl.BlockSpec((None, n, w), lambda b: (b, 0, 0))
    na_scratch = pltpu.VMEM((lc + NA_WIN_ROWS * GRID_W + SUBLANES, NA_HEADS * GRID_W), F32)
    vt_scratch = pltpu.VMEM((2, NA_WIDTH, seq), BF16)
    vct_scratch = pltpu.VMEM((NA_WIDTH, lc), BF16)
    wa_scratch = pltpu.VMEM((WA_KV_HEADS, lc + WA_BAND + SUBLANES, WA_GROUP * WA_BLOCK), F32)
    outs = pl.pallas_call(
        functools.partial(_attn_kernel, n_cast=len(to_bf16)),
        out_shape=[jax.ShapeDtypeStruct((batch, seq, NA_WIDTH), BF16),
                   jax.ShapeDtypeStruct((batch, seq, WA_QW), BF16),
                   *[jax.ShapeDtypeStruct(w.shape[1:], BF16) for w, _ in to_bf16]],
        grid=(batch,),
        in_specs=[pl.BlockSpec(memory_space=pltpu.SMEM),
                  per_b(seq, NA_WIDTH), per_b(seq, NA_WIDTH), per_b(seq, NA_WIDTH),
                  per_b(lc, NA_WIDTH), per_b(lc, NA_WIDTH),
                  pl.BlockSpec((None,) + bias.shape[1:], lambda b: (layer, 0, 0, 0)),
                  pl.BlockSpec((1, NA_WIDTH), lambda b: (0, 0)),
                  per_b(seq, WA_QW), per_b(seq, WA_K2W), per_b(seq, WA_V3W),
                  per_b(lc, WA_K2W), per_b(lc, WA_V3W),
                  pl.BlockSpec((1, WA_QW), lambda b: (0, 0)),
                  *[slab_in(w, j) for w, j in to_bf16]],
        out_specs=[per_b(seq, NA_WIDTH), per_b(seq, WA_QW), *[slab_out(w) for w, _ in to_bf16]],
        scratch_shapes=[na_scratch, na_scratch, wa_scratch, wa_scratch, vt_scratch, vct_scratch],
        compiler_params=_params(),
        name="attention",
    )(sink, nq, nk, nv, nkc, nvc, bias, gain_na, wq, wk2, wv3, wkc2, wvc3, gain_wa, *[w for w, _ in to_bf16])
    return outs[0], outs[1], outs[2:]


def _ctx_attn(sink_ref, qn_ref, kn_ref, vn_ref, qw_ref, kw_ref, vw_ref, gn_ref, gw_ref):
    qs = _stack_heads(qn_ref[...], NA_HEADS)
    o4 = _softmax_pv([(_dot_nt(qs, kn_ref[...]), vn_ref[...])])
    na_n = _group_rms(_unstack_heads(o4, NA_HEADS), gn_ref[...]).astype(BF16)
    outs = []
    for kh in range(WA_KV_HEADS):
        lanes = slice(kh * LANES, (kh + 1) * LANES)
        qg = qw_ref[:, kh * WA_GROUP * HEAD_DIM:(kh + 1) * WA_GROUP * HEAD_DIM]
        sinks = [sink_ref[kh * WA_GROUP + g] for g in range(WA_GROUP)]
        outs.append(_wa_finish(_wa_scores(qg, [(kw_ref[:, lanes], None)]), [vw_ref[:, lanes]], sinks))
    return na_n, _group_rms(jnp.concatenate(outs, axis=1), gw_ref[...]).astype(BF16)


def _out_mlp_kernel(*refs, seq_len, sub, ctx_attn):
    x_ref, mod_ref = refs[:2]
    n_attn = 9 if ctx_attn else 2
    attn_refs = refs[2:2 + n_attn]
    u_ref, uprev_ref, unext_ref, cvb_ref, cw_ref, cb_ref, g_ref, g2_ref, wo_ref = refs[2 + n_attn:11 + n_attn]
    rest = refs[11 + n_attn:]
    n_ffn = FFN_DIM // FFN_CHUNK
    w1_refs, w2_refs, (o_ref,) = rest[:n_ffn], rest[n_ffn:2 * n_ffn], rest[2 * n_ffn:]
    tm = x_ref.shape[0]
    i = pl.program_id(0)
    if ctx_attn:
        na_t, wa_t = _ctx_attn(*attn_refs)
    else:
        na_t, wa_t = attn_refs
    u = u_ref[...]
    at_start = (i * tm) % seq_len == 0
    at_end = ((i + 1) * tm) % seq_len == 0
    prev_row = jnp.where(at_start, 0.0, uprev_ref[7:8, :])
    next_row = jnp.where(at_end, 0.0, unext_ref[0:1, :])
    row = _iota(u.shape, 0)
    up = jnp.where(row == 0, prev_row, pltpu.roll(u, 1, axis=0))
    dn = jnp.where(row == tm - 1, next_row, pltpu.roll(u, tm - 1, axis=0))
    y = cb_ref[...] + cw_ref[0:1, :] * up + cw_ref[1:2, :] * u + cw_ref[2:3, :] * dn
    cv = _group_rms(cvb_ref[...] * y, g_ref[...]).astype(BF16)
    for r0 in range(0, tm, sub):
        rows = slice(r0, r0 + sub)
        mixed = jnp.concatenate([na_t[rows, :], cv[rows, :], wa_t[rows, :]], axis=1)
        x = x_ref[rows, :] + mod_ref[2] * _dot(mixed, wo_ref[...])
        ms = jnp.mean(x * x, axis=-1, keepdims=True)
        h = x * lax.rsqrt(ms + EPS) * g2_ref[...]
        h = (h * (1.0 + mod_ref[4]) + mod_ref[3]).astype(BF16)
        acc = None
        for w1_ref, w2_ref in zip(w1_refs, w2_refs):
            a = jnp.maximum(_dot(h, w1_ref[...]), 0.0)
            part = _dot((a * a).astype(BF16), w2_ref[...])
            acc = part if acc is None else acc + part
        o_ref[rows, :] = x + mod_ref[5] * acc


def _out_mlp(x2d, mod, layer, mod_row, attn, cv_u, cv_b, conv_w, conv_b, gain_cv, w_o, g2, w1, w2, seq_len, tm):
    rows = x2d.shape[0]
    ctx_attn = len(attn) == 9
    assert not ctx_attn or tm == seq_len
    halo = 8
    per_tile = tm // halo
    last = rows // halo - 1
    row_spec = lambda w: pl.BlockSpec((tm, w), lambda i: (i, 0))
    const = lambda shape: pl.BlockSpec(shape, lambda i: (0,) * len(shape), pipeline_mode=pl.Buffered(1))
    once = pl.Buffered(1)
    n_ffn = FFN_DIM // FFN_CHUNK
    wo_spec = pl.BlockSpec((D_MODEL, D_MODEL), lambda i: (0, 0), pipeline_mode=once)
    w1_specs = [pl.BlockSpec((D_MODEL, FFN_CHUNK), functools.partial(lambda j, i: (0, j), j),
                             pipeline_mode=once) for j in range(n_ffn)]
    w2_specs = [pl.BlockSpec((FFN_CHUNK, D_MODEL), functools.partial(lambda j, i: (j, 0), j),
                             pipeline_mode=once) for j in range(n_ffn)]
    small = lambda a: pl.BlockSpec(a.shape, lambda i: (0,) * a.ndim)
    if ctx_attn:
        attn_specs = [pl.BlockSpec(memory_space=pltpu.SMEM), *[row_spec(a.shape[1]) for a in attn[1:7]],
                      small(attn[7]), small(attn[8])]
    else:
        attn_specs = [row_spec(NA_WIDTH), row_spec(WA_QW)]
    return pl.pallas_call(
        functools.partial(_out_mlp_kernel, seq_len=seq_len, sub=min(tm, MLP_SUB), ctx_attn=ctx_attn),
        out_shape=jax.ShapeDtypeStruct((rows, D_MODEL), F32),
        grid=(rows // tm,),
        in_specs=[
            row_spec(D_MODEL),
            pl.BlockSpec((None, None, 6, 1, D_MODEL), lambda i: (layer, mod_row(i), 0, 0, 0)),
            *attn_specs, row_spec(CONV_CH),
            pl.BlockSpec((halo, CONV_CH), lambda i: (jnp.maximum(i * per_tile - 1, 0), 0)),
            pl.BlockSpec((halo, CONV_CH), lambda i: (jnp.minimum((i + 1) * per_tile, last), 0)),
            row_spec(CONV_CH),
            const((3, CONV_CH)), const((1, CONV_CH)), const((1, CONV_CH)),
            const((1, D_MODEL)),
            wo_spec, *w1_specs, *w2_specs,
        ],
        out_specs=row_spec(D_MODEL),
        compiler_params=_params(),
        name="out_mlp",
    )(x2d, mod, *attn, cv_u, cv_u, cv_u, cv_b, conv_w, conv_b, gain_cv, g2,
      w_o, *([w1] * n_ffn), *([w2] * n_ffn))


def _rope_tables(seq):
    quarter = HEAD_DIM // 4
    inv = ROPE_BASE ** (-jnp.arange(quarter, dtype=F32) / quarter)
    t = jnp.arange(seq)
    ang_r = (t // GRID_W).astype(F32)[:, None] * inv[None, :]
    ang_c = (t % GRID_W).astype(F32)[:, None] * inv[None, :]
    cos = jnp.concatenate([jnp.cos(ang_r), jnp.cos(ang_r), jnp.cos(ang_c), jnp.cos(ang_c)], axis=1)
    sin = jnp.concatenate([-jnp.sin(ang_r), jnp.sin(ang_r), -jnp.sin(ang_c), jnp.sin(ang_c)], axis=1)
    reps = LANES // HEAD_DIM
    return jnp.tile(cos, (1, reps)), jnp.tile(sin, (1, reps))


def kernel(x, c, ctx, c_ctx, w_mod, b_mod, g_norm1, g_norm2, w_in, na_q_gain, na_k_gain, na_rpb,
           conv_w, conv_bias, wa_q_gain, wa_k_gain, wa_sink, g_out, w_o, w_fc1, w_fc2):
    batch, seq, d = x.shape
    lc = ctx.shape[1]
    depth = w_mod.shape[0]
    assert d == D_MODEL and seq % WA_BLOCK == 0 and seq % GRID_W == 0 and batch < MOD_ROWS
    tm = 1024
    tm_in = 2048
    tm_ctx = 512
    assert seq % tm == 0 and seq % tm_in == 0 and (batch * lc) % tm_ctx == 0

    cond = jnp.zeros((MOD_ROWS, d), F32).at[:batch].set(c).at[batch].set(c_ctx)
    mod = _modulation(cond, w_mod, b_mod)
    rpb_bias, w_in_bf = _rpb_tables(na_rpb, w_in)
    rope_tabs = _rope_tables(seq)

    lat_row = lambda i: i // (seq // tm)
    lat_row_in = lambda i: i // (seq // tm_in)
    ctx_row = lambda i: batch

    xs = x.reshape(batch * seq, d)
    cs = ctx.reshape(batch * lc, d)
    for l in range(depth):
        last = l == depth - 1
        g1 = g_norm1[l].reshape(1, d)
        g2 = g_norm2[l].reshape(1, d)
        gains = [jnp.tile(na_q_gain[l], NA_HEADS).reshape(1, -1), jnp.tile(na_k_gain[l], NA_HEADS).reshape(1, -1),
                 jnp.tile(wa_q_gain[l], WA_HEADS).reshape(1, -1), jnp.tile(wa_k_gain[l], WA_KV_HEADS).reshape(1, -1)]
        go_na = g_out[l, :NA_WIDTH].reshape(1, -1)
        go_cv = g_out[l, NA_WIDTH:NA_WIDTH + CONV_CH].reshape(1, -1)
        go_wa = g_out[l, NA_WIDTH + CONV_CH:].reshape(1, -1)
        cb = conv_bias[l].reshape(1, -1)

        naq, nak, nav, cvu, cvb, waq, wak, wav = _in_proj(xs, mod, l, lat_row_in, g1, w_in_bf, gains, rope_tabs, tm_in)
        if last:
            cnak, cnav, cwak, cwav = _in_proj(cs, mod, l, ctx_row, g1, w_in_bf, gains, None, tm_ctx, kv_only=True)
        else:
            cnaq, cnak, cnav, ccvu, ccvb, cwaq, cwak, cwav = _in_proj(cs, mod, l, ctx_row, g1, w_in_bf, gains, None, tm_ctx)

        b3 = lambda a, n: a.reshape(batch, n, a.shape[-1])
        casts = [(w_o, l), (w_fc1, l), (w_fc2, l)] + ([] if last else [(w_in, l + 1)])
        na_n, wa_n, cast_out = _attention(
            wa_sink[l], b3(naq, seq), b3(nak, seq), b3(nav, seq), b3(cnak, lc), b3(cnav, lc), rpb_bias, l, go_na,
            b3(waq, seq), b3(wak, seq), b3(wav, seq), b3(cwak, lc), b3(cwav, lc), go_wa, casts)
        w_o_bf, w1_bf, w2_bf = cast_out[:3]
        if not last:
            w_in_bf = cast_out[3]
        xs = _out_mlp(xs, mod, l, lat_row, (na_n.reshape(batch * seq, -1), wa_n.reshape(batch * seq, -1)),
                      cvu, cvb, conv_w[l], cb, go_cv, w_o_bf, g2, w1_bf, w2_bf, seq, tm)
        if not last:
            ctx_attn = (wa_sink[l], cnaq, cnak, cnav, cwaq, cwak, cwav, go_na, go_wa)
            cs = _out_mlp(cs, mod, l, ctx_row, ctx_attn, ccvu, ccvb, conv_w[l], cb, go_cv,
                          w_o_bf, g2, w1_bf, w2_bf, lc, lc)
    return xs.reshape(batch, seq, d)
```

```python
import functools

import jax
import jax.numpy as jnp
from jax import lax
from jax.experimental import pallas as pl
from jax.experimental.pallas import tpu as pltpu

D_MODEL = 1024
GRID_W = 64
HEAD_DIM = 64
NA_HEADS = 4
NA_WIDTH = NA_HEADS * HEAD_DIM
CONV_CH = 256
WA_HEADS = 8
WA_KV_HEADS = 2
WA_GROUP = WA_HEADS // WA_KV_HEADS
WA_QW = WA_HEADS * HEAD_DIM
WA_KVW = WA_KV_HEADS * HEAD_DIM
WA_K2W = 2 * WA_KVW
WA_V3W = 2 * WA_KVW
NA_WIN_ROWS = 8
NA_WIN_COLS = 16
NA_DROWS = 2 * NA_WIN_ROWS - 1
NA_DCOLS = 2 * NA_WIN_COLS - 1
WA_WINDOW = 128
WA_BLOCK = 128
WA_BAND = 3 * WA_BLOCK
FFN_DIM = 4 * D_MODEL
FFN_CHUNK = 1024
ROPE_BASE = 10000.0
EPS = 1e-6
NEG_INF = -1e30
IN_WIDTH = 2304
OFF_NA_Q, OFF_NA_K, OFF_NA_V = 0, 256, 512
OFF_CV_X, OFF_CV_B, OFF_CV_C = 768, 1024, 1280
OFF_WA_Q, OFF_WA_K, OFF_WA_V = 1536, 2048, 2176
MOD_ROWS = 16
MOD_TN = 2048
IN_SUB = 256
MLP_SUB = 1024
LANES = 128
MXU_DIM = 256
VMEM_LIMIT = 60 * 1024 * 1024
LOG2E = 1.4426950408889634
Q_SCALE = LOG2E * HEAD_DIM ** -0.5

F32 = jnp.float32
BF16 = jnp.bfloat16


def _dot(a, b):
    return jnp.dot(a, b, preferred_element_type=F32)


def _dot_nt(a, b):
    return lax.dot_general(a, b, (((1,), (1,)), ((), ())), preferred_element_type=F32)


def _split_bf16(a):
    hi = a.astype(BF16)
    lo = (a - hi.astype(F32)).astype(BF16)
    return hi, lo


def _iota(shape, dim):
    return lax.broadcasted_iota(jnp.int32, shape, dim)


def _params(**kw):
    return pltpu.CompilerParams(vmem_limit_bytes=VMEM_LIMIT, **kw)


def _mod_kernel(cond_ref, w_ref, b_ref, o_ref):
    a = cond_ref[...]
    a = a * (1.0 / (1.0 + jnp.exp(-a)))
    ah, al = _split_bf16(a)
    wh, wl = _split_bf16(w_ref[...])
    rows = a.shape[0]
    both = _dot(jnp.concatenate([ah, al], axis=0), wh)
    res = both[:rows] + both[rows:] + _dot(ah, wl) + b_ref[...]
    for c in range(o_ref.shape[1]):
        o_ref[:, c, 0, :] = res[:, c * D_MODEL:(c + 1) * D_MODEL]


def _modulation(cond, w_mod, b_mod):
    depth = w_mod.shape[0]
    n_out = w_mod.shape[2]
    per_tile = MOD_TN // D_MODEL
    return pl.pallas_call(
        _mod_kernel,
        out_shape=jax.ShapeDtypeStruct((depth, MOD_ROWS, n_out // D_MODEL, 1, D_MODEL), F32),
        grid=(depth, n_out // MOD_TN),
        in_specs=[
            pl.BlockSpec((MOD_ROWS, D_MODEL), lambda l, j: (0, 0)),
            pl.BlockSpec((None, D_MODEL, MOD_TN), lambda l, j: (l, 0, j)),
            pl.BlockSpec((None, 1, MOD_TN), lambda l, j: (l, 0, j)),
        ],
        out_specs=pl.BlockSpec((None, MOD_ROWS, per_tile, 1, D_MODEL), lambda l, j: (l, 0, j, 0, 0)),
        compiler_params=_params(),
        name="modulation",
    )(cond, w_mod, b_mod.reshape(depth, 1, n_out))


def _rpb_kernel(rpb_ref, w_ref, o_ref, wbf_ref):
    wbf_ref[...] = w_ref[...].astype(BF16)
    shape = (GRID_W, LANES)
    q = _iota(shape, 0)
    lane = _iota(shape, 1)
    k = lane % GRID_W
    left = lane < GRID_W
    col_start = jnp.clip(q - NA_WIN_COLS // 2, 0, GRID_W - NA_WIN_COLS)
    col_ok = (k >= col_start) & (k < col_start + NA_WIN_COLS)
    centre = NA_WIN_COLS - 1
    for d in range(NA_DROWS - 1):
        tiles = []
        for h in range(NA_HEADS):
            t1 = pltpu.roll(jnp.broadcast_to(rpb_ref[h, d:d + 1, :], shape), LANES - centre, axis=1,
                            stride=1, stride_axis=0)
            t2 = pltpu.roll(jnp.broadcast_to(rpb_ref[h, d + 1:d + 2, :], shape), GRID_W - centre, axis=1,
                            stride=1, stride_axis=0)
            t = jnp.where(left, t1, t2)
            tiles.append(jnp.where(col_ok, t * LOG2E, NEG_INF))
        o_ref[d] = jnp.concatenate(tiles, axis=0).T


def _rpb_tables(na_rpb, w_in):
    depth = na_rpb.shape[0]
    padded = jnp.pad(na_rpb, ((0, 0), (0, 0), (0, 1), (0, LANES - NA_DCOLS)))
    slab = w_in.shape[1] // depth
    return pl.pallas_call(
        _rpb_kernel,
        out_shape=[jax.ShapeDtypeStruct((depth, NA_DROWS - 1, LANES, NA_HEADS * GRID_W), F32),
                   jax.ShapeDtypeStruct(w_in.shape[1:], BF16)],
        grid=(depth,),
        in_specs=[pl.BlockSpec((None,) + padded.shape[1:], lambda l: (l, 0, 0, 0)),
                  pl.BlockSpec((None, slab, w_in.shape[2]), lambda l: (0, l, 0))],
        out_specs=[pl.BlockSpec((None, NA_DROWS - 1, LANES, NA_HEADS * GRID_W), lambda l: (l, 0, 0, 0)),
                   pl.BlockSpec((slab, w_in.shape[2]), lambda l: (l, 0))],
        compiler_params=_params(),
        name="rpb_tables",
    )(padded, w_in)


def _head_rms(t, gain):
    width = t.shape[1]
    sq = (t * t).astype(BF16)
    blk = min(width, MXU_DIM)
    ones = jnp.where(_iota((blk, blk), 0) // HEAD_DIM == _iota((blk, blk), 1) // HEAD_DIM, 1.0, 0.0).astype(BF16)
    sums = [_dot(sq[:, c:c + blk], ones) for c in range(0, width, blk)]
    ss = sums[0] if len(sums) == 1 else jnp.concatenate(sums, axis=1)
    return t * lax.rsqrt(ss * (1.0 / HEAD_DIM) + EPS) * gain


def _rope(t, cos, sin):
    lane = _iota((t.shape[0], LANES), 1)
    first = (lane % (HEAD_DIM // 2)) < (HEAD_DIM // 4)
    out = []
    for c in range(0, t.shape[1], LANES):
        u = t[:, c:c + LANES]
        partner = jnp.where(first, pltpu.roll(u, LANES - HEAD_DIM // 4, axis=1), pltpu.roll(u, HEAD_DIM // 4, axis=1))
        out.append(u * cos + partner * sin)
    return out[0] if len(out) == 1 else jnp.concatenate(out, axis=1)


def _dup_kv(t):
    lane = _iota(t.shape, 1)
    swapped = pltpu.roll(t, HEAD_DIM, axis=1)
    left = lane < HEAD_DIM
    return jnp.concatenate([jnp.where(left, t, swapped), jnp.where(left, swapped, t)], axis=1)


def _inproj_kernel(*refs, rope, kv_only, n_w, sub):
    x_ref, mod_ref, g1_ref = refs[:3]
    w_refs = refs[3:3 + n_w]
    gnq_ref, gnk_ref, gwq_ref, gwk_ref = refs[3 + n_w:7 + n_w]
    rest = refs[7 + n_w:]
    if rope:
        cos_ref, sin_ref = rest[:2]
        rest = rest[2:]
    out_refs = rest
    if kv_only:
        nak_ref, nav_ref, wak_ref, wav_ref = out_refs
        o_nk, o_nv, o_wk, o_wv, o_end = 0, NA_WIDTH, 2 * NA_WIDTH, 2 * NA_WIDTH + WA_KVW, 2 * NA_WIDTH + 2 * WA_KVW
    else:
        naq_ref, nak_ref, nav_ref, cvu_ref, cvb_ref, waq_ref, wak_ref, wav_ref = out_refs
        o_nk, o_nv, o_wk, o_wv, o_end = OFF_NA_K, OFF_NA_V, OFF_WA_K, OFF_WA_V, IN_WIDTH
    def project(rows):
        x = x_ref[rows, :]
        ms = jnp.mean(x * x, axis=-1, keepdims=True)
        h = x * lax.rsqrt(ms + EPS) * g1_ref[...]
        h = h * (1.0 + mod_ref[1]) + mod_ref[0]
        hb = h.astype(BF16)
        return jnp.concatenate([_dot(hb, w_ref[...]) for w_ref in w_refs], axis=1)

    def finish(rows, p):
        nak_ref[rows, :] = _head_rms(p[:, o_nk:o_nk + NA_WIDTH], gnk_ref[...]).astype(BF16)
        nav_ref[rows, :] = p[:, o_nv:o_nv + NA_WIDTH].astype(BF16)
        wk = _head_rms(p[:, o_wk:o_wv], gwk_ref[...])
        if rope:
            wk = _rope(wk, cos_ref[rows, :], sin_ref[rows, :])
        wak_ref[rows, :] = _dup_kv(wk).astype(BF16)
        vv = p[:, o_wv:o_end]
        left = _iota(vv.shape, 1) < HEAD_DIM
        wav_ref[rows, :] = jnp.concatenate([jnp.where(left, vv, 1.0), jnp.where(left, pltpu.roll(vv, HEAD_DIM, axis=1), 1.0)],
                                           axis=1).astype(BF16)
        if kv_only:
            return
        naq_ref[rows, :] = (_head_rms(p[:, OFF_NA_Q:OFF_NA_K], gnq_ref[...]) * Q_SCALE).astype(BF16)
        cvu_ref[rows, :] = p[:, OFF_CV_C:OFF_WA_Q] * p[:, OFF_CV_X:OFF_CV_B]
        cvb_ref[rows, :] = p[:, OFF_CV_B:OFF_CV_C]
        wq = _head_rms(p[:, OFF_WA_Q:OFF_WA_K], gwq_ref[...])
        if rope:
            wq = _rope(wq, cos_ref[rows, :], sin_ref[rows, :])
        waq_ref[rows, :] = (wq * Q_SCALE).astype(BF16)

    for r0 in range(0, x_ref.shape[0], sub):
        rows = slice(r0, r0 + sub)
        finish(rows, project(rows))


def _in_proj(x2d, mod, layer, mod_row, g1, w_in, gains, rope_tabs, tm, kv_only=False):
    rows = x2d.shape[0]
    rope = rope_tabs is not None
    row_spec = lambda w: pl.BlockSpec((tm, w), lambda i: (i, 0))
    const = lambda shape: pl.BlockSpec(shape, lambda i: (0,) * len(shape))
    once = pl.Buffered(1)
    if kv_only:
        col_blocks = [OFF_NA_K // MXU_DIM, OFF_NA_V // MXU_DIM, OFF_WA_K // MXU_DIM]
        w_specs = [pl.BlockSpec((D_MODEL, MXU_DIM), functools.partial(lambda j, i: (0, j), j),
                                pipeline_mode=once) for j in col_blocks]
        widths = [(NA_WIDTH, BF16), (NA_WIDTH, BF16), (WA_K2W, BF16), (WA_V3W, BF16)]
    else:
        w_specs = [pl.BlockSpec((D_MODEL, IN_WIDTH), lambda i: (0, 0), pipeline_mode=once)]
        widths = [(NA_WIDTH, BF16), (NA_WIDTH, BF16), (NA_WIDTH, BF16), (CONV_CH, F32), (CONV_CH, F32),
                  (WA_QW, BF16), (WA_K2W, BF16), (WA_V3W, BF16)]
    n_w = len(w_specs)
    in_specs = [
        row_spec(D_MODEL),
        pl.BlockSpec((None, None, 6, 1, D_MODEL), lambda i: (layer, mod_row(i), 0, 0, 0)),
        const((1, D_MODEL)),
        *w_specs,
        const((1, NA_WIDTH)), const((1, NA_WIDTH)), const((1, WA_QW)), const((1, WA_KVW)),
    ]
    args = [x2d, mod, g1, *([w_in] * n_w), *gains]
    if rope:
        seq_tiles = rope_tabs[0].shape[0] // tm
        in_specs += [pl.BlockSpec((tm, LANES), lambda i: (i % seq_tiles, 0))] * 2
        args += list(rope_tabs)
    return pl.pallas_call(
        functools.partial(_inproj_kernel, rope=rope, kv_only=kv_only, n_w=n_w, sub=min(tm, IN_SUB)),
        out_shape=[jax.ShapeDtypeStruct((rows, w), dt) for w, dt in widths],
        grid=(rows // tm,),
        in_specs=in_specs,
        out_specs=[row_spec(w) for w, _ in widths],
        compiler_params=_params(),
        name="in_proj_rope" if rope else ("in_proj_ctx_kv" if kv_only else "in_proj_ctx"),
    )(*args)


def _lane_chunks(s):
    return [s[:, c:c + LANES] for c in range(0, s.shape[1], LANES)]


def _row_max(scores, extra=None):
    chunks = [ch for s in scores for ch in _lane_chunks(s)]
    if extra is not None:
        chunks.append(extra)
    folded = functools.reduce(jnp.maximum, chunks)
    return jnp.broadcast_to(jnp.max(folded, axis=-1, keepdims=True), folded.shape)


def _softmax_pv(parts, extra=None, sums_from_values=False, m=None):
    lane_chunks = _lane_chunks
    if m is None:
        m = _row_max([s for s, _ in parts], extra)
    ps = [jnp.concatenate([jnp.exp2(ch - m) for ch in lane_chunks(s)], axis=1) for s, _ in parts]
    pv = _dot(jnp.concatenate([p.astype(BF16) for p in ps], axis=1),
              jnp.concatenate([v for _, v in parts], axis=0))
    if sums_from_values:
        denom = pv[:, -LANES:]
        pv = pv[:, :-LANES]
        if extra is not None:
            denom = denom + jnp.exp2(extra - m)
        return pv / denom
    psum = functools.reduce(jnp.add, [ch for p in ps for ch in lane_chunks(p)])
    if extra is not None:
        psum = psum + jnp.where(_iota(extra.shape, 1) == 0, jnp.exp2(extra - m), 0.0)
    return pv / jnp.sum(psum, axis=-1, keepdims=True)


def _stack_heads(q, n_heads):
    head = _iota(q.shape, 1) // HEAD_DIM
    zero = jnp.zeros_like(q)
    return jnp.concatenate([jnp.where(head == h, q, zero) for h in range(n_heads)], axis=0)


def _unstack_heads(o, n_heads):
    rows = o.shape[0] // n_heads
    head = _iota((rows, o.shape[1]), 1) // HEAD_DIM
    out = jnp.where(head == 0, o[:rows], 0.0)
    for h in range(1, n_heads):
        out = jnp.where(head == h, o[h * rows:(h + 1) * rows], out)
    return out


def _group_rms(o, gain):
    ms = jnp.mean(o * o, axis=-1, keepdims=True)
    return o * lax.rsqrt(ms + EPS) * gain


SUBLANES = 8


def _wa_scores(qg, keys):
    rows = qg.shape[0]
    left = _iota((rows, LANES), 1) < HEAD_DIM
    zero = jnp.zeros((rows, LANES), qg.dtype)
    blocks = []
    for pair in range(WA_GROUP // 2):
        qp = qg[:, pair * LANES:(pair + 1) * LANES]
        blocks += [jnp.where(left, qp, zero), jnp.where(left, zero, qp)]
    qs = jnp.concatenate(blocks, axis=0)
    scores = []
    for k2, bias in keys:
        s = _dot_nt(k2, qs)
        if bias is not None:
            s = s + jnp.concatenate([bias] * WA_GROUP, axis=1)
        scores.append(s)
    return scores


def _wa_sink_rows(rows, sinks):
    return jnp.concatenate([jnp.full((SUBLANES, rows), sinks[g] * LOG2E, F32) for g in range(WA_GROUP)], axis=1)


def _wa_col_max(scores, sink_rows):
    m = sink_rows[:1]
    for s in scores:
        m = jnp.maximum(m, jnp.max(s, axis=0, keepdims=True))
    return jnp.broadcast_to(m, sink_rows.shape)


def _wa_finish(scores, values, sinks, m=None):
    rows = scores[0].shape[1] // WA_GROUP
    sink_rows = _wa_sink_rows(rows, sinks)
    if m is None:
        m = _wa_col_max(scores, sink_rows)
    p = jnp.concatenate([jnp.exp2(s - m[:1]).astype(BF16) for s in scores], axis=0)
    v = jnp.concatenate(values, axis=0)
    ot = lax.dot_general(v, p, (((0,), (0,)), ((), ())), preferred_element_type=F32)
    denom = ot[HEAD_DIM:HEAD_DIM + SUBLANES] + jnp.exp2(sink_rows - m)
    on = ot[:HEAD_DIM] / denom[:1]
    pairs = [jnp.concatenate([on[:, (2 * pr) * rows:(2 * pr + 1) * rows],
                              on[:, (2 * pr + 1) * rows:(2 * pr + 2) * rows]], axis=0).T
             for pr in range(WA_GROUP // 2)]
    return jnp.concatenate(pairs, axis=1)


def _na_steps(q_ref, k_ref, v_ref, kc_ref, vc_ref, bias_ref, g_ref, o_ref, vt_ref, vct_ref):
    seq = q_ref.shape[0]
    n_rows = seq // GRID_W
    win = NA_WIN_ROWS * GRID_W
    lc = kc_ref.shape[0]
    gain = g_ref[...]

    def prepare():
        vt_ref[0] = v_ref[...].astype(F32).T.astype(BF16)
        vt_ref[1, :, :seq - GRID_W] = v_ref[GRID_W:, :].astype(F32).T.astype(BF16)
        vct_ref[...] = vc_ref[...].astype(F32).T.astype(BF16)

    def window(r):
        start = jnp.clip(r - NA_WIN_ROWS // 2, 0, n_rows - NA_WIN_ROWS)
        return start, start - r + (NA_WIN_ROWS - 1)

    def scores(r, s_ref):
        start, d0 = window(r)
        tok0 = pl.multiple_of(start * GRID_W, GRID_W)
        q = q_ref[pl.ds(pl.multiple_of(r * GRID_W, GRID_W), GRID_W), :]
        qs = _stack_heads(q, NA_HEADS)
        bias = jnp.concatenate([bias_ref[d0 + 2 * j] for j in range(NA_WIN_ROWS // 2)], axis=0)
        s_ctx = _dot_nt(kc_ref[...], qs)
        s_loc = _dot_nt(k_ref[pl.ds(tok0, win), :], qs) + bias
        m = jnp.maximum(jnp.max(s_ctx, axis=0, keepdims=True), jnp.max(s_loc, axis=0, keepdims=True))
        s_ref[:lc, :] = s_ctx
        s_ref[lc:lc + win, :] = s_loc
        s_ref[lc + win:, :] = jnp.broadcast_to(m, (SUBLANES, m.shape[1]))

    def finish(r, s_ref):
        start, _ = window(r)
        odd = start % 2
        lane0 = pl.multiple_of((start - odd) * GRID_W, LANES)
        m = s_ref[lc + win:lc + win + 1, :]
        p_ctx = jnp.exp2(s_ref[:lc, :] - m)
        p_loc = jnp.exp2(s_ref[lc:lc + win, :] - m)
        denom = jnp.sum(p_ctx, axis=0, keepdims=True) + jnp.sum(p_loc, axis=0, keepdims=True)
        ot = _dot(vct_ref[...], p_ctx.astype(BF16)) + _dot(vt_ref[odd, :, pl.ds(lane0, win)], p_loc.astype(BF16))
        o4 = (ot * (1.0 / denom)).T
        o = _unstack_heads(o4, NA_HEADS)
        o_ref[pl.ds(pl.multiple_of(r * GRID_W, GRID_W), GRID_W), :] = _group_rms(o, gain).astype(o_ref.dtype)

    def step(r, cur_ref, nxt_ref):
        scores(jnp.minimum(r + 1, n_rows - 1), nxt_ref)
        finish(r, cur_ref)

    return prepare, scores, step


def _wa_steps(sink_ref, q_ref, k_ref, v_ref, kc_ref, vc_ref, g_ref, o_ref):
    seq = q_ref.shape[0]
    n_blocks = seq // WA_BLOCK
    lc = kc_ref.shape[0]
    gain = g_ref[...]
    key = _iota((WA_BAND, WA_BLOCK), 0)
    qry = _iota((WA_BAND, WA_BLOCK), 1)

    def window(n):
        q0 = pl.multiple_of(n * WA_BLOCK, WA_BLOCK)
        k0 = pl.multiple_of(jnp.clip(q0 - WA_BLOCK, 0, seq - WA_BAND), WA_BLOCK)
        return q0, k0

    def scores(n, kh, s_ref):
        q0, k0 = window(n)
        band = jnp.where(jnp.abs((k0 + key) - (q0 + qry)) <= WA_WINDOW, 0.0, NEG_INF)
        lanes = slice(kh * LANES, (kh + 1) * LANES)
        qg = q_ref[pl.ds(q0, WA_BLOCK), kh * WA_GROUP * HEAD_DIM:(kh + 1) * WA_GROUP * HEAD_DIM]
        s_ctx, s_loc = _wa_scores(qg, [(kc_ref[:, lanes], None), (k_ref[pl.ds(k0, WA_BAND), lanes], band)])
        s_ref[kh, :lc, :] = s_ctx
        s_ref[kh, lc:lc + WA_BAND, :] = s_loc
        s_ref[kh, lc + WA_BAND:, :] = _wa_col_max([s_ctx, s_loc], _wa_sink_rows(WA_BLOCK, sinks(kh)))

    def sinks(kh):
        return [sink_ref[kh * WA_GROUP + g] for g in range(WA_GROUP)]

    def finish(n, kh, s_ref):
        _, k0 = window(n)
        lanes = slice(kh * LANES, (kh + 1) * LANES)
        return _wa_finish([s_ref[kh, :lc, :], s_ref[kh, lc:lc + WA_BAND, :]],
                          [vc_ref[:, lanes], v_ref[pl.ds(k0, WA_BAND), lanes]], sinks(kh),
                          m=s_ref[kh, lc + WA_BAND:, :])

    def step(n, cur_ref, nxt_ref):
        nxt = jnp.minimum(n + 1, n_blocks - 1)
        for kh in range(WA_KV_HEADS):
            scores(nxt, kh, nxt_ref)
        o = jnp.concatenate([finish(n, kh, cur_ref) for kh in range(WA_KV_HEADS)], axis=1)
        q0, _ = window(n)
        o_ref[pl.ds(q0, WA_BLOCK), :] = _group_rms(o, gain).astype(o_ref.dtype)

    return scores, step


def _attn_kernel(sink_ref, nq_ref, nk_ref, nv_ref, nkc_ref, nvc_ref, bias_ref, gn_ref,
                 wq_ref, wk_ref, wv_ref, wkc_ref, wvc_ref, gw_ref, *rest, n_cast):
    f32_refs = rest[:n_cast]
    on_ref, ow_ref = rest[n_cast:n_cast + 2]
    bf16_refs = rest[n_cast + 2:2 * n_cast + 2]
    nsa_ref, nsb_ref, wsa_ref, wsb_ref, vt_ref, vct_ref = rest[2 * n_cast + 2:]
    for src_ref, dst_ref in zip(f32_refs, bf16_refs):
        dst_ref[...] = src_ref[...].astype(dst_ref.dtype)
    na_prepare, na_scores, na_step = _na_steps(nq_ref, nk_ref, nv_ref, nkc_ref, nvc_ref, bias_ref, gn_ref, on_ref,
                                               vt_ref, vct_ref)
    na_prepare()
    wa_scores, wa_step = _wa_steps(sink_ref, wq_ref, wk_ref, wv_ref, wkc_ref, wvc_ref, gw_ref, ow_ref)
    n_blocks = wq_ref.shape[0] // WA_BLOCK

    for kh in range(WA_KV_HEADS):
        wa_scores(0, kh, wsa_ref)
    na_scores(0, nsa_ref)

    def body(i, carry):
        for half, (w_cur, w_nxt) in enumerate(((wsa_ref, wsb_ref), (wsb_ref, wsa_ref))):
            n = 2 * i + half
            wa_step(n, w_cur, w_nxt)
            na_step(2 * n, nsa_ref, nsb_ref)
            na_step(2 * n + 1, nsb_ref, nsa_ref)
        return carry

    lax.fori_loop(0, n_blocks // 2, body, 0, unroll=2)


def _attention(sink, nq, nk, nv, nkc, nvc, bias, layer, gain_na, wq, wk2, wv3, wkc2, wvc3, gain_wa, to_bf16):
    batch, seq, _ = nq.shape
    slab_in = lambda w, j: pl.BlockSpec((None, w.shape[1] // batch, w.shape[2]), lambda b: (j, b, 0))
    slab_out = lambda w: pl.BlockSpec((w.shape[1] // batch, w.shape[2]), lambda b: (b, 0))
    lc = nkc.shape[1]
    assert WA_BLOCK == 2 * GRID_W and (seq // WA_BLOCK) % 2 == 0
    per_b = lambda n, w: pl.BlockSpec((None, n, w), lambda b: (b, 0, 0))
    na_scratch = pltpu.VMEM((lc + NA_WIN_ROWS * GRID_W + SUBLANES, NA_HEADS * GRID_W), F32)
    vt_scratch = pltpu.VMEM((2, NA_WIDTH, seq), BF16)
    vct_scratch = pltpu.VMEM((NA_WIDTH, lc), BF16)
    wa_scratch = pltpu.VMEM((WA_KV_HEADS, lc + WA_BAND + SUBLANES, WA_GROUP * WA_BLOCK), F32)
    outs = pl.pallas_call(
        functools.partial(_attn_kernel, n_cast=len(to_bf16)),
        out_shape=[jax.ShapeDtypeStruct((batch, seq, NA_WIDTH), BF16),
                   jax.ShapeDtypeStruct((batch, seq, WA_QW), BF16),
                   *[jax.ShapeDtypeStruct(w.shape[1:], BF16) for w, _ in to_bf16]],
        grid=(batch,),
        in_specs=[pl.BlockSpec(memory_space=pltpu.SMEM),
                  per_b(seq, NA_WIDTH), per_b(seq, NA_WIDTH), per_b(seq, NA_WIDTH),
                  per_b(lc, NA_WIDTH), per_b(lc, NA_WIDTH),
                  pl.BlockSpec((None,) + bias.shape[1:], lambda b: (layer, 0, 0, 0)),
                  pl.BlockSpec((1, NA_WIDTH), lambda b: (0, 0)),
                  per_b(seq, WA_QW), per_b(seq, WA_K2W), per_b(seq, WA_V3W),
                  per_b(lc, WA_K2W), per_b(lc, WA_V3W),
                  pl.BlockSpec((1, WA_QW), lambda b: (0, 0)),
                  *[slab_in(w, j) for w, j in to_bf16]],
        out_specs=[per_b(seq, NA_WIDTH), per_b(seq, WA_QW), *[slab_out(w) for w, _ in to_bf16]],
        scratch_shapes=[na_scratch, na_scratch, wa_scratch, wa_scratch, vt_scratch, vct_scratch],
        compiler_params=_params(),
        name="attention",
    )(sink, nq, nk, nv, nkc, nvc, bias, gain_na, wq, wk2, wv3, wkc2, wvc3, gain_wa, *[w for w, _ in to_bf16])
    return outs[0], outs[1], outs[2:]


def _ctx_attn(sink_ref, qn_ref, kn_ref, vn_ref, qw_ref, kw_ref, vw_ref, gn_ref, gw_ref):
    qs = _stack_heads(qn_ref[...], NA_HEADS)
    o4 = _softmax_pv([(_dot_nt(qs, kn_ref[...]), vn_ref[...])])
    na_n = _group_rms(_unstack_heads(o4, NA_HEADS), gn_ref[...]).astype(BF16)
    outs = []
    for kh in range(WA_KV_HEADS):
        lanes = slice(kh * LANES, (kh + 1) * LANES)
        qg = qw_ref[:, kh * WA_GROUP * HEAD_DIM:(kh + 1) * WA_GROUP * HEAD_DIM]
        sinks = [sink_ref[kh * WA_GROUP + g] for g in range(WA_GROUP)]
        outs.append(_wa_finish(_wa_scores(qg, [(kw_ref[:, lanes], None)]), [vw_ref[:, lanes]], sinks))
    return na_n, _group_rms(jnp.concatenate(outs, axis=1), gw_ref[...]).astype(BF16)


def _out_mlp_kernel(*refs, seq_len, sub, ctx_attn):
    x_ref, mod_ref = refs[:2]
    n_attn = 9 if ctx_attn else 2
    attn_refs = refs[2:2 + n_attn]
    u_ref, uprev_ref, unext_ref, cvb_ref, cw_ref, cb_ref, g_ref, g2_ref, wo_ref = refs[2 + n_attn:11 + n_attn]
    rest = refs[11 + n_attn:]
    n_ffn = FFN_DIM // FFN_CHUNK
    w1_refs, w2_refs, (o_ref,) = rest[:n_ffn], rest[n_ffn:2 * n_ffn], rest[2 * n_ffn:]
    tm = x_ref.shape[0]
    i = pl.program_id(0)
    if ctx_attn:
        na_t, wa_t = _ctx_attn(*attn_refs)
    else:
        na_t, wa_t = attn_refs
    u = u_ref[...]
    at_start = (i * tm) % seq_len == 0
    at_end = ((i + 1) * tm) % seq_len == 0
    prev_row = jnp.where(at_start, 0.0, uprev_ref[7:8, :])
    next_row = jnp.where(at_end, 0.0, unext_ref[0:1, :])
    row = _iota(u.shape, 0)
    up = jnp.where(row == 0, prev_row, pltpu.roll(u, 1, axis=0))
    dn = jnp.where(row == tm - 1, next_row, pltpu.roll(u, tm - 1, axis=0))
    y = cb_ref[...] + cw_ref[0:1, :] * up + cw_ref[1:2, :] * u + cw_ref[2:3, :] * dn
    cv = _group_rms(cvb_ref[...] * y, g_ref[...]).astype(BF16)
    for r0 in range(0, tm, sub):
        rows = slice(r0, r0 + sub)
        mixed = jnp.concatenate([na_t[rows, :], cv[rows, :], wa_t[rows, :]], axis=1)
        x = x_ref[rows, :] + mod_ref[2] * _dot(mixed, wo_ref[...])
        ms = jnp.mean(x * x, axis=-1, keepdims=True)
        h = x * lax.rsqrt(ms + EPS) * g2_ref[...]
        h = (h * (1.0 + mod_ref[4]) + mod_ref[3]).astype(BF16)
        acc = None
        for w1_ref, w2_ref in zip(w1_refs, w2_refs):
            a = jnp.maximum(_dot(h, w1_ref[...]), 0.0)
            part = _dot((a * a).astype(BF16), w2_ref[...])
            acc = part if acc is None else acc + part
        o_ref[rows, :] = x + mod_ref[5] * acc


def _out_mlp(x2d, mod, layer, mod_row, attn, cv_u, cv_b, conv_w, conv_b, gain_cv, w_o, g2, w1, w2, seq_len, tm):
    rows = x2d.shape[0]
    ctx_attn = len(attn) == 9
    assert not ctx_attn or tm == seq_len
    halo = 8
    per_tile = tm // halo
    last = rows // halo - 1
    row_spec = lambda w: pl.BlockSpec((tm, w), lambda i: (i, 0))
    const = lambda shape: pl.BlockSpec(shape, lambda i: (0,) * len(shape), pipeline_mode=pl.Buffered(1))
    once = pl.Buffered(1)
    n_ffn = FFN_DIM // FFN_CHUNK
    wo_spec = pl.BlockSpec((D_MODEL, D_MODEL), lambda i: (0, 0), pipeline_mode=once)
    w1_specs = [pl.BlockSpec((D_MODEL, FFN_CHUNK), functools.partial(lambda j, i: (0, j), j),
                             pipeline_mode=once) for j in range(n_ffn)]
    w2_specs = [pl.BlockSpec((FFN_CHUNK, D_MODEL), functools.partial(lambda j, i: (j, 0), j),
                             pipeline_mode=once) for j in range(n_ffn)]
    small = lambda a: pl.BlockSpec(a.shape, lambda i: (0,) * a.ndim)
    if ctx_attn:
        attn_specs = [pl.BlockSpec(memory_space=pltpu.SMEM), *[row_spec(a.shape[1]) for a in attn[1:7]],
                      small(attn[7]), small(attn[8])]
    else:
        attn_specs = [row_spec(NA_WIDTH), row_spec(WA_QW)]
    return pl.pallas_call(
        functools.partial(_out_mlp_kernel, seq_len=seq_len, sub=min(tm, MLP_SUB), ctx_attn=ctx_attn),
        out_shape=jax.ShapeDtypeStruct((rows, D_MODEL), F32),
        grid=(rows // tm,),
        in_specs=[
            row_spec(D_MODEL),
            pl.BlockSpec((None, None, 6, 1, D_MODEL), lambda i: (layer, mod_row(i), 0, 0, 0)),
            *attn_specs, row_spec(CONV_CH),
            pl.BlockSpec((halo, CONV_CH), lambda i: (jnp.maximum(i * per_tile - 1, 0), 0)),
            pl.BlockSpec((halo, CONV_CH), lambda i: (jnp.minimum((i + 1) * per_tile, last), 0)),
            row_spec(CONV_CH),
            const((3, CONV_CH)), const((1, CONV_CH)), const((1, CONV_CH)),
            const((1, D_MODEL)),
            wo_spec, *w1_specs, *w2_specs,
        ],
        out_specs=row_spec(D_MODEL),
        compiler_params=_params(),
        name="out_mlp",
    )(x2d, mod, *attn, cv_u, cv_u, cv_u, cv_b, conv_w, conv_b, gain_cv, g2,
      w_o, *([w1] * n_ffn), *([w2] * n_ffn))


def _rope_tables(seq):
    quarter = HEAD_DIM // 4
    inv = ROPE_BASE ** (-jnp.arange(quarter, dtype=F32) / quarter)
    lane = jnp.arange(LANES)
    t = jnp.arange(seq)[:, None]
    pos = jnp.where((lane % HEAD_DIM) < HEAD_DIM // 2, t // GRID_W, t % GRID_W).astype(F32)
    ang = pos * inv[lane % quarter][None, :]
    sin = jnp.sin(ang)
    return jnp.cos(ang), jnp.where((lane % (HEAD_DIM // 2)) < quarter, -sin, sin)


def kernel(x, c, ctx, c_ctx, w_mod, b_mod, g_norm1, g_norm2, w_in, na_q_gain, na_k_gain, na_rpb,
           conv_w, conv_bias, wa_q_gain, wa_k_gain, wa_sink, g_out, w_o, w_fc1, w_fc2):
    batch, seq, d = x.shape
    lc = ctx.shape[1]
    depth = w_mod.shape[0]
    assert d == D_MODEL and seq % WA_BLOCK == 0 and seq % GRID_W == 0 and batch < MOD_ROWS
    tm = 1024
    tm_in = 2048
    tm_ctx = 512
    assert seq % tm == 0 and seq % tm_in == 0 and (batch * lc) % tm_ctx == 0

    cond = jnp.concatenate([c, c_ctx[None, :], jnp.zeros((MOD_ROWS - batch - 1, d), F32)], axis=0)
    mod = _modulation(cond, w_mod, b_mod)
    rpb_bias, w_in_bf = _rpb_tables(na_rpb, w_in)
    rope_tabs = _rope_tables(seq)

    lat_row = lambda i: i // (seq // tm)
    lat_row_in = lambda i: i // (seq // tm_in)
    ctx_row = lambda i: batch

    xs = x.reshape(batch * seq, d)
    cs = ctx.reshape(batch * lc, d)
    for l in range(depth):
        last = l == depth - 1
        g1 = g_norm1[l].reshape(1, d)
        g2 = g_norm2[l].reshape(1, d)
        gains = [jnp.tile(na_q_gain[l], NA_HEADS).reshape(1, -1), jnp.tile(na_k_gain[l], NA_HEADS).reshape(1, -1),
                 jnp.tile(wa_q_gain[l], WA_HEADS).reshape(1, -1), jnp.tile(wa_k_gain[l], WA_KV_HEADS).reshape(1, -1)]
        go_na = g_out[l, :NA_WIDTH].reshape(1, -1)
        go_cv = g_out[l, NA_WIDTH:NA_WIDTH + CONV_CH].reshape(1, -1)
        go_wa = g_out[l, NA_WIDTH + CONV_CH:].reshape(1, -1)
        cb = conv_bias[l].reshape(1, -1)

        naq, nak, nav, cvu, cvb, waq, wak, wav = _in_proj(xs, mod, l, lat_row_in, g1, w_in_bf, gains, rope_tabs, tm_in)
        if last:
            cnak, cnav, cwak, cwav = _in_proj(cs, mod, l, ctx_row, g1, w_in_bf, gains, None, tm_ctx, kv_only=True)
        else:
            cnaq, cnak, cnav, ccvu, ccvb, cwaq, cwak, cwav = _in_proj(cs, mod, l, ctx_row, g1, w_in_bf, gains, None, tm_ctx)

        b3 = lambda a, n: a.reshape(batch, n, a.shape[-1])
        casts = [(w_o, l), (w_fc1, l), (w_fc2, l)] + ([] if last else [(w_in, l + 1)])
        na_n, wa_n, cast_out = _attention(
            wa_sink[l], b3(naq, seq), b3(nak, seq), b3(nav, seq), b3(cnak, lc), b3(cnav, lc), rpb_bias, l, go_na,
            b3(waq, seq), b3(wak, seq), b3(wav, seq), b3(cwak, lc), b3(cwav, lc), go_wa, casts)
        w_o_bf, w1_bf, w2_bf = cast_out[:3]
        if not last:
            w_in_bf = cast_out[3]
        xs = _out_mlp(xs, mod, l, lat_row, (na_n.reshape(batch * seq, -1), wa_n.reshape(batch * seq, -1)),
                      cvu, cvb, conv_w[l], cb, go_cv, w_o_bf, g2, w1_bf, w2_bf, seq, tm)
        if not last:
            ctx_attn = (wa_sink[l], cnaq, cnak, cnav, cwaq, cwak, cwav, go_na, go_wa)
            cs = _out_mlp(cs, mod, l, ctx_row, ctx_attn, ccvu, ccvb, conv_w[l], cb, go_cv,
                          w_o_bf, g2, w1_bf, w2_bf, lc, lc)
    return xs.reshape(batch, seq, d)
```

```python
import functools

import jax
import jax.numpy as jnp
from jax import lax
from jax.experimental import pallas as pl
from jax.experimental.pallas import tpu as pltpu

D_MODEL = 1024
GRID_W = 64
HEAD_DIM = 64
NA_HEADS = 4
NA_WIDTH = NA_HEADS * HEAD_DIM
CONV_CH = 256
WA_HEADS = 8
WA_KV_HEADS = 2
WA_GROUP = WA_HEADS // WA_KV_HEADS
WA_QW = WA_HEADS * HEAD_DIM
WA_KVW = WA_KV_HEADS * HEAD_DIM
WA_K2W = 2 * WA_KVW
WA_V3W = 2 * WA_KVW
NA_WIN_ROWS = 8
NA_WIN_COLS = 16
NA_DROWS = 2 * NA_WIN_ROWS - 1
NA_DCOLS = 2 * NA_WIN_COLS - 1
WA_WINDOW = 128
WA_BLOCK = 128
WA_BAND = 3 * WA_BLOCK
FFN_DIM = 4 * D_MODEL
FFN_CHUNK = 1024
ROPE_BASE = 10000.0
EPS = 1e-6
NEG_INF = -1e30
IN_WIDTH = 2304
OFF_NA_Q, OFF_NA_K, OFF_NA_V = 0, 256, 512
OFF_CV_X, OFF_CV_B, OFF_CV_C = 768, 1024, 1280
OFF_WA_Q, OFF_WA_K, OFF_WA_V = 1536, 2048, 2176
MOD_ROWS = 16
MOD_TN = 2048
IN_SUB = 256
MLP_SUB = 1024
LANES = 128
MXU_DIM = 256
VMEM_LIMIT = 60 * 1024 * 1024
LOG2E = 1.4426950408889634
Q_SCALE = LOG2E * HEAD_DIM ** -0.5

F32 = jnp.float32
BF16 = jnp.bfloat16


def _dot(a, b):
    return jnp.dot(a, b, preferred_element_type=F32)


def _dot_nt(a, b):
    return lax.dot_general(a, b, (((1,), (1,)), ((), ())), preferred_element_type=F32)


def _split_bf16(a):
    hi = a.astype(BF16)
    lo = (a - hi.astype(F32)).astype(BF16)
    return hi, lo


def _iota(shape, dim):
    return lax.broadcasted_iota(jnp.int32, shape, dim)


def _params(**kw):
    return pltpu.CompilerParams(vmem_limit_bytes=VMEM_LIMIT, **kw)


def _mod_kernel(cond_ref, w_ref, b_ref, o_ref):
    a = cond_ref[...]
    a = a * (1.0 / (1.0 + jnp.exp(-a)))
    ah, al = _split_bf16(a)
    wh, wl = _split_bf16(w_ref[...])
    rows = a.shape[0]
    both = _dot(jnp.concatenate([ah, al], axis=0), wh)
    res = both[:rows] + both[rows:] + _dot(ah, wl) + b_ref[...]
    for c in range(o_ref.shape[1]):
        o_ref[:, c, 0, :] = res[:, c * D_MODEL:(c + 1) * D_MODEL]


def _modulation(cond, w_mod, b_mod):
    depth = w_mod.shape[0]
    n_out = w_mod.shape[2]
    per_tile = MOD_TN // D_MODEL
    return pl.pallas_call(
        _mod_kernel,
        out_shape=jax.ShapeDtypeStruct((depth, MOD_ROWS, n_out // D_MODEL, 1, D_MODEL), F32),
        grid=(depth, n_out // MOD_TN),
        in_specs=[
            pl.BlockSpec((MOD_ROWS, D_MODEL), lambda l, j: (0, 0)),
            pl.BlockSpec((None, D_MODEL, MOD_TN), lambda l, j: (l, 0, j)),
            pl.BlockSpec((None, 1, MOD_TN), lambda l, j: (l, 0, j)),
        ],
        out_specs=pl.BlockSpec((None, MOD_ROWS, per_tile, 1, D_MODEL), lambda l, j: (l, 0, j, 0, 0)),
        compiler_params=_params(),
        name="modulation",
    )(cond, w_mod, b_mod.reshape(depth, 1, n_out))


def _rpb_kernel(rpb_ref, w_ref, o_ref, wbf_ref):
    wbf_ref[...] = w_ref[...].astype(BF16)
    shape = (GRID_W, LANES)
    q = _iota(shape, 0)
    lane = _iota(shape, 1)
    k = lane % GRID_W
    left = lane < GRID_W
    col_start = jnp.clip(q - NA_WIN_COLS // 2, 0, GRID_W - NA_WIN_COLS)
    col_ok = (k >= col_start) & (k < col_start + NA_WIN_COLS)
    centre = NA_WIN_COLS - 1
    for d in range(NA_DROWS - 1):
        tiles = []
        for h in range(NA_HEADS):
            t1 = pltpu.roll(jnp.broadcast_to(rpb_ref[h, d:d + 1, :], shape), LANES - centre, axis=1,
                            stride=1, stride_axis=0)
            t2 = pltpu.roll(jnp.broadcast_to(rpb_ref[h, d + 1:d + 2, :], shape), GRID_W - centre, axis=1,
                            stride=1, stride_axis=0)
            t = jnp.where(left, t1, t2)
            tiles.append(jnp.where(col_ok, t * LOG2E, NEG_INF))
        o_ref[d] = jnp.concatenate(tiles, axis=0).T


def _rpb_tables(na_rpb, w_in):
    depth = na_rpb.shape[0]
    padded = jnp.pad(na_rpb, ((0, 0), (0, 0), (0, 1), (0, LANES - NA_DCOLS)))
    slab = w_in.shape[1] // depth
    return pl.pallas_call(
        _rpb_kernel,
        out_shape=[jax.ShapeDtypeStruct((depth, NA_DROWS - 1, LANES, NA_HEADS * GRID_W), F32),
                   jax.ShapeDtypeStruct(w_in.shape[1:], BF16)],
        grid=(depth,),
        in_specs=[pl.BlockSpec((None,) + padded.shape[1:], lambda l: (l, 0, 0, 0)),
                  pl.BlockSpec((None, slab, w_in.shape[2]), lambda l: (0, l, 0))],
        out_specs=[pl.BlockSpec((None, NA_DROWS - 1, LANES, NA_HEADS * GRID_W), lambda l: (l, 0, 0, 0)),
                   pl.BlockSpec((slab, w_in.shape[2]), lambda l: (l, 0))],
        compiler_params=_params(),
        name="rpb_tables",
    )(padded, w_in)


def _head_rms(t, gain):
    width = t.shape[1]
    sq = (t * t).astype(BF16)
    blk = min(width, MXU_DIM)
    ones = jnp.where(_iota((blk, blk), 0) // HEAD_DIM == _iota((blk, blk), 1) // HEAD_DIM, 1.0, 0.0).astype(BF16)
    sums = [_dot(sq[:, c:c + blk], ones) for c in range(0, width, blk)]
    ss = sums[0] if len(sums) == 1 else jnp.concatenate(sums, axis=1)
    return t * lax.rsqrt(ss * (1.0 / HEAD_DIM) + EPS) * gain


def _rope(t, cos, sin):
    lane = _iota((t.shape[0], LANES), 1)
    first = (lane % (HEAD_DIM // 2)) < (HEAD_DIM // 4)
    out = []
    for c in range(0, t.shape[1], LANES):
        u = t[:, c:c + LANES]
        partner = jnp.where(first, pltpu.roll(u, LANES - HEAD_DIM // 4, axis=1), pltpu.roll(u, HEAD_DIM // 4, axis=1))
        out.append(u * cos + partner * sin)
    return out[0] if len(out) == 1 else jnp.concatenate(out, axis=1)


def _dup_kv(t):
    lane = _iota(t.shape, 1)
    swapped = pltpu.roll(t, HEAD_DIM, axis=1)
    left = lane < HEAD_DIM
    return jnp.concatenate([jnp.where(left, t, swapped), jnp.where(left, swapped, t)], axis=1)


def _inproj_kernel(*refs, rope, kv_only, n_w, sub):
    x_ref, mod_ref, g1_ref = refs[:3]
    w_refs = refs[3:3 + n_w]
    gnq_ref, gnk_ref, gwq_ref, gwk_ref = refs[3 + n_w:7 + n_w]
    rest = refs[7 + n_w:]
    if rope:
        cos_ref, sin_ref = rest[:2]
        rest = rest[2:]
    out_refs = rest
    if kv_only:
        nak_ref, nav_ref, wak_ref, wav_ref = out_refs
        o_nk, o_nv, o_wk, o_wv, o_end = 0, NA_WIDTH, 2 * NA_WIDTH, 2 * NA_WIDTH + WA_KVW, 2 * NA_WIDTH + 2 * WA_KVW
    else:
        naq_ref, nak_ref, nav_ref, cvu_ref, cvb_ref, waq_ref, wak_ref, wav_ref = out_refs
        o_nk, o_nv, o_wk, o_wv, o_end = OFF_NA_K, OFF_NA_V, OFF_WA_K, OFF_WA_V, IN_WIDTH
    def project(rows):
        x = x_ref[rows, :]
        ms = jnp.mean(x * x, axis=-1, keepdims=True)
        h = x * lax.rsqrt(ms + EPS) * g1_ref[...]
        h = h * (1.0 + mod_ref[1]) + mod_ref[0]
        hb = h.astype(BF16)
        return jnp.concatenate([_dot(hb, w_ref[...]) for w_ref in w_refs], axis=1)

    def finish(rows, p):
        nak_ref[rows, :] = _head_rms(p[:, o_nk:o_nk + NA_WIDTH], gnk_ref[...]).astype(BF16)
        nav_ref[rows, :] = p[:, o_nv:o_nv + NA_WIDTH].astype(BF16)
        wk = _head_rms(p[:, o_wk:o_wv], gwk_ref[...])
        if rope:
            wk = _rope(wk, cos_ref[rows, :], sin_ref[rows, :])
        wak_ref[rows, :] = _dup_kv(wk).astype(BF16)
        vv = p[:, o_wv:o_end]
        left = _iota(vv.shape, 1) < HEAD_DIM
        wav_ref[rows, :] = jnp.concatenate([jnp.where(left, vv, 1.0), jnp.where(left, pltpu.roll(vv, HEAD_DIM, axis=1), 1.0)],
                                           axis=1).astype(BF16)
        if kv_only:
            return
        naq_ref[rows, :] = (_head_rms(p[:, OFF_NA_Q:OFF_NA_K], gnq_ref[...]) * Q_SCALE).astype(BF16)
        cvu_ref[rows, :] = p[:, OFF_CV_C:OFF_WA_Q] * p[:, OFF_CV_X:OFF_CV_B]
        cvb_ref[rows, :] = p[:, OFF_CV_B:OFF_CV_C]
        wq = _head_rms(p[:, OFF_WA_Q:OFF_WA_K], gwq_ref[...])
        if rope:
            wq = _rope(wq, cos_ref[rows, :], sin_ref[rows, :])
        waq_ref[rows, :] = (wq * Q_SCALE).astype(BF16)

    for r0 in range(0, x_ref.shape[0], sub):
        rows = slice(r0, r0 + sub)
        finish(rows, project(rows))


def _in_proj(x2d, mod, layer, mod_row, g1, w_in, gains, rope_tabs, tm, kv_only=False):
    rows = x2d.shape[0]
    rope = rope_tabs is not None
    row_spec = lambda w: pl.BlockSpec((tm, w), lambda i: (i, 0))
    const = lambda shape: pl.BlockSpec(shape, lambda i: (0,) * len(shape))
    once = pl.Buffered(1)
    if kv_only:
        col_blocks = [OFF_NA_K // MXU_DIM, OFF_NA_V // MXU_DIM, OFF_WA_K // MXU_DIM]
        w_specs = [pl.BlockSpec((D_MODEL, MXU_DIM), functools.partial(lambda j, i: (0, j), j),
                                pipeline_mode=once) for j in col_blocks]
        widths = [(NA_WIDTH, BF16), (NA_WIDTH, BF16), (WA_K2W, BF16), (WA_V3W, BF16)]
    else:
        w_specs = [pl.BlockSpec((D_MODEL, IN_WIDTH), lambda i: (0, 0), pipeline_mode=once)]
        widths = [(NA_WIDTH, BF16), (NA_WIDTH, BF16), (NA_WIDTH, BF16), (CONV_CH, F32), (CONV_CH, F32),
                  (WA_QW, BF16), (WA_K2W, BF16), (WA_V3W, BF16)]
    n_w = len(w_specs)
    in_specs = [
        row_spec(D_MODEL),
        pl.BlockSpec((None, None, 6, 1, D_MODEL), lambda i: (layer, mod_row(i), 0, 0, 0)),
        const((1, D_MODEL)),
        *w_specs,
        const((1, NA_WIDTH)), const((1, NA_WIDTH)), const((1, WA_QW)), const((1, WA_KVW)),
    ]
    args = [x2d, mod, g1, *([w_in] * n_w), *gains]
    if rope:
        seq_tiles = rope_tabs[0].shape[0] // tm
        in_specs += [pl.BlockSpec((tm, LANES), lambda i: (i % seq_tiles, 0))] * 2
        args += list(rope_tabs)
    return pl.pallas_call(
        functools.partial(_inproj_kernel, rope=rope, kv_only=kv_only, n_w=n_w, sub=min(tm, IN_SUB)),
        out_shape=[jax.ShapeDtypeStruct((rows, w), dt) for w, dt in widths],
        grid=(rows // tm,),
        in_specs=in_specs,
        out_specs=[row_spec(w) for w, _ in widths],
        compiler_params=_params(),
        name="in_proj_rope" if rope else ("in_proj_ctx_kv" if kv_only else "in_proj_ctx"),
    )(*args)


def _lane_chunks(s):
    return [s[:, c:c + LANES] for c in range(0, s.shape[1], LANES)]


def _row_max(scores, extra=None):
    chunks = [ch for s in scores for ch in _lane_chunks(s)]
    if extra is not None:
        chunks.append(extra)
    folded = functools.reduce(jnp.maximum, chunks)
    return jnp.broadcast_to(jnp.max(folded, axis=-1, keepdims=True), folded.shape)


def _softmax_pv(parts, extra=None, sums_from_values=False, m=None):
    lane_chunks = _lane_chunks
    if m is None:
        m = _row_max([s for s, _ in parts], extra)
    ps = [jnp.concatenate([jnp.exp2(ch - m) for ch in lane_chunks(s)], axis=1) for s, _ in parts]
    pv = _dot(jnp.concatenate([p.astype(BF16) for p in ps], axis=1),
              jnp.concatenate([v for _, v in parts], axis=0))
    if sums_from_values:
        denom = pv[:, -LANES:]
        pv = pv[:, :-LANES]
        if extra is not None:
            denom = denom + jnp.exp2(extra - m)
        return pv / denom
    psum = functools.reduce(jnp.add, [ch for p in ps for ch in lane_chunks(p)])
    if extra is not None:
        psum = psum + jnp.where(_iota(extra.shape, 1) == 0, jnp.exp2(extra - m), 0.0)
    return pv / jnp.sum(psum, axis=-1, keepdims=True)


def _stack_heads(q, n_heads):
    head = _iota(q.shape, 1) // HEAD_DIM
    zero = jnp.zeros_like(q)
    return jnp.concatenate([jnp.where(head == h, q, zero) for h in range(n_heads)], axis=0)


def _unstack_heads(o, n_heads):
    rows = o.shape[0] // n_heads
    head = _iota((rows, o.shape[1]), 1) // HEAD_DIM
    out = jnp.where(head == 0, o[:rows], 0.0)
    for h in range(1, n_heads):
        out = jnp.where(head == h, o[h * rows:(h + 1) * rows], out)
    return out


def _group_rms(o, gain):
    ms = jnp.mean(o * o, axis=-1, keepdims=True)
    return o * lax.rsqrt(ms + EPS) * gain


SUBLANES = 8


def _wa_scores(qg, keys):
    rows = qg.shape[0]
    left = _iota((rows, LANES), 1) < HEAD_DIM
    zero = jnp.zeros((rows, LANES), qg.dtype)
    blocks = []
    for pair in range(WA_GROUP // 2):
        qp = qg[:, pair * LANES:(pair + 1) * LANES]
        blocks += [jnp.where(left, qp, zero), jnp.where(left, zero, qp)]
    qs = jnp.concatenate(blocks, axis=0)
    scores = []
    for k2, bias in keys:
        s = _dot_nt(k2, qs)
        if bias is not None:
            s = s + jnp.concatenate([bias] * WA_GROUP, axis=1)
        scores.append(s)
    return scores


def _wa_sink_rows(rows, sinks):
    return jnp.concatenate([jnp.full((SUBLANES, rows), sinks[g] * LOG2E, F32) for g in range(WA_GROUP)], axis=1)


def _wa_col_max(scores, sink_rows):
    m = sink_rows[:1]
    for s in scores:
        m = jnp.maximum(m, jnp.max(s, axis=0, keepdims=True))
    return jnp.broadcast_to(m, sink_rows.shape)


def _wa_finish(scores, values, sinks, m=None):
    rows = scores[0].shape[1] // WA_GROUP
    sink_rows = _wa_sink_rows(rows, sinks)
    if m is None:
        m = _wa_col_max(scores, sink_rows)
    p = jnp.concatenate([jnp.exp2(s - m[:1]).astype(BF16) for s in scores], axis=0)
    v = jnp.concatenate(values, axis=0)
    ot = lax.dot_general(v, p, (((0,), (0,)), ((), ())), preferred_element_type=F32)
    denom = ot[HEAD_DIM:HEAD_DIM + SUBLANES] + jnp.exp2(sink_rows - m)
    on = ot[:HEAD_DIM] / denom[:1]
    pairs = [jnp.concatenate([on[:, (2 * pr) * rows:(2 * pr + 1) * rows],
                              on[:, (2 * pr + 1) * rows:(2 * pr + 2) * rows]], axis=0).T
             for pr in range(WA_GROUP // 2)]
    return jnp.concatenate(pairs, axis=1)


def _na_steps(q_ref, k_ref, v_ref, kc_ref, vc_ref, bias_ref, g_ref, o_ref, vt_ref, vct_ref):
    seq = q_ref.shape[0]
    n_rows = seq // GRID_W
    win = NA_WIN_ROWS * GRID_W
    lc = kc_ref.shape[0]
    gain = g_ref[...]

    def prepare():
        vt_ref[0] = v_ref[...].astype(F32).T.astype(BF16)
        vt_ref[1, :, :seq - GRID_W] = v_ref[GRID_W:, :].astype(F32).T.astype(BF16)
        vct_ref[...] = vc_ref[...].astype(F32).T.astype(BF16)

    def window(r):
        start = jnp.clip(r - NA_WIN_ROWS // 2, 0, n_rows - NA_WIN_ROWS)
        return start, start - r + (NA_WIN_ROWS - 1)

    def scores(r, s_ref):
        start, d0 = window(r)
        tok0 = pl.multiple_of(start * GRID_W, GRID_W)
        q = q_ref[pl.ds(pl.multiple_of(r * GRID_W, GRID_W), GRID_W), :]
        qs = _stack_heads(q, NA_HEADS)
        bias = jnp.concatenate([bias_ref[d0 + 2 * j] for j in range(NA_WIN_ROWS // 2)], axis=0)
        s_ctx = _dot_nt(kc_ref[...], qs)
        s_loc = _dot_nt(k_ref[pl.ds(tok0, win), :], qs) + bias
        m = jnp.maximum(jnp.max(s_ctx, axis=0, keepdims=True), jnp.max(s_loc, axis=0, keepdims=True))
        s_ref[:lc, :] = s_ctx
        s_ref[lc:lc + win, :] = s_loc
        s_ref[lc + win:, :] = jnp.broadcast_to(m, (SUBLANES, m.shape[1]))

    def finish(r, s_ref):
        start, _ = window(r)
        odd = start % 2
        lane0 = pl.multiple_of((start - odd) * GRID_W, LANES)
        m = s_ref[lc + win:lc + win + 1, :]
        p_ctx = jnp.exp2(s_ref[:lc, :] - m)
        p_loc = jnp.exp2(s_ref[lc:lc + win, :] - m)
        denom = jnp.sum(p_ctx, axis=0, keepdims=True) + jnp.sum(p_loc, axis=0, keepdims=True)
        ot = _dot(vct_ref[...], p_ctx.astype(BF16)) + _dot(vt_ref[odd, :, pl.ds(lane0, win)], p_loc.astype(BF16))
        o4 = (ot * (1.0 / denom)).T
        o = _unstack_heads(o4, NA_HEADS)
        o_ref[pl.ds(pl.multiple_of(r * GRID_W, GRID_W), GRID_W), :] = _group_rms(o, gain).astype(o_ref.dtype)

    def step(r, cur_ref, nxt_ref, lookahead=True):
        if lookahead:
            scores(r + 1, nxt_ref)
        finish(r, cur_ref)

    return prepare, scores, step


def _wa_steps(sink_ref, q_ref, k_ref, v_ref, kc_ref, vc_ref, g_ref, o_ref):
    seq = q_ref.shape[0]
    n_blocks = seq // WA_BLOCK
    lc = kc_ref.shape[0]
    gain = g_ref[...]
    key = _iota((WA_BAND, WA_BLOCK), 0)
    qry = _iota((WA_BAND, WA_BLOCK), 1)

    def window(n):
        q0 = pl.multiple_of(n * WA_BLOCK, WA_BLOCK)
        k0 = pl.multiple_of(jnp.clip(q0 - WA_BLOCK, 0, seq - WA_BAND), WA_BLOCK)
        return q0, k0

    def scores(n, kh, s_ref):
        q0, k0 = window(n)
        band = jnp.where(jnp.abs((k0 + key) - (q0 + qry)) <= WA_WINDOW, 0.0, NEG_INF)
        lanes = slice(kh * LANES, (kh + 1) * LANES)
        qg = q_ref[pl.ds(q0, WA_BLOCK), kh * WA_GROUP * HEAD_DIM:(kh + 1) * WA_GROUP * HEAD_DIM]
        s_ctx, s_loc = _wa_scores(qg, [(kc_ref[:, lanes], None), (k_ref[pl.ds(k0, WA_BAND), lanes], band)])
        s_ref[kh, :lc, :] = s_ctx
        s_ref[kh, lc:lc + WA_BAND, :] = s_loc
        s_ref[kh, lc + WA_BAND:, :] = _wa_col_max([s_ctx, s_loc], _wa_sink_rows(WA_BLOCK, sinks(kh)))

    def sinks(kh):
        return [sink_ref[kh * WA_GROUP + g] for g in range(WA_GROUP)]

    def finish(n, kh, s_ref):
        _, k0 = window(n)
        lanes = slice(kh * LANES, (kh + 1) * LANES)
        return _wa_finish([s_ref[kh, :lc, :], s_ref[kh, lc:lc + WA_BAND, :]],
                          [vc_ref[:, lanes], v_ref[pl.ds(k0, WA_BAND), lanes]], sinks(kh),
                          m=s_ref[kh, lc + WA_BAND:, :])

    def step(n, cur_ref, nxt_ref, lookahead=True):
        if lookahead:
            for kh in range(WA_KV_HEADS):
                scores(n + 1, kh, nxt_ref)
        o = jnp.concatenate([finish(n, kh, cur_ref) for kh in range(WA_KV_HEADS)], axis=1)
        q0, _ = window(n)
        o_ref[pl.ds(q0, WA_BLOCK), :] = _group_rms(o, gain).astype(o_ref.dtype)

    return scores, step


def _attn_kernel(sink_ref, nq_ref, nk_ref, nv_ref, nkc_ref, nvc_ref, bias_ref, gn_ref,
                 wq_ref, wk_ref, wv_ref, wkc_ref, wvc_ref, gw_ref, *rest, n_cast):
    f32_refs = rest[:n_cast]
    on_ref, ow_ref = rest[n_cast:n_cast + 2]
    bf16_refs = rest[n_cast + 2:2 * n_cast + 2]
    nsa_ref, nsb_ref, wsa_ref, wsb_ref, vt_ref, vct_ref = rest[2 * n_cast + 2:]
    for src_ref, dst_ref in zip(f32_refs, bf16_refs):
        dst_ref[...] = src_ref[...].astype(dst_ref.dtype)
    na_prepare, na_scores, na_step = _na_steps(nq_ref, nk_ref, nv_ref, nkc_ref, nvc_ref, bias_ref, gn_ref, on_ref,
                                               vt_ref, vct_ref)
    na_prepare()
    wa_scores, wa_step = _wa_steps(sink_ref, wq_ref, wk_ref, wv_ref, wkc_ref, wvc_ref, gw_ref, ow_ref)
    n_blocks = wq_ref.shape[0] // WA_BLOCK

    for kh in range(WA_KV_HEADS):
        wa_scores(0, kh, wsa_ref)
    na_scores(0, nsa_ref)

    def pair(i, last=False):
        for half, (w_cur, w_nxt) in enumerate(((wsa_ref, wsb_ref), (wsb_ref, wsa_ref))):
            n = 2 * i + half
            final = last and half == 1
            wa_step(n, w_cur, w_nxt, lookahead=not final)
            na_step(2 * n, nsa_ref, nsb_ref)
            na_step(2 * n + 1, nsb_ref, nsa_ref, lookahead=not final)

    def body(i, carry):
        pair(i)
        return carry

    n_pairs = n_blocks // 2
    lax.fori_loop(0, n_pairs - 2, body, 0, unroll=2)
    pair(n_pairs - 2)
    pair(n_pairs - 1, last=True)


def _attention(sink, nq, nk, nv, nkc, nvc, bias, layer, gain_na, wq, wk2, wv3, wkc2, wvc3, gain_wa, to_bf16):
    batch, seq, _ = nq.shape
    slab_in = lambda w, j: pl.BlockSpec((None, w.shape[1] // batch, w.shape[2]), lambda b: (j, b, 0))
    slab_out = lambda w: pl.BlockSpec((w.shape[1] // batch, w.shape[2]), lambda b: (b, 0))
    lc = nkc.shape[1]
    assert WA_BLOCK == 2 * GRID_W and (seq // WA_BLOCK) % 2 == 0
    per_b = lambda n, w: pl.BlockSpec((None, n, w), lambda b: (b, 0, 0))
    na_scratch = pltpu.VMEM((lc + NA_WIN_ROWS * GRID_W + SUBLANES, NA_HEADS * GRID_W), F32)
    vt_scratch = pltpu.VMEM((2, NA_WIDTH, seq), BF16)
    vct_scratch = pltpu.VMEM((NA_WIDTH, lc), BF16)
    wa_scratch = pltpu.VMEM((WA_KV_HEADS, lc + WA_BAND + SUBLANES, WA_GROUP * WA_BLOCK), F32)
    outs = pl.pallas_call(
        functools.partial(_attn_kernel, n_cast=len(to_bf16)),
        out_shape=[jax.ShapeDtypeStruct((batch, seq, NA_WIDTH), BF16),
                   jax.ShapeDtypeStruct((batch, seq, WA_QW), BF16),
                   *[jax.ShapeDtypeStruct(w.shape[1:], BF16) for w, _ in to_bf16]],
        grid=(batch,),
        in_specs=[pl.BlockSpec(memory_space=pltpu.SMEM),
                  per_b(seq, NA_WIDTH), per_b(seq, NA_WIDTH), per_b(seq, NA_WIDTH),
                  per_b(lc, NA_WIDTH), per_b(lc, NA_WIDTH),
                  pl.BlockSpec((None,) + bias.shape[1:], lambda b: (layer, 0, 0, 0)),
                  pl.BlockSpec((1, NA_WIDTH), lambda b: (0, 0)),
                  per_b(seq, WA_QW), per_b(seq, WA_K2W), per_b(seq, WA_V3W),
                  per_b(lc, WA_K2W), per_b(lc, WA_V3W),
                  pl.BlockSpec((1, WA_QW), lambda b: (0, 0)),
                  *[slab_in(w, j) for w, j in to_bf16]],
        out_specs=[per_b(seq, NA_WIDTH), per_b(seq, WA_QW), *[slab_out(w) for w, _ in to_bf16]],
        scratch_shapes=[na_scratch, na_scratch, wa_scratch, wa_scratch, vt_scratch, vct_scratch],
        compiler_params=_params(),
        name="attention",
    )(sink, nq, nk, nv, nkc, nvc, bias, gain_na, wq, wk2, wv3, wkc2, wvc3, gain_wa, *[w for w, _ in to_bf16])
    return outs[0], outs[1], outs[2:]


def _ctx_attn(sink_ref, qn_ref, kn_ref, vn_ref, qw_ref, kw_ref, vw_ref, gn_ref, gw_ref):
    qs = _stack_heads(qn_ref[...], NA_HEADS)
    o4 = _softmax_pv([(_dot_nt(qs, kn_ref[...]), vn_ref[...])])
    na_n = _group_rms(_unstack_heads(o4, NA_HEADS), gn_ref[...]).astype(BF16)
    outs = []
    for kh in range(WA_KV_HEADS):
        lanes = slice(kh * LANES, (kh + 1) * LANES)
        qg = qw_ref[:, kh * WA_GROUP * HEAD_DIM:(kh + 1) * WA_GROUP * HEAD_DIM]
        sinks = [sink_ref[kh * WA_GROUP + g] for g in range(WA_GROUP)]
        outs.append(_wa_finish(_wa_scores(qg, [(kw_ref[:, lanes], None)]), [vw_ref[:, lanes]], sinks))
    return na_n, _group_rms(jnp.concatenate(outs, axis=1), gw_ref[...]).astype(BF16)


def _out_mlp_kernel(*refs, seq_len, sub, ctx_attn):
    x_ref, mod_ref = refs[:2]
    n_attn = 9 if ctx_attn else 2
    attn_refs = refs[2:2 + n_attn]
    u_ref, uprev_ref, unext_ref, cvb_ref, cw_ref, cb_ref, g_ref, g2_ref, wo_ref = refs[2 + n_attn:11 + n_attn]
    rest = refs[11 + n_attn:]
    n_ffn = FFN_DIM // FFN_CHUNK
    w1_refs, w2_refs, (o_ref,) = rest[:n_ffn], rest[n_ffn:2 * n_ffn], rest[2 * n_ffn:]
    tm = x_ref.shape[0]
    i = pl.program_id(0)
    if ctx_attn:
        na_t, wa_t = _ctx_attn(*attn_refs)
    else:
        na_t, wa_t = attn_refs
    u = u_ref[...]
    at_start = (i * tm) % seq_len == 0
    at_end = ((i + 1) * tm) % seq_len == 0
    prev_row = jnp.where(at_start, 0.0, uprev_ref[7:8, :])
    next_row = jnp.where(at_end, 0.0, unext_ref[0:1, :])
    row = _iota(u.shape, 0)
    up = jnp.where(row == 0, prev_row, pltpu.roll(u, 1, axis=0))
    dn = jnp.where(row == tm - 1, next_row, pltpu.roll(u, tm - 1, axis=0))
    y = cb_ref[...] + cw_ref[0:1, :] * up + cw_ref[1:2, :] * u + cw_ref[2:3, :] * dn
    cv = _group_rms(cvb_ref[...] * y, g_ref[...]).astype(BF16)
    for r0 in range(0, tm, sub):
        rows = slice(r0, r0 + sub)
        mixed = jnp.concatenate([na_t[rows, :], cv[rows, :], wa_t[rows, :]], axis=1)
        x = x_ref[rows, :] + mod_ref[2] * _dot(mixed, wo_ref[...])
        ms = jnp.mean(x * x, axis=-1, keepdims=True)
        h = x * lax.rsqrt(ms + EPS) * g2_ref[...]
        h = (h * (1.0 + mod_ref[4]) + mod_ref[3]).astype(BF16)
        acc = None
        for w1_ref, w2_ref in zip(w1_refs, w2_refs):
            a = jnp.maximum(_dot(h, w1_ref[...]), 0.0)
            part = _dot((a * a).astype(BF16), w2_ref[...])
            acc = part if acc is None else acc + part
        o_ref[rows, :] = x + mod_ref[5] * acc


def _out_mlp(x2d, mod, layer, mod_row, attn, cv_u, cv_b, conv_w, conv_b, gain_cv, w_o, g2, w1, w2, seq_len, tm):
    rows = x2d.shape[0]
    ctx_attn = len(attn) == 9
    assert not ctx_attn or tm == seq_len
    halo = 8
    per_tile = tm // halo
    last = rows // halo - 1
    row_spec = lambda w: pl.BlockSpec((tm, w), lambda i: (i, 0))
    const = lambda shape: pl.BlockSpec(shape, lambda i: (0,) * len(shape), pipeline_mode=pl.Buffered(1))
    once = pl.Buffered(1)
    n_ffn = FFN_DIM // FFN_CHUNK
    wo_spec = pl.BlockSpec((D_MODEL, D_MODEL), lambda i: (0, 0), pipeline_mode=once)
    w1_specs = [pl.BlockSpec((D_MODEL, FFN_CHUNK), functools.partial(lambda j, i: (0, j), j),
                             pipeline_mode=once) for j in range(n_ffn)]
    w2_specs = [pl.BlockSpec((FFN_CHUNK, D_MODEL), functools.partial(lambda j, i: (j, 0), j),
                             pipeline_mode=once) for j in range(n_ffn)]
    small = lambda a: pl.BlockSpec(a.shape, lambda i: (0,) * a.ndim)
    if ctx_attn:
        attn_specs = [pl.BlockSpec(memory_space=pltpu.SMEM), *[row_spec(a.shape[1]) for a in attn[1:7]],
                      small(attn[7]), small(attn[8])]
    else:
        attn_specs = [row_spec(NA_WIDTH), row_spec(WA_QW)]
    return pl.pallas_call(
        functools.partial(_out_mlp_kernel, seq_len=seq_len, sub=min(tm, MLP_SUB), ctx_attn=ctx_attn),
        out_shape=jax.ShapeDtypeStruct((rows, D_MODEL), F32),
        grid=(rows // tm,),
        in_specs=[
            row_spec(D_MODEL),
            pl.BlockSpec((None, None, 6, 1, D_MODEL), lambda i: (layer, mod_row(i), 0, 0, 0)),
            *attn_specs, row_spec(CONV_CH),
            pl.BlockSpec((halo, CONV_CH), lambda i: (jnp.maximum(i * per_tile - 1, 0), 0)),
            pl.BlockSpec((halo, CONV_CH), lambda i: (jnp.minimum((i + 1) * per_tile, last), 0)),
            row_spec(CONV_CH),
            const((3, CONV_CH)), const((1, CONV_CH)), const((1, CONV_CH)),
            const((1, D_MODEL)),
            wo_spec, *w1_specs, *w2_specs,
        ],
        out_specs=row_spec(D_MODEL),
        compiler_params=_params(),
        name="out_mlp",
    )(x2d, mod, *attn, cv_u, cv_u, cv_u, cv_b, conv_w, conv_b, gain_cv, g2,
      w_o, *([w1] * n_ffn), *([w2] * n_ffn))


def _rope_tables(seq):
    quarter = HEAD_DIM // 4
    inv = ROPE_BASE ** (-jnp.arange(quarter, dtype=F32) / quarter)
    t = jnp.arange(seq)
    ang_r = (t // GRID_W).astype(F32)[:, None] * inv[None, :]
    ang_c = (t % GRID_W).astype(F32)[:, None] * inv[None, :]
    cos = jnp.concatenate([jnp.cos(ang_r), jnp.cos(ang_r), jnp.cos(ang_c), jnp.cos(ang_c)], axis=1)
    sin = jnp.concatenate([-jnp.sin(ang_r), jnp.sin(ang_r), -jnp.sin(ang_c), jnp.sin(ang_c)], axis=1)
    reps = LANES // HEAD_DIM
    return jnp.tile(cos, (1, reps)), jnp.tile(sin, (1, reps))


def kernel(x, c, ctx, c_ctx, w_mod, b_mod, g_norm1, g_norm2, w_in, na_q_gain, na_k_gain, na_rpb,
           conv_w, conv_bias, wa_q_gain, wa_k_gain, wa_sink, g_out, w_o, w_fc1, w_fc2):
    batch, seq, d = x.shape
    lc = ctx.shape[1]
    depth = w_mod.shape[0]
    assert d == D_MODEL and seq % WA_BLOCK == 0 and seq % GRID_W == 0 and batch < MOD_ROWS
    tm = 1024
    tm_in = 2048
    tm_ctx = 512
    assert seq % tm == 0 and seq % tm_in == 0 and (batch * lc) % tm_ctx == 0

    cond = jnp.zeros((MOD_ROWS, d), F32).at[:batch].set(c).at[batch].set(c_ctx)
    mod = _modulation(cond, w_mod, b_mod)
    rpb_bias, w_in_bf = _rpb_tables(na_rpb, w_in)
    rope_tabs = _rope_tables(seq)

    lat_row = lambda i: i // (seq // tm)
    lat_row_in = lambda i: i // (seq // tm_in)
    ctx_row = lambda i: batch

    xs = x.reshape(batch * seq, d)
    cs = ctx.reshape(batch * lc, d)
    for l in range(depth):
        last = l == depth - 1
        g1 = g_norm1[l].reshape(1, d)
        g2 = g_norm2[l].reshape(1, d)
        gains = [jnp.tile(na_q_gain[l], NA_HEADS).reshape(1, -1), jnp.tile(na_k_gain[l], NA_HEADS).reshape(1, -1),
                 jnp.tile(wa_q_gain[l], WA_HEADS).reshape(1, -1), jnp.tile(wa_k_gain[l], WA_KV_HEADS).reshape(1, -1)]
        go_na = g_out[l, :NA_WIDTH].reshape(1, -1)
        go_cv = g_out[l, NA_WIDTH:NA_WIDTH + CONV_CH].reshape(1, -1)
        go_wa = g_out[l, NA_WIDTH + CONV_CH:].reshape(1, -1)
        cb = conv_bias[l].reshape(1, -1)

        naq, nak, nav, cvu, cvb, waq, wak, wav = _in_proj(xs, mod, l, lat_row_in, g1, w_in_bf, gains, rope_tabs, tm_in)
        if last:
            cnak, cnav, cwak, cwav = _in_proj(cs, mod, l, ctx_row, g1, w_in_bf, gains, None, tm_ctx, kv_only=True)
        else:
            cnaq, cnak, cnav, ccvu, ccvb, cwaq, cwak, cwav = _in_proj(cs, mod, l, ctx_row, g1, w_in_bf, gains, None, tm_ctx)

        b3 = lambda a, n: a.reshape(batch, n, a.shape[-1])
        casts = [(w_o, l), (w_fc1, l), (w_fc2, l)] + ([] if last else [(w_in, l + 1)])
        na_n, wa_n, cast_out = _attention(
            wa_sink[l], b3(naq, seq), b3(nak, seq), b3(nav, seq), b3(cnak, lc), b3(cnav, lc), rpb_bias, l, go_na,
            b3(waq, seq), b3(wak, seq), b3(wav, seq), b3(cwak, lc), b3(cwav, lc), go_wa, casts)
        w_o_bf, w1_bf, w2_bf = cast_out[:3]
        if not last:
            w_in_bf = cast_out[3]
        xs = _out_mlp(xs, mod, l, lat_row, (na_n.reshape(batch * seq, -1), wa_n.reshape(batch * seq, -1)),
                      cvu, cvb, conv_w[l], cb, go_cv, w_o_bf, g2, w1_bf, w2_bf, seq, tm)
        if not last:
            ctx_attn = (wa_sink[l], cnaq, cnak, cnav, cwaq, cwak, cwav, go_na, go_wa)
            cs = _out_mlp(cs, mod, l, ctx_row, ctx_attn, ccvu, ccvb, conv_w[l], cb, go_cv,
                          w_o_bf, g2, w1_bf, w2_bf, lc, lc)
    return xs.reshape(batch, seq, d)
```

```python
import functools

import jax
import jax.numpy as jnp
from jax import lax
from jax.experimental import pallas as pl
from jax.experimental.pallas import tpu as pltpu

D_MODEL = 1024
GRID_W = 64
HEAD_DIM = 64
NA_HEADS = 4
NA_WIDTH = NA_HEADS * HEAD_DIM
CONV_CH = 256
WA_HEADS = 8
WA_KV_HEADS = 2
WA_GROUP = WA_HEADS // WA_KV_HEADS
WA_QW = WA_HEADS * HEAD_DIM
WA_KVW = WA_KV_HEADS * HEAD_DIM
WA_K2W = 2 * WA_KVW
WA_V3W = 2 * WA_KVW
NA_WIN_ROWS = 8
NA_WIN_COLS = 16
NA_DROWS = 2 * NA_WIN_ROWS - 1
NA_DCOLS = 2 * NA_WIN_COLS - 1
WA_WINDOW = 128
WA_BLOCK = 128
WA_BAND = 3 * WA_BLOCK
FFN_DIM = 4 * D_MODEL
FFN_CHUNK = 1024
ROPE_BASE = 10000.0
EPS = 1e-6
NEG_INF = -1e30
IN_WIDTH = 2304
OFF_NA_Q, OFF_NA_K, OFF_NA_V = 0, 256, 512
OFF_CV_X, OFF_CV_B, OFF_CV_C = 768, 1024, 1280
OFF_WA_Q, OFF_WA_K, OFF_WA_V = 1536, 2048, 2176
MOD_ROWS = 16
MOD_TN = 2048
IN_SUB = 256
MLP_SUB = 1024
LANES = 128
MXU_DIM = 256
VMEM_LIMIT = 60 * 1024 * 1024
LOG2E = 1.4426950408889634
Q_SCALE = LOG2E * HEAD_DIM ** -0.5

F32 = jnp.float32
BF16 = jnp.bfloat16


def _dot(a, b):
    return jnp.dot(a, b, preferred_element_type=F32)


def _dot_nt(a, b):
    return lax.dot_general(a, b, (((1,), (1,)), ((), ())), preferred_element_type=F32)


def _split_bf16(a):
    hi = a.astype(BF16)
    lo = (a - hi.astype(F32)).astype(BF16)
    return hi, lo


def _iota(shape, dim):
    return lax.broadcasted_iota(jnp.int32, shape, dim)


def _params(**kw):
    return pltpu.CompilerParams(vmem_limit_bytes=VMEM_LIMIT, **kw)


def _mod_kernel(cond_ref, w_ref, b_ref, o_ref):
    a = cond_ref[...]
    a = a * (1.0 / (1.0 + jnp.exp(-a)))
    ah, al = _split_bf16(a)
    wh, wl = _split_bf16(w_ref[...])
    rows = a.shape[0]
    both = _dot(jnp.concatenate([ah, al], axis=0), wh)
    res = both[:rows] + both[rows:] + _dot(ah, wl) + b_ref[...]
    for c in range(o_ref.shape[1]):
        o_ref[:, c, 0, :] = res[:, c * D_MODEL:(c + 1) * D_MODEL]


def _modulation(cond, w_mod, b_mod):
    depth = w_mod.shape[0]
    n_out = w_mod.shape[2]
    per_tile = MOD_TN // D_MODEL
    return pl.pallas_call(
        _mod_kernel,
        out_shape=jax.ShapeDtypeStruct((depth, MOD_ROWS, n_out // D_MODEL, 1, D_MODEL), F32),
        grid=(depth, n_out // MOD_TN),
        in_specs=[
            pl.BlockSpec((MOD_ROWS, D_MODEL), lambda l, j: (0, 0)),
            pl.BlockSpec((None, D_MODEL, MOD_TN), lambda l, j: (l, 0, j)),
            pl.BlockSpec((None, 1, MOD_TN), lambda l, j: (l, 0, j)),
        ],
        out_specs=pl.BlockSpec((None, MOD_ROWS, per_tile, 1, D_MODEL), lambda l, j: (l, 0, j, 0, 0)),
        compiler_params=_params(),
        name="modulation",
    )(cond, w_mod, b_mod.reshape(depth, 1, n_out))


def _rpb_kernel(rpb_ref, w_ref, o_ref, wbf_ref):
    wbf_ref[...] = w_ref[...].astype(BF16)
    shape = (GRID_W, LANES)
    q = _iota(shape, 0)
    lane = _iota(shape, 1)
    k = lane % GRID_W
    left = lane < GRID_W
    col_start = jnp.clip(q - NA_WIN_COLS // 2, 0, GRID_W - NA_WIN_COLS)
    col_ok = (k >= col_start) & (k < col_start + NA_WIN_COLS)
    centre = NA_WIN_COLS - 1
    for d in range(NA_DROWS - 1):
        tiles = []
        for h in range(NA_HEADS):
            t1 = pltpu.roll(jnp.broadcast_to(rpb_ref[h, d:d + 1, :], shape), LANES - centre, axis=1,
                            stride=1, stride_axis=0)
            t2 = pltpu.roll(jnp.broadcast_to(rpb_ref[h, d + 1:d + 2, :], shape), GRID_W - centre, axis=1,
                            stride=1, stride_axis=0)
            t = jnp.where(left, t1, t2)
            tiles.append(jnp.where(col_ok, t * LOG2E, NEG_INF))
        o_ref[d] = jnp.concatenate(tiles, axis=0).T


def _rpb_tables(na_rpb, w_in):
    depth = na_rpb.shape[0]
    padded = jnp.pad(na_rpb, ((0, 0), (0, 0), (0, 1), (0, LANES - NA_DCOLS)))
    slab = w_in.shape[1] // depth
    return pl.pallas_call(
        _rpb_kernel,
        out_shape=[jax.ShapeDtypeStruct((depth, NA_DROWS - 1, LANES, NA_HEADS * GRID_W), F32),
                   jax.ShapeDtypeStruct(w_in.shape[1:], BF16)],
        grid=(depth,),
        in_specs=[pl.BlockSpec((None,) + padded.shape[1:], lambda l: (l, 0, 0, 0)),
                  pl.BlockSpec((None, slab, w_in.shape[2]), lambda l: (0, l, 0))],
        out_specs=[pl.BlockSpec((None, NA_DROWS - 1, LANES, NA_HEADS * GRID_W), lambda l: (l, 0, 0, 0)),
                   pl.BlockSpec((slab, w_in.shape[2]), lambda l: (l, 0))],
        compiler_params=_params(),
        name="rpb_tables",
    )(padded, w_in)


def _head_rms(t, gain):
    width = t.shape[1]
    sq = (t * t).astype(BF16)
    blk = min(width, MXU_DIM)
    ones = jnp.where(_iota((blk, blk), 0) // HEAD_DIM == _iota((blk, blk), 1) // HEAD_DIM, 1.0, 0.0).astype(BF16)
    sums = [_dot(sq[:, c:c + blk], ones) for c in range(0, width, blk)]
    ss = sums[0] if len(sums) == 1 else jnp.concatenate(sums, axis=1)
    return t * lax.rsqrt(ss * (1.0 / HEAD_DIM) + EPS) * gain


def _rope(t, cos, sin):
    lane = _iota((t.shape[0], LANES), 1)
    first = (lane % (HEAD_DIM // 2)) < (HEAD_DIM // 4)
    out = []
    for c in range(0, t.shape[1], LANES):
        u = t[:, c:c + LANES]
        partner = jnp.where(first, pltpu.roll(u, LANES - HEAD_DIM // 4, axis=1), pltpu.roll(u, HEAD_DIM // 4, axis=1))
        out.append(u * cos + partner * sin)
    return out[0] if len(out) == 1 else jnp.concatenate(out, axis=1)


def _dup_kv(t):
    lane = _iota(t.shape, 1)
    swapped = pltpu.roll(t, HEAD_DIM, axis=1)
    left = lane < HEAD_DIM
    return jnp.concatenate([jnp.where(left, t, swapped), jnp.where(left, swapped, t)], axis=1)


def _inproj_kernel(*refs, rope, kv_only, n_w, sub):
    x_ref, mod_ref, g1_ref = refs[:3]
    w_refs = refs[3:3 + n_w]
    gnq_ref, gnk_ref, gwq_ref, gwk_ref = refs[3 + n_w:7 + n_w]
    rest = refs[7 + n_w:]
    if rope:
        cos_ref, sin_ref = rest[:2]
        rest = rest[2:]
    out_refs = rest
    if kv_only:
        nak_ref, nav_ref, wak_ref, wav_ref = out_refs
        o_nk, o_nv, o_wk, o_wv, o_end = 0, NA_WIDTH, 2 * NA_WIDTH, 2 * NA_WIDTH + WA_KVW, 2 * NA_WIDTH + 2 * WA_KVW
    else:
        naq_ref, nak_ref, nav_ref, cvu_ref, cvb_ref, waq_ref, wak_ref, wav_ref = out_refs
        o_nk, o_nv, o_wk, o_wv, o_end = OFF_NA_K, OFF_NA_V, OFF_WA_K, OFF_WA_V, IN_WIDTH
    def project(rows):
        x = x_ref[rows, :]
        ms = jnp.mean(x * x, axis=-1, keepdims=True)
        h = x * lax.rsqrt(ms + EPS) * g1_ref[...]
        h = h * (1.0 + mod_ref[1]) + mod_ref[0]
        hb = h.astype(BF16)
        return jnp.concatenate([_dot(hb, w_ref[...]) for w_ref in w_refs], axis=1)

    def finish(rows, p):
        nak_ref[rows, :] = _head_rms(p[:, o_nk:o_nk + NA_WIDTH], gnk_ref[...]).astype(BF16)
        nav_ref[rows, :] = p[:, o_nv:o_nv + NA_WIDTH].astype(BF16)
        wk = _head_rms(p[:, o_wk:o_wv], gwk_ref[...])
        if rope:
            wk = _rope(wk, cos_ref[rows, :], sin_ref[rows, :])
        wak_ref[rows, :] = _dup_kv(wk).astype(BF16)
        vv = p[:, o_wv:o_end]
        left = _iota(vv.shape, 1) < HEAD_DIM
        wav_ref[rows, :] = jnp.concatenate([jnp.where(left, vv, 1.0), jnp.where(left, pltpu.roll(vv, HEAD_DIM, axis=1), 1.0)],
                                           axis=1).astype(BF16)
        if kv_only:
            return
        naq_ref[rows, :] = (_head_rms(p[:, OFF_NA_Q:OFF_NA_K], gnq_ref[...]) * Q_SCALE).astype(BF16)
        cvu_ref[rows, :] = p[:, OFF_CV_C:OFF_WA_Q] * p[:, OFF_CV_X:OFF_CV_B]
        cvb_ref[rows, :] = p[:, OFF_CV_B:OFF_CV_C]
        wq = _head_rms(p[:, OFF_WA_Q:OFF_WA_K], gwq_ref[...])
        if rope:
            wq = _rope(wq, cos_ref[rows, :], sin_ref[rows, :])
        waq_ref[rows, :] = (wq * Q_SCALE).astype(BF16)

    for r0 in range(0, x_ref.shape[0], sub):
        rows = slice(r0, r0 + sub)
        finish(rows, project(rows))


def _in_proj(x2d, mod, layer, mod_row, g1, w_in, gains, rope_tabs, tm, kv_only=False):
    rows = x2d.shape[0]
    rope = rope_tabs is not None
    row_spec = lambda w: pl.BlockSpec((tm, w), lambda i: (i, 0))
    const = lambda shape: pl.BlockSpec(shape, lambda i: (0,) * len(shape))
    once = pl.Buffered(1)
    if kv_only:
        col_blocks = [OFF_NA_K // MXU_DIM, OFF_NA_V // MXU_DIM, OFF_WA_K // MXU_DIM]
        w_specs = [pl.BlockSpec((D_MODEL, MXU_DIM), functools.partial(lambda j, i: (0, j), j),
                                pipeline_mode=once) for j in col_blocks]
        widths = [(NA_WIDTH, BF16), (NA_WIDTH, BF16), (WA_K2W, BF16), (WA_V3W, BF16)]
    else:
        w_specs = [pl.BlockSpec((D_MODEL, IN_WIDTH), lambda i: (0, 0), pipeline_mode=once)]
        widths = [(NA_WIDTH, BF16), (NA_WIDTH, BF16), (NA_WIDTH, BF16), (CONV_CH, F32), (CONV_CH, F32),
                  (WA_QW, BF16), (WA_K2W, BF16), (WA_V3W, BF16)]
    n_w = len(w_specs)
    in_specs = [
        row_spec(D_MODEL),
        pl.BlockSpec((None, None, 6, 1, D_MODEL), lambda i: (layer, mod_row(i), 0, 0, 0)),
        const((1, D_MODEL)),
        *w_specs,
        const((1, NA_WIDTH)), const((1, NA_WIDTH)), const((1, WA_QW)), const((1, WA_KVW)),
    ]
    args = [x2d, mod, g1, *([w_in] * n_w), *gains]
    if rope:
        seq_tiles = rope_tabs[0].shape[0] // tm
        in_specs += [pl.BlockSpec((tm, LANES), lambda i: (i % seq_tiles, 0))] * 2
        args += list(rope_tabs)
    return pl.pallas_call(
        functools.partial(_inproj_kernel, rope=rope, kv_only=kv_only, n_w=n_w, sub=min(tm, IN_SUB)),
        out_shape=[jax.ShapeDtypeStruct((rows, w), dt) for w, dt in widths],
        grid=(rows // tm,),
        in_specs=in_specs,
        out_specs=[row_spec(w) for w, _ in widths],
        compiler_params=_params(),
        name="in_proj_rope" if rope else ("in_proj_ctx_kv" if kv_only else "in_proj_ctx"),
    )(*args)


def _lane_chunks(s):
    return [s[:, c:c + LANES] for c in range(0, s.shape[1], LANES)]


def _row_max(scores, extra=None):
    chunks = [ch for s in scores for ch in _lane_chunks(s)]
    if extra is not None:
        chunks.append(extra)
    folded = functools.reduce(jnp.maximum, chunks)
    return jnp.broadcast_to(jnp.max(folded, axis=-1, keepdims=True), folded.shape)


def _softmax_pv(parts, extra=None, sums_from_values=False, m=None):
    lane_chunks = _lane_chunks
    if m is None:
        m = _row_max([s for s, _ in parts], extra)
    ps = [jnp.concatenate([jnp.exp2(ch - m) for ch in lane_chunks(s)], axis=1) for s, _ in parts]
    pv = _dot(jnp.concatenate([p.astype(BF16) for p in ps], axis=1),
              jnp.concatenate([v for _, v in parts], axis=0))
    if sums_from_values:
        denom = pv[:, -LANES:]
        pv = pv[:, :-LANES]
        if extra is not None:
            denom = denom + jnp.exp2(extra - m)
        return pv / denom
    psum = functools.reduce(jnp.add, [ch for p in ps for ch in lane_chunks(p)])
    if extra is not None:
        psum = psum + jnp.where(_iota(extra.shape, 1) == 0, jnp.exp2(extra - m), 0.0)
    return pv / jnp.sum(psum, axis=-1, keepdims=True)


def _stack_heads(q, n_heads):
    head = _iota(q.shape, 1) // HEAD_DIM
    zero = jnp.zeros_like(q)
    return jnp.concatenate([jnp.where(head == h, q, zero) for h in range(n_heads)], axis=0)


def _unstack_heads(o, n_heads):
    rows = o.shape[0] // n_heads
    head = _iota((rows, o.shape[1]), 1) // HEAD_DIM
    out = jnp.where(head == 0, o[:rows], 0.0)
    for h in range(1, n_heads):
        out = jnp.where(head == h, o[h * rows:(h + 1) * rows], out)
    return out


def _group_rms(o, gain):
    ms = jnp.mean(o * o, axis=-1, keepdims=True)
    return o * lax.rsqrt(ms + EPS) * gain


SUBLANES = 8


def _wa_scores(qg, keys):
    rows = qg.shape[0]
    left = _iota((rows, LANES), 1) < HEAD_DIM
    zero = jnp.zeros((rows, LANES), qg.dtype)
    blocks = []
    for pair in range(WA_GROUP // 2):
        qp = qg[:, pair * LANES:(pair + 1) * LANES]
        blocks += [jnp.where(left, qp, zero), jnp.where(left, zero, qp)]
    qs = jnp.concatenate(blocks, axis=0)
    scores = []
    for k2, bias in keys:
        s = _dot_nt(k2, qs)
        if bias is not None:
            s = s + jnp.concatenate([bias] * WA_GROUP, axis=1)
        scores.append(s)
    return scores


def _wa_sink_rows(rows, sinks):
    return jnp.concatenate([jnp.full((SUBLANES, rows), sinks[g] * LOG2E, F32) for g in range(WA_GROUP)], axis=1)


def _wa_col_max(scores, sink_rows):
    m = sink_rows[:1]
    for s in scores:
        m = jnp.maximum(m, jnp.max(s, axis=0, keepdims=True))
    return jnp.broadcast_to(m, sink_rows.shape)


def _wa_finish(scores, values, sinks, m=None):
    rows = scores[0].shape[1] // WA_GROUP
    sink_rows = _wa_sink_rows(rows, sinks)
    if m is None:
        m = _wa_col_max(scores, sink_rows)
    p = jnp.concatenate([jnp.exp2(s - m[:1]).astype(BF16) for s in scores], axis=0)
    v = jnp.concatenate(values, axis=0)
    ot = lax.dot_general(v, p, (((0,), (0,)), ((), ())), preferred_element_type=F32)
    denom = ot[HEAD_DIM:HEAD_DIM + SUBLANES] + jnp.exp2(sink_rows - m)
    on = ot[:HEAD_DIM] / denom[:1]
    pairs = [jnp.concatenate([on[:, (2 * pr) * rows:(2 * pr + 1) * rows],
                              on[:, (2 * pr + 1) * rows:(2 * pr + 2) * rows]], axis=0).T
             for pr in range(WA_GROUP // 2)]
    return jnp.concatenate(pairs, axis=1)


def _na_steps(q_ref, k_ref, v_ref, kc_ref, vc_ref, bias_ref, g_ref, o_ref, vt_ref, vct_ref):
    seq = q_ref.shape[0]
    n_rows = seq // GRID_W
    win = NA_WIN_ROWS * GRID_W
    lc = kc_ref.shape[0]
    gain = g_ref[...]

    def prepare():
        vt_ref[0] = v_ref[...].astype(F32).T.astype(BF16)
        vt_ref[1, :, :seq - GRID_W] = v_ref[GRID_W:, :].astype(F32).T.astype(BF16)
        vct_ref[...] = vc_ref[...].astype(F32).T.astype(BF16)

    def window(r):
        start = jnp.clip(r - NA_WIN_ROWS // 2, 0, n_rows - NA_WIN_ROWS)
        return start, start - r + (NA_WIN_ROWS - 1)

    def scores(r, s_ref):
        start, d0 = window(r)
        tok0 = pl.multiple_of(start * GRID_W, GRID_W)
        q = q_ref[pl.ds(pl.multiple_of(r * GRID_W, GRID_W), GRID_W), :]
        qs = _stack_heads(q, NA_HEADS)
        bias = jnp.concatenate([bias_ref[d0 + 2 * j] for j in range(NA_WIN_ROWS // 2)], axis=0)
        s_ctx = _dot_nt(kc_ref[...], qs)
        s_loc = _dot_nt(k_ref[pl.ds(tok0, win), :], qs) + bias
        m = jnp.maximum(jnp.max(s_ctx, axis=0, keepdims=True), jnp.max(s_loc, axis=0, keepdims=True))
        s_ref[:lc, :] = s_ctx
        s_ref[lc:lc + win, :] = s_loc
        s_ref[lc + win:, :] = jnp.broadcast_to(m, (SUBLANES, m.shape[1]))

    def finish(r, s_ref):
        start, _ = window(r)
        odd = start % 2
        lane0 = pl.multiple_of((start - odd) * GRID_W, LANES)
        m = s_ref[lc + win:lc + win + 1, :]
        p_ctx = jnp.exp2(s_ref[:lc, :] - m)
        p_loc = jnp.exp2(s_ref[lc:lc + win, :] - m)
        denom = jnp.sum(p_ctx, axis=0, keepdims=True) + jnp.sum(p_loc, axis=0, keepdims=True)
        ot = _dot(vct_ref[...], p_ctx.astype(BF16)) + _dot(vt_ref[odd, :, pl.ds(lane0, win)], p_loc.astype(BF16))
        o4 = (ot * (1.0 / denom)).T
        o = _unstack_heads(o4, NA_HEADS)
        o_ref[pl.ds(pl.multiple_of(r * GRID_W, GRID_W), GRID_W), :] = _group_rms(o, gain).astype(o_ref.dtype)

    def step(r, cur_ref, nxt_ref, lookahead=True):
        if lookahead:
            scores(r + 1, nxt_ref)
        finish(r, cur_ref)

    return prepare, scores, step


def _wa_steps(sink_ref, q_ref, k_ref, v_ref, kc_ref, vc_ref, g_ref, o_ref):
    seq = q_ref.shape[0]
    n_blocks = seq // WA_BLOCK
    lc = kc_ref.shape[0]
    gain = g_ref[...]
    key = _iota((WA_BAND, WA_BLOCK), 0)
    qry = _iota((WA_BAND, WA_BLOCK), 1)

    def window(n):
        q0 = pl.multiple_of(n * WA_BLOCK, WA_BLOCK)
        k0 = pl.multiple_of(jnp.clip(q0 - WA_BLOCK, 0, seq - WA_BAND), WA_BLOCK)
        return q0, k0

    def scores(n, kh, s_ref):
        q0, k0 = window(n)
        band = jnp.where(jnp.abs((k0 + key) - (q0 + qry)) <= WA_WINDOW, 0.0, NEG_INF)
        lanes = slice(kh * LANES, (kh + 1) * LANES)
        qg = q_ref[pl.ds(q0, WA_BLOCK), kh * WA_GROUP * HEAD_DIM:(kh + 1) * WA_GROUP * HEAD_DIM]
        s_ctx, s_loc = _wa_scores(qg, [(kc_ref[:, lanes], None), (k_ref[pl.ds(k0, WA_BAND), lanes], band)])
        s_ref[kh, :lc, :] = s_ctx
        s_ref[kh, lc:lc + WA_BAND, :] = s_loc
        s_ref[kh, lc + WA_BAND:, :] = _wa_col_max([s_ctx, s_loc], _wa_sink_rows(WA_BLOCK, sinks(kh)))

    def sinks(kh):
        return [sink_ref[kh * WA_GROUP + g] for g in range(WA_GROUP)]

    def finish(n, kh, s_ref):
        _, k0 = window(n)
        lanes = slice(kh * LANES, (kh + 1) * LANES)
        return _wa_finish([s_ref[kh, :lc, :], s_ref[kh, lc:lc + WA_BAND, :]],
                          [vc_ref[:, lanes], v_ref[pl.ds(k0, WA_BAND), lanes]], sinks(kh),
                          m=s_ref[kh, lc + WA_BAND:, :])

    def step(n, cur_ref, nxt_ref, lookahead=True):
        if lookahead:
            for kh in range(WA_KV_HEADS):
                scores(n + 1, kh, nxt_ref)
        o = jnp.concatenate([finish(n, kh, cur_ref) for kh in range(WA_KV_HEADS)], axis=1)
        q0, _ = window(n)
        o_ref[pl.ds(q0, WA_BLOCK), :] = _group_rms(o, gain).astype(o_ref.dtype)

    return scores, step


def _attn_kernel(sink_ref, nq_ref, nk_ref, nv_ref, nkc_ref, nvc_ref, bias_ref, gn_ref,
                 wq_ref, wk_ref, wv_ref, wkc_ref, wvc_ref, gw_ref, *rest, n_cast):
    f32_refs = rest[:n_cast]
    on_ref, ow_ref = rest[n_cast:n_cast + 2]
    bf16_refs = rest[n_cast + 2:2 * n_cast + 2]
    nsa_ref, nsb_ref, wsa_ref, wsb_ref, vt_ref, vct_ref = rest[2 * n_cast + 2:]
    for src_ref, dst_ref in zip(f32_refs, bf16_refs):
        dst_ref[...] = src_ref[...].astype(dst_ref.dtype)
    na_prepare, na_scores, na_step = _na_steps(nq_ref, nk_ref, nv_ref, nkc_ref, nvc_ref, bias_ref, gn_ref, on_ref,
                                               vt_ref, vct_ref)
    na_prepare()
    wa_scores, wa_step = _wa_steps(sink_ref, wq_ref, wk_ref, wv_ref, wkc_ref, wvc_ref, gw_ref, ow_ref)
    n_blocks = wq_ref.shape[0] // WA_BLOCK

    for kh in range(WA_KV_HEADS):
        wa_scores(0, kh, wsa_ref)
    na_scores(0, nsa_ref)

    def pair(i, last=False):
        for half, (w_cur, w_nxt) in enumerate(((wsa_ref, wsb_ref), (wsb_ref, wsa_ref))):
            n = 2 * i + half
            final = last and half == 1
            wa_step(n, w_cur, w_nxt, lookahead=not final)
            na_step(2 * n, nsa_ref, nsb_ref)
            na_step(2 * n + 1, nsb_ref, nsa_ref, lookahead=not final)

    def body(i, carry):
        pair(i)
        return carry

    n_pairs = n_blocks // 2
    lax.fori_loop(0, n_pairs - 2, body, 0, unroll=3)
    pair(n_pairs - 2)
    pair(n_pairs - 1, last=True)


def _attention(sink, nq, nk, nv, nkc, nvc, bias, layer, gain_na, wq, wk2, wv3, wkc2, wvc3, gain_wa, to_bf16):
    batch, seq, _ = nq.shape
    slab_in = lambda w, j: pl.BlockSpec((None, w.shape[1] // batch, w.shape[2]), lambda b: (j, b, 0))
    slab_out = lambda w: pl.BlockSpec((w.shape[1] // batch, w.shape[2]), lambda b: (b, 0))
    lc = nkc.shape[1]
    assert WA_BLOCK == 2 * GRID_W and (seq // WA_BLOCK) % 2 == 0
    per_b = lambda n, w: pl.BlockSpec((None, n, w), lambda b: (b, 0, 0))
    na_scratch = pltpu.VMEM((lc + NA_WIN_ROWS * GRID_W + SUBLANES, NA_HEADS * GRID_W), F32)
    vt_scratch = pltpu.VMEM((2, NA_WIDTH, seq), BF16)
    vct_scratch = pltpu.VMEM((NA_WIDTH, lc), BF16)
    wa_scratch = pltpu.VMEM((WA_KV_HEADS, lc + WA_BAND + SUBLANES, WA_GROUP * WA_BLOCK), F32)
    outs = pl.pallas_call(
        functools.partial(_attn_kernel, n_cast=len(to_bf16)),
        out_shape=[jax.ShapeDtypeStruct((batch, seq, NA_WIDTH), BF16),
                   jax.ShapeDtypeStruct((batch, seq, WA_QW), BF16),
                   *[jax.ShapeDtypeStruct(w.shape[1:], BF16) for w, _ in to_bf16]],
        grid=(batch,),
        in_specs=[pl.BlockSpec(memory_space=pltpu.SMEM),
                  per_b(seq, NA_WIDTH), per_b(seq, NA_WIDTH), per_b(seq, NA_WIDTH),
                  per_b(lc, NA_WIDTH), per_b(lc, NA_WIDTH),
                  pl.BlockSpec((None,) + bias.shape[1:], lambda b: (layer, 0, 0, 0)),
                  pl.BlockSpec((1, NA_WIDTH), lambda b: (0, 0)),
                  per_b(seq, WA_QW), per_b(seq, WA_K2W), per_b(seq, WA_V3W),
                  per_b(lc, WA_K2W), per_b(lc, WA_V3W),
                  pl.BlockSpec((1, WA_QW), lambda b: (0, 0)),
                  *[slab_in(w, j) for w, j in to_bf16]],
        out_specs=[per_b(seq, NA_WIDTH), per_b(seq, WA_QW), *[slab_out(w) for w, _ in to_bf16]],
        scratch_shapes=[na_scratch, na_scratch, wa_scratch, wa_scratch, vt_scratch, vct_scratch],
        compiler_params=_params(),
        name="attention",
    )(sink, nq, nk, nv, nkc, nvc, bias, gain_na, wq, wk2, wv3, wkc2, wvc3, gain_wa, *[w for w, _ in to_bf16])
    return outs[0], outs[1], outs[2:]


def _ctx_attn(sink_ref, qn_ref, kn_ref, vn_ref, qw_ref, kw_ref, vw_ref, gn_ref, gw_ref):
    qs = _stack_heads(qn_ref[...], NA_HEADS)
    o4 = _softmax_pv([(_dot_nt(qs, kn_ref[...]), vn_ref[...])])
    na_n = _group_rms(_unstack_heads(o4, NA_HEADS), gn_ref[...]).astype(BF16)
    outs = []
    for kh in range(WA_KV_HEADS):
        lanes = slice(kh * LANES, (kh + 1) * LANES)
        qg = qw_ref[:, kh * WA_GROUP * HEAD_DIM:(kh + 1) * WA_GROUP * HEAD_DIM]
        sinks = [sink_ref[kh * WA_GROUP + g] for g in range(WA_GROUP)]
        outs.append(_wa_finish(_wa_scores(qg, [(kw_ref[:, lanes], None)]), [vw_ref[:, lanes]], sinks))
    return na_n, _group_rms(jnp.concatenate(outs, axis=1), gw_ref[...]).astype(BF16)


def _out_mlp_kernel(*refs, seq_len, sub, ctx_attn):
    x_ref, mod_ref = refs[:2]
    n_attn = 9 if ctx_attn else 2
    attn_refs = refs[2:2 + n_attn]
    u_ref, uprev_ref, unext_ref, cvb_ref, cw_ref, cb_ref, g_ref, g2_ref, wo_ref = refs[2 + n_attn:11 + n_attn]
    rest = refs[11 + n_attn:]
    n_ffn = FFN_DIM // FFN_CHUNK
    w1_refs, w2_refs, (o_ref,) = rest[:n_ffn], rest[n_ffn:2 * n_ffn], rest[2 * n_ffn:]
    tm = x_ref.shape[0]
    i = pl.program_id(0)
    if ctx_attn:
        na_t, wa_t = _ctx_attn(*attn_refs)
    else:
        na_t, wa_t = attn_refs
    u = u_ref[...]
    at_start = (i * tm) % seq_len == 0
    at_end = ((i + 1) * tm) % seq_len == 0
    prev_row = jnp.where(at_start, 0.0, uprev_ref[7:8, :])
    next_row = jnp.where(at_end, 0.0, unext_ref[0:1, :])
    row = _iota(u.shape, 0)
    up = jnp.where(row == 0, prev_row, pltpu.roll(u, 1, axis=0))
    dn = jnp.where(row == tm - 1, next_row, pltpu.roll(u, tm - 1, axis=0))
    y = cb_ref[...] + cw_ref[0:1, :] * up + cw_ref[1:2, :] * u + cw_ref[2:3, :] * dn
    cv = _group_rms(cvb_ref[...] * y, g_ref[...]).astype(BF16)
    for r0 in range(0, tm, sub):
        rows = slice(r0, r0 + sub)
        mixed = jnp.concatenate([na_t[rows, :], cv[rows, :], wa_t[rows, :]], axis=1)
        x = x_ref[rows, :] + mod_ref[2] * _dot(mixed, wo_ref[...])
        ms = jnp.mean(x * x, axis=-1, keepdims=True)
        h = x * lax.rsqrt(ms + EPS) * g2_ref[...]
        h = (h * (1.0 + mod_ref[4]) + mod_ref[3]).astype(BF16)
        acc = None
        for w1_ref, w2_ref in zip(w1_refs, w2_refs):
            a = jnp.maximum(_dot(h, w1_ref[...]), 0.0)
            part = _dot((a * a).astype(BF16), w2_ref[...])
            acc = part if acc is None else acc + part
        o_ref[rows, :] = x + mod_ref[5] * acc


def _out_mlp(x2d, mod, layer, mod_row, attn, cv_u, cv_b, conv_w, conv_b, gain_cv, w_o, g2, w1, w2, seq_len, tm):
    rows = x2d.shape[0]
    ctx_attn = len(attn) == 9
    assert not ctx_attn or tm == seq_len
    halo = 8
    per_tile = tm // halo
    last = rows // halo - 1
    row_spec = lambda w: pl.BlockSpec((tm, w), lambda i: (i, 0))
    const = lambda shape: pl.BlockSpec(shape, lambda i: (0,) * len(shape), pipeline_mode=pl.Buffered(1))
    once = pl.Buffered(1)
    n_ffn = FFN_DIM // FFN_CHUNK
    wo_spec = pl.BlockSpec((D_MODEL, D_MODEL), lambda i: (0, 0), pipeline_mode=once)
    w1_specs = [pl.BlockSpec((D_MODEL, FFN_CHUNK), functools.partial(lambda j, i: (0, j), j),
                             pipeline_mode=once) for j in range(n_ffn)]
    w2_specs = [pl.BlockSpec((FFN_CHUNK, D_MODEL), functools.partial(lambda j, i: (j, 0), j),
                             pipeline_mode=once) for j in range(n_ffn)]
    small = lambda a: pl.BlockSpec(a.shape, lambda i: (0,) * a.ndim)
    if ctx_attn:
        attn_specs = [pl.BlockSpec(memory_space=pltpu.SMEM), *[row_spec(a.shape[1]) for a in attn[1:7]],
                      small(attn[7]), small(attn[8])]
    else:
        attn_specs = [row_spec(NA_WIDTH), row_spec(WA_QW)]
    return pl.pallas_call(
        functools.partial(_out_mlp_kernel, seq_len=seq_len, sub=min(tm, MLP_SUB), ctx_attn=ctx_attn),
        out_shape=jax.ShapeDtypeStruct((rows, D_MODEL), F32),
        grid=(rows // tm,),
        in_specs=[
            row_spec(D_MODEL),
            pl.BlockSpec((None, None, 6, 1, D_MODEL), lambda i: (layer, mod_row(i), 0, 0, 0)),
            *attn_specs, row_spec(CONV_CH),
            pl.BlockSpec((halo, CONV_CH), lambda i: (jnp.maximum(i * per_tile - 1, 0), 0)),
            pl.BlockSpec((halo, CONV_CH), lambda i: (jnp.minimum((i + 1) * per_tile, last), 0)),
            row_spec(CONV_CH),
            const((3, CONV_CH)), const((1, CONV_CH)), const((1, CONV_CH)),
            const((1, D_MODEL)),
            wo_spec, *w1_specs, *w2_specs,
        ],
        out_specs=row_spec(D_MODEL),
        compiler_params=_params(),
        name="out_mlp",
    )(x2d, mod, *attn, cv_u, cv_u, cv_u, cv_b, conv_w, conv_b, gain_cv, g2,
      w_o, *([w1] * n_ffn), *([w2] * n_ffn))


def _rope_tables(seq):
    quarter = HEAD_DIM // 4
    inv = ROPE_BASE ** (-jnp.arange(quarter, dtype=F32) / quarter)
    t = jnp.arange(seq)
    ang_r = (t // GRID_W).astype(F32)[:, None] * inv[None, :]
    ang_c = (t % GRID_W).astype(F32)[:, None] * inv[None, :]
    cos = jnp.concatenate([jnp.cos(ang_r), jnp.cos(ang_r), jnp.cos(ang_c), jnp.cos(ang_c)], axis=1)
    sin = jnp.concatenate([-jnp.sin(ang_r), jnp.sin(ang_r), -jnp.sin(ang_c), jnp.sin(ang_c)], axis=1)
    reps = LANES // HEAD_DIM
    return jnp.tile(cos, (1, reps)), jnp.tile(sin, (1, reps))


def kernel(x, c, ctx, c_ctx, w_mod, b_mod, g_norm1, g_norm2, w_in, na_q_gain, na_k_gain, na_rpb,
           conv_w, conv_bias, wa_q_gain, wa_k_gain, wa_sink, g_out, w_o, w_fc1, w_fc2):
    batch, seq, d = x.shape
    lc = ctx.shape[1]
    depth = w_mod.shape[0]
    assert d == D_MODEL and seq % WA_BLOCK == 0 and seq % GRID_W == 0 and batch < MOD_ROWS
    tm = 1024
    tm_in = 2048
    tm_ctx = 512
    assert seq % tm == 0 and seq % tm_in == 0 and (batch * lc) % tm_ctx == 0

    cond = jnp.zeros((MOD_ROWS, d), F32).at[:batch].set(c).at[batch].set(c_ctx)
    mod = _modulation(cond, w_mod, b_mod)
    rpb_bias, w_in_bf = _rpb_tables(na_rpb, w_in)
    rope_tabs = _rope_tables(seq)

    lat_row = lambda i: i // (seq // tm)
    lat_row_in = lambda i: i // (seq // tm_in)
    ctx_row = lambda i: batch

    xs = x.reshape(batch * seq, d)
    cs = ctx.reshape(batch * lc, d)
    for l in range(depth):
        last = l == depth - 1
        g1 = g_norm1[l].reshape(1, d)
        g2 = g_norm2[l].reshape(1, d)
        gains = [jnp.tile(na_q_gain[l], NA_HEADS).reshape(1, -1), jnp.tile(na_k_gain[l], NA_HEADS).reshape(1, -1),
                 jnp.tile(wa_q_gain[l], WA_HEADS).reshape(1, -1), jnp.tile(wa_k_gain[l], WA_KV_HEADS).reshape(1, -1)]
        go_na = g_out[l, :NA_WIDTH].reshape(1, -1)
        go_cv = g_out[l, NA_WIDTH:NA_WIDTH + CONV_CH].reshape(1, -1)
        go_wa = g_out[l, NA_WIDTH + CONV_CH:].reshape(1, -1)
        cb = conv_bias[l].reshape(1, -1)

        naq, nak, nav, cvu, cvb, waq, wak, wav = _in_proj(xs, mod, l, lat_row_in, g1, w_in_bf, gains, rope_tabs, tm_in)
        if last:
            cnak, cnav, cwak, cwav = _in_proj(cs, mod, l, ctx_row, g1, w_in_bf, gains, None, tm_ctx, kv_only=True)
        else:
            cnaq, cnak, cnav, ccvu, ccvb, cwaq, cwak, cwav = _in_proj(cs, mod, l, ctx_row, g1, w_in_bf, gains, None, tm_ctx)

        b3 = lambda a, n: a.reshape(batch, n, a.shape[-1])
        casts = [(w_o, l), (w_fc1, l), (w_fc2, l)] + ([] if last else [(w_in, l + 1)])
        na_n, wa_n, cast_out = _attention(
            wa_sink[l], b3(naq, seq), b3(nak, seq), b3(nav, seq), b3(cnak, lc), b3(cnav, lc), rpb_bias, l, go_na,
            b3(waq, seq), b3(wak, seq), b3(wav, seq), b3(cwak, lc), b3(cwav, lc), go_wa, casts)
        w_o_bf, w1_bf, w2_bf = cast_out[:3]
        if not last:
            w_in_bf = cast_out[3]
        xs = _out_mlp(xs, mod, l, lat_row, (na_n.reshape(batch * seq, -1), wa_n.reshape(batch * seq, -1)),
                      cvu, cvb, conv_w[l], cb, go_cv, w_o_bf, g2, w1_bf, w2_bf, seq, tm)
        if not last:
            ctx_attn = (wa_sink[l], cnaq, cnak, cnav, cwaq, cwak, cwav, go_na, go_wa)
            cs = _out_mlp(cs, mod, l, ctx_row, ctx_attn, ccvu, ccvb, conv_w[l], cb, go_cv,
                          w_o_bf, g2, w1_bf, w2_bf, lc, lc)
    return xs.reshape(batch, seq, d)
```

```python
import functools

import jax
import jax.numpy as jnp
from jax import lax
from jax.experimental import pallas as pl
from jax.experimental.pallas import tpu as pltpu

D_MODEL = 1024
GRID_W = 64
HEAD_DIM = 64
NA_HEADS = 4
NA_WIDTH = NA_HEADS * HEAD_DIM
CONV_CH = 256
WA_HEADS = 8
WA_KV_HEADS = 2
WA_GROUP = WA_HEADS // WA_KV_HEADS
WA_QW = WA_HEADS * HEAD_DIM
WA_KVW = WA_KV_HEADS * HEAD_DIM
WA_K2W = 2 * WA_KVW
WA_V3W = 2 * WA_KVW
NA_WIN_ROWS = 8
NA_WIN_COLS = 16
NA_DROWS = 2 * NA_WIN_ROWS - 1
NA_DCOLS = 2 * NA_WIN_COLS - 1
WA_WINDOW = 128
WA_BLOCK = 128
WA_BAND = 3 * WA_BLOCK
FFN_DIM = 4 * D_MODEL
FFN_CHUNK = 1024
ROPE_BASE = 10000.0
EPS = 1e-6
NEG_INF = -1e30
IN_WIDTH = 2304
OFF_NA_Q, OFF_NA_K, OFF_NA_V = 0, 256, 512
OFF_CV_X, OFF_CV_B, OFF_CV_C = 768, 1024, 1280
OFF_WA_Q, OFF_WA_K, OFF_WA_V = 1536, 2048, 2176
MOD_ROWS = 16
MOD_TN = 2048
IN_SUB = 256
MLP_SUB = 1024
LANES = 128
MXU_DIM = 256
VMEM_LIMIT = 60 * 1024 * 1024
LOG2E = 1.4426950408889634
Q_SCALE = LOG2E * HEAD_DIM ** -0.5

F32 = jnp.float32
BF16 = jnp.bfloat16


def _dot(a, b):
    return jnp.dot(a, b, preferred_element_type=F32)


def _dot_nt(a, b):
    return lax.dot_general(a, b, (((1,), (1,)), ((), ())), preferred_element_type=F32)


def _split_bf16(a):
    hi = a.astype(BF16)
    lo = (a - hi.astype(F32)).astype(BF16)
    return hi, lo


def _iota(shape, dim):
    return lax.broadcasted_iota(jnp.int32, shape, dim)


def _params(**kw):
    return pltpu.CompilerParams(vmem_limit_bytes=VMEM_LIMIT, **kw)


def _mod_kernel(cond_ref, w_ref, b_ref, o_ref):
    a = cond_ref[...]
    a = a * (1.0 / (1.0 + jnp.exp(-a)))
    ah, al = _split_bf16(a)
    wh, wl = _split_bf16(w_ref[...])
    rows = a.shape[0]
    both = _dot(jnp.concatenate([ah, al], axis=0), wh)
    res = both[:rows] + both[rows:] + _dot(ah, wl) + b_ref[...]
    for c in range(o_ref.shape[1]):
        o_ref[:, c, 0, :] = res[:, c * D_MODEL:(c + 1) * D_MODEL]


def _modulation(cond, w_mod, b_mod):
    depth = w_mod.shape[0]
    n_out = w_mod.shape[2]
    per_tile = MOD_TN // D_MODEL
    return pl.pallas_call(
        _mod_kernel,
        out_shape=jax.ShapeDtypeStruct((depth, MOD_ROWS, n_out // D_MODEL, 1, D_MODEL), F32),
        grid=(depth, n_out // MOD_TN),
        in_specs=[
            pl.BlockSpec((MOD_ROWS, D_MODEL), lambda l, j: (0, 0)),
            pl.BlockSpec((None, D_MODEL, MOD_TN), lambda l, j: (l, 0, j)),
            pl.BlockSpec((None, 1, MOD_TN), lambda l, j: (l, 0, j)),
        ],
        out_specs=pl.BlockSpec((None, MOD_ROWS, per_tile, 1, D_MODEL), lambda l, j: (l, 0, j, 0, 0)),
        compiler_params=_params(),
        name="modulation",
    )(cond, w_mod, b_mod.reshape(depth, 1, n_out))


def _rpb_kernel(rpb_ref, w_ref, o_ref, wbf_ref):
    wbf_ref[...] = w_ref[...].astype(BF16)
    shape = (GRID_W, LANES)
    q = _iota(shape, 0)
    lane = _iota(shape, 1)
    k = lane % GRID_W
    left = lane < GRID_W
    col_start = jnp.clip(q - NA_WIN_COLS // 2, 0, GRID_W - NA_WIN_COLS)
    col_ok = (k >= col_start) & (k < col_start + NA_WIN_COLS)
    centre = NA_WIN_COLS - 1
    for d in range(NA_DROWS - 1):
        tiles = []
        for h in range(NA_HEADS):
            t1 = pltpu.roll(jnp.broadcast_to(rpb_ref[h, d:d + 1, :], shape), LANES - centre, axis=1,
                            stride=1, stride_axis=0)
            t2 = pltpu.roll(jnp.broadcast_to(rpb_ref[h, d + 1:d + 2, :], shape), GRID_W - centre, axis=1,
                            stride=1, stride_axis=0)
            t = jnp.where(left, t1, t2)
            tiles.append(jnp.where(col_ok, t * LOG2E, NEG_INF))
        o_ref[d] = jnp.concatenate(tiles, axis=0).T


def _rpb_tables(na_rpb, w_in):
    depth = na_rpb.shape[0]
    padded = jnp.pad(na_rpb, ((0, 0), (0, 0), (0, 1), (0, LANES - NA_DCOLS)))
    slab = w_in.shape[1] // depth
    return pl.pallas_call(
        _rpb_kernel,
        out_shape=[jax.ShapeDtypeStruct((depth, NA_DROWS - 1, LANES, NA_HEADS * GRID_W), F32),
                   jax.ShapeDtypeStruct(w_in.shape[1:], BF16)],
        grid=(depth,),
        in_specs=[pl.BlockSpec((None,) + padded.shape[1:], lambda l: (l, 0, 0, 0)),
                  pl.BlockSpec((None, slab, w_in.shape[2]), lambda l: (0, l, 0))],
        out_specs=[pl.BlockSpec((None, NA_DROWS - 1, LANES, NA_HEADS * GRID_W), lambda l: (l, 0, 0, 0)),
                   pl.BlockSpec((slab, w_in.shape[2]), lambda l: (l, 0))],
        compiler_params=_params(),
        name="rpb_tables",
    )(padded, w_in)


def _head_rms(t, gain):
    width = t.shape[1]
    sq = (t * t).astype(BF16)
    blk = min(width, MXU_DIM)
    ones = jnp.where(_iota((blk, blk), 0) // HEAD_DIM == _iota((blk, blk), 1) // HEAD_DIM, 1.0, 0.0).astype(BF16)
    sums = [_dot(sq[:, c:c + blk], ones) for c in range(0, width, blk)]
    ss = sums[0] if len(sums) == 1 else jnp.concatenate(sums, axis=1)
    return t * lax.rsqrt(ss * (1.0 / HEAD_DIM) + EPS) * gain


def _rope(t, cos, sin):
    lane = _iota((t.shape[0], LANES), 1)
    first = (lane % (HEAD_DIM // 2)) < (HEAD_DIM // 4)
    out = []
    for c in range(0, t.shape[1], LANES):
        u = t[:, c:c + LANES]
        partner = jnp.where(first, pltpu.roll(u, LANES - HEAD_DIM // 4, axis=1), pltpu.roll(u, HEAD_DIM // 4, axis=1))
        out.append(u * cos + partner * sin)
    return out[0] if len(out) == 1 else jnp.concatenate(out, axis=1)


def _dup_kv(t):
    lane = _iota(t.shape, 1)
    swapped = pltpu.roll(t, HEAD_DIM, axis=1)
    left = lane < HEAD_DIM
    return jnp.concatenate([jnp.where(left, t, swapped), jnp.where(left, swapped, t)], axis=1)


def _inproj_kernel(*refs, rope, kv_only, n_w, sub):
    x_ref, mod_ref, g1_ref = refs[:3]
    w_refs = refs[3:3 + n_w]
    gnq_ref, gnk_ref, gwq_ref, gwk_ref = refs[3 + n_w:7 + n_w]
    rest = refs[7 + n_w:]
    if rope:
        cos_ref, sin_ref = rest[:2]
        rest = rest[2:]
    out_refs = rest
    if kv_only:
        nak_ref, nav_ref, wak_ref, wav_ref = out_refs
        o_nk, o_nv, o_wk, o_wv, o_end = 0, NA_WIDTH, 2 * NA_WIDTH, 2 * NA_WIDTH + WA_KVW, 2 * NA_WIDTH + 2 * WA_KVW
    else:
        naq_ref, nak_ref, nav_ref, cvu_ref, cvb_ref, waq_ref, wak_ref, wav_ref = out_refs
        o_nk, o_nv, o_wk, o_wv, o_end = OFF_NA_K, OFF_NA_V, OFF_WA_K, OFF_WA_V, IN_WIDTH
    def project(rows):
        x = x_ref[rows, :]
        ms = jnp.mean(x * x, axis=-1, keepdims=True)
        h = x * lax.rsqrt(ms + EPS) * g1_ref[...]
        h = h * (1.0 + mod_ref[1]) + mod_ref[0]
        hb = h.astype(BF16)
        return jnp.concatenate([_dot(hb, w_ref[...]) for w_ref in w_refs], axis=1)

    def finish(rows, p):
        nak_ref[rows, :] = _head_rms(p[:, o_nk:o_nk + NA_WIDTH], gnk_ref[...]).astype(BF16)
        nav_ref[rows, :] = p[:, o_nv:o_nv + NA_WIDTH].astype(BF16)
        wk = _head_rms(p[:, o_wk:o_wv], gwk_ref[...])
        if rope:
            wk = _rope(wk, cos_ref[rows, :], sin_ref[rows, :])
        wak_ref[rows, :] = _dup_kv(wk).astype(BF16)
        vv = p[:, o_wv:o_end]
        left = _iota(vv.shape, 1) < HEAD_DIM
        wav_ref[rows, :] = jnp.concatenate([jnp.where(left, vv, 1.0), jnp.where(left, pltpu.roll(vv, HEAD_DIM, axis=1), 1.0)],
                                           axis=1).astype(BF16)
        if kv_only:
            return
        naq_ref[rows, :] = (_head_rms(p[:, OFF_NA_Q:OFF_NA_K], gnq_ref[...]) * Q_SCALE).astype(BF16)
        cvu_ref[rows, :] = p[:, OFF_CV_C:OFF_WA_Q] * p[:, OFF_CV_X:OFF_CV_B]
        cvb_ref[rows, :] = p[:, OFF_CV_B:OFF_CV_C]
        wq = _head_rms(p[:, OFF_WA_Q:OFF_WA_K], gwq_ref[...])
        if rope:
            wq = _rope(wq, cos_ref[rows, :], sin_ref[rows, :])
        waq_ref[rows, :] = (wq * Q_SCALE).astype(BF16)

    for r0 in range(0, x_ref.shape[0], sub):
        rows = slice(r0, r0 + sub)
        finish(rows, project(rows))


def _in_proj(x2d, mod, layer, mod_row, g1, w_in, gains, rope_tabs, tm, kv_only=False):
    rows = x2d.shape[0]
    rope = rope_tabs is not None
    row_spec = lambda w: pl.BlockSpec((tm, w), lambda i: (i, 0))
    const = lambda shape: pl.BlockSpec(shape, lambda i: (0,) * len(shape))
    once = pl.Buffered(1)
    if kv_only:
        col_blocks = [OFF_NA_K // MXU_DIM, OFF_NA_V // MXU_DIM, OFF_WA_K // MXU_DIM]
        w_specs = [pl.BlockSpec((D_MODEL, MXU_DIM), functools.partial(lambda j, i: (0, j), j),
                                pipeline_mode=once) for j in col_blocks]
        widths = [(NA_WIDTH, BF16), (NA_WIDTH, BF16), (WA_K2W, BF16), (WA_V3W, BF16)]
    else:
        w_specs = [pl.BlockSpec((D_MODEL, IN_WIDTH), lambda i: (0, 0), pipeline_mode=once)]
        widths = [(NA_WIDTH, BF16), (NA_WIDTH, BF16), (NA_WIDTH, BF16), (CONV_CH, F32), (CONV_CH, F32),
                  (WA_QW, BF16), (WA_K2W, BF16), (WA_V3W, BF16)]
    n_w = len(w_specs)
    in_specs = [
        row_spec(D_MODEL),
        pl.BlockSpec((None, None, 6, 1, D_MODEL), lambda i: (layer, mod_row(i), 0, 0, 0)),
        const((1, D_MODEL)),
        *w_specs,
        const((1, NA_WIDTH)), const((1, NA_WIDTH)), const((1, WA_QW)), const((1, WA_KVW)),
    ]
    args = [x2d, mod, g1, *([w_in] * n_w), *gains]
    if rope:
        seq_tiles = rope_tabs[0].shape[0] // tm
        in_specs += [pl.BlockSpec((tm, LANES), lambda i: (i % seq_tiles, 0))] * 2
        args += list(rope_tabs)
    return pl.pallas_call(
        functools.partial(_inproj_kernel, rope=rope, kv_only=kv_only, n_w=n_w, sub=min(tm, IN_SUB)),
        out_shape=[jax.ShapeDtypeStruct((rows, w), dt) for w, dt in widths],
        grid=(rows // tm,),
        in_specs=in_specs,
        out_specs=[row_spec(w) for w, _ in widths],
        compiler_params=_params(),
        name="in_proj_rope" if rope else ("in_proj_ctx_kv" if kv_only else "in_proj_ctx"),
    )(*args)


def _lane_chunks(s):
    return [s[:, c:c + LANES] for c in range(0, s.shape[1], LANES)]


def _row_max(scores, extra=None):
    chunks = [ch for s in scores for ch in _lane_chunks(s)]
    if extra is not None:
        chunks.append(extra)
    folded = functools.reduce(jnp.maximum, chunks)
    return jnp.broadcast_to(jnp.max(folded, axis=-1, keepdims=True), folded.shape)


def _softmax_pv(parts, extra=None, sums_from_values=False, m=None):
    lane_chunks = _lane_chunks
    if m is None:
        m = _row_max([s for s, _ in parts], extra)
    ps = [jnp.concatenate([jnp.exp2(ch - m) for ch in lane_chunks(s)], axis=1) for s, _ in parts]
    pv = _dot(jnp.concatenate([p.astype(BF16) for p in ps], axis=1),
              jnp.concatenate([v for _, v in parts], axis=0))
    if sums_from_values:
        denom = pv[:, -LANES:]
        pv = pv[:, :-LANES]
        if extra is not None:
            denom = denom + jnp.exp2(extra - m)
        return pv / denom
    psum = functools.reduce(jnp.add, [ch for p in ps for ch in lane_chunks(p)])
    if extra is not None:
        psum = psum + jnp.where(_iota(extra.shape, 1) == 0, jnp.exp2(extra - m), 0.0)
    return pv / jnp.sum(psum, axis=-1, keepdims=True)


def _stack_heads(q, n_heads):
    head = _iota(q.shape, 1) // HEAD_DIM
    zero = jnp.zeros_like(q)
    return jnp.concatenate([jnp.where(head == h, q, zero) for h in range(n_heads)], axis=0)


def _unstack_heads(o, n_heads):
    rows = o.shape[0] // n_heads
    head = _iota((rows, o.shape[1]), 1) // HEAD_DIM
    out = jnp.where(head == 0, o[:rows], 0.0)
    for h in range(1, n_heads):
        out = jnp.where(head == h, o[h * rows:(h + 1) * rows], out)
    return out


def _group_rms(o, gain):
    ms = jnp.mean(o * o, axis=-1, keepdims=True)
    return o * lax.rsqrt(ms + EPS) * gain


SUBLANES = 8


def _wa_scores(qg, keys):
    rows = qg.shape[0]
    left = _iota((rows, LANES), 1) < HEAD_DIM
    zero = jnp.zeros((rows, LANES), qg.dtype)
    blocks = []
    for pair in range(WA_GROUP // 2):
        qp = qg[:, pair * LANES:(pair + 1) * LANES]
        blocks += [jnp.where(left, qp, zero), jnp.where(left, zero, qp)]
    qs = jnp.concatenate(blocks, axis=0)
    scores = []
    for k2, bias in keys:
        s = _dot_nt(k2, qs)
        if bias is not None:
            s = s + jnp.concatenate([bias] * WA_GROUP, axis=1)
        scores.append(s)
    return scores


def _wa_sink_rows(rows, sinks):
    return jnp.concatenate([jnp.full((SUBLANES, rows), sinks[g] * LOG2E, F32) for g in range(WA_GROUP)], axis=1)


def _wa_col_max(scores, sink_rows):
    m = sink_rows[:1]
    for s in scores:
        m = jnp.maximum(m, jnp.max(s, axis=0, keepdims=True))
    return jnp.broadcast_to(m, sink_rows.shape)


def _wa_finish(scores, values, sinks, m=None):
    rows = scores[0].shape[1] // WA_GROUP
    sink_rows = _wa_sink_rows(rows, sinks)
    if m is None:
        m = _wa_col_max(scores, sink_rows)
    ot = None
    for s, v in zip(scores, values):
        p = jnp.exp2(s - m[:1]).astype(BF16)
        part = lax.dot_general(v, p, (((0,), (0,)), ((), ())), preferred_element_type=F32)
        ot = part if ot is None else ot + part
    denom = ot[HEAD_DIM:HEAD_DIM + SUBLANES] + jnp.exp2(sink_rows - m)
    on = ot[:HEAD_DIM] / denom[:1]
    pairs = [jnp.concatenate([on[:, (2 * pr) * rows:(2 * pr + 1) * rows],
                              on[:, (2 * pr + 1) * rows:(2 * pr + 2) * rows]], axis=0).T
             for pr in range(WA_GROUP // 2)]
    return jnp.concatenate(pairs, axis=1)


def _na_steps(q_ref, k_ref, v_ref, kc_ref, vc_ref, bias_ref, g_ref, o_ref, vt_ref, vct_ref):
    seq = q_ref.shape[0]
    n_rows = seq // GRID_W
    win = NA_WIN_ROWS * GRID_W
    lc = kc_ref.shape[0]
    gain = g_ref[...]

    def prepare():
        vt_ref[0] = v_ref[...].astype(F32).T.astype(BF16)
        vt_ref[1, :, :seq - GRID_W] = v_ref[GRID_W:, :].astype(F32).T.astype(BF16)
        vct_ref[...] = vc_ref[...].astype(F32).T.astype(BF16)

    def window(r):
        start = jnp.clip(r - NA_WIN_ROWS // 2, 0, n_rows - NA_WIN_ROWS)
        return start, start - r + (NA_WIN_ROWS - 1)

    def scores(r, s_ref):
        start, d0 = window(r)
        tok0 = pl.multiple_of(start * GRID_W, GRID_W)
        q = q_ref[pl.ds(pl.multiple_of(r * GRID_W, GRID_W), GRID_W), :]
        qs = _stack_heads(q, NA_HEADS)
        bias = jnp.concatenate([bias_ref[d0 + 2 * j] for j in range(NA_WIN_ROWS // 2)], axis=0)
        s_ctx = _dot_nt(kc_ref[...], qs)
        s_loc = _dot_nt(k_ref[pl.ds(tok0, win), :], qs) + bias
        m = jnp.maximum(jnp.max(s_ctx, axis=0, keepdims=True), jnp.max(s_loc, axis=0, keepdims=True))
        s_ref[:lc, :] = s_ctx
        s_ref[lc:lc + win, :] = s_loc
        s_ref[lc + win:, :] = jnp.broadcast_to(m, (SUBLANES, m.shape[1]))

    def finish(r, s_ref):
        start, _ = window(r)
        odd = start % 2
        lane0 = pl.multiple_of((start - odd) * GRID_W, LANES)
        m = s_ref[lc + win:lc + win + 1, :]
        p_ctx = jnp.exp2(s_ref[:lc, :] - m)
        p_loc = jnp.exp2(s_ref[lc:lc + win, :] - m)
        denom = jnp.sum(p_ctx, axis=0, keepdims=True) + jnp.sum(p_loc, axis=0, keepdims=True)
        ot = _dot(vct_ref[...], p_ctx.astype(BF16)) + _dot(vt_ref[odd, :, pl.ds(lane0, win)], p_loc.astype(BF16))
        o4 = (ot * (1.0 / denom)).T
        o = _unstack_heads(o4, NA_HEADS)
        o_ref[pl.ds(pl.multiple_of(r * GRID_W, GRID_W), GRID_W), :] = _group_rms(o, gain).astype(o_ref.dtype)

    def step(r, cur_ref, nxt_ref, lookahead=True):
        if lookahead:
            scores(r + 1, nxt_ref)
        finish(r, cur_ref)

    return prepare, scores, step


def _wa_steps(sink_ref, q_ref, k_ref, v_ref, kc_ref, vc_ref, g_ref, o_ref):
    seq = q_ref.shape[0]
    n_blocks = seq // WA_BLOCK
    lc = kc_ref.shape[0]
    gain = g_ref[...]
    key = _iota((WA_BAND, WA_BLOCK), 0)
    qry = _iota((WA_BAND, WA_BLOCK), 1)

    def window(n):
        q0 = pl.multiple_of(n * WA_BLOCK, WA_BLOCK)
        k0 = pl.multiple_of(jnp.clip(q0 - WA_BLOCK, 0, seq - WA_BAND), WA_BLOCK)
        return q0, k0

    def scores(n, kh, s_ref):
        q0, k0 = window(n)
        band = jnp.where(jnp.abs((k0 + key) - (q0 + qry)) <= WA_WINDOW, 0.0, NEG_INF)
        lanes = slice(kh * LANES, (kh + 1) * LANES)
        qg = q_ref[pl.ds(q0, WA_BLOCK), kh * WA_GROUP * HEAD_DIM:(kh + 1) * WA_GROUP * HEAD_DIM]
        s_ctx, s_loc = _wa_scores(qg, [(kc_ref[:, lanes], None), (k_ref[pl.ds(k0, WA_BAND), lanes], band)])
        s_ref[kh, :lc, :] = s_ctx
        s_ref[kh, lc:lc + WA_BAND, :] = s_loc
        s_ref[kh, lc + WA_BAND:, :] = _wa_col_max([s_ctx, s_loc], _wa_sink_rows(WA_BLOCK, sinks(kh)))

    def sinks(kh):
        return [sink_ref[kh * WA_GROUP + g] for g in range(WA_GROUP)]

    def finish(n, kh, s_ref):
        _, k0 = window(n)
        lanes = slice(kh * LANES, (kh + 1) * LANES)
        return _wa_finish([s_ref[kh, :lc, :], s_ref[kh, lc:lc + WA_BAND, :]],
                          [vc_ref[:, lanes], v_ref[pl.ds(k0, WA_BAND), lanes]], sinks(kh),
                          m=s_ref[kh, lc + WA_BAND:, :])

    def step(n, cur_ref, nxt_ref, lookahead=True):
        if lookahead:
            for kh in range(WA_KV_HEADS):
                scores(n + 1, kh, nxt_ref)
        o = jnp.concatenate([finish(n, kh, cur_ref) for kh in range(WA_KV_HEADS)], axis=1)
        q0, _ = window(n)
        o_ref[pl.ds(q0, WA_BLOCK), :] = _group_rms(o, gain).astype(o_ref.dtype)

    return scores, step


def _attn_kernel(sink_ref, nq_ref, nk_ref, nv_ref, nkc_ref, nvc_ref, bias_ref, gn_ref,
                 wq_ref, wk_ref, wv_ref, wkc_ref, wvc_ref, gw_ref, *rest, n_cast):
    f32_refs = rest[:n_cast]
    on_ref, ow_ref = rest[n_cast:n_cast + 2]
    bf16_refs = rest[n_cast + 2:2 * n_cast + 2]
    nsa_ref, nsb_ref, wsa_ref, wsb_ref, vt_ref, vct_ref = rest[2 * n_cast + 2:]
    for src_ref, dst_ref in zip(f32_refs, bf16_refs):
        dst_ref[...] = src_ref[...].astype(dst_ref.dtype)
    na_prepare, na_scores, na_step = _na_steps(nq_ref, nk_ref, nv_ref, nkc_ref, nvc_ref, bias_ref, gn_ref, on_ref,
                                               vt_ref, vct_ref)
    na_prepare()
    wa_scores, wa_step = _wa_steps(sink_ref, wq_ref, wk_ref, wv_ref, wkc_ref, wvc_ref, gw_ref, ow_ref)
    n_blocks = wq_ref.shape[0] // WA_BLOCK

    for kh in range(WA_KV_HEADS):
        wa_scores(0, kh, wsa_ref)
    na_scores(0, nsa_ref)

    def pair(i, last=False):
        for half, (w_cur, w_nxt) in enumerate(((wsa_ref, wsb_ref), (wsb_ref, wsa_ref))):
            n = 2 * i + half
            final = last and half == 1
            wa_step(n, w_cur, w_nxt, lookahead=not final)
            na_step(2 * n, nsa_ref, nsb_ref)
            na_step(2 * n + 1, nsb_ref, nsa_ref, lookahead=not final)

    def body(i, carry):
        pair(i)
        return carry

    n_pairs = n_blocks // 2
    lax.fori_loop(0, n_pairs - 2, body, 0, unroll=3)
    pair(n_pairs - 2)
    pair(n_pairs - 1, last=True)


def _attention(sink, nq, nk, nv, nkc, nvc, bias, layer, gain_na, wq, wk2, wv3, wkc2, wvc3, gain_wa, to_bf16):
    batch, seq, _ = nq.shape
    slab_in = lambda w, j: pl.BlockSpec((None, w.shape[1] // batch, w.shape[2]), lambda b: (j, b, 0))
    slab_out = lambda w: pl.BlockSpec((w.shape[1] // batch, w.shape[2]), lambda b: (b, 0))
    lc = nkc.shape[1]
    assert WA_BLOCK == 2 * GRID_W and (seq // WA_BLOCK) % 2 == 0
    per_b = lambda n, w: pl.BlockSpec((None, n, w), lambda b: (b, 0, 0))
    na_scratch = pltpu.VMEM((lc + NA_WIN_ROWS * GRID_W + SUBLANES, NA_HEADS * GRID_W), F32)
    vt_scratch = pltpu.VMEM((2, NA_WIDTH, seq), BF16)
    vct_scratch = pltpu.VMEM((NA_WIDTH, lc), BF16)
    wa_scratch = pltpu.VMEM((WA_KV_HEADS, lc + WA_BAND + SUBLANES, WA_GROUP * WA_BLOCK), F32)
    outs = pl.pallas_call(
        functools.partial(_attn_kernel, n_cast=len(to_bf16)),
        out_shape=[jax.ShapeDtypeStruct((batch, seq, NA_WIDTH), BF16),
                   jax.ShapeDtypeStruct((batch, seq, WA_QW), BF16),
                   *[jax.ShapeDtypeStruct(w.shape[1:], BF16) for w, _ in to_bf16]],
        grid=(batch,),
        in_specs=[pl.BlockSpec(memory_space=pltpu.SMEM),
                  per_b(seq, NA_WIDTH), per_b(seq, NA_WIDTH), per_b(seq, NA_WIDTH),
                  per_b(lc, NA_WIDTH), per_b(lc, NA_WIDTH),
                  pl.BlockSpec((None,) + bias.shape[1:], lambda b: (layer, 0, 0, 0)),
                  pl.BlockSpec((1, NA_WIDTH), lambda b: (0, 0)),
                  per_b(seq, WA_QW), per_b(seq, WA_K2W), per_b(seq, WA_V3W),
                  per_b(lc, WA_K2W), per_b(lc, WA_V3W),
                  pl.BlockSpec((1, WA_QW), lambda b: (0, 0)),
                  *[slab_in(w, j) for w, j in to_bf16]],
        out_specs=[per_b(seq, NA_WIDTH), per_b(seq, WA_QW), *[slab_out(w) for w, _ in to_bf16]],
        scratch_shapes=[na_scratch, na_scratch, wa_scratch, wa_scratch, vt_scratch, vct_scratch],
        compiler_params=_params(),
        name="attention",
    )(sink, nq, nk, nv, nkc, nvc, bias, gain_na, wq, wk2, wv3, wkc2, wvc3, gain_wa, *[w for w, _ in to_bf16])
    return outs[0], outs[1], outs[2:]


def _ctx_attn(sink_ref, qn_ref, kn_ref, vn_ref, qw_ref, kw_ref, vw_ref, gn_ref, gw_ref):
    qs = _stack_heads(qn_ref[...], NA_HEADS)
    o4 = _softmax_pv([(_dot_nt(qs, kn_ref[...]), vn_ref[...])])
    na_n = _group_rms(_unstack_heads(o4, NA_HEADS), gn_ref[...]).astype(BF16)
    outs = []
    for kh in range(WA_KV_HEADS):
        lanes = slice(kh * LANES, (kh + 1) * LANES)
        qg = qw_ref[:, kh * WA_GROUP * HEAD_DIM:(kh + 1) * WA_GROUP * HEAD_DIM]
        sinks = [sink_ref[kh * WA_GROUP + g] for g in range(WA_GROUP)]
        outs.append(_wa_finish(_wa_scores(qg, [(kw_ref[:, lanes], None)]), [vw_ref[:, lanes]], sinks))
    return na_n, _group_rms(jnp.concatenate(outs, axis=1), gw_ref[...]).astype(BF16)


def _out_mlp_kernel(*refs, seq_len, sub, ctx_attn):
    x_ref, mod_ref = refs[:2]
    n_attn = 9 if ctx_attn else 2
    attn_refs = refs[2:2 + n_attn]
    u_ref, uprev_ref, unext_ref, cvb_ref, cw_ref, cb_ref, g_ref, g2_ref, wo_ref = refs[2 + n_attn:11 + n_attn]
    rest = refs[11 + n_attn:]
    n_ffn = FFN_DIM // FFN_CHUNK
    w1_refs, w2_refs, (o_ref,) = rest[:n_ffn], rest[n_ffn:2 * n_ffn], rest[2 * n_ffn:]
    tm = x_ref.shape[0]
    i = pl.program_id(0)
    if ctx_attn:
        na_t, wa_t = _ctx_attn(*attn_refs)
    else:
        na_t, wa_t = attn_refs
    u = u_ref[...]
    at_start = (i * tm) % seq_len == 0
    at_end = ((i + 1) * tm) % seq_len == 0
    prev_row = jnp.where(at_start, 0.0, uprev_ref[7:8, :])
    next_row = jnp.where(at_end, 0.0, unext_ref[0:1, :])
    row = _iota(u.shape, 0)
    up = jnp.where(row == 0, prev_row, pltpu.roll(u, 1, axis=0))
    dn = jnp.where(row == tm - 1, next_row, pltpu.roll(u, tm - 1, axis=0))
    y = cb_ref[...] + cw_ref[0:1, :] * up + cw_ref[1:2, :] * u + cw_ref[2:3, :] * dn
    cv = _group_rms(cvb_ref[...] * y, g_ref[...]).astype(BF16)
    for r0 in range(0, tm, sub):
        rows = slice(r0, r0 + sub)
        mixed = jnp.concatenate([na_t[rows, :], cv[rows, :], wa_t[rows, :]], axis=1)
        x = x_ref[rows, :] + mod_ref[2] * _dot(mixed, wo_ref[...])
        ms = jnp.mean(x * x, axis=-1, keepdims=True)
        h = x * lax.rsqrt(ms + EPS) * g2_ref[...]
        h = (h * (1.0 + mod_ref[4]) + mod_ref[3]).astype(BF16)
        acc = None
        for w1_ref, w2_ref in zip(w1_refs, w2_refs):
            a = jnp.maximum(_dot(h, w1_ref[...]), 0.0)
            part = _dot((a * a).astype(BF16), w2_ref[...])
            acc = part if acc is None else acc + part
        o_ref[rows, :] = x + mod_ref[5] * acc


def _out_mlp(x2d, mod, layer, mod_row, attn, cv_u, cv_b, conv_w, conv_b, gain_cv, w_o, g2, w1, w2, seq_len, tm):
    rows = x2d.shape[0]
    ctx_attn = len(attn) == 9
    assert not ctx_attn or tm == seq_len
    halo = 8
    per_tile = tm // halo
    last = rows // halo - 1
    row_spec = lambda w: pl.BlockSpec((tm, w), lambda i: (i, 0))
    const = lambda shape: pl.BlockSpec(shape, lambda i: (0,) * len(shape), pipeline_mode=pl.Buffered(1))
    once = pl.Buffered(1)
    n_ffn = FFN_DIM // FFN_CHUNK
    wo_spec = pl.BlockSpec((D_MODEL, D_MODEL), lambda i: (0, 0), pipeline_mode=once)
    w1_specs = [pl.BlockSpec((D_MODEL, FFN_CHUNK), functools.partial(lambda j, i: (0, j), j),
                             pipeline_mode=once) for j in range(n_ffn)]
    w2_specs = [pl.BlockSpec((FFN_CHUNK, D_MODEL), functools.partial(lambda j, i: (j, 0), j),
                             pipeline_mode=once) for j in range(n_ffn)]
    small = lambda a: pl.BlockSpec(a.shape, lambda i: (0,) * a.ndim)
    if ctx_attn:
        attn_specs = [pl.BlockSpec(memory_space=pltpu.SMEM), *[row_spec(a.shape[1]) for a in attn[1:7]],
                      small(attn[7]), small(attn[8])]
    else:
        attn_specs = [row_spec(NA_WIDTH), row_spec(WA_QW)]
    return pl.pallas_call(
        functools.partial(_out_mlp_kernel, seq_len=seq_len, sub=min(tm, MLP_SUB), ctx_attn=ctx_attn),
        out_shape=jax.ShapeDtypeStruct((rows, D_MODEL), F32),
        grid=(rows // tm,),
        in_specs=[
            row_spec(D_MODEL),
            pl.BlockSpec((None, None, 6, 1, D_MODEL), lambda i: (layer, mod_row(i), 0, 0, 0)),
            *attn_specs, row_spec(CONV_CH),
            pl.BlockSpec((halo, CONV_CH), lambda i: (jnp.maximum(i * per_tile - 1, 0), 0)),
            pl.BlockSpec((halo, CONV_CH), lambda i: (jnp.minimum((i + 1) * per_tile, last), 0)),
            row_spec(CONV_CH),
            const((3, CONV_CH)), const((1, CONV_CH)), const((1, CONV_CH)),
            const((1, D_MODEL)),
            wo_spec, *w1_specs, *w2_specs,
        ],
        out_specs=row_spec(D_MODEL),
        compiler_params=_params(),
        name="out_mlp",
    )(x2d, mod, *attn, cv_u, cv_u, cv_u, cv_b, conv_w, conv_b, gain_cv, g2,
      w_o, *([w1] * n_ffn), *([w2] * n_ffn))


def _rope_tables(seq):
    quarter = HEAD_DIM // 4
    inv = ROPE_BASE ** (-jnp.arange(quarter, dtype=F32) / quarter)
    t = jnp.arange(seq)
    ang_r = (t // GRID_W).astype(F32)[:, None] * inv[None, :]
    ang_c = (t % GRID_W).astype(F32)[:, None] * inv[None, :]
    cos = jnp.concatenate([jnp.cos(ang_r), jnp.cos(ang_r), jnp.cos(ang_c), jnp.cos(ang_c)], axis=1)
    sin = jnp.concatenate([-jnp.sin(ang_r), jnp.sin(ang_r), -jnp.sin(ang_c), jnp.sin(ang_c)], axis=1)
    reps = LANES // HEAD_DIM
    return jnp.tile(cos, (1, reps)), jnp.tile(sin, (1, reps))


def kernel(x, c, ctx, c_ctx, w_mod, b_mod, g_norm1, g_norm2, w_in, na_q_gain, na_k_gain, na_rpb,
           conv_w, conv_bias, wa_q_gain, wa_k_gain, wa_sink, g_out, w_o, w_fc1, w_fc2):
    batch, seq, d = x.shape
    lc = ctx.shape[1]
    depth = w_mod.shape[0]
    assert d == D_MODEL and seq % WA_BLOCK == 0 and seq % GRID_W == 0 and batch < MOD_ROWS
    tm = 1024
    tm_in = 2048
    tm_ctx = 512
    assert seq % tm == 0 and seq % tm_in == 0 and (batch * lc) % tm_ctx == 0

    cond = jnp.zeros((MOD_ROWS, d), F32).at[:batch].set(c).at[batch].set(c_ctx)
    mod = _modulation(cond, w_mod, b_mod)
    rpb_bias, w_in_bf = _rpb_tables(na_rpb, w_in)
    rope_tabs = _rope_tables(seq)

    lat_row = lambda i: i // (seq // tm)
    lat_row_in = lambda i: i // (seq // tm_in)
    ctx_row = lambda i: batch

    xs = x.reshape(batch * seq, d)
    cs = ctx.reshape(batch * lc, d)
    for l in range(depth):
        last = l == depth - 1
        g1 = g_norm1[l].reshape(1, d)
        g2 = g_norm2[l].reshape(1, d)
        gains = [jnp.tile(na_q_gain[l], NA_HEADS).reshape(1, -1), jnp.tile(na_k_gain[l], NA_HEADS).reshape(1, -1),
                 jnp.tile(wa_q_gain[l], WA_HEADS).reshape(1, -1), jnp.tile(wa_k_gain[l], WA_KV_HEADS).reshape(1, -1)]
        go_na = g_out[l, :NA_WIDTH].reshape(1, -1)
        go_cv = g_out[l, NA_WIDTH:NA_WIDTH + CONV_CH].reshape(1, -1)
        go_wa = g_out[l, NA_WIDTH + CONV_CH:].reshape(1, -1)
        cb = conv_bias[l].reshape(1, -1)

        naq, nak, nav, cvu, cvb, waq, wak, wav = _in_proj(xs, mod, l, lat_row_in, g1, w_in_bf, gains, rope_tabs, tm_in)
        if last:
            cnak, cnav, cwak, cwav = _in_proj(cs, mod, l, ctx_row, g1, w_in_bf, gains, None, tm_ctx, kv_only=True)
        else:
            cnaq, cnak, cnav, ccvu, ccvb, cwaq, cwak, cwav = _in_proj(cs, mod, l, ctx_row, g1, w_in_bf, gains, None, tm_ctx)

        b3 = lambda a, n: a.reshape(batch, n, a.shape[-1])
        casts = [(w_o, l), (w_fc1, l), (w_fc2, l)] + ([] if last else [(w_in, l + 1)])
        na_n, wa_n, cast_out = _attention(
            wa_sink[l], b3(naq, seq), b3(nak, seq), b3(nav, seq), b3(cnak, lc), b3(cnav, lc), rpb_bias, l, go_na,
            b3(waq, seq), b3(wak, seq), b3(wav, seq), b3(cwak, lc), b3(cwav, lc), go_wa, casts)
        w_o_bf, w1_bf, w2_bf = cast_out[:3]
        if not last:
            w_in_bf = cast_out[3]
        xs = _out_mlp(xs, mod, l, lat_row, (na_n.reshape(batch * seq, -1), wa_n.reshape(batch * seq, -1)),
                      cvu, cvb, conv_w[l], cb, go_cv, w_o_bf, g2, w1_bf, w2_bf, seq, tm)
        if not last:
            ctx_attn = (wa_sink[l], cnaq, cnak, cnav, cwaq, cwak, cwav, go_na, go_wa)
            cs = _out_mlp(cs, mod, l, ctx_row, ctx_attn, ccvu, ccvb, conv_w[l], cb, go_cv,
                          w_o_bf, g2, w1_bf, w2_bf, lc, lc)
    return xs.reshape(batch, seq, d)
```
